```python
import jax, jax.numpy as jnp
from jax import lax
import numpy as np

D_MODEL = 2048
BATCH = 8
SEQ = 8192
DEPTH = 4

N_MIXERS = 3
GRID_W = 64
BRANCH_W = D_MODEL
NORM_EPS = 1e-6

POOL_WINDOWS = (2, 4, 8, 16)
N_POOL_GROUPS = 4
POOL_GROUP_DIM = BRANCH_W // N_POOL_GROUPS

GLA_HEADS = 4
GLA_KEY_W = D_MODEL // 2
GLA_VAL_W = BRANCH_W
GLA_DK = GLA_KEY_W // GLA_HEADS
GLA_DV = GLA_VAL_W // GLA_HEADS
GLA_LOWRANK = 16
GLA_TAU = 16.0
GLA_CHUNK = 64

ATTN_HEAD_DIM = 128
ATTN_HEADS = BRANCH_W // ATTN_HEAD_DIM
ATTN_KV_HEADS = 4
ATTN_GROUP = ATTN_HEADS // ATTN_KV_HEADS
ATTN_Q_W = ATTN_HEADS * ATTN_HEAD_DIM
ATTN_KV_W = ATTN_KV_HEADS * ATTN_HEAD_DIM
ATTN_BLOCK = 128
ROPE_AXIS_DIM = ATTN_HEAD_DIM // 2
ROPE_THETA = 10000.0

kernel_name = "hybrid_pool_gla_gqa_encoder"


def rms_norm(x, eps=NORM_EPS):
    xf = x.astype(jnp.float32)
    return (xf * lax.rsqrt(jnp.mean(xf * xf, axis=-1, keepdims=True) + eps)).astype(x.dtype)


def ada_modulation(c, w, b):
    m = jnp.dot(jax.nn.silu(c), w) + b
    shift, scale, gate = jnp.split(m, 3, axis=-1)
    return shift[:, None], scale[:, None], gate[:, None]


def centred_window_mean(u, w):
    T = u.shape[1]
    csum = jnp.cumsum(u.astype(jnp.float32), axis=1)
    P = jnp.concatenate([jnp.zeros_like(csum[:, :1]), csum], axis=1)
    t = jnp.arange(T)
    lo = jnp.maximum(t - w // 2, 0)
    hi = jnp.minimum(t + w // 2, T)
    cnt = (hi - lo).astype(jnp.float32)
    return ((P[:, hi] - P[:, lo]) / cnt[None, :, None]).astype(u.dtype)


def pool_mixer(h, w_in, w_grp, scale, w_out):
    B, T, _ = h.shape
    u, g = jnp.split(h @ w_in, 2, axis=-1)
    ug = u.reshape(B, T, N_POOL_GROUPS, POOL_GROUP_DIM)
    pooled = jnp.stack(
        [centred_window_mean(ug[:, :, i], w) - ug[:, :, i] for i, w in enumerate(POOL_WINDOWS)],
        axis=2)
    y = jnp.einsum('btgc,gcd->btgd', pooled, w_grp).reshape(B, T, BRANCH_W) * scale
    return (y * jax.nn.silu(g)) @ w_out


def gla_chunked_scan(q, k, v, log_a, include_diag):
    B, T, H, DK = q.shape
    DV = v.shape[-1]
    nc = T // GLA_CHUNK

    def to_chunks(z):
        return z.astype(jnp.float32).reshape(B, nc, GLA_CHUNK, H, z.shape[-1]).transpose(1, 0, 3, 2, 4)

    qc, kc, vc, ac = to_chunks(q), to_chunks(k), to_chunks(v), to_chunks(log_a)
    mask = jnp.tril(jnp.ones((GLA_CHUNK, GLA_CHUNK), dtype=bool), 0 if include_diag else -1)

    def step(S, inp):
        qi, ki, vi, ai = inp
        b = jnp.cumsum(ai, axis=2)
        o_inter = jnp.einsum('bhcd,bhde->bhce', qi * jnp.exp(b), S)
        diff = b[:, :, :, None, :] - b[:, :, None, :, :]
        decay = jnp.exp(jnp.where(mask[:, :, None], diff, -jnp.inf))
        A = jnp.sum(qi[:, :, :, None, :] * ki[:, :, None, :, :] * decay, axis=-1)
        o_intra = jnp.einsum('bhij,bhje->bhie', A, vi)
        b_last = b[:, :, -1:, :]
        S_new = jnp.exp(b_last[:, :, 0, :])[..., None] * S + jnp.einsum(
            'bhjd,bhje->bhde', ki * jnp.exp(b_last - b), vi)
        return S_new, o_inter + o_intra

    S0 = jnp.zeros((B, H, DK, DV), jnp.float32)
    _, o = lax.scan(step, S0, (qc, kc, vc, ac))
    return o.transpose(1, 0, 3, 2, 4).reshape(B, T, H, DV)


def gla_mixer(h, w_in, fwd_w1, fwd_w2, fwd_b, bwd_w1, bwd_w2, bwd_b, norm_g, w_out):
    B, T, _ = h.shape
    q, k, v, g = jnp.split(h @ w_in, [GLA_KEY_W, 2 * GLA_KEY_W, 2 * GLA_KEY_W + GLA_VAL_W], axis=-1)
    q = (q * GLA_DK ** -0.5).reshape(B, T, GLA_HEADS, GLA_DK)
    k = k.reshape(B, T, GLA_HEADS, GLA_DK)
    v = v.reshape(B, T, GLA_HEADS, GLA_DV)

    def log_decay(w1, w2, b):
        z = ((h @ w1) @ w2 + b).astype(jnp.float32)
        return (jax.nn.log_sigmoid(z) / GLA_TAU).reshape(B, T, GLA_HEADS, GLA_DK)

    rev = lambda z: z[:, ::-1]
    o_fwd = gla_chunked_scan(q, k, v, log_decay(fwd_w1, fwd_w2, fwd_b), include_diag=True)
    o_bwd = rev(gla_chunked_scan(rev(q), rev(k), rev(v), rev(log_decay(bwd_w1, bwd_w2, bwd_b)),
                                 include_diag=False))
    o = (rms_norm(o_fwd + o_bwd) * norm_g).reshape(B, T, GLA_VAL_W).astype(h.dtype)
    return (o * jax.nn.silu(g)) @ w_out


def axial_rope_tables(T):
    rows = T // GRID_W
    t = jnp.arange(T)
    row = (t // GRID_W - rows // 2).astype(jnp.float32)
    col = (t % GRID_W - GRID_W // 2).astype(jnp.float32)
    inv = ROPE_THETA ** (-jnp.arange(0, ROPE_AXIS_DIM, 2, dtype=jnp.float32) / ROPE_AXIS_DIM)
    ang = jnp.concatenate([row[:, None] * inv, col[:, None] * inv], axis=-1)
    return jnp.cos(ang), jnp.sin(ang)


def apply_rope(x, cos, sin):
    shp = (cos.shape[0],) + (1,) * (x.ndim - 3) + (cos.shape[1],)
    cos = cos.reshape(shp).astype(x.dtype)
    sin = sin.reshape(shp).astype(x.dtype)
    xr = x.reshape(x.shape[:-1] + (-1, 2))
    x0, x1 = xr[..., 0], xr[..., 1]
    return jnp.stack([x0 * cos - x1 * sin, x0 * sin + x1 * cos], axis=-1).reshape(x.shape)


def attn_mixer(h, w_in, q_norm_g, k_norm_g, w_out):
    B, T, _ = h.shape
    q, k, v, g = jnp.split(h @ w_in, [ATTN_Q_W, ATTN_Q_W + ATTN_KV_W, ATTN_Q_W + 2 * ATTN_KV_W], axis=-1)
    q = rms_norm(q.reshape(B, T, ATTN_KV_HEADS, ATTN_GROUP, ATTN_HEAD_DIM)) * q_norm_g
    k = rms_norm(k.reshape(B, T, ATTN_KV_HEADS, ATTN_HEAD_DIM)) * k_norm_g
    v = v.reshape(B, T, ATTN_KV_HEADS, ATTN_HEAD_DIM)
    cos, sin = axial_rope_tables(T)
    q = apply_rope(q, cos, sin) * ATTN_HEAD_DIM ** -0.5
    k = apply_rope(k, cos, sin)
    nb = T // ATTN_BLOCK
    qb = q.reshape(B, nb, ATTN_BLOCK, ATTN_KV_HEADS, ATTN_GROUP, ATTN_HEAD_DIM).transpose(1, 0, 2, 3, 4, 5)

    def block(qi):
        s = jnp.einsum('bqkgd,bskd->bkgqs', qi, k, preferred_element_type=jnp.float32)
        p = jax.nn.softmax(s, axis=-1).astype(v.dtype)
        return jnp.einsum('bkgqs,bskd->bqkgd', p, v)

    o = lax.map(block, qb).transpose(1, 0, 2, 3, 4, 5).reshape(B, T, ATTN_Q_W)
    return (o * jax.nn.silu(g)) @ w_out


def _fwd_setup_inputs(seed: int = 0) -> dict:
    key = jax.random.key(seed)
    ks = iter(jax.random.split(key, 40))
    n_pool = len(range(0, DEPTH, N_MIXERS))
    n_gla = len(range(1, DEPTH, N_MIXERS))
    n_attn = len(range(2, DEPTH, N_MIXERS))
    D = D_MODEL

    def nrm(shape, scale):
        return jax.random.normal(next(ks), shape, jnp.float32) * scale

    def gain(shape):
        return 1.0 + 0.1 * jax.random.normal(next(ks), shape, jnp.float32)

    return {
        "x": nrm((BATCH, SEQ, D), 1.0),
        "c": nrm((BATCH, D), 1.0),
        "w_mod": nrm((DEPTH, D, 3 * D), 0.5 * D ** -0.5),
        "b_mod": nrm((DEPTH, 3 * D), 0.02),
        "pool_w_in": nrm((n_pool, D, 2 * BRANCH_W), D ** -0.5),
        "pool_w_grp": nrm((n_pool, N_POOL_GROUPS, POOL_GROUP_DIM, POOL_GROUP_DIM), POOL_GROUP_DIM ** -0.5),
        "pool_scale": gain((n_pool, BRANCH_W)),
        "pool_w_out": nrm((n_pool, BRANCH_W, D), BRANCH_W ** -0.5),
        "gla_w_in": nrm((n_gla, D, 2 * GLA_KEY_W + 2 * GLA_VAL_W), D ** -0.5),
        "gla_fwd_w1": nrm((n_gla, D, GLA_LOWRANK), D ** -0.5),
        "gla_fwd_w2": nrm((n_gla, GLA_LOWRANK, GLA_KEY_W), GLA_LOWRANK ** -0.5),
        "gla_fwd_b": nrm((n_gla, GLA_KEY_W), 0.1),
        "gla_bwd_w1": nrm((n_gla, D, GLA_LOWRANK), D ** -0.5),
        "gla_bwd_w2": nrm((n_gla, GLA_LOWRANK, GLA_KEY_W), GLA_LOWRANK ** -0.5),
        "gla_bwd_b": nrm((n_gla, GLA_KEY_W), 0.1),
        "gla_norm_g": gain((n_gla, GLA_DV)),
        "gla_w_out": nrm((n_gla, GLA_VAL_W, D), GLA_VAL_W ** -0.5),
        "attn_w_in": nrm((n_attn, D, 2 * ATTN_Q_W + 2 * ATTN_KV_W), D ** -0.5),
        "attn_q_norm_g": gain((n_attn, ATTN_HEAD_DIM)),
        "attn_k_norm_g": gain((n_attn, ATTN_HEAD_DIM)),
        "attn_w_out": nrm((n_attn, ATTN_Q_W, D), ATTN_Q_W ** -0.5),
        "final_norm_g": gain((D,)),
    }


def _fwd_reference(x, c, w_mod, b_mod, pool_w_in, pool_w_grp, pool_scale, pool_w_out,
              gla_w_in, gla_fwd_w1, gla_fwd_w2, gla_fwd_b, gla_bwd_w1, gla_bwd_w2, gla_bwd_b,
              gla_norm_g, gla_w_out, attn_w_in, attn_q_norm_g, attn_k_norm_g, attn_w_out,
              final_norm_g):
    for i in range(DEPTH):
        shift, scale, gate = ada_modulation(c, w_mod[i], b_mod[i])
        h = rms_norm(x) * (1.0 + scale) + shift
        kind, j = i % N_MIXERS, i // N_MIXERS
        if kind == 0:
            y = pool_mixer(h, pool_w_in[j], pool_w_grp[j], pool_scale[j], pool_w_out[j])
        elif kind == 1:
            y = gla_mixer(h, gla_w_in[j], gla_fwd_w1[j], gla_fwd_w2[j], gla_fwd_b[j],
                          gla_bwd_w1[j], gla_bwd_w2[j], gla_bwd_b[j], gla_norm_g[j], gla_w_out[j])
        else:
            y = attn_mixer(h, attn_w_in[j], attn_q_norm_g[j], attn_k_norm_g[j], attn_w_out[j])
        x = x + gate * y
    return rms_norm(x) * final_norm_g


import jax as _jax
import jax.numpy as _jnp

TWIN_FORMAT = 'train_step'
FWD_PARAMS = ['x', 'c', 'w_mod', 'b_mod', 'pool_w_in', 'pool_w_grp', 'pool_scale', 'pool_w_out', 'gla_w_in', 'gla_fwd_w1', 'gla_fwd_w2', 'gla_fwd_b', 'gla_bwd_w1', 'gla_bwd_w2', 'gla_bwd_b', 'gla_norm_g', 'gla_w_out', 'attn_w_in', 'attn_q_norm_g', 'attn_k_norm_g', 'attn_w_out', 'final_norm_g']
TWIN_WEIGHTS = ['w_mod', 'b_mod', 'pool_w_in', 'pool_w_grp', 'pool_scale', 'pool_w_out', 'gla_w_in', 'gla_fwd_w1', 'gla_fwd_w2', 'gla_fwd_b', 'gla_bwd_w1', 'gla_bwd_w2', 'gla_bwd_b', 'gla_norm_g', 'gla_w_out', 'attn_w_in', 'attn_q_norm_g', 'attn_k_norm_g', 'attn_w_out', 'final_norm_g']
TWIN_DIFF_INPUT = 'x'
TWIN_INPUTS = ['x', 'c', 'w_mod', 'b_mod', 'pool_w_in', 'pool_w_grp', 'pool_scale', 'pool_w_out', 'gla_w_in', 'gla_fwd_w1', 'gla_fwd_w2', 'gla_fwd_b', 'gla_bwd_w1', 'gla_bwd_w2', 'gla_bwd_b', 'gla_norm_g', 'gla_w_out', 'attn_w_in', 'attn_q_norm_g', 'attn_k_norm_g', 'attn_w_out', 'final_norm_g', 'loss_target', 'm_w_mod', 'm_b_mod', 'm_pool_w_in', 'm_pool_w_grp', 'm_pool_scale', 'm_pool_w_out', 'm_gla_w_in', 'm_gla_fwd_w1', 'm_gla_fwd_w2', 'm_gla_fwd_b', 'm_gla_bwd_w1', 'm_gla_bwd_w2', 'm_gla_bwd_b', 'm_gla_norm_g', 'm_gla_w_out', 'm_attn_w_in', 'm_attn_q_norm_g', 'm_attn_k_norm_g', 'm_attn_w_out', 'm_final_norm_g', 'v_w_mod', 'v_b_mod', 'v_pool_w_in', 'v_pool_w_grp', 'v_pool_scale', 'v_pool_w_out', 'v_gla_w_in', 'v_gla_fwd_w1', 'v_gla_fwd_w2', 'v_gla_fwd_b', 'v_gla_bwd_w1', 'v_gla_bwd_w2', 'v_gla_bwd_b', 'v_gla_norm_g', 'v_gla_w_out', 'v_attn_w_in', 'v_attn_q_norm_g', 'v_attn_k_norm_g', 'v_attn_w_out', 'v_final_norm_g']
TWIN_OUTPUTS = ['loss', 'grad_x', 'grad_w_mod', 'grad_b_mod', 'grad_pool_w_in', 'grad_pool_w_grp', 'grad_pool_scale', 'grad_pool_w_out', 'grad_gla_w_in', 'grad_gla_fwd_w1', 'grad_gla_fwd_w2', 'grad_gla_fwd_b', 'grad_gla_bwd_w1', 'grad_gla_bwd_w2', 'grad_gla_bwd_b', 'grad_gla_norm_g', 'grad_gla_w_out', 'grad_attn_w_in', 'grad_attn_q_norm_g', 'grad_attn_k_norm_g', 'grad_attn_w_out', 'grad_final_norm_g', 'delta_w_mod', 'delta_b_mod', 'delta_pool_w_in', 'delta_pool_w_grp', 'delta_pool_scale', 'delta_pool_w_out', 'delta_gla_w_in', 'delta_gla_fwd_w1', 'delta_gla_fwd_w2', 'delta_gla_fwd_b', 'delta_gla_bwd_w1', 'delta_gla_bwd_w2', 'delta_gla_bwd_b', 'delta_gla_norm_g', 'delta_gla_w_out', 'delta_attn_w_in', 'delta_attn_q_norm_g', 'delta_attn_k_norm_g', 'delta_attn_w_out', 'delta_final_norm_g', 'new_m_w_mod', 'new_m_b_mod', 'new_m_pool_w_in', 'new_m_pool_w_grp', 'new_m_pool_scale', 'new_m_pool_w_out', 'new_m_gla_w_in', 'new_m_gla_fwd_w1', 'new_m_gla_fwd_w2', 'new_m_gla_fwd_b', 'new_m_gla_bwd_w1', 'new_m_gla_bwd_w2', 'new_m_gla_bwd_b', 'new_m_gla_norm_g', 'new_m_gla_w_out', 'new_m_attn_w_in', 'new_m_attn_q_norm_g', 'new_m_attn_k_norm_g', 'new_m_attn_w_out', 'new_m_final_norm_g', 'new_v_w_mod', 'new_v_b_mod', 'new_v_pool_w_in', 'new_v_pool_w_grp', 'new_v_pool_scale', 'new_v_pool_w_out', 'new_v_gla_w_in', 'new_v_gla_fwd_w1', 'new_v_gla_fwd_w2', 'new_v_gla_fwd_b', 'new_v_gla_bwd_w1', 'new_v_gla_bwd_w2', 'new_v_gla_bwd_b', 'new_v_gla_norm_g', 'new_v_gla_w_out', 'new_v_attn_w_in', 'new_v_attn_q_norm_g', 'new_v_attn_k_norm_g', 'new_v_attn_w_out', 'new_v_final_norm_g']
TWIN_LEAF_KINDS = {'loss': 'loss', 'grad_x': 'grad_x', 'grad_w_mod': 'grad_w', 'grad_b_mod': 'grad_w', 'grad_pool_w_in': 'grad_w', 'grad_pool_w_grp': 'grad_w', 'grad_pool_scale': 'grad_w', 'grad_pool_w_out': 'grad_w', 'grad_gla_w_in': 'grad_w', 'grad_gla_fwd_w1': 'grad_w', 'grad_gla_fwd_w2': 'grad_w', 'grad_gla_fwd_b': 'grad_w', 'grad_gla_bwd_w1': 'grad_w', 'grad_gla_bwd_w2': 'grad_w', 'grad_gla_bwd_b': 'grad_w', 'grad_gla_norm_g': 'grad_w', 'grad_gla_w_out': 'grad_w', 'grad_attn_w_in': 'grad_w', 'grad_attn_q_norm_g': 'grad_w', 'grad_attn_k_norm_g': 'grad_w', 'grad_attn_w_out': 'grad_w', 'grad_final_norm_g': 'grad_w', 'delta_w_mod': 'delta_w', 'delta_b_mod': 'delta_w', 'delta_pool_w_in': 'delta_w', 'delta_pool_w_grp': 'delta_w', 'delta_pool_scale': 'delta_w', 'delta_pool_w_out': 'delta_w', 'delta_gla_w_in': 'delta_w', 'delta_gla_fwd_w1': 'delta_w', 'delta_gla_fwd_w2': 'delta_w', 'delta_gla_fwd_b': 'delta_w', 'delta_gla_bwd_w1': 'delta_w', 'delta_gla_bwd_w2': 'delta_w', 'delta_gla_bwd_b': 'delta_w', 'delta_gla_norm_g': 'delta_w', 'delta_gla_w_out': 'delta_w', 'delta_attn_w_in': 'delta_w', 'delta_attn_q_norm_g': 'delta_w', 'delta_attn_k_norm_g': 'delta_w', 'delta_attn_w_out': 'delta_w', 'delta_final_norm_g': 'delta_w', 'new_m_w_mod': 'new_m', 'new_m_b_mod': 'new_m', 'new_m_pool_w_in': 'new_m', 'new_m_pool_w_grp': 'new_m', 'new_m_pool_scale': 'new_m', 'new_m_pool_w_out': 'new_m', 'new_m_gla_w_in': 'new_m', 'new_m_gla_fwd_w1': 'new_m', 'new_m_gla_fwd_w2': 'new_m', 'new_m_gla_fwd_b': 'new_m', 'new_m_gla_bwd_w1': 'new_m', 'new_m_gla_bwd_w2': 'new_m', 'new_m_gla_bwd_b': 'new_m', 'new_m_gla_norm_g': 'new_m', 'new_m_gla_w_out': 'new_m', 'new_m_attn_w_in': 'new_m', 'new_m_attn_q_norm_g': 'new_m', 'new_m_attn_k_norm_g': 'new_m', 'new_m_attn_w_out': 'new_m', 'new_m_final_norm_g': 'new_m', 'new_v_w_mod': 'new_v', 'new_v_b_mod': 'new_v', 'new_v_pool_w_in': 'new_v', 'new_v_pool_w_grp': 'new_v', 'new_v_pool_scale': 'new_v', 'new_v_pool_w_out': 'new_v', 'new_v_gla_w_in': 'new_v', 'new_v_gla_fwd_w1': 'new_v', 'new_v_gla_fwd_w2': 'new_v', 'new_v_gla_fwd_b': 'new_v', 'new_v_gla_bwd_w1': 'new_v', 'new_v_gla_bwd_w2': 'new_v', 'new_v_gla_bwd_b': 'new_v', 'new_v_gla_norm_g': 'new_v', 'new_v_gla_w_out': 'new_v', 'new_v_attn_w_in': 'new_v', 'new_v_attn_q_norm_g': 'new_v', 'new_v_attn_k_norm_g': 'new_v', 'new_v_attn_w_out': 'new_v', 'new_v_final_norm_g': 'new_v'}


def _forward(args):
    return _fwd_reference(*[args[k] for k in FWD_PARAMS])


def _output_shape():
    def fwd():
        inp = _fwd_setup_inputs(0)
        return _fwd_reference(*[inp[k] for k in FWD_PARAMS])
    out = _jax.eval_shape(fwd)
    return out.shape, out.dtype

N_MICROBATCH = 1
ADAM_LR = 0.001
ADAM_B1 = 0.9
ADAM_B2 = 0.999
ADAM_EPS = 1e-08
ADAM_WD = 0.01
ADAM_STEP = 10
PER_EXAMPLE_BATCH_AXIS = {'x': 0, 'c': 0, 'loss_target': 0}
SHARED_INPUTS = []
_WEIGHT_DTYPES = {'w_mod': _jnp.float32, 'b_mod': _jnp.float32, 'pool_w_in': _jnp.float32, 'pool_w_grp': _jnp.float32, 'pool_scale': _jnp.float32, 'pool_w_out': _jnp.float32, 'gla_w_in': _jnp.float32, 'gla_fwd_w1': _jnp.float32, 'gla_fwd_w2': _jnp.float32, 'gla_fwd_b': _jnp.float32, 'gla_bwd_w1': _jnp.float32, 'gla_bwd_w2': _jnp.float32, 'gla_bwd_b': _jnp.float32, 'gla_norm_g': _jnp.float32, 'gla_w_out': _jnp.float32, 'attn_w_in': _jnp.float32, 'attn_q_norm_g': _jnp.float32, 'attn_k_norm_g': _jnp.float32, 'attn_w_out': _jnp.float32, 'final_norm_g': _jnp.float32}
MOMENT_SCALE = {'w_mod': 5.586404e-02, 'b_mod': 1.176057e-01, 'pool_w_in': 2.368609e-02, 'pool_w_grp': 2.335705e-02, 'pool_scale': 2.355586e-02, 'pool_w_out': 2.350734e-02, 'gla_w_in': 3.067122e-02, 'gla_fwd_w1': 3.643949e-02, 'gla_fwd_w2': 4.557190e-03, 'gla_fwd_b': 1.219710e-02, 'gla_bwd_w1': 3.744243e-02, 'gla_bwd_w2': 4.658442e-03, 'gla_bwd_b': 1.232086e-02, 'gla_norm_g': 5.056244e-02, 'gla_w_out': 2.548796e-02, 'attn_w_in': 7.457656e-03, 'attn_q_norm_g': 8.895444e-03, 'attn_k_norm_g': 9.426902e-03, 'attn_w_out': 7.941744e-03, 'final_norm_g': 3.221908e+01}


def _to_microbatches(a, axis):
    t = _jnp.moveaxis(a, axis, 0)
    t = t.reshape((N_MICROBATCH, t.shape[0] // N_MICROBATCH) + t.shape[1:])
    return _jnp.moveaxis(t, 1, axis + 1)


def setup_inputs(seed: int = 0) -> dict:
    inp = _fwd_setup_inputs(seed)
    key = _jax.random.fold_in(_jax.random.key(seed), 7919)
    shape, _ = _output_shape()
    out = dict(inp)
    out["loss_target"] = _jax.random.normal(_jax.random.fold_in(key, 0), shape, _jnp.float32)
    for i, name in enumerate(TWIN_WEIGHTS):
        w = inp[name].astype(_jnp.float32)
        if MOMENT_SCALE is None:
            s = _jnp.sqrt(_jnp.mean(_jnp.square(w)) + 1e-30)
        else:
            s = MOMENT_SCALE[name]
        km, kv = _jax.random.split(_jax.random.fold_in(key, i + 1))
        out[name] = w
        out["m_" + name] = s * _jax.random.normal(km, w.shape, _jnp.float32)
        out["v_" + name] = (s * s) * _jax.random.uniform(kv, w.shape, _jnp.float32, 0.5, 1.5)
    if N_MICROBATCH > 1:
        for name, axis in PER_EXAMPLE_BATCH_AXIS.items():
            out[name] = _to_microbatches(out[name], axis)
    return {'x': out['x'], 'c': out['c'], 'w_mod': out['w_mod'], 'b_mod': out['b_mod'], 'pool_w_in': out['pool_w_in'], 'pool_w_grp': out['pool_w_grp'], 'pool_scale': out['pool_scale'], 'pool_w_out': out['pool_w_out'], 'gla_w_in': out['gla_w_in'], 'gla_fwd_w1': out['gla_fwd_w1'], 'gla_fwd_w2': out['gla_fwd_w2'], 'gla_fwd_b': out['gla_fwd_b'], 'gla_bwd_w1': out['gla_bwd_w1'], 'gla_bwd_w2': out['gla_bwd_w2'], 'gla_bwd_b': out['gla_bwd_b'], 'gla_norm_g': out['gla_norm_g'], 'gla_w_out': out['gla_w_out'], 'attn_w_in': out['attn_w_in'], 'attn_q_norm_g': out['attn_q_norm_g'], 'attn_k_norm_g': out['attn_k_norm_g'], 'attn_w_out': out['attn_w_out'], 'final_norm_g': out['final_norm_g'], 'loss_target': out['loss_target'], 'm_w_mod': out['m_w_mod'], 'm_b_mod': out['m_b_mod'], 'm_pool_w_in': out['m_pool_w_in'], 'm_pool_w_grp': out['m_pool_w_grp'], 'm_pool_scale': out['m_pool_scale'], 'm_pool_w_out': out['m_pool_w_out'], 'm_gla_w_in': out['m_gla_w_in'], 'm_gla_fwd_w1': out['m_gla_fwd_w1'], 'm_gla_fwd_w2': out['m_gla_fwd_w2'], 'm_gla_fwd_b': out['m_gla_fwd_b'], 'm_gla_bwd_w1': out['m_gla_bwd_w1'], 'm_gla_bwd_w2': out['m_gla_bwd_w2'], 'm_gla_bwd_b': out['m_gla_bwd_b'], 'm_gla_norm_g': out['m_gla_norm_g'], 'm_gla_w_out': out['m_gla_w_out'], 'm_attn_w_in': out['m_attn_w_in'], 'm_attn_q_norm_g': out['m_attn_q_norm_g'], 'm_attn_k_norm_g': out['m_attn_k_norm_g'], 'm_attn_w_out': out['m_attn_w_out'], 'm_final_norm_g': out['m_final_norm_g'], 'v_w_mod': out['v_w_mod'], 'v_b_mod': out['v_b_mod'], 'v_pool_w_in': out['v_pool_w_in'], 'v_pool_w_grp': out['v_pool_w_grp'], 'v_pool_scale': out['v_pool_scale'], 'v_pool_w_out': out['v_pool_w_out'], 'v_gla_w_in': out['v_gla_w_in'], 'v_gla_fwd_w1': out['v_gla_fwd_w1'], 'v_gla_fwd_w2': out['v_gla_fwd_w2'], 'v_gla_fwd_b': out['v_gla_fwd_b'], 'v_gla_bwd_w1': out['v_gla_bwd_w1'], 'v_gla_bwd_w2': out['v_gla_bwd_w2'], 'v_gla_bwd_b': out['v_gla_bwd_b'], 'v_gla_norm_g': out['v_gla_norm_g'], 'v_gla_w_out': out['v_gla_w_out'], 'v_attn_w_in': out['v_attn_w_in'], 'v_attn_q_norm_g': out['v_attn_q_norm_g'], 'v_attn_k_norm_g': out['v_attn_k_norm_g'], 'v_attn_w_out': out['v_attn_w_out'], 'v_final_norm_g': out['v_final_norm_g']}


def _loss(weights, diff, rest, loss_target):
    with _jax.named_scope("forward"):
        args = {**rest, TWIN_DIFF_INPUT: diff, **{k: w.astype(_WEIGHT_DTYPES[k]) for k, w in weights.items()}}
        y = _forward(args)
    with _jax.named_scope("loss_head"):
        err = _jnp.square(y.astype(_jnp.float32) - loss_target)
        return 0.5 * _jnp.sum(_jnp.mean(err, axis=-1)) if err.ndim else 0.5 * err


def _adamw(w, g, m, v):
    m = ADAM_B1 * m + (1.0 - ADAM_B1) * g
    v = ADAM_B2 * v + (1.0 - ADAM_B2) * _jnp.square(g)
    m_hat = m / (1.0 - ADAM_B1 ** ADAM_STEP)
    v_hat = v / (1.0 - ADAM_B2 ** ADAM_STEP)
    delta = -ADAM_LR * (m_hat / (_jnp.sqrt(v_hat) + ADAM_EPS) + ADAM_WD * w)
    return delta, m, v


def reference(x, c, w_mod, b_mod, pool_w_in, pool_w_grp, pool_scale, pool_w_out, gla_w_in, gla_fwd_w1, gla_fwd_w2, gla_fwd_b, gla_bwd_w1, gla_bwd_w2, gla_bwd_b, gla_norm_g, gla_w_out, attn_w_in, attn_q_norm_g, attn_k_norm_g, attn_w_out, final_norm_g, loss_target, m_w_mod, m_b_mod, m_pool_w_in, m_pool_w_grp, m_pool_scale, m_pool_w_out, m_gla_w_in, m_gla_fwd_w1, m_gla_fwd_w2, m_gla_fwd_b, m_gla_bwd_w1, m_gla_bwd_w2, m_gla_bwd_b, m_gla_norm_g, m_gla_w_out, m_attn_w_in, m_attn_q_norm_g, m_attn_k_norm_g, m_attn_w_out, m_final_norm_g, v_w_mod, v_b_mod, v_pool_w_in, v_pool_w_grp, v_pool_scale, v_pool_w_out, v_gla_w_in, v_gla_fwd_w1, v_gla_fwd_w2, v_gla_fwd_b, v_gla_bwd_w1, v_gla_bwd_w2, v_gla_bwd_b, v_gla_norm_g, v_gla_w_out, v_attn_w_in, v_attn_q_norm_g, v_attn_k_norm_g, v_attn_w_out, v_final_norm_g):
    given = dict(x=x, c=c, w_mod=w_mod, b_mod=b_mod, pool_w_in=pool_w_in, pool_w_grp=pool_w_grp, pool_scale=pool_scale, pool_w_out=pool_w_out, gla_w_in=gla_w_in, gla_fwd_w1=gla_fwd_w1, gla_fwd_w2=gla_fwd_w2, gla_fwd_b=gla_fwd_b, gla_bwd_w1=gla_bwd_w1, gla_bwd_w2=gla_bwd_w2, gla_bwd_b=gla_bwd_b, gla_norm_g=gla_norm_g, gla_w_out=gla_w_out, attn_w_in=attn_w_in, attn_q_norm_g=attn_q_norm_g, attn_k_norm_g=attn_k_norm_g, attn_w_out=attn_w_out, final_norm_g=final_norm_g, loss_target=loss_target, m_w_mod=m_w_mod, m_b_mod=m_b_mod, m_pool_w_in=m_pool_w_in, m_pool_w_grp=m_pool_w_grp, m_pool_scale=m_pool_scale, m_pool_w_out=m_pool_w_out, m_gla_w_in=m_gla_w_in, m_gla_fwd_w1=m_gla_fwd_w1, m_gla_fwd_w2=m_gla_fwd_w2, m_gla_fwd_b=m_gla_fwd_b, m_gla_bwd_w1=m_gla_bwd_w1, m_gla_bwd_w2=m_gla_bwd_w2, m_gla_bwd_b=m_gla_bwd_b, m_gla_norm_g=m_gla_norm_g, m_gla_w_out=m_gla_w_out, m_attn_w_in=m_attn_w_in, m_attn_q_norm_g=m_attn_q_norm_g, m_attn_k_norm_g=m_attn_k_norm_g, m_attn_w_out=m_attn_w_out, m_final_norm_g=m_final_norm_g, v_w_mod=v_w_mod, v_b_mod=v_b_mod, v_pool_w_in=v_pool_w_in, v_pool_w_grp=v_pool_w_grp, v_pool_scale=v_pool_scale, v_pool_w_out=v_pool_w_out, v_gla_w_in=v_gla_w_in, v_gla_fwd_w1=v_gla_fwd_w1, v_gla_fwd_w2=v_gla_fwd_w2, v_gla_fwd_b=v_gla_fwd_b, v_gla_bwd_w1=v_gla_bwd_w1, v_gla_bwd_w2=v_gla_bwd_w2, v_gla_bwd_b=v_gla_bwd_b, v_gla_norm_g=v_gla_norm_g, v_gla_w_out=v_gla_w_out, v_attn_w_in=v_attn_w_in, v_attn_q_norm_g=v_attn_q_norm_g, v_attn_k_norm_g=v_attn_k_norm_g, v_attn_w_out=v_attn_w_out, v_final_norm_g=v_final_norm_g)
    weights = {n: given[n] for n in TWIN_WEIGHTS}
    shared = {n: given[n] for n in SHARED_INPUTS}
    per_example = {n: given[n] for n in ['x', 'c']}
    grad_fn = _jax.value_and_grad(_loss, argnums=(0, 1))

    def one_microbatch(ex, loss_target):
        ex = dict(ex)
        diff = ex.pop(TWIN_DIFF_INPUT)
        return grad_fn(weights, diff, {**shared, **ex}, loss_target)

    if N_MICROBATCH == 1:
        loss, (grad_w, grad_x) = one_microbatch(per_example, given["loss_target"])
    else:
        def body(carry, xs):
            loss_sum, grad_sum = carry
            l_k, (gw_k, gx_k) = one_microbatch(xs[0], xs[1])
            with _jax.named_scope("update"):
                return (loss_sum + l_k, _jax.tree.map(_jnp.add, grad_sum, gw_k)), gx_k

        init = (_jnp.zeros((), _jnp.float32), _jax.tree.map(_jnp.zeros_like, weights))
        (loss, grad_w), grad_x = _jax.lax.scan(body, init, (per_example, given["loss_target"]))
    with _jax.named_scope("update"):
        delta_w, new_m, new_v = {}, {}, {}
        for n in TWIN_WEIGHTS:
            delta_w[n], new_m[n], new_v[n] = _adamw(weights[n], grad_w[n], given["m_" + n], given["v_" + n])
    return (loss, grad_x, *[grad_w[n] for n in TWIN_WEIGHTS], *[delta_w[n] for n in TWIN_WEIGHTS],
            *[new_m[n] for n in TWIN_WEIGHTS], *[new_v[n] for n in TWIN_WEIGHTS])
```

```python
import functools
import math

import jax
import jax.numpy as jnp
from jax import lax
from jax.experimental import pallas as pl
from jax.experimental.pallas import tpu as pltpu

F32 = jnp.float32
BF16 = jnp.bfloat16
NDEV = 8
LANES = 128
VMEM_LIMIT = 56 * 1024 * 1024

D_MODEL = 2048
DEPTH = 4
N_MIXERS = 3
GRID_W = 64
NORM_EPS = 1e-6
POOL_WINDOWS = (2, 4, 8, 16)
GLA_HEADS = 4
GLA_LOWRANK = 16
GLA_TAU = 16.0
GLA_CHUNK = 64
ATTN_HEAD_DIM = 128
ATTN_KV_HEADS = 4
ROPE_THETA = 10000.0
ADAM_LR = 0.001
ADAM_B1 = 0.9
ADAM_B2 = 0.999
ADAM_EPS = 1e-08
ADAM_WD = 0.01
ADAM_STEP = 10

WEIGHTS = ['w_mod', 'b_mod', 'pool_w_in', 'pool_w_grp', 'pool_scale', 'pool_w_out', 'gla_w_in', 'gla_fwd_w1',
           'gla_fwd_w2', 'gla_fwd_b', 'gla_bwd_w1', 'gla_bwd_w2', 'gla_bwd_b', 'gla_norm_g', 'gla_w_out',
           'attn_w_in', 'attn_q_norm_g', 'attn_k_norm_g', 'attn_w_out', 'final_norm_g']


def _params(n_axes=0):
    sem = ("arbitrary",) * n_axes if n_axes else None
    return pltpu.CompilerParams(dimension_semantics=sem, vmem_limit_bytes=VMEM_LIMIT)


def _silu(g):
    return g * jax.nn.sigmoid(g)


def _dsilu(g):
    s = jax.nn.sigmoid(g)
    return s * (1.0 + g * (1.0 - s))


def _dot(a, b):
    return jnp.dot(a, b, preferred_element_type=F32)


def _dot_nt(a, b):
    return lax.dot_general(a, b, (((1,), (1,)), ((), ())), preferred_element_type=F32)


def _dot_tn(a, b):
    return lax.dot_general(a, b, (((0,), (0,)), ((), ())), preferred_element_type=F32)


def _tile(n, pref):
    if n <= pref:
        return n
    for step in (128, 16, 8):
        t = pref - pref % step
        while t >= step:
            if n % t == 0:
                return t
            t -= step
    raise ValueError((n, pref))


def _peer(k):
    x, y, c = lax.axis_index("x"), lax.axis_index("y"), lax.axis_index("c")
    px = 1 - x if k & 4 else x
    py = 1 - y if k & 2 else y
    pc = 1 - c if k & 1 else c
    return (px, py, pc), 4 * px + 2 * py + pc


def _my_id():
    return 4 * lax.axis_index("x") + 2 * lax.axis_index("y") + lax.axis_index("c")


def all_gather(x, name):
    R, C = x.shape

    def body(x_ref, o_ref, send_sems, recv_sems, local_sem):
        me = _my_id()
        mine = pltpu.make_async_copy(x_ref, o_ref.at[me], local_sem)
        mine.start()
        sends = []
        for k in range(1, NDEV):
            peer, _ = _peer(k)
            cp = pltpu.make_async_remote_copy(
                src_ref=x_ref, dst_ref=o_ref.at[me], send_sem=send_sems.at[k - 1], recv_sem=recv_sems.at[k - 1],
                device_id=peer, device_id_type=pl.DeviceIdType.MESH)
            cp.start()
            sends.append(cp)
        for k in range(1, NDEV):
            peer, pid = _peer(k)
            pltpu.make_async_remote_copy(
                src_ref=x_ref, dst_ref=o_ref.at[pid], send_sem=send_sems.at[k - 1], recv_sem=recv_sems.at[k - 1],
                device_id=peer, device_id_type=pl.DeviceIdType.MESH).wait_recv()
        for cp in sends:
            cp.wait_send()
        mine.wait()

    return pl.pallas_call(
        body, name=name,
        out_shape=jax.ShapeDtypeStruct((NDEV, R, C), x.dtype),
        in_specs=[pl.BlockSpec(memory_space=pl.ANY)],
        out_specs=pl.BlockSpec(memory_space=pl.ANY),
        scratch_shapes=[pltpu.SemaphoreType.DMA((NDEV - 1,)), pltpu.SemaphoreType.DMA((NDEV - 1,)),
                        pltpu.SemaphoreType.DMA],
    )(x)


def all_to_all(x, name):
    _, R, C = x.shape

    def body(x_ref, o_ref, send_sems, recv_sems, local_sem):
        me = _my_id()
        mine = pltpu.make_async_copy(x_ref.at[me], o_ref.at[me], local_sem)
        mine.start()
        sends = []
        for k in range(1, NDEV):
            peer, pid = _peer(k)
            cp = pltpu.make_async_remote_copy(
                src_ref=x_ref.at[pid], dst_ref=o_ref.at[me], send_sem=send_sems.at[k - 1],
                recv_sem=recv_sems.at[k - 1], device_id=peer, device_id_type=pl.DeviceIdType.MESH)
            cp.start()
            sends.append(cp)
        for k in range(1, NDEV):
            peer, pid = _peer(k)
            pltpu.make_async_remote_copy(
                src_ref=x_ref.at[pid], dst_ref=o_ref.at[pid], send_sem=send_sems.at[k - 1],
                recv_sem=recv_sems.at[k - 1], device_id=peer, device_id_type=pl.DeviceIdType.MESH).wait_recv()
        for cp in sends:
            cp.wait_send()
        mine.wait()

    return pl.pallas_call(
        body, name=name,
        out_shape=jax.ShapeDtypeStruct((NDEV, R, C), x.dtype),
        in_specs=[pl.BlockSpec(memory_space=pl.ANY)],
        out_specs=pl.BlockSpec(memory_space=pl.ANY),
        scratch_shapes=[pltpu.SemaphoreType.DMA((NDEV - 1,)), pltpu.SemaphoreType.DMA((NDEV - 1,)),
                        pltpu.SemaphoreType.DMA],
    )(x)


def _pack(arrs, dtype, lead=None):
    unit = 16 * LANES
    if lead is None:
        flat = [a.astype(dtype).reshape(-1) for a in arrs]
        n = sum(f.shape[0] for f in flat)
        pad = (-n) % unit
        if pad:
            flat.append(jnp.zeros((pad,), dtype))
        return jnp.concatenate(flat).reshape(-1, LANES)
    flat = [a.astype(dtype).reshape(lead, -1) for a in arrs]
    n = sum(f.shape[1] for f in flat)
    pad = (-n) % unit
    if pad:
        flat.append(jnp.zeros((lead, pad), dtype))
    return jnp.concatenate(flat, axis=1).reshape(lead, -1, LANES)


def _unpack(buf, shapes, lead=None):
    out = []
    off = 0
    if lead is None:
        flat = buf.reshape(-1)
        for s in shapes:
            n = math.prod(s)
            out.append(flat[off:off + n].reshape(s))
            off += n
        return out
    flat = buf.reshape(lead, -1)
    for s in shapes:
        n = math.prod(s)
        out.append(flat[:, off:off + n].reshape((lead,) + tuple(s)))
        off += n
    return out


def mm(a, b, *, name, ta=False, tb=False, b_split=False, out_split=0, out_dtype=BF16, tm=1024, tn=1024, tk=512,
       bias=None, add=None, a_silu=False):
    K, M = a.shape if ta else a.shape[::-1]
    if b_split:
        S, d1, n = b.shape
        if tb:
            N, Kb = d1, S * n
        else:
            Kb, N = d1, S * n
    else:
        Kb, N = b.shape[::-1] if tb else b.shape
    assert Kb == K, (a.shape, b.shape, ta, tb, b_split)
    tm = _tile(M, tm)
    if b_split and not tb:
        tn = _tile(n, tn)
    elif out_split:
        tn = _tile(out_split, tn)
    else:
        tn = _tile(N, tn)
    if b_split and tb:
        tk = _tile(n, 1024)
    else:
        tk = _tile(K, tk)
    nk = K // tk

    a_spec = pl.BlockSpec((tk, tm), lambda i, j, k: (k, i)) if ta else pl.BlockSpec((tm, tk), lambda i, j, k: (i, k))
    if b_split and not tb:
        per = n // tn
        b_spec = pl.BlockSpec((None, tk, tn), lambda i, j, k: (j // per, k, j % per))
    elif b_split and tb:
        per = n // tk
        b_spec = pl.BlockSpec((None, tn, tk), lambda i, j, k: (k // per, j, k % per))
    elif tb:
        b_spec = pl.BlockSpec((tn, tk), lambda i, j, k: (j, k))
    else:
        b_spec = pl.BlockSpec((tk, tn), lambda i, j, k: (k, j))
    if out_split:
        per_o = out_split // tn
        o_spec = pl.BlockSpec((None, tm, tn), lambda i, j, k: (j // per_o, i, j % per_o))
        o_shape = jax.ShapeDtypeStruct((N // out_split, M, out_split), out_dtype)
    else:
        o_spec = pl.BlockSpec((tm, tn), lambda i, j, k: (i, j))
        o_shape = jax.ShapeDtypeStruct((M, N), out_dtype)
    ins = [a, b]
    in_specs = [a_spec, b_spec]
    if bias is not None:
        ins.append(bias)
        in_specs.append(pl.BlockSpec((1, tn), lambda i, j, k: (0, j)))
    if add is not None:
        ins.append(add)
        in_specs.append(pl.BlockSpec((tm, tn), lambda i, j, k: (i, j)))
    has_bias, has_add = bias is not None, add is not None

    def body(*refs):
        a_ref, b_ref = refs[0], refs[1]
        pos = 2
        bias_ref = add_ref = None
        if has_bias:
            bias_ref = refs[pos]
            pos += 1
        if has_add:
            add_ref = refs[pos]
            pos += 1
        o_ref, acc_ref = refs[pos], refs[pos + 1]
        k = pl.program_id(2)

        @pl.when(k == 0)
        def _():
            acc_ref[...] = jnp.zeros_like(acc_ref)

        av = a_ref[...]
        if a_silu:
            av = _silu(av.astype(F32))
        av = av.astype(BF16)
        bv = b_ref[...].astype(BF16)
        dn = (((0 if ta else 1,), (1 if tb else 0,)), ((), ()))
        acc_ref[...] += lax.dot_general(av, bv, dn, preferred_element_type=F32)

        @pl.when(k == nk - 1)
        def _():
            r = acc_ref[...]
            if has_bias:
                r = r + bias_ref[...]
            if has_add:
                r = r + add_ref[...].astype(F32)
            o_ref[...] = r.astype(out_dtype)

    return pl.pallas_call(
        body, name=name, grid=(M // tm, N // tn, nk), in_specs=in_specs, out_specs=o_spec, out_shape=o_shape,
        scratch_shapes=[pltpu.VMEM((tm, tn), F32)], compiler_params=_params(3))(*ins)


def gmm_nn(a, w, *, name, tb=False, out_dtype=BF16, tm=512):
    T = a.shape[0]
    G = w.shape[0]
    Kg = a.shape[1] // G
    Ng = w.shape[1] if tb else w.shape[2]
    tm = _tile(T, tm)

    def body(a_ref, w_ref, o_ref):
        wv = w_ref[...].astype(BF16)
        av = a_ref[...].astype(BF16)
        r = _dot_nt(av, wv) if tb else _dot(av, wv)
        o_ref[...] = r.astype(out_dtype)

    return pl.pallas_call(
        body, name=name, grid=(G, T // tm),
        in_specs=[pl.BlockSpec((tm, Kg), lambda g, i: (i, g)),
                  pl.BlockSpec((None,) + tuple(w.shape[1:]), lambda g, i: (g, 0, 0))],
        out_specs=pl.BlockSpec((tm, Ng), lambda g, i: (i, g)),
        out_shape=jax.ShapeDtypeStruct((T, G * Ng), out_dtype), compiler_params=_params(2))(a, w)


def gmm_tn(a, b, G, *, name, tk=512):
    T = a.shape[0]
    Kg = a.shape[1] // G
    Ng = b.shape[1] // G
    tk = _tile(T, tk)

    def body(a_ref, b_ref, o_ref):
        @pl.when(pl.program_id(1) == 0)
        def _():
            o_ref[...] = jnp.zeros_like(o_ref)

        o_ref[...] += _dot_tn(a_ref[...].astype(BF16), b_ref[...].astype(BF16))

    return pl.pallas_call(
        body, name=name, grid=(G, T // tk),
        in_specs=[pl.BlockSpec((tk, Kg), lambda g, i: (i, g)), pl.BlockSpec((tk, Ng), lambda g, i: (i, g))],
        out_specs=pl.BlockSpec((None, Kg, Ng), lambda g, i: (g, 0, 0)),
        out_shape=jax.ShapeDtypeStruct((G, Kg, Ng), F32), compiler_params=_params(2))(a, b)


def _row(tr, w, cb=0):
    return pl.BlockSpec((tr, w), lambda i: (i, cb))


def _bc(w):
    return pl.BlockSpec((1, w), lambda i: (0, 0))


def prenorm(x, scale, shift, *, name):
    T, D = x.shape
    tr = _tile(T, 256)

    def body(x_ref, sc_ref, sh_ref, h_ref):
        xv = x_ref[...]
        rstd = lax.rsqrt(jnp.mean(xv * xv, axis=-1, keepdims=True) + NORM_EPS)
        h_ref[...] = ((xv * rstd) * (1.0 + sc_ref[...]) + sh_ref[...]).astype(BF16)

    return pl.pallas_call(
        body, name=name, grid=(T // tr,), in_specs=[_row(tr, D), _bc(D), _bc(D)], out_specs=_row(tr, D),
        out_shape=jax.ShapeDtypeStruct((T, D), BF16), compiler_params=_params(1))(x, scale, shift)


def prenorm_bwd(x, dh, dxn, scale, *, name):
    T, D = x.shape
    tr = _tile(T, 256)

    def body(x_ref, dh_ref, dxn_ref, sc_ref, dx_ref, dsc_ref, dsh_ref):
        @pl.when(pl.program_id(0) == 0)
        def _():
            dsc_ref[...] = jnp.zeros_like(dsc_ref)
            dsh_ref[...] = jnp.zeros_like(dsh_ref)

        xv = x_ref[...]
        dhv = dh_ref[...].astype(F32)
        rstd = lax.rsqrt(jnp.mean(xv * xv, axis=-1, keepdims=True) + NORM_EPS)
        r = xv * rstd
        dsc_ref[...] += jnp.sum(dhv * r, axis=0, keepdims=True)
        dsh_ref[...] += jnp.sum(dhv, axis=0, keepdims=True)
        dr = dhv * (1.0 + sc_ref[...])
        dx_ref[...] = dxn_ref[...] + rstd * (dr - r * jnp.mean(dr * r, axis=-1, keepdims=True))

    return pl.pallas_call(
        body, name=name, grid=(T // tr,), in_specs=[_row(tr, D), _row(tr, D), _row(tr, D), _bc(D)],
        out_specs=[_row(tr, D), _bc(D), _bc(D)],
        out_shape=[jax.ShapeDtypeStruct((T, D), F32), jax.ShapeDtypeStruct((1, D), F32),
                   jax.ShapeDtypeStruct((1, D), F32)], compiler_params=_params(1))(x, dh, dxn, scale)


def resid(x, y, gate, *, name):
    T, D = x.shape
    tr = _tile(T, 256)

    def body(x_ref, y_ref, g_ref, o_ref):
        o_ref[...] = x_ref[...] + g_ref[...] * y_ref[...].astype(F32)

    return pl.pallas_call(
        body, name=name, grid=(T // tr,), in_specs=[_row(tr, D), _row(tr, D), _bc(D)], out_specs=_row(tr, D),
        out_shape=jax.ShapeDtypeStruct((T, D), F32), compiler_params=_params(1))(x, y, gate)


def resid_bwd(dxn, y, gate, *, name):
    T, D = dxn.shape
    tr = _tile(T, 256)

    def body(d_ref, y_ref, g_ref, dy_ref, dg_ref):
        @pl.when(pl.program_id(0) == 0)
        def _():
            dg_ref[...] = jnp.zeros_like(dg_ref)

        d = d_ref[...]
        dy_ref[...] = (d * g_ref[...]).astype(BF16)
        dg_ref[...] += jnp.sum(d * y_ref[...].astype(F32), axis=0, keepdims=True)

    return pl.pallas_call(
        body, name=name, grid=(T // tr,), in_specs=[_row(tr, D), _row(tr, D), _bc(D)],
        out_specs=[_row(tr, D), _bc(D)],
        out_shape=[jax.ShapeDtypeStruct((T, D), BF16), jax.ShapeDtypeStruct((1, D), F32)],
        compiler_params=_params(1))(dxn, y, gate)


def loss_head(x, g, target, *, name):
    T, D = x.shape
    tr = _tile(T, 256)

    def body(x_ref, g_ref, t_ref, loss_ref, dx_ref, dg_ref):
        @pl.when(pl.program_id(0) == 0)
        def _():
            loss_ref[...] = jnp.zeros_like(loss_ref)
            dg_ref[...] = jnp.zeros_like(dg_ref)

        xv = x_ref[...]
        gv = g_ref[...]
        rstd = lax.rsqrt(jnp.mean(xv * xv, axis=-1, keepdims=True) + NORM_EPS)
        r = xv * rstd
        e = r * gv - t_ref[...]
        loss_ref[...] += 0.5 * jnp.sum(jnp.mean(e * e, axis=-1, keepdims=True), axis=0, keepdims=True)
        dout = e * (1.0 / D)
        dg_ref[...] += jnp.sum(dout * r, axis=0, keepdims=True)
        dr = dout * gv
        dx_ref[...] = rstd * (dr - r * jnp.mean(dr * r, axis=-1, keepdims=True))

    return pl.pallas_call(
        body, name=name, grid=(T // tr,), in_specs=[_row(tr, D), _bc(D), _row(tr, D)],
        out_specs=[pl.BlockSpec((1, 1), lambda i: (0, 0)), _row(tr, D), _bc(D)],
        out_shape=[jax.ShapeDtypeStruct((1, 1), F32), jax.ShapeDtypeStruct((T, D), F32),
                   jax.ShapeDtypeStruct((1, D), F32)], compiler_params=_params(1))(x, g, target)


def _col_tile(W, off):
    cw = math.gcd(W, off) if off else W
    cw = math.gcd(cw, 1024) if cw > 1024 else cw
    return cw


def gated(a, proj, g_off, scale, *, name):
    T, W = a.shape
    cw = _col_tile(W, g_off)
    gb = g_off // cw
    tr = _tile(T, 256)

    def body(a_ref, g_ref, s_ref, o_ref):
        o_ref[...] = (a_ref[...].astype(F32) * s_ref[...] * _silu(g_ref[...].astype(F32))).astype(BF16)

    return pl.pallas_call(
        body, name=name, grid=(W // cw, T // tr),
        in_specs=[pl.BlockSpec((tr, cw), lambda j, i: (i, j)), pl.BlockSpec((tr, cw), lambda j, i: (i, gb + j)),
                  pl.BlockSpec((1, cw), lambda j, i: (0, j))],
        out_specs=pl.BlockSpec((tr, cw), lambda j, i: (i, j)),
        out_shape=jax.ShapeDtypeStruct((T, W), BF16), compiler_params=_params(2))(a, proj, scale)


def gated_bwd(dy2, a, proj, g_off, scale, *, name):
    T, W = a.shape
    cw = _col_tile(W, g_off)
    gb = g_off // cw
    tr = _tile(T, 256)

    def body(d_ref, a_ref, g_ref, s_ref, da_ref, dg_ref, ds_ref):
        @pl.when(pl.program_id(1) == 0)
        def _():
            ds_ref[...] = jnp.zeros_like(ds_ref)

        d = d_ref[...].astype(F32)
        av = a_ref[...].astype(F32)
        gv = g_ref[...].astype(F32)
        sv = s_ref[...]
        dsg = d * _silu(gv)
        da_ref[...] = (dsg * sv).astype(BF16)
        dg_ref[...] = (d * av * sv * _dsilu(gv)).astype(BF16)
        ds_ref[...] += jnp.sum(dsg * av, axis=0, keepdims=True)

    blk = pl.BlockSpec((tr, cw), lambda j, i: (i, j))
    return pl.pallas_call(
        body, name=name, grid=(W // cw, T // tr),
        in_specs=[blk, blk, pl.BlockSpec((tr, cw), lambda j, i: (i, gb + j)),
                  pl.BlockSpec((1, cw), lambda j, i: (0, j))],
        out_specs=[blk, blk, pl.BlockSpec((1, cw), lambda j, i: (0, j))],
        out_shape=[jax.ShapeDtypeStruct((T, W), BF16), jax.ShapeDtypeStruct((T, W), BF16),
                   jax.ShapeDtypeStruct((1, W), F32)], compiler_params=_params(2))(dy2, a, proj, scale)


HALO = 16


def band(u, W, *, transpose, name):
    T = u.shape[0]
    R = _tile(T, 256)
    nb = T // R
    G = len(POOL_WINDOWS)
    Cg = W // G
    hal = min(HALO, R)

    def body(p_ref, c_ref, n_ref, o_ref):
        i = pl.program_id(0)
        out_pos = lax.broadcasted_iota(jnp.int32, (R, 1), 0) + i * R
        parts = ((p_ref, i * R - hal, hal, R - hal), (c_ref, i * R, R, 0), (n_ref, (i + 1) * R, hal, 0))
        for gi, w in enumerate(POOL_WINDOWS):
            half = w // 2
            cols = slice(gi * Cg, (gi + 1) * Cg)
            acc = jnp.zeros((R, Cg), F32)
            for ref, base, n, r0 in parts:
                src_pos = lax.broadcasted_iota(jnp.int32, (1, n), 1) + base
                valid = (src_pos >= 0) & (src_pos < T)
                src = ref[r0:r0 + n, cols]
                if not transpose:
                    m = (src_pos >= out_pos - half) & (src_pos < out_pos + half) & valid
                else:
                    m = (out_pos >= src_pos - half) & (out_pos < src_pos + half) & valid
                    sp = lax.broadcasted_iota(jnp.int32, (n, 1), 0) + base
                    cnt = jnp.minimum(sp + half, T) - jnp.maximum(sp - half, 0)
                    src = (src.astype(F32) / jnp.maximum(cnt, 1).astype(F32)).astype(BF16)
                acc = acc + _dot(m.astype(BF16), src.astype(BF16))
            if not transpose:
                cnt = jnp.minimum(out_pos + half, T) - jnp.maximum(out_pos - half, 0)
                acc = acc / cnt.astype(F32)
            o_ref[:, cols] = (acc - c_ref[:, cols].astype(F32)).astype(BF16)

    return pl.pallas_call(
        body, name=name, grid=(nb,),
        in_specs=[pl.BlockSpec((R, W), lambda i: (jnp.maximum(i - 1, 0), 0)), pl.BlockSpec((R, W), lambda i: (i, 0)),
                  pl.BlockSpec((R, W), lambda i: (jnp.minimum(i + 1, nb - 1), 0))],
        out_specs=pl.BlockSpec((R, W), lambda i: (i, 0)),
        out_shape=jax.ShapeDtypeStruct((T, W), BF16), compiler_params=_params(1))(u, u, u)


def _log_sigmoid(x):
    return jnp.minimum(x, 0.0) - jnp.log1p(jnp.exp(-jnp.abs(x)))


def _split3(x):
    hi = x.astype(BF16)
    r1 = x - hi.astype(F32)
    md = r1.astype(BF16)
    lo = (r1 - md.astype(F32)).astype(BF16)
    return hi, md, lo


def _tri_sum(tri, x):
    hi, md, lo = _split3(x)
    return _dot(tri, hi) + _dot(tri, md) + _dot(tri, lo)


def _gla_masks(C, reverse):
    row = lax.broadcasted_iota(jnp.int32, (C, C), 0)
    col = lax.broadcasted_iota(jnp.int32, (C, C), 1)
    if not reverse:
        return (col <= row), (col >= row), (col <= row)
    return (col >= row), (col <= row), (col > row)


def _gla_dims(proj):
    VW = proj.shape[1] // 3
    KW = VW // 2
    return KW, VW, KW // GLA_HEADS, VW // GLA_HEADS


def gla_fwd(proj, z, bias, *, reverse, name):
    T = proj.shape[0]
    KW, VW, DK, DV = _gla_dims(proj)
    H = GLA_HEADS
    C = _tile(T, GLA_CHUNK)
    NC = T // C
    cidx = (lambda i: NC - 1 - i) if reverse else (lambda i: i)
    last = 0 if reverse else C - 1

    def body(q_ref, k_ref, v_ref, z_ref, b_ref, o_ref, s_ref, S_scr):
        @pl.when(pl.program_id(0) == 0)
        def _():
            S_scr[...] = jnp.zeros_like(S_scr)

        cum, _, amask = _gla_masks(C, reverse)
        la = _log_sigmoid(z_ref[...] + b_ref[...]) * (1.0 / GLA_TAU)
        b = _tri_sum(cum.astype(BF16), la)
        for h in range(H):
            ks = slice(h * DK, (h + 1) * DK)
            vs = slice(h * DV, (h + 1) * DV)
            bh = b[:, ks]
            mid = bh[C // 2:C // 2 + 1, :]
            bl = bh[last:last + 1, :]
            q = q_ref[:, ks].astype(F32) * (DK ** -0.5)
            k = k_ref[:, ks].astype(F32)
            v = v_ref[:, vs]
            qe = (q * jnp.exp(bh)).astype(BF16)
            qt = (q * jnp.exp(bh - mid)).astype(BF16)
            kt = (k * jnp.exp(mid - bh)).astype(BF16)
            kd = (k * jnp.exp(bl - bh)).astype(BF16)
            St = S_scr[h]
            Sb = St.astype(BF16)
            s_ref[0, h] = Sb
            A = jnp.where(amask, _dot_nt(qt, kt), 0.0).astype(BF16)
            o_ref[:, vs] = _dot_nt(qe, Sb) + _dot(A, v)
            S_scr[h] = St * jnp.exp(bl) + _dot_tn(v, kd)

    return pl.pallas_call(
        body, name=name, grid=(NC,),
        in_specs=[pl.BlockSpec((C, KW), lambda i: (cidx(i), 0)), pl.BlockSpec((C, KW), lambda i: (cidx(i), 1)),
                  pl.BlockSpec((C, VW), lambda i: (cidx(i), 1)), pl.BlockSpec((C, KW), lambda i: (cidx(i), 0)),
                  pl.BlockSpec((1, KW), lambda i: (0, 0))],
        out_specs=[pl.BlockSpec((C, VW), lambda i: (cidx(i), 0)),
                   pl.BlockSpec((1, H, DV, DK), lambda i: (cidx(i), 0, 0, 0))],
        out_shape=[jax.ShapeDtypeStruct((T, VW), F32), jax.ShapeDtypeStruct((NC, H, DV, DK), BF16)],
        scratch_shapes=[pltpu.VMEM((H, DV, DK), F32)], compiler_params=_params(1))(proj, proj, proj, z, bias)


def gla_bwd(proj, z, bias, states, do, prev, *, reverse, name):
    T = proj.shape[0]
    KW, VW, DK, DV = _gla_dims(proj)
    H = GLA_HEADS
    C = _tile(T, GLA_CHUNK)
    NC = T // C
    cidx = (lambda i: i) if reverse else (lambda i: NC - 1 - i)
    last = 0 if reverse else C - 1
    has_prev = prev is not None
    odt = BF16 if has_prev else F32

    def body(*refs):
        q_ref, k_ref, v_ref, z_ref, b_ref, s_ref, do_ref = refs[:7]
        pos = 7
        if has_prev:
            pq_ref, pk_ref, pv_ref = refs[7:10]
            pos = 10
        dq_ref, dk_ref, dv_ref, dz_ref, db_ref, dS_scr = refs[pos:pos + 6]

        @pl.when(pl.program_id(0) == 0)
        def _():
            dS_scr[...] = jnp.zeros_like(dS_scr)
            db_ref[...] = jnp.zeros_like(db_ref)

        cum, cum_t, amask = _gla_masks(C, reverse)
        xg = z_ref[...] + b_ref[...]
        la = _log_sigmoid(xg) * (1.0 / GLA_TAU)
        b = _tri_sum(cum.astype(BF16), la)
        cum_t_bf = cum_t.astype(BF16)
        for h in range(H):
            ks = slice(h * DK, (h + 1) * DK)
            vs = slice(h * DV, (h + 1) * DV)
            bh = b[:, ks]
            mid = bh[C // 2:C // 2 + 1, :]
            bl = bh[last:last + 1, :]
            q = q_ref[:, ks].astype(F32) * (DK ** -0.5)
            k = k_ref[:, ks].astype(F32)
            v = v_ref[:, vs]
            dov = do_ref[:, vs]
            e_b = jnp.exp(bh)
            e_up = jnp.exp(bh - mid)
            e_dn = jnp.exp(mid - bh)
            e_l = jnp.exp(bl - bh)
            e_bl = jnp.exp(bl)
            qe = (q * e_b).astype(BF16)
            qt = (q * e_up).astype(BF16)
            kt = (k * e_dn).astype(BF16)
            kd = (k * e_l).astype(BF16)
            Sb = s_ref[0, h]
            dSt = dS_scr[h]
            dSb = dSt.astype(BF16)
            A = jnp.where(amask, _dot_nt(qt, kt), 0.0).astype(BF16)
            dA = jnp.where(amask, _dot_nt(dov, v), 0.0).astype(BF16)
            dv = _dot_tn(A, dov) + _dot_nt(kd, dSb)
            dq = _dot(dA, kt) * e_up + _dot(dov, Sb) * e_b
            dk_state = _dot(v, dSb) * e_l
            dk = _dot_tn(dA, qt) * e_dn + dk_state
            d_b = dq * q - dk * k
            d_bl = (jnp.sum(dk_state * k, axis=0, keepdims=True)
                    + e_bl * jnp.sum(dSt * Sb.astype(F32), axis=0, keepdims=True))
            d_la = _tri_sum(cum_t_bf, d_b) + d_bl
            dz = d_la * (1.0 / GLA_TAU) * jax.nn.sigmoid(-xg[:, ks])
            dq = dq * (DK ** -0.5)
            if has_prev:
                dq = dq + pq_ref[:, ks]
                dk = dk + pk_ref[:, ks]
                dv = dv + pv_ref[:, vs]
            dq_ref[:, ks] = dq.astype(odt)
            dk_ref[:, ks] = dk.astype(odt)
            dv_ref[:, vs] = dv.astype(odt)
            dz_ref[:, ks] = dz.astype(BF16)
            db_ref[:, ks] += jnp.sum(dz, axis=0, keepdims=True)
            dS_scr[h] = dSt * e_bl + _dot_tn(dov, qe)

    kspec = lambda cb: pl.BlockSpec((C, KW), lambda i: (cidx(i), cb))
    vspec = lambda cb: pl.BlockSpec((C, VW), lambda i: (cidx(i), cb))
    ins = [proj, proj, proj, z, bias, states, do]
    in_specs = [kspec(0), kspec(1), vspec(1), kspec(0), pl.BlockSpec((1, KW), lambda i: (0, 0)),
                pl.BlockSpec((1, H, DV, DK), lambda i: (cidx(i), 0, 0, 0)), vspec(0)]
    if has_prev:
        ins += list(prev)
        in_specs += [kspec(0), kspec(0), vspec(0)]
    return pl.pallas_call(
        body, name=name, grid=(NC,), in_specs=in_specs,
        out_specs=[kspec(0), kspec(0), vspec(0), kspec(0), pl.BlockSpec((1, KW), lambda i: (0, 0))],
        out_shape=[jax.ShapeDtypeStruct((T, KW), odt), jax.ShapeDtypeStruct((T, KW), odt),
                   jax.ShapeDtypeStruct((T, VW), odt), jax.ShapeDtypeStruct((T, KW), BF16),
                   jax.ShapeDtypeStruct((1, KW), F32)],
        scratch_shapes=[pltpu.VMEM((H, DV, DK), F32)], compiler_params=_params(1))(*ins)


def gla_out(o_f, o_b, proj, g_off, norm_g, *, name):
    T, VW = o_f.shape
    H = GLA_HEADS
    DV = VW // H
    gb = g_off // VW
    tr = _tile(T, 256)

    def body(f_ref, b_ref, g_ref, n_ref, y_ref):
        for h in range(H):
            vs = slice(h * DV, (h + 1) * DV)
            o = f_ref[:, vs] + b_ref[:, vs]
            rstd = lax.rsqrt(jnp.mean(o * o, axis=-1, keepdims=True) + NORM_EPS)
            y_ref[:, vs] = (o * rstd * n_ref[...] * _silu(g_ref[:, vs].astype(F32))).astype(BF16)

    return pl.pallas_call(
        body, name=name, grid=(T // tr,),
        in_specs=[_row(tr, VW), _row(tr, VW), _row(tr, VW, gb), _bc(DV)], out_specs=_row(tr, VW),
        out_shape=jax.ShapeDtypeStruct((T, VW), BF16), compiler_params=_params(1))(o_f, o_b, proj, norm_g)


def gla_out_bwd(dy2, o_f, o_b, proj, g_off, norm_g, *, name):
    T, VW = o_f.shape
    H = GLA_HEADS
    DV = VW // H
    gb = g_off // VW
    tr = _tile(T, 256)

    def body(d_ref, f_ref, b_ref, g_ref, n_ref, do_ref, dg_ref, dn_ref):
        @pl.when(pl.program_id(0) == 0)
        def _():
            dn_ref[...] = jnp.zeros_like(dn_ref)

        nv = n_ref[...]
        for h in range(H):
            vs = slice(h * DV, (h + 1) * DV)
            o = f_ref[:, vs] + b_ref[:, vs]
            rstd = lax.rsqrt(jnp.mean(o * o, axis=-1, keepdims=True) + NORM_EPS)
            r = o * rstd
            gv = g_ref[:, vs].astype(F32)
            d = d_ref[:, vs].astype(F32)
            dg_ref[:, vs] = (d * r * nv * _dsilu(gv)).astype(BF16)
            dn_o = d * _silu(gv)
            dn_ref[...] += jnp.sum(dn_o * r, axis=0, keepdims=True)
            dr = dn_o * nv
            do_ref[:, vs] = (rstd * (dr - r * jnp.mean(dr * r, axis=-1, keepdims=True))).astype(BF16)

    return pl.pallas_call(
        body, name=name, grid=(T // tr,),
        in_specs=[_row(tr, VW), _row(tr, VW), _row(tr, VW), _row(tr, VW, gb), _bc(DV)],
        out_specs=[_row(tr, VW), _row(tr, VW), _bc(DV)],
        out_shape=[jax.ShapeDtypeStruct((T, VW), BF16), jax.ShapeDtypeStruct((T, VW), BF16),
                   jax.ShapeDtypeStruct((1, DV), F32)], compiler_params=_params(1))(dy2, o_f, o_b, proj, norm_g)


def _rope_tables(T):
    hd = ATTN_HEAD_DIM
    axis_dim = hd // 2
    rows = T // GRID_W
    t = jnp.arange(T)
    row = (t // GRID_W - rows // 2).astype(F32)
    col = (t % GRID_W - GRID_W // 2).astype(F32)
    inv = ROPE_THETA ** (-jnp.arange(0, axis_dim, 2, dtype=F32) / axis_dim)
    ang = jnp.concatenate([row[:, None] * inv, col[:, None] * inv], axis=-1)
    cos = jnp.repeat(jnp.cos(ang), 2, axis=-1)
    sin = jnp.repeat(jnp.sin(ang), 2, axis=-1)
    sign = jnp.where(jnp.arange(hd) % 2 == 0, -1.0, 1.0).astype(F32)
    return cos, sin * sign


def _pair_swap(x):
    n = x.shape[-1]
    lane = lax.broadcasted_iota(jnp.int32, x.shape, x.ndim - 1)
    return jnp.where(lane % 2 == 0, pltpu.roll(x, n - 1, x.ndim - 1), pltpu.roll(x, 1, x.ndim - 1))


def _attn_dims(proj):
    hd = ATTN_HEAD_DIM
    kvw = ATTN_KV_HEADS * hd
    qw = (proj.shape[1] - 2 * kvw) // 2
    return qw, kvw, qw // hd, (qw // hd) // ATTN_KV_HEADS


def attn_prep(proj, qg, kg, cos, sin, *, name):
    T = proj.shape[0]
    QW, KVW, NH, G = _attn_dims(proj)
    hd = ATTN_HEAD_DIM
    assert QW % KVW == 0
    tr = _tile(T, 256)

    def body(q_ref, k_ref, qg_ref, kg_ref, c_ref, s_ref, qo_ref, ko_ref):
        cv, sv = c_ref[...], s_ref[...]

        def one(x, gain, mult):
            rstd = lax.rsqrt(jnp.mean(x * x, axis=-1, keepdims=True) + NORM_EPS)
            xs = x * rstd * gain
            return (xs * cv + _pair_swap(xs) * sv) * mult

        for h in range(NH):
            hs = slice(h * hd, (h + 1) * hd)
            qo_ref[:, hs] = one(q_ref[:, hs].astype(F32), qg_ref[...], hd ** -0.5).astype(BF16)
        for h in range(ATTN_KV_HEADS):
            hs = slice(h * hd, (h + 1) * hd)
            ko_ref[:, hs] = one(k_ref[:, hs].astype(F32), kg_ref[...], 1.0).astype(BF16)

    return pl.pallas_call(
        body, name=name, grid=(T // tr,),
        in_specs=[_row(tr, QW), _row(tr, KVW, QW // KVW), _bc(hd), _bc(hd), _row(tr, hd), _row(tr, hd)],
        out_specs=[_row(tr, QW), _row(tr, KVW)],
        out_shape=[jax.ShapeDtypeStruct((T, QW), BF16), jax.ShapeDtypeStruct((T, KVW), BF16)],
        compiler_params=_params(1))(proj, proj, qg, kg, cos, sin)


def attn_prep_bwd(dqr, dkr, proj, qg, kg, cos, sin, *, name):
    T = proj.shape[0]
    QW, KVW, NH, G = _attn_dims(proj)
    hd = ATTN_HEAD_DIM
    tr = _tile(T, 256)

    def body(dq_ref, dk_ref, q_ref, k_ref, qg_ref, kg_ref, c_ref, s_ref, oq_ref, ok_ref, dqg_ref, dkg_ref):
        @pl.when(pl.program_id(0) == 0)
        def _():
            dqg_ref[...] = jnp.zeros_like(dqg_ref)
            dkg_ref[...] = jnp.zeros_like(dkg_ref)

        cv, sv = c_ref[...], s_ref[...]

        def one(d, x, gain, mult):
            d = d * mult
            dxs = d * cv - _pair_swap(d) * sv
            rstd = lax.rsqrt(jnp.mean(x * x, axis=-1, keepdims=True) + NORM_EPS)
            xn = x * rstd
            dgain = jnp.sum(dxs * xn, axis=0, keepdims=True)
            dxn = dxs * gain
            return rstd * (dxn - xn * jnp.mean(dxn * xn, axis=-1, keepdims=True)), dgain

        for h in range(NH):
            hs = slice(h * hd, (h + 1) * hd)
            dx, dgain = one(dq_ref[:, hs].astype(F32), q_ref[:, hs].astype(F32), qg_ref[...], hd ** -0.5)
            oq_ref[:, hs] = dx.astype(BF16)
            dqg_ref[...] += dgain
        for h in range(ATTN_KV_HEADS):
            hs = slice(h * hd, (h + 1) * hd)
            dx, dgain = one(dk_ref[:, hs].astype(F32), k_ref[:, hs].astype(F32), kg_ref[...], 1.0)
            ok_ref[:, hs] = dx.astype(BF16)
            dkg_ref[...] += dgain

    return pl.pallas_call(
        body, name=name, grid=(T // tr,),
        in_specs=[_row(tr, QW), _row(tr, KVW), _row(tr, QW), _row(tr, KVW, QW // KVW), _bc(hd), _bc(hd),
                  _row(tr, hd), _row(tr, hd)],
        out_specs=[_row(tr, QW), _row(tr, KVW), _bc(hd), _bc(hd)],
        out_shape=[jax.ShapeDtypeStruct((T, QW), BF16), jax.ShapeDtypeStruct((T, KVW), BF16),
                   jax.ShapeDtypeStruct((1, hd), F32), jax.ShapeDtypeStruct((1, hd), F32)],
        compiler_params=_params(1))(dqr, dkr, proj, proj, qg, kg, cos, sin)


def flash_fwd(qr, kr, proj, v_off, *, name):
    T, QW = qr.shape
    hd = ATTN_HEAD_DIM
    KV = ATTN_KV_HEADS
    NH = QW // hd
    G = NH // KV
    GW = G * hd
    bq = _tile(T, 256)
    bk = _tile(T, 512)
    nk = T // bk
    vb = v_off // hd

    def body(q_ref, k_ref, v_ref, o_ref, lse_ref, m_scr, l_scr, acc_scr):
        j = pl.program_id(2)

        @pl.when(j == 0)
        def _():
            m_scr[...] = jnp.full_like(m_scr, -jnp.inf)
            l_scr[...] = jnp.zeros_like(l_scr)
            acc_scr[...] = jnp.zeros_like(acc_scr)

        kv = k_ref[...]
        vv = v_ref[...]
        for g in range(G):
            hs = slice(g * hd, (g + 1) * hd)
            s = _dot_nt(q_ref[:, hs], kv)
            m_old = m_scr[g]
            m_new = jnp.maximum(m_old, jnp.max(s, axis=-1, keepdims=True))
            alpha = jnp.exp(m_old - m_new)
            p = jnp.exp(s - m_new)
            l_scr[g] = alpha * l_scr[g] + jnp.sum(p, axis=-1, keepdims=True)
            acc_scr[:, hs] = alpha * acc_scr[:, hs] + _dot(p.astype(BF16), vv)
            m_scr[g] = m_new

        @pl.when(j == nk - 1)
        def _():
            for g in range(G):
                hs = slice(g * hd, (g + 1) * hd)
                l = l_scr[g]
                o_ref[:, hs] = acc_scr[:, hs] / l
                lse_ref[g] = m_scr[g] + jnp.log(l)

    return pl.pallas_call(
        body, name=name, grid=(KV, T // bq, nk),
        in_specs=[pl.BlockSpec((bq, GW), lambda h, i, j: (i, h)), pl.BlockSpec((bk, hd), lambda h, i, j: (j, h)),
                  pl.BlockSpec((bk, hd), lambda h, i, j: (j, vb + h))],
        out_specs=[pl.BlockSpec((bq, GW), lambda h, i, j: (i, h)),
                   pl.BlockSpec((G, bq, 1), lambda h, i, j: (h, i, 0))],
        out_shape=[jax.ShapeDtypeStruct((T, QW), F32), jax.ShapeDtypeStruct((NH, T, 1), F32)],
        scratch_shapes=[pltpu.VMEM((G, bq, 1), F32), pltpu.VMEM((G, bq, 1), F32), pltpu.VMEM((bq, GW), F32)],
        compiler_params=_params(3))(qr, kr, proj)


def attn_delta(do, o, *, name):
    T, QW = o.shape
    hd = ATTN_HEAD_DIM
    NH = QW // hd
    tr = _tile(T, 256)

    def body(d_ref, o_ref, out_ref):
        for h in range(NH):
            hs = slice(h * hd, (h + 1) * hd)
            out_ref[h] = jnp.sum(d_ref[:, hs].astype(F32) * o_ref[:, hs], axis=-1, keepdims=True)

    return pl.pallas_call(
        body, name=name, grid=(T // tr,), in_specs=[_row(tr, QW), _row(tr, QW)],
        out_specs=pl.BlockSpec((NH, tr, 1), lambda i: (0, i, 0)),
        out_shape=jax.ShapeDtypeStruct((NH, T, 1), F32), compiler_params=_params(1))(do, o)


def flash_dq(qr, kr, proj, v_off, do, lse, delta, *, name):
    T, QW = qr.shape
    hd = ATTN_HEAD_DIM
    KV = ATTN_KV_HEADS
    G = (QW // hd) // KV
    GW = G * hd
    bq = _tile(T, 256)
    bk = _tile(T, 512)
    nk = T // bk
    vb = v_off // hd

    def body(q_ref, k_ref, v_ref, do_ref, lse_ref, dl_ref, dq_ref, acc_scr):
        j = pl.program_id(2)

        @pl.when(j == 0)
        def _():
            acc_scr[...] = jnp.zeros_like(acc_scr)

        kv = k_ref[...]
        vv = v_ref[...]
        for g in range(G):
            hs = slice(g * hd, (g + 1) * hd)
            s = _dot_nt(q_ref[:, hs], kv)
            p = jnp.exp(s - lse_ref[g])
            dp = _dot_nt(do_ref[:, hs], vv)
            ds = (p * (dp - dl_ref[g])).astype(BF16)
            acc_scr[:, hs] += _dot(ds, kv)

        @pl.when(j == nk - 1)
        def _():
            dq_ref[...] = acc_scr[...].astype(BF16)

    qspec = pl.BlockSpec((bq, GW), lambda h, i, j: (i, h))
    cspec = pl.BlockSpec((G, bq, 1), lambda h, i, j: (h, i, 0))
    return pl.pallas_call(
        body, name=name, grid=(KV, T // bq, nk),
        in_specs=[qspec, pl.BlockSpec((bk, hd), lambda h, i, j: (j, h)),
                  pl.BlockSpec((bk, hd), lambda h, i, j: (j, vb + h)), qspec, cspec, cspec],
        out_specs=qspec, out_shape=jax.ShapeDtypeStruct((T, QW), BF16),
        scratch_shapes=[pltpu.VMEM((bq, GW), F32)], compiler_params=_params(3))(qr, kr, proj, do, lse, delta)


def flash_dkv(qr, kr, proj, v_off, do, lse, delta, *, name):
    T, QW = qr.shape
    hd = ATTN_HEAD_DIM
    KV = ATTN_KV_HEADS
    G = (QW // hd) // KV
    GW = G * hd
    bq = _tile(T, 256)
    bk = _tile(T, 512)
    nq = T // bq
    vb = v_off // hd

    def body(q_ref, k_ref, v_ref, do_ref, lse_ref, dl_ref, dk_ref, dv_ref, dk_scr, dv_scr):
        i = pl.program_id(2)

        @pl.when(i == 0)
        def _():
            dk_scr[...] = jnp.zeros_like(dk_scr)
            dv_scr[...] = jnp.zeros_like(dv_scr)

        kv = k_ref[...]
        vv = v_ref[...]
        for g in range(G):
            hs = slice(g * hd, (g + 1) * hd)
            qv = q_ref[:, hs]
            dov = do_ref[:, hs]
            s = _dot_nt(qv, kv)
            p = jnp.exp(s - lse_ref[g])
            dv_scr[...] += _dot_tn(p.astype(BF16), dov)
            dp = _dot_nt(dov, vv)
            ds = (p * (dp - dl_ref[g])).astype(BF16)
            dk_scr[...] += _dot_tn(ds, qv)

        @pl.when(i == nq - 1)
        def _():
            dk_ref[...] = dk_scr[...].astype(BF16)
            dv_ref[...] = dv_scr[...].astype(BF16)

    qspec = pl.BlockSpec((bq, GW), lambda h, j, i: (i, h))
    cspec = pl.BlockSpec((G, bq, 1), lambda h, j, i: (h, i, 0))
    kspec = pl.BlockSpec((bk, hd), lambda h, j, i: (j, h))
    return pl.pallas_call(
        body, name=name, grid=(KV, T // bk, nq),
        in_specs=[qspec, kspec, pl.BlockSpec((bk, hd), lambda h, j, i: (j, vb + h)), qspec, cspec, cspec],
        out_specs=[kspec, kspec],
        out_shape=[jax.ShapeDtypeStruct((T, KV * hd), BF16), jax.ShapeDtypeStruct((T, KV * hd), BF16)],
        scratch_shapes=[pltpu.VMEM((bk, hd), F32), pltpu.VMEM((bk, hd), F32)],
        compiler_params=_params(3))(qr, kr, proj, do, lse, delta)


def outer_silu(c_t, dm, *, name):
    K, B = c_t.shape
    N = dm.shape[1]
    tk = _tile(K, 256)

    def body(c_ref, d_ref, o_ref):
        s = _silu(c_ref[...])
        acc = jnp.zeros((tk, N), F32)
        for b in range(B):
            acc = acc + s[:, b:b + 1] * d_ref[b:b + 1, :]
        o_ref[...] = acc

    return pl.pallas_call(
        body, name=name, grid=(K // tk,),
        in_specs=[pl.BlockSpec((tk, B), lambda i: (i, 0)), pl.BlockSpec((B, N), lambda i: (0, 0))],
        out_specs=pl.BlockSpec((tk, N), lambda i: (i, 0)),
        out_shape=jax.ShapeDtypeStruct((K, N), F32), compiler_params=_params(1))(c_t, dm)


def sum_slots(x, *, name):
    S, R, C = x.shape
    tr = _tile(R, 512)

    def body(x_ref, o_ref):
        acc = x_ref[0].astype(F32)
        for s in range(1, S):
            acc = acc + x_ref[s].astype(F32)
        o_ref[...] = acc

    return pl.pallas_call(
        body, name=name, grid=(R // tr,), in_specs=[pl.BlockSpec((S, tr, C), lambda i: (0, i, 0))],
        out_specs=pl.BlockSpec((tr, C), lambda i: (i, 0)),
        out_shape=jax.ShapeDtypeStruct((R, C), F32), compiler_params=_params(1))(x)


def adamw(w, g, m, v, *, name):
    R, C = w.shape
    tr = _tile(R, 512) if R % 8 == 0 else R

    def body(w_ref, g_ref, m_ref, v_ref, d_ref, nm_ref, nv_ref):
        gv = g_ref[...]
        mn = ADAM_B1 * m_ref[...] + (1.0 - ADAM_B1) * gv
        vn = ADAM_B2 * v_ref[...] + (1.0 - ADAM_B2) * jnp.square(gv)
        m_hat = mn / (1.0 - ADAM_B1 ** ADAM_STEP)
        v_hat = vn / (1.0 - ADAM_B2 ** ADAM_STEP)
        d_ref[...] = -ADAM_LR * (m_hat / (jnp.sqrt(v_hat) + ADAM_EPS) + ADAM_WD * w_ref[...])
        nm_ref[...] = mn
        nv_ref[...] = vn

    blk = pl.BlockSpec((tr, C), lambda i: (i, 0))
    sh = jax.ShapeDtypeStruct((R, C), F32)
    return pl.pallas_call(
        body, name=name, grid=(R // tr,), in_specs=[blk] * 4, out_specs=[blk] * 3, out_shape=[sh] * 3,
        compiler_params=_params(1))(w, g, m, v)


def _adamw_nd(w, g, m, v, name):
    shp = w.shape
    if w.ndim == 1:
        two = (1, shp[0])
    else:
        two = (math.prod(shp[:-1]), shp[-1])
    d, nm, nv = adamw(w.reshape(two), g.reshape(two), m.reshape(two), v.reshape(two), name=name)
    return d.reshape(shp), nm.reshape(shp), nv.reshape(shp)


def _pad_cols(w, n):
    return jnp.pad(w, ((0, 0), (0, n - w.shape[1])))


def _pad_rows(w, n):
    return jnp.pad(w, ((0, n - w.shape[0]), (0, 0)))


def kernel(x, c, w_mod, b_mod, pool_w_in, pool_w_grp, pool_scale, pool_w_out, gla_w_in, gla_fwd_w1, gla_fwd_w2, gla_fwd_b, gla_bwd_w1, gla_bwd_w2, gla_bwd_b, gla_norm_g, gla_w_out, attn_w_in, attn_q_norm_g, attn_k_norm_g, attn_w_out, final_norm_g, loss_target, m_w_mod, m_b_mod, m_pool_w_in, m_pool_w_grp, m_pool_scale, m_pool_w_out, m_gla_w_in, m_gla_fwd_w1, m_gla_fwd_w2, m_gla_fwd_b, m_gla_bwd_w1, m_gla_bwd_w2, m_gla_bwd_b, m_gla_norm_g, m_gla_w_out, m_attn_w_in, m_attn_q_norm_g, m_attn_k_norm_g, m_attn_w_out, m_final_norm_g, v_w_mod, v_b_mod, v_pool_w_in, v_pool_w_grp, v_pool_scale, v_pool_w_out, v_gla_w_in, v_gla_fwd_w1, v_gla_fwd_w2, v_gla_fwd_b, v_gla_bwd_w1, v_gla_bwd_w2, v_gla_bwd_b, v_gla_norm_g, v_gla_w_out, v_attn_w_in, v_attn_q_norm_g, v_attn_k_norm_g, v_attn_w_out, v_final_norm_g):
    W = dict(w_mod=w_mod, b_mod=b_mod, pool_w_in=pool_w_in, pool_w_grp=pool_w_grp, pool_scale=pool_scale,
             pool_w_out=pool_w_out, gla_w_in=gla_w_in, gla_fwd_w1=gla_fwd_w1, gla_fwd_w2=gla_fwd_w2,
             gla_fwd_b=gla_fwd_b, gla_bwd_w1=gla_bwd_w1, gla_bwd_w2=gla_bwd_w2, gla_bwd_b=gla_bwd_b,
             gla_norm_g=gla_norm_g, gla_w_out=gla_w_out, attn_w_in=attn_w_in, attn_q_norm_g=attn_q_norm_g,
             attn_k_norm_g=attn_k_norm_g, attn_w_out=attn_w_out, final_norm_g=final_norm_g)
    M = dict(w_mod=m_w_mod, b_mod=m_b_mod, pool_w_in=m_pool_w_in, pool_w_grp=m_pool_w_grp, pool_scale=m_pool_scale,
             pool_w_out=m_pool_w_out, gla_w_in=m_gla_w_in, gla_fwd_w1=m_gla_fwd_w1, gla_fwd_w2=m_gla_fwd_w2,
             gla_fwd_b=m_gla_fwd_b, gla_bwd_w1=m_gla_bwd_w1, gla_bwd_w2=m_gla_bwd_w2, gla_bwd_b=m_gla_bwd_b,
             gla_norm_g=m_gla_norm_g, gla_w_out=m_gla_w_out, attn_w_in=m_attn_w_in, attn_q_norm_g=m_attn_q_norm_g,
             attn_k_norm_g=m_attn_k_norm_g, attn_w_out=m_attn_w_out, final_norm_g=m_final_norm_g)
    V = dict(w_mod=v_w_mod, b_mod=v_b_mod, pool_w_in=v_pool_w_in, pool_w_grp=v_pool_w_grp, pool_scale=v_pool_scale,
             pool_w_out=v_pool_w_out, gla_w_in=v_gla_w_in, gla_fwd_w1=v_gla_fwd_w1, gla_fwd_w2=v_gla_fwd_w2,
             gla_fwd_b=v_gla_fwd_b, gla_bwd_w1=v_gla_bwd_w1, gla_bwd_w2=v_gla_bwd_w2, gla_bwd_b=v_gla_bwd_b,
             gla_norm_g=v_gla_norm_g, gla_w_out=v_gla_w_out, attn_w_in=v_attn_w_in, attn_q_norm_g=v_attn_q_norm_g,
             attn_k_norm_g=v_attn_k_norm_g, attn_w_out=v_attn_w_out, final_norm_g=v_final_norm_g)

    me = _my_id()
    T, D = x.shape[1], x.shape[2]
    x0 = x.reshape(T, D)
    target = loss_target.reshape(T, D)
    BW = D
    n_pool, n_gla, n_attn = pool_w_in.shape[0], gla_w_in.shape[0], attn_w_in.shape[0]
    LR = GLA_LOWRANK
    LRP = LANES
    KWg = gla_fwd_w2.shape[2] * NDEV
    Dm = w_mod.shape[2]

    small_shard_names = ['pool_scale', 'gla_fwd_w1', 'gla_fwd_w2', 'gla_bwd_w1', 'gla_bwd_w2']
    small_items = [c] + [W[n] for n in small_shard_names]
    small_shapes = [a.shape for a in small_items]
    g1 = all_gather(_pack(small_items, F32), name="gather_small")
    c_all_, ps_all, fw1_all, fw2_all, bw1_all, bw2_all = _unpack(g1, small_shapes, lead=NDEV)
    c_all = c_all_.reshape(NDEV, D)
    pool_scale_full = jnp.transpose(ps_all, (1, 0, 2)).reshape(n_pool, BW)
    w1_full = {'f': fw1_all.transpose(1, 0, 2, 3).reshape(n_gla, D, LR),
               'b': bw1_all.transpose(1, 0, 2, 3).reshape(n_gla, D, LR)}
    w2_full = {'f': fw2_all.transpose(1, 2, 0, 3).reshape(n_gla, LR, KWg),
               'b': bw2_all.transpose(1, 2, 0, 3).reshape(n_gla, LR, KWg)}

    c16 = _pad_rows(c_all, 16)
    b_slab = lax.dynamic_slice_in_dim(b_mod, me * Dm, Dm, axis=1)
    mod_parts = [mm(c16, w_mod[i], name=f"mod_fwd{i}", out_dtype=F32, a_silu=True, tm=16, tn=Dm,
                    bias=b_slab[i:i + 1])[:NDEV] for i in range(DEPTH)]
    mod_slab = jnp.stack(mod_parts)
    g2 = all_gather(_pack([mod_slab], F32), name="gather_mod")
    (mod_all,) = _unpack(g2, [mod_slab.shape], lead=NDEV)
    mod_mine = lax.dynamic_index_in_dim(mod_all, me, axis=2, keepdims=False)
    mod_mine = mod_mine.transpose(1, 0, 2).reshape(DEPTH, NDEV * Dm)
    shift = [mod_mine[i:i + 1, 0:D] for i in range(DEPTH)]
    scale = [mod_mine[i:i + 1, D:2 * D] for i in range(DEPTH)]
    gate = [mod_mine[i:i + 1, 2 * D:3 * D] for i in range(DEPTH)]

    big_names = ['pool_w_in', 'pool_w_grp', 'pool_w_out', 'gla_w_in', 'gla_w_out', 'attn_w_in', 'attn_w_out']
    big_shapes = [W[n].shape for n in big_names]
    g3 = all_gather(_pack([W[n] for n in big_names], BF16), name="gather_big")
    pw_in, pw_grp, pw_out, gw_in, gw_out, aw_in, aw_out = _unpack(g3, big_shapes, lead=NDEV)
    pw_grp_full = pw_grp.transpose(1, 2, 0, 3, 4).reshape(n_pool, pool_w_grp.shape[1], -1, pool_w_grp.shape[3])

    cos, sin = _rope_tables(T)
    ones_bw = jnp.ones((1, BW), F32)

    xs = [x0]
    saved = []
    xi = x0
    for i in range(DEPTH):
        kind, j = i % N_MIXERS, i // N_MIXERS
        h = prenorm(xi, scale[i], shift[i], name=f"prenorm{i}")
        sv = dict(h=h)
        if kind == 0:
            ug = mm(h, pw_in[:, j], b_split=True, name=f"pool_in{i}")
            pooled = band(ug, BW, transpose=False, name=f"pool_band{i}")
            zz = gmm_nn(pooled, pw_grp_full[j], name=f"pool_grp{i}")
            y2 = gated(zz, ug, BW, pool_scale_full[j:j + 1], name=f"pool_gate{i}")
            w_out_full = pw_out[:, j].reshape(BW, D)
            sv.update(ug=ug, pooled=pooled, z=zz)
        elif kind == 1:
            proj = mm(h, gw_in[:, j], b_split=True, name=f"gla_in{i}")
            sv.update(proj=proj)
            for dname in ('f', 'b'):
                w1p = _pad_cols(w1_full[dname][j], LRP)
                w2p = _pad_rows(w2_full[dname][j], LRP)
                bias = (gla_fwd_b if dname == 'f' else gla_bwd_b)[j:j + 1]
                hw1 = mm(h, w1p, name=f"gla_w1{dname}{i}", tn=LRP)
                zg = mm(hw1, w2p, name=f"gla_w2{dname}{i}", out_dtype=F32, tk=LRP)
                o_d, st_d = gla_fwd(proj, zg, bias, reverse=(dname == 'b'), name=f"gla_scan_{dname}{i}")
                sv.update({f"hw1{dname}": hw1, f"z{dname}": zg, f"o{dname}": o_d, f"st{dname}": st_d,
                           f"w1p{dname}": w1p, f"w2p{dname}": w2p, f"bias{dname}": bias})
            y2 = gla_out(sv['of'], sv['ob'], proj, 2 * BW, gla_norm_g[j:j + 1], name=f"gla_out{i}")
            w_out_full = gw_out[:, j].reshape(BW, D)
        else:
            proj = mm(h, aw_in[:, j], b_split=True, name=f"attn_in{i}")
            QW, KVW, _, _ = _attn_dims(proj)
            qr, kr = attn_prep(proj, attn_q_norm_g[j:j + 1], attn_k_norm_g[j:j + 1], cos, sin, name=f"attn_prep{i}")
            o, lse = flash_fwd(qr, kr, proj, QW + KVW, name=f"attn_flash{i}")
            y2 = gated(o, proj, QW + 2 * KVW, ones_bw, name=f"attn_gate{i}")
            w_out_full = aw_out[:, j].reshape(BW, D)
            sv.update(proj=proj, qr=qr, kr=kr, o=o, lse=lse)
        y = mm(y2, w_out_full, name=f"out_proj{i}")
        sv.update(y2=y2, y=y, w_out=w_out_full)
        saved.append(sv)
        xi = resid(xi, y, gate[i], name=f"resid{i}")
        xs.append(xi)

    loss_part, dx, d_final_g = loss_head(xi, final_norm_g.reshape(1, D), target, name="loss_head")

    d_mod = [None] * DEPTH
    big_grads = {n: [None] * W[n].shape[0] for n in big_names}
    small_grads = {}
    for i in reversed(range(DEPTH)):
        kind, j = i % N_MIXERS, i // N_MIXERS
        sv = saved[i]
        h = sv['h']
        dy, d_gate = resid_bwd(dx, sv['y'], gate[i], name=f"resid_bwd{i}")
        dy2 = mm(dy, sv['w_out'], tb=True, name=f"out_proj_dx{i}")
        g_wout = mm(sv['y2'], dy, ta=True, name=f"out_proj_dw{i}").reshape(NDEV, BW // NDEV, D)
        if kind == 0:
            ug = sv['ug']
            dz, dg, d_ps = gated_bwd(dy2, sv['z'], ug, BW, pool_scale_full[j:j + 1], name=f"pool_gate_bwd{i}")
            dpooled = gmm_nn(dz, pw_grp_full[j], tb=True, name=f"pool_grp_dx{i}")
            g_grp = gmm_tn(sv['pooled'], dz, len(POOL_WINDOWS), name=f"pool_grp_dw{i}")
            du = band(dpooled, BW, transpose=True, name=f"pool_band_bwd{i}")
            dproj = jnp.concatenate([du, dg], axis=1)
            dh = mm(dproj, pw_in[:, j], tb=True, b_split=True, name=f"pool_in_dx{i}")
            g_win = mm(h, dproj, ta=True, out_split=pool_w_in.shape[2], name=f"pool_in_dw{i}")
            Gp, Cg = g_grp.shape[0], g_grp.shape[1]
            big_grads['pool_w_in'][j] = g_win
            big_grads['pool_w_grp'][j] = g_grp.astype(BF16).reshape(Gp, NDEV, Cg // NDEV, Cg).transpose(1, 0, 2, 3)
            big_grads['pool_w_out'][j] = g_wout
            small_grads.setdefault('pool_scale', [None] * n_pool)[j] = d_ps
        elif kind == 1:
            proj = sv['proj']
            do, dg, d_ng = gla_out_bwd(dy2, sv['of'], sv['ob'], proj, 2 * BW, gla_norm_g[j:j + 1],
                                       name=f"gla_out_bwd{i}")
            prev = None
            dh_acc = None
            for dname in ('f', 'b'):
                dq, dk, dv, dzg, dbias = gla_bwd(proj, sv[f"z{dname}"], sv[f"bias{dname}"], sv[f"st{dname}"], do,
                                                 prev, reverse=(dname == 'b'), name=f"gla_scan_bwd_{dname}{i}")
                prev = (dq, dk, dv)
                dhw1 = mm(dzg, sv[f"w2p{dname}"], tb=True, name=f"gla_w2{dname}_dx{i}", tn=LRP)
                g_w2 = mm(sv[f"hw1{dname}"], dzg, ta=True, out_dtype=F32, name=f"gla_w2{dname}_dw{i}", tm=LRP)
                g_w1 = mm(h, dhw1, ta=True, out_dtype=F32, name=f"gla_w1{dname}_dw{i}", tn=LRP)
                dh_acc = mm(dhw1, sv[f"w1p{dname}"], tb=True, add=dh_acc, name=f"gla_w1{dname}_dx{i}", tk=LRP)
                key = 'gla_fwd' if dname == 'f' else 'gla_bwd'
                small_grads[key + '_w1'] = g_w1[:, :LR]
                small_grads[key + '_w2'] = g_w2[:LR]
                small_grads[key + '_b'] = dbias
            dproj = jnp.concatenate([prev[0], prev[1], prev[2], dg], axis=1)
            dh = mm(dproj, gw_in[:, j], tb=True, b_split=True, add=dh_acc, name=f"gla_in_dx{i}")
            big_grads['gla_w_in'][j] = mm(h, dproj, ta=True, out_split=gla_w_in.shape[2], name=f"gla_in_dw{i}")
            big_grads['gla_w_out'][j] = g_wout
            small_grads['gla_norm_g'] = d_ng
        else:
            proj = sv['proj']
            QW, KVW, _, _ = _attn_dims(proj)
            do, dg, _ = gated_bwd(dy2, sv['o'], proj, QW + 2 * KVW, ones_bw, name=f"attn_gate_bwd{i}")
            delta = attn_delta(do, sv['o'], name=f"attn_delta{i}")
            dqr = flash_dq(sv['qr'], sv['kr'], proj, QW + KVW, do, sv['lse'], delta, name=f"attn_flash_dq{i}")
            dkr, dv = flash_dkv(sv['qr'], sv['kr'], proj, QW + KVW, do, sv['lse'], delta, name=f"attn_flash_dkv{i}")
            dq, dk, d_qg, d_kg = attn_prep_bwd(dqr, dkr, proj, attn_q_norm_g[j:j + 1], attn_k_norm_g[j:j + 1],
                                               cos, sin, name=f"attn_prep_bwd{i}")
            dproj = jnp.concatenate([dq, dk, dv, dg], axis=1)
            dh = mm(dproj, aw_in[:, j], tb=True, b_split=True, name=f"attn_in_dx{i}")
            big_grads['attn_w_in'][j] = mm(h, dproj, ta=True, out_split=attn_w_in.shape[2], name=f"attn_in_dw{i}")
            big_grads['attn_w_out'][j] = g_wout
            small_grads['attn_q_norm_g'] = d_qg
            small_grads['attn_k_norm_g'] = d_kg
        dx, d_scale, d_shift = prenorm_bwd(xs[i], dh, dx, scale[i], name=f"prenorm_bwd{i}")
        d_mod[i] = jnp.concatenate([d_shift, d_scale, d_gate], axis=1)
    grad_x = dx.reshape(1, T, D)

    send_big = _pack([jnp.stack(big_grads[n], axis=1) for n in big_names], BF16, lead=NDEV)
    recv_big = all_to_all(send_big, name="exchange_big")
    big_sum = sum_slots(recv_big, name="sum_big")
    big_g = dict(zip(big_names, _unpack(big_sum, big_shapes)))

    small_order = ['b_mod', 'gla_fwd_b', 'gla_bwd_b', 'gla_norm_g', 'attn_q_norm_g', 'attn_k_norm_g', 'final_norm_g',
                   'pool_scale', 'gla_fwd_w1', 'gla_fwd_w2', 'gla_bwd_w1', 'gla_bwd_w2']
    small_grads['b_mod'] = jnp.concatenate(d_mod, axis=0)
    small_grads['final_norm_g'] = d_final_g
    small_grads['pool_scale'] = jnp.concatenate(small_grads['pool_scale'], axis=0)
    part_items = [jnp.pad(loss_part.reshape(1), (0, LANES - 1))] + [small_grads[n] for n in small_order]
    part_shapes = [a.shape for a in part_items]
    g4 = all_gather(_pack(part_items, F32), name="gather_parts")
    tot = dict(zip(['loss'] + small_order, _unpack(sum_slots(g4, name="sum_parts"), part_shapes)))
    loss = tot['loss'][0]
    d_mod_all = _unpack(g4, part_shapes, lead=NDEV)[1]

    grads = {}
    grads.update(big_g)
    grads['b_mod'] = tot['b_mod']
    for n in ('gla_fwd_b', 'gla_bwd_b', 'gla_norm_g', 'attn_q_norm_g', 'attn_k_norm_g'):
        grads[n] = tot[n].reshape(W[n].shape)
    grads['final_norm_g'] = tot['final_norm_g'].reshape(D)
    ps_n = pool_scale.shape[1]
    grads['pool_scale'] = lax.dynamic_slice_in_dim(tot['pool_scale'], me * ps_n, ps_n, axis=1)
    rows = gla_fwd_w1.shape[1]
    cols = gla_fwd_w2.shape[2]
    for key in ('gla_fwd', 'gla_bwd'):
        grads[key + '_w1'] = lax.dynamic_slice_in_dim(tot[key + '_w1'], me * rows, rows, axis=0).reshape(1, rows, LR)
        grads[key + '_w2'] = lax.dynamic_slice_in_dim(tot[key + '_w2'], me * cols, cols, axis=1).reshape(1, LR, cols)

    c_t = c_all.T
    dm_slab = lax.dynamic_slice_in_dim(d_mod_all, me * Dm, Dm, axis=2)
    grads['w_mod'] = jnp.stack([outer_silu(c_t, dm_slab[:, i], name=f"mod_dw{i}") for i in range(DEPTH)])

    deltas, new_m, new_v = {}, {}, {}
    for n in WEIGHTS:
        deltas[n], new_m[n], new_v[n] = _adamw_nd(W[n], grads[n], M[n], V[n], name=f"adamw_{n}")

    return (loss, grad_x, *[grads[n] for n in WEIGHTS], *[deltas[n] for n in WEIGHTS],
            *[new_m[n] for n in WEIGHTS], *[new_v[n] for n in WEIGHTS])
```

```python
import functools
import math

import jax
import jax.numpy as jnp
from jax import lax
from jax.experimental import pallas as pl
from jax.experimental.pallas import tpu as pltpu

F32 = jnp.float32
BF16 = jnp.bfloat16
NDEV = 8
LANES = 128
VMEM_LIMIT = 56 * 1024 * 1024

D_MODEL = 2048
DEPTH = 4
N_MIXERS = 3
GRID_W = 64
NORM_EPS = 1e-6
POOL_WINDOWS = (2, 4, 8, 16)
GLA_HEADS = 4
GLA_LOWRANK = 16
GLA_TAU = 16.0
GLA_CHUNK = 64
ATTN_HEAD_DIM = 128
ATTN_KV_HEADS = 4
ROPE_THETA = 10000.0
ADAM_LR = 0.001
ADAM_B1 = 0.9
ADAM_B2 = 0.999
ADAM_EPS = 1e-08
ADAM_WD = 0.01
ADAM_STEP = 10

WEIGHTS = ['w_mod', 'b_mod', 'pool_w_in', 'pool_w_grp', 'pool_scale', 'pool_w_out', 'gla_w_in', 'gla_fwd_w1',
           'gla_fwd_w2', 'gla_fwd_b', 'gla_bwd_w1', 'gla_bwd_w2', 'gla_bwd_b', 'gla_norm_g', 'gla_w_out',
           'attn_w_in', 'attn_q_norm_g', 'attn_k_norm_g', 'attn_w_out', 'final_norm_g']


def _params(n_axes=0):
    sem = ("arbitrary",) * n_axes if n_axes else None
    return pltpu.CompilerParams(dimension_semantics=sem, vmem_limit_bytes=VMEM_LIMIT)


def _silu(g):
    return g * jax.nn.sigmoid(g)


def _dsilu(g):
    s = jax.nn.sigmoid(g)
    return s * (1.0 + g * (1.0 - s))


def _dot(a, b):
    return jnp.dot(a, b, preferred_element_type=F32)


def _dot_nt(a, b):
    return lax.dot_general(a, b, (((1,), (1,)), ((), ())), preferred_element_type=F32)


def _dot_tn(a, b):
    return lax.dot_general(a, b, (((0,), (0,)), ((), ())), preferred_element_type=F32)


def _tile(n, pref):
    if n <= pref:
        return n
    for step in (128, 16, 8):
        t = pref - pref % step
        while t >= step:
            if n % t == 0:
                return t
            t -= step
    raise ValueError((n, pref))


def _peer(k):
    x, y, c = lax.axis_index("x"), lax.axis_index("y"), lax.axis_index("c")
    px = 1 - x if k & 4 else x
    py = 1 - y if k & 2 else y
    pc = 1 - c if k & 1 else c
    return (px, py, pc), 4 * px + 2 * py + pc


def _my_id():
    return 4 * lax.axis_index("x") + 2 * lax.axis_index("y") + lax.axis_index("c")


GATHER = "gather"
SCATTER = "scatter"

EXCHANGE_SCRATCH = [pltpu.SemaphoreType.DMA((NDEV - 1,)), pltpu.SemaphoreType.DMA((NDEV - 1,)),
                    pltpu.SemaphoreType.DMA]


def _exchange_copies(kind, x_ref, o_ref, send_sems, recv_sems, local_sem, incoming):
    me = _my_id()
    copies = []
    if not incoming:
        copies.append(pltpu.make_async_copy(x_ref if kind == GATHER else x_ref.at[me], o_ref.at[me], local_sem))
    for k in range(1, NDEV):
        peer, pid = _peer(k)
        copies.append(pltpu.make_async_remote_copy(
            src_ref=x_ref if kind == GATHER else x_ref.at[pid], dst_ref=o_ref.at[pid if incoming else me],
            send_sem=send_sems.at[k - 1], recv_sem=recv_sems.at[k - 1], device_id=peer,
            device_id_type=pl.DeviceIdType.MESH))
    return copies


def _exchange_start(*refs):
    for cp in _exchange_copies(*refs, incoming=False):
        cp.start()


def _exchange_wait(*refs):
    for cp in _exchange_copies(*refs, incoming=True):
        cp.wait_recv()
    local, *sends = _exchange_copies(*refs, incoming=False)
    for cp in sends:
        cp.wait_send()
    local.wait()


def exchange(kind, x, name):
    R, C = x.shape[-2:]

    def body(x_ref, o_ref, send_sems, recv_sems, local_sem):
        _exchange_start(kind, x_ref, o_ref, send_sems, recv_sems, local_sem)
        _exchange_wait(kind, x_ref, o_ref, send_sems, recv_sems, local_sem)

    return pl.pallas_call(
        body, name=name,
        out_shape=jax.ShapeDtypeStruct((NDEV, R, C), x.dtype),
        in_specs=[pl.BlockSpec(memory_space=pl.ANY)],
        out_specs=pl.BlockSpec(memory_space=pl.ANY),
        scratch_shapes=EXCHANGE_SCRATCH,
    )(x)


def _pack(arrs, dtype, lead=None):
    unit = 16 * LANES
    if lead is None:
        flat = [a.astype(dtype).reshape(-1) for a in arrs]
        n = sum(f.shape[0] for f in flat)
        pad = (-n) % unit
        if pad:
            flat.append(jnp.zeros((pad,), dtype))
        return jnp.concatenate(flat).reshape(-1, LANES)
    flat = [a.astype(dtype).reshape(lead, -1) for a in arrs]
    n = sum(f.shape[1] for f in flat)
    pad = (-n) % unit
    if pad:
        flat.append(jnp.zeros((lead, pad), dtype))
    return jnp.concatenate(flat, axis=1).reshape(lead, -1, LANES)


def _unpack(buf, shapes, lead=None):
    out = []
    off = 0
    if lead is None:
        flat = buf.reshape(-1)
        for s in shapes:
            n = math.prod(s)
            out.append(flat[off:off + n].reshape(s))
            off += n
        return out
    flat = buf.reshape(lead, -1)
    for s in shapes:
        n = math.prod(s)
        out.append(flat[:, off:off + n].reshape((lead,) + tuple(s)))
        off += n
    return out


def mm(a, b, *, name, ta=False, tb=False, b_split=False, out_split=0, out_dtype=BF16, tm=1024, tn=1024, tk=512,
       bias=None, add=None, a_silu=False, side=None):
    K, M = a.shape if ta else a.shape[::-1]
    if b_split:
        S, d1, n = b.shape
        if tb:
            N, Kb = d1, S * n
        else:
            Kb, N = d1, S * n
    else:
        Kb, N = b.shape[::-1] if tb else b.shape
    assert Kb == K, (a.shape, b.shape, ta, tb, b_split)
    tm = _tile(M, tm)
    if b_split and not tb:
        tn = _tile(n, tn)
    elif out_split:
        tn = _tile(out_split, tn)
    else:
        tn = _tile(N, tn)
    if b_split and tb:
        tk = _tile(n, 1024)
    else:
        tk = _tile(K, tk)
    nk = K // tk

    a_spec = pl.BlockSpec((tk, tm), lambda i, j, k: (k, i)) if ta else pl.BlockSpec((tm, tk), lambda i, j, k: (i, k))
    if b_split and not tb:
        per = n // tn
        b_spec = pl.BlockSpec((None, tk, tn), lambda i, j, k: (j // per, k, j % per))
    elif b_split and tb:
        per = n // tk
        b_spec = pl.BlockSpec((None, tn, tk), lambda i, j, k: (k // per, j, k % per))
    elif tb:
        b_spec = pl.BlockSpec((tn, tk), lambda i, j, k: (j, k))
    else:
        b_spec = pl.BlockSpec((tk, tn), lambda i, j, k: (k, j))
    if out_split:
        per_o = out_split // tn
        o_spec = pl.BlockSpec((None, tm, tn), lambda i, j, k: (j // per_o, i, j % per_o))
        o_shape = jax.ShapeDtypeStruct((N // out_split, M, out_split), out_dtype)
    else:
        o_spec = pl.BlockSpec((tm, tn), lambda i, j, k: (i, j))
        o_shape = jax.ShapeDtypeStruct((M, N), out_dtype)
    ins = [a, b]
    in_specs = [a_spec, b_spec]
    if bias is not None:
        ins.append(bias)
        in_specs.append(pl.BlockSpec((1, tn), lambda i, j, k: (0, j)))
    if add is not None:
        ins.append(add)
        in_specs.append(pl.BlockSpec((tm, tn), lambda i, j, k: (i, j)))
    has_bias, has_add, has_side = bias is not None, add is not None, side is not None
    out_specs, out_shapes, scratch = [o_spec], [o_shape], [pltpu.VMEM((tm, tn), F32)]
    if has_side:
        side_kind, side_x = side
        ins.append(side_x)
        in_specs.append(pl.BlockSpec(memory_space=pl.ANY))
        out_specs.append(pl.BlockSpec(memory_space=pl.ANY))
        out_shapes.append(jax.ShapeDtypeStruct((NDEV,) + tuple(side_x.shape[-2:]), side_x.dtype))
        scratch += EXCHANGE_SCRATCH
    gm, gn = M // tm, N // tn

    def body(*refs):
        a_ref, b_ref = refs[0], refs[1]
        pos = 2
        bias_ref = add_ref = None
        if has_bias:
            bias_ref = refs[pos]
            pos += 1
        if has_add:
            add_ref = refs[pos]
            pos += 1
        if has_side:
            side_refs = (side_kind, refs[pos], refs[pos + 2]) + tuple(refs[pos + 4:pos + 7])
            o_ref, acc_ref = refs[pos + 1], refs[pos + 3]
        else:
            o_ref, acc_ref = refs[pos], refs[pos + 1]
        k = pl.program_id(2)
        if has_side:
            i_, j_ = pl.program_id(0), pl.program_id(1)

            @pl.when((i_ == 0) & (j_ == 0) & (k == 0))
            def _():
                _exchange_start(*side_refs)

        @pl.when(k == 0)
        def _():
            acc_ref[...] = jnp.zeros_like(acc_ref)

        av = a_ref[...]
        if a_silu:
            av = _silu(av.astype(F32))
        av = av.astype(BF16)
        bv = b_ref[...].astype(BF16)
        dn = (((0 if ta else 1,), (1 if tb else 0,)), ((), ()))
        acc_ref[...] += lax.dot_general(av, bv, dn, preferred_element_type=F32)

        @pl.when(k == nk - 1)
        def _():
            r = acc_ref[...]
            if has_bias:
                r = r + bias_ref[...]
            if has_add:
                r = r + add_ref[...].astype(F32)
            o_ref[...] = r.astype(out_dtype)

        if has_side:
            @pl.when((i_ == gm - 1) & (j_ == gn - 1) & (k == nk - 1))
            def _():
                _exchange_wait(*side_refs)

    res = pl.pallas_call(
        body, name=name, grid=(gm, gn, nk), in_specs=in_specs, out_specs=out_specs, out_shape=out_shapes,
        scratch_shapes=scratch, compiler_params=_params(3))(*ins)
    return (res[0], res[1]) if has_side else res[0]


def gmm_nn(a, w, *, name, tb=False, out_dtype=BF16, tm=512):
    T = a.shape[0]
    G = w.shape[0]
    Kg = a.shape[1] // G
    Ng = w.shape[1] if tb else w.shape[2]
    tm = _tile(T, tm)

    def body(a_ref, w_ref, o_ref):
        wv = w_ref[...].astype(BF16)
        av = a_ref[...].astype(BF16)
        r = _dot_nt(av, wv) if tb else _dot(av, wv)
        o_ref[...] = r.astype(out_dtype)

    return pl.pallas_call(
        body, name=name, grid=(G, T // tm),
        in_specs=[pl.BlockSpec((tm, Kg), lambda g, i: (i, g)),
                  pl.BlockSpec((None,) + tuple(w.shape[1:]), lambda g, i: (g, 0, 0))],
        out_specs=pl.BlockSpec((tm, Ng), lambda g, i: (i, g)),
        out_shape=jax.ShapeDtypeStruct((T, G * Ng), out_dtype), compiler_params=_params(2))(a, w)


def gmm_tn(a, b, G, *, name, tk=512):
    T = a.shape[0]
    Kg = a.shape[1] // G
    Ng = b.shape[1] // G
    tk = _tile(T, tk)

    def body(a_ref, b_ref, o_ref):
        @pl.when(pl.program_id(1) == 0)
        def _():
            o_ref[...] = jnp.zeros_like(o_ref)

        o_ref[...] += _dot_tn(a_ref[...].astype(BF16), b_ref[...].astype(BF16))

    return pl.pallas_call(
        body, name=name, grid=(G, T // tk),
        in_specs=[pl.BlockSpec((tk, Kg), lambda g, i: (i, g)), pl.BlockSpec((tk, Ng), lambda g, i: (i, g))],
        out_specs=pl.BlockSpec((None, Kg, Ng), lambda g, i: (g, 0, 0)),
        out_shape=jax.ShapeDtypeStruct((G, Kg, Ng), F32), compiler_params=_params(2))(a, b)


def _row(tr, w, cb=0):
    return pl.BlockSpec((tr, w), lambda i: (i, cb))


def _bc(w):
    return pl.BlockSpec((1, w), lambda i: (0, 0))


def prenorm(x, scale, shift, *, name):
    T, D = x.shape
    tr = _tile(T, 256)

    def body(x_ref, sc_ref, sh_ref, h_ref):
        xv = x_ref[...]
        rstd = lax.rsqrt(jnp.mean(xv * xv, axis=-1, keepdims=True) + NORM_EPS)
        h_ref[...] = ((xv * rstd) * (1.0 + sc_ref[...]) + sh_ref[...]).astype(BF16)

    return pl.pallas_call(
        body, name=name, grid=(T // tr,), in_specs=[_row(tr, D), _bc(D), _bc(D)], out_specs=_row(tr, D),
        out_shape=jax.ShapeDtypeStruct((T, D), BF16), compiler_params=_params(1))(x, scale, shift)


def prenorm_bwd(x, dh, dxn, scale, *, name):
    T, D = x.shape
    tr = _tile(T, 256)

    def body(x_ref, dh_ref, dxn_ref, sc_ref, dx_ref, dsc_ref, dsh_ref):
        @pl.when(pl.program_id(0) == 0)
        def _():
            dsc_ref[...] = jnp.zeros_like(dsc_ref)
            dsh_ref[...] = jnp.zeros_like(dsh_ref)

        xv = x_ref[...]
        dhv = dh_ref[...].astype(F32)
        rstd = lax.rsqrt(jnp.mean(xv * xv, axis=-1, keepdims=True) + NORM_EPS)
        r = xv * rstd
        dsc_ref[...] += jnp.sum(dhv * r, axis=0, keepdims=True)
        dsh_ref[...] += jnp.sum(dhv, axis=0, keepdims=True)
        dr = dhv * (1.0 + sc_ref[...])
        dx_ref[...] = dxn_ref[...] + rstd * (dr - r * jnp.mean(dr * r, axis=-1, keepdims=True))

    return pl.pallas_call(
        body, name=name, grid=(T // tr,), in_specs=[_row(tr, D), _row(tr, D), _row(tr, D), _bc(D)],
        out_specs=[_row(tr, D), _bc(D), _bc(D)],
        out_shape=[jax.ShapeDtypeStruct((T, D), F32), jax.ShapeDtypeStruct((1, D), F32),
                   jax.ShapeDtypeStruct((1, D), F32)], compiler_params=_params(1))(x, dh, dxn, scale)


def resid(x, y, gate, *, name):
    T, D = x.shape
    tr = _tile(T, 256)

    def body(x_ref, y_ref, g_ref, o_ref):
        o_ref[...] = x_ref[...] + g_ref[...] * y_ref[...].astype(F32)

    return pl.pallas_call(
        body, name=name, grid=(T // tr,), in_specs=[_row(tr, D), _row(tr, D), _bc(D)], out_specs=_row(tr, D),
        out_shape=jax.ShapeDtypeStruct((T, D), F32), compiler_params=_params(1))(x, y, gate)


def resid_bwd(dxn, y, gate, *, name):
    T, D = dxn.shape
    tr = _tile(T, 256)

    def body(d_ref, y_ref, g_ref, dy_ref, dg_ref):
        @pl.when(pl.program_id(0) == 0)
        def _():
            dg_ref[...] = jnp.zeros_like(dg_ref)

        d = d_ref[...]
        dy_ref[...] = (d * g_ref[...]).astype(BF16)
        dg_ref[...] += jnp.sum(d * y_ref[...].astype(F32), axis=0, keepdims=True)

    return pl.pallas_call(
        body, name=name, grid=(T // tr,), in_specs=[_row(tr, D), _row(tr, D), _bc(D)],
        out_specs=[_row(tr, D), _bc(D)],
        out_shape=[jax.ShapeDtypeStruct((T, D), BF16), jax.ShapeDtypeStruct((1, D), F32)],
        compiler_params=_params(1))(dxn, y, gate)


def loss_head(x, g, target, *, name):
    T, D = x.shape
    tr = _tile(T, 256)

    def body(x_ref, g_ref, t_ref, loss_ref, dx_ref, dg_ref):
        @pl.when(pl.program_id(0) == 0)
        def _():
            loss_ref[...] = jnp.zeros_like(loss_ref)
            dg_ref[...] = jnp.zeros_like(dg_ref)

        xv = x_ref[...]
        gv = g_ref[...]
        rstd = lax.rsqrt(jnp.mean(xv * xv, axis=-1, keepdims=True) + NORM_EPS)
        r = xv * rstd
        e = r * gv - t_ref[...]
        loss_ref[...] += 0.5 * jnp.sum(jnp.mean(e * e, axis=-1, keepdims=True), axis=0, keepdims=True)
        dout = e * (1.0 / D)
        dg_ref[...] += jnp.sum(dout * r, axis=0, keepdims=True)
        dr = dout * gv
        dx_ref[...] = rstd * (dr - r * jnp.mean(dr * r, axis=-1, keepdims=True))

    return pl.pallas_call(
        body, name=name, grid=(T // tr,), in_specs=[_row(tr, D), _bc(D), _row(tr, D)],
        out_specs=[pl.BlockSpec((1, 1), lambda i: (0, 0)), _row(tr, D), _bc(D)],
        out_shape=[jax.ShapeDtypeStruct((1, 1), F32), jax.ShapeDtypeStruct((T, D), F32),
                   jax.ShapeDtypeStruct((1, D), F32)], compiler_params=_params(1))(x, g, target)


def _col_tile(W, off):
    cw = math.gcd(W, off) if off else W
    cw = math.gcd(cw, 1024) if cw > 1024 else cw
    return cw


def gated(a, proj, g_off, scale, *, name):
    T, W = a.shape
    cw = _col_tile(W, g_off)
    gb = g_off // cw
    tr = _tile(T, 256)

    def body(a_ref, g_ref, s_ref, o_ref):
        o_ref[...] = (a_ref[...].astype(F32) * s_ref[...] * _silu(g_ref[...].astype(F32))).astype(BF16)

    return pl.pallas_call(
        body, name=name, grid=(W // cw, T // tr),
        in_specs=[pl.BlockSpec((tr, cw), lambda j, i: (i, j)), pl.BlockSpec((tr, cw), lambda j, i: (i, gb + j)),
                  pl.BlockSpec((1, cw), lambda j, i: (0, j))],
        out_specs=pl.BlockSpec((tr, cw), lambda j, i: (i, j)),
        out_shape=jax.ShapeDtypeStruct((T, W), BF16), compiler_params=_params(2))(a, proj, scale)


def gated_bwd(dy2, a, proj, g_off, scale, *, name):
    T, W = a.shape
    cw = _col_tile(W, g_off)
    gb = g_off // cw
    tr = _tile(T, 256)

    def body(d_ref, a_ref, g_ref, s_ref, da_ref, dg_ref, ds_ref):
        @pl.when(pl.program_id(1) == 0)
        def _():
            ds_ref[...] = jnp.zeros_like(ds_ref)

        d = d_ref[...].astype(F32)
        av = a_ref[...].astype(F32)
        gv = g_ref[...].astype(F32)
        sv = s_ref[...]
        dsg = d * _silu(gv)
        da_ref[...] = (dsg * sv).astype(BF16)
        dg_ref[...] = (d * av * sv * _dsilu(gv)).astype(BF16)
        ds_ref[...] += jnp.sum(dsg * av, axis=0, keepdims=True)

    blk = pl.BlockSpec((tr, cw), lambda j, i: (i, j))
    return pl.pallas_call(
        body, name=name, grid=(W // cw, T // tr),
        in_specs=[blk, blk, pl.BlockSpec((tr, cw), lambda j, i: (i, gb + j)),
                  pl.BlockSpec((1, cw), lambda j, i: (0, j))],
        out_specs=[blk, blk, pl.BlockSpec((1, cw), lambda j, i: (0, j))],
        out_shape=[jax.ShapeDtypeStruct((T, W), BF16), jax.ShapeDtypeStruct((T, W), BF16),
                   jax.ShapeDtypeStruct((1, W), F32)], compiler_params=_params(2))(dy2, a, proj, scale)


HALO = 16


def band(u, W, *, transpose, name):
    T = u.shape[0]
    R = _tile(T, 256)
    nb = T // R
    G = len(POOL_WINDOWS)
    Cg = W // G
    hal = min(HALO, R)

    def body(p_ref, c_ref, n_ref, o_ref):
        i = pl.program_id(0)
        out_pos = lax.broadcasted_iota(jnp.int32, (R, 1), 0) + i * R
        parts = ((p_ref, i * R - hal, hal, R - hal), (c_ref, i * R, R, 0), (n_ref, (i + 1) * R, hal, 0))
        for gi, w in enumerate(POOL_WINDOWS):
            half = w // 2
            cols = slice(gi * Cg, (gi + 1) * Cg)
            acc = jnp.zeros((R, Cg), F32)
            for ref, base, n, r0 in parts:
                src_pos = lax.broadcasted_iota(jnp.int32, (1, n), 1) + base
                valid = (src_pos >= 0) & (src_pos < T)
                src = ref[r0:r0 + n, cols]
                if not transpose:
                    m = (src_pos >= out_pos - half) & (src_pos < out_pos + half) & valid
                else:
                    m = (out_pos >= src_pos - half) & (out_pos < src_pos + half) & valid
                    sp = lax.broadcasted_iota(jnp.int32, (n, 1), 0) + base
                    cnt = jnp.minimum(sp + half, T) - jnp.maximum(sp - half, 0)
                    src = (src.astype(F32) / jnp.maximum(cnt, 1).astype(F32)).astype(BF16)
                acc = acc + _dot(m.astype(BF16), src.astype(BF16))
            if not transpose:
                cnt = jnp.minimum(out_pos + half, T) - jnp.maximum(out_pos - half, 0)
                acc = acc / cnt.astype(F32)
            o_ref[:, cols] = (acc - c_ref[:, cols].astype(F32)).astype(BF16)

    return pl.pallas_call(
        body, name=name, grid=(nb,),
        in_specs=[pl.BlockSpec((R, W), lambda i: (jnp.maximum(i - 1, 0), 0)), pl.BlockSpec((R, W), lambda i: (i, 0)),
                  pl.BlockSpec((R, W), lambda i: (jnp.minimum(i + 1, nb - 1), 0))],
        out_specs=pl.BlockSpec((R, W), lambda i: (i, 0)),
        out_shape=jax.ShapeDtypeStruct((T, W), BF16), compiler_params=_params(1))(u, u, u)


def _log_sigmoid(x):
    return jnp.minimum(x, 0.0) - jnp.log1p(jnp.exp(-jnp.abs(x)))


def _split3(x):
    hi = x.astype(BF16)
    r1 = x - hi.astype(F32)
    md = r1.astype(BF16)
    lo = (r1 - md.astype(F32)).astype(BF16)
    return hi, md, lo


def _tri_sum(tri, x):
    hi, md, lo = _split3(x)
    return _dot(tri, hi) + _dot(tri, md) + _dot(tri, lo)


def _gla_masks(C, reverse):
    row = lax.broadcasted_iota(jnp.int32, (C, C), 0)
    col = lax.broadcasted_iota(jnp.int32, (C, C), 1)
    if not reverse:
        return (col <= row), (col >= row), (col <= row)
    return (col >= row), (col <= row), (col > row)


def _gla_dims(proj):
    VW = proj.shape[1] // 3
    KW = VW // 2
    return KW, VW, KW // GLA_HEADS, VW // GLA_HEADS


def gla_fwd(proj, z, bias, *, reverse, name):
    T = proj.shape[0]
    KW, VW, DK, DV = _gla_dims(proj)
    H = GLA_HEADS
    C = _tile(T, GLA_CHUNK)
    NC = T // C
    cidx = (lambda i: NC - 1 - i) if reverse else (lambda i: i)
    last = 0 if reverse else C - 1

    def body(q_ref, k_ref, v_ref, z_ref, b_ref, o_ref, s_ref, S_scr):
        @pl.when(pl.program_id(0) == 0)
        def _():
            S_scr[...] = jnp.zeros_like(S_scr)

        cum, _, amask = _gla_masks(C, reverse)
        la = _log_sigmoid(z_ref[...] + b_ref[...]) * (1.0 / GLA_TAU)
        b = _tri_sum(cum.astype(BF16), la)
        for h in range(H):
            ks = slice(h * DK, (h + 1) * DK)
            vs = slice(h * DV, (h + 1) * DV)
            bh = b[:, ks]
            mid = bh[C // 2:C // 2 + 1, :]
            bl = bh[last:last + 1, :]
            q = q_ref[:, ks].astype(F32) * (DK ** -0.5)
            k = k_ref[:, ks].astype(F32)
            v = v_ref[:, vs]
            qe = (q * jnp.exp(bh)).astype(BF16)
            qt = (q * jnp.exp(bh - mid)).astype(BF16)
            kt = (k * jnp.exp(mid - bh)).astype(BF16)
            kd = (k * jnp.exp(bl - bh)).astype(BF16)
            St = S_scr[h]
            Sb = St.astype(BF16)
            s_ref[0, h] = Sb
            A = jnp.where(amask, _dot_nt(qt, kt), 0.0).astype(BF16)
            o_ref[:, vs] = _dot_nt(qe, Sb) + _dot(A, v)
            S_scr[h] = St * jnp.exp(bl) + _dot_tn(v, kd)

    return pl.pallas_call(
        body, name=name, grid=(NC,),
        in_specs=[pl.BlockSpec((C, KW), lambda i: (cidx(i), 0)), pl.BlockSpec((C, KW), lambda i: (cidx(i), 1)),
                  pl.BlockSpec((C, VW), lambda i: (cidx(i), 1)), pl.BlockSpec((C, KW), lambda i: (cidx(i), 0)),
                  pl.BlockSpec((1, KW), lambda i: (0, 0))],
        out_specs=[pl.BlockSpec((C, VW), lambda i: (cidx(i), 0)),
                   pl.BlockSpec((1, H, DV, DK), lambda i: (cidx(i), 0, 0, 0))],
        out_shape=[jax.ShapeDtypeStruct((T, VW), F32), jax.ShapeDtypeStruct((NC, H, DV, DK), BF16)],
        scratch_shapes=[pltpu.VMEM((H, DV, DK), F32)], compiler_params=_params(1))(proj, proj, proj, z, bias)


def gla_bwd(proj, z, bias, states, do, prev, *, reverse, name):
    T = proj.shape[0]
    KW, VW, DK, DV = _gla_dims(proj)
    H = GLA_HEADS
    C = _tile(T, GLA_CHUNK)
    NC = T // C
    cidx = (lambda i: i) if reverse else (lambda i: NC - 1 - i)
    last = 0 if reverse else C - 1
    has_prev = prev is not None
    odt = BF16 if has_prev else F32

    def body(*refs):
        q_ref, k_ref, v_ref, z_ref, b_ref, s_ref, do_ref = refs[:7]
        pos = 7
        if has_prev:
            pq_ref, pk_ref, pv_ref = refs[7:10]
            pos = 10
        dq_ref, dk_ref, dv_ref, dz_ref, db_ref, dS_scr = refs[pos:pos + 6]

        @pl.when(pl.program_id(0) == 0)
        def _():
            dS_scr[...] = jnp.zeros_like(dS_scr)
            db_ref[...] = jnp.zeros_like(db_ref)

        cum, cum_t, amask = _gla_masks(C, reverse)
        xg = z_ref[...] + b_ref[...]
        la = _log_sigmoid(xg) * (1.0 / GLA_TAU)
        b = _tri_sum(cum.astype(BF16), la)
        cum_t_bf = cum_t.astype(BF16)
        for h in range(H):
            ks = slice(h * DK, (h + 1) * DK)
            vs = slice(h * DV, (h + 1) * DV)
            bh = b[:, ks]
            mid = bh[C // 2:C // 2 + 1, :]
            bl = bh[last:last + 1, :]
            q = q_ref[:, ks].astype(F32) * (DK ** -0.5)
            k = k_ref[:, ks].astype(F32)
            v = v_ref[:, vs]
            dov = do_ref[:, vs]
            e_b = jnp.exp(bh)
            e_up = jnp.exp(bh - mid)
            e_dn = jnp.exp(mid - bh)
            e_l = jnp.exp(bl - bh)
            e_bl = jnp.exp(bl)
            qe = (q * e_b).astype(BF16)
            qt = (q * e_up).astype(BF16)
            kt = (k * e_dn).astype(BF16)
            kd = (k * e_l).astype(BF16)
            Sb = s_ref[0, h]
            dSt = dS_scr[h]
            dSb = dSt.astype(BF16)
            A = jnp.where(amask, _dot_nt(qt, kt), 0.0).astype(BF16)
            dA = jnp.where(amask, _dot_nt(dov, v), 0.0).astype(BF16)
            dv = _dot_tn(A, dov) + _dot_nt(kd, dSb)
            dq = _dot(dA, kt) * e_up + _dot(dov, Sb) * e_b
            dk_state = _dot(v, dSb) * e_l
            dk = _dot_tn(dA, qt) * e_dn + dk_state
            d_b = dq * q - dk * k
            d_bl = (jnp.sum(dk_state * k, axis=0, keepdims=True)
                    + e_bl * jnp.sum(dSt * Sb.astype(F32), axis=0, keepdims=True))
            d_la = _tri_sum(cum_t_bf, d_b) + d_bl
            dz = d_la * (1.0 / GLA_TAU) * jax.nn.sigmoid(-xg[:, ks])
            dq = dq * (DK ** -0.5)
            if has_prev:
                dq = dq + pq_ref[:, ks]
                dk = dk + pk_ref[:, ks]
                dv = dv + pv_ref[:, vs]
            dq_ref[:, ks] = dq.astype(odt)
            dk_ref[:, ks] = dk.astype(odt)
            dv_ref[:, vs] = dv.astype(odt)
            dz_ref[:, ks] = dz.astype(BF16)
            db_ref[:, ks] += jnp.sum(dz, axis=0, keepdims=True)
            dS_scr[h] = dSt * e_bl + _dot_tn(dov, qe)

    kspec = lambda cb: pl.BlockSpec((C, KW), lambda i: (cidx(i), cb))
    vspec = lambda cb: pl.BlockSpec((C, VW), lambda i: (cidx(i), cb))
    ins = [proj, proj, proj, z, bias, states, do]
    in_specs = [kspec(0), kspec(1), vspec(1), kspec(0), pl.BlockSpec((1, KW), lambda i: (0, 0)),
                pl.BlockSpec((1, H, DV, DK), lambda i: (cidx(i), 0, 0, 0)), vspec(0)]
    if has_prev:
        ins += list(prev)
        in_specs += [kspec(0), kspec(0), vspec(0)]
    return pl.pallas_call(
        body, name=name, grid=(NC,), in_specs=in_specs,
        out_specs=[kspec(0), kspec(0), vspec(0), kspec(0), pl.BlockSpec((1, KW), lambda i: (0, 0))],
        out_shape=[jax.ShapeDtypeStruct((T, KW), odt), jax.ShapeDtypeStruct((T, KW), odt),
                   jax.ShapeDtypeStruct((T, VW), odt), jax.ShapeDtypeStruct((T, KW), BF16),
                   jax.ShapeDtypeStruct((1, KW), F32)],
        scratch_shapes=[pltpu.VMEM((H, DV, DK), F32)], compiler_params=_params(1))(*ins)


def gla_out(o_f, o_b, proj, g_off, norm_g, *, name):
    T, VW = o_f.shape
    H = GLA_HEADS
    DV = VW // H
    gb = g_off // VW
    tr = _tile(T, 256)

    def body(f_ref, b_ref, g_ref, n_ref, y_ref):
        for h in range(H):
            vs = slice(h * DV, (h + 1) * DV)
            o = f_ref[:, vs] + b_ref[:, vs]
            rstd = lax.rsqrt(jnp.mean(o * o, axis=-1, keepdims=True) + NORM_EPS)
            y_ref[:, vs] = (o * rstd * n_ref[...] * _silu(g_ref[:, vs].astype(F32))).astype(BF16)

    return pl.pallas_call(
        body, name=name, grid=(T // tr,),
        in_specs=[_row(tr, VW), _row(tr, VW), _row(tr, VW, gb), _bc(DV)], out_specs=_row(tr, VW),
        out_shape=jax.ShapeDtypeStruct((T, VW), BF16), compiler_params=_params(1))(o_f, o_b, proj, norm_g)


def gla_out_bwd(dy2, o_f, o_b, proj, g_off, norm_g, *, name):
    T, VW = o_f.shape
    H = GLA_HEADS
    DV = VW // H
    gb = g_off // VW
    tr = _tile(T, 256)

    def body(d_ref, f_ref, b_ref, g_ref, n_ref, do_ref, dg_ref, dn_ref):
        @pl.when(pl.program_id(0) == 0)
        def _():
            dn_ref[...] = jnp.zeros_like(dn_ref)

        nv = n_ref[...]
        for h in range(H):
            vs = slice(h * DV, (h + 1) * DV)
            o = f_ref[:, vs] + b_ref[:, vs]
            rstd = lax.rsqrt(jnp.mean(o * o, axis=-1, keepdims=True) + NORM_EPS)
            r = o * rstd
            gv = g_ref[:, vs].astype(F32)
            d = d_ref[:, vs].astype(F32)
            dg_ref[:, vs] = (d * r * nv * _dsilu(gv)).astype(BF16)
            dn_o = d * _silu(gv)
            dn_ref[...] += jnp.sum(dn_o * r, axis=0, keepdims=True)
            dr = dn_o * nv
            do_ref[:, vs] = (rstd * (dr - r * jnp.mean(dr * r, axis=-1, keepdims=True))).astype(BF16)

    return pl.pallas_call(
        body, name=name, grid=(T // tr,),
        in_specs=[_row(tr, VW), _row(tr, VW), _row(tr, VW), _row(tr, VW, gb), _bc(DV)],
        out_specs=[_row(tr, VW), _row(tr, VW), _bc(DV)],
        out_shape=[jax.ShapeDtypeStruct((T, VW), BF16), jax.ShapeDtypeStruct((T, VW), BF16),
                   jax.ShapeDtypeStruct((1, DV), F32)], compiler_params=_params(1))(dy2, o_f, o_b, proj, norm_g)


def _rope_tables(T):
    hd = ATTN_HEAD_DIM
    axis_dim = hd // 2
    rows = T // GRID_W
    t = jnp.arange(T)
    row = (t // GRID_W - rows // 2).astype(F32)
    col = (t % GRID_W - GRID_W // 2).astype(F32)
    inv = ROPE_THETA ** (-jnp.arange(0, axis_dim, 2, dtype=F32) / axis_dim)
    ang = jnp.concatenate([row[:, None] * inv, col[:, None] * inv], axis=-1)
    cos = jnp.repeat(jnp.cos(ang), 2, axis=-1)
    sin = jnp.repeat(jnp.sin(ang), 2, axis=-1)
    sign = jnp.where(jnp.arange(hd) % 2 == 0, -1.0, 1.0).astype(F32)
    return cos, sin * sign


def _pair_swap(x):
    n = x.shape[-1]
    lane = lax.broadcasted_iota(jnp.int32, x.shape, x.ndim - 1)
    return jnp.where(lane % 2 == 0, pltpu.roll(x, n - 1, x.ndim - 1), pltpu.roll(x, 1, x.ndim - 1))


def _attn_dims(proj):
    hd = ATTN_HEAD_DIM
    kvw = ATTN_KV_HEADS * hd
    qw = (proj.shape[1] - 2 * kvw) // 2
    return qw, kvw, qw // hd, (qw // hd) // ATTN_KV_HEADS


LOG2E = 1.4426950408889634
LN2 = 0.6931471805599453


def _q_mult():
    return ATTN_HEAD_DIM ** -0.5 * LOG2E


def attn_prep(proj, qg, kg, cos, sin, *, name):
    T = proj.shape[0]
    QW, KVW, NH, G = _attn_dims(proj)
    hd = ATTN_HEAD_DIM
    KV = ATTN_KV_HEADS
    assert QW % KVW == 0
    tr = _tile(T, 256)

    def body(q_ref, k_ref, v_ref, qg_ref, kg_ref, c_ref, s_ref, qo_ref, ko_ref, vo_ref):
        cv, sv = c_ref[...], s_ref[...]

        def one(x, gain, mult):
            rstd = lax.rsqrt(jnp.mean(x * x, axis=-1, keepdims=True) + NORM_EPS)
            xs = x * rstd * gain
            return (xs * cv + _pair_swap(xs) * sv) * mult

        for h in range(NH):
            hs = slice(h * hd, (h + 1) * hd)
            qo_ref[h] = one(q_ref[:, hs].astype(F32), qg_ref[...], _q_mult()).astype(BF16)
        for h in range(KV):
            hs = slice(h * hd, (h + 1) * hd)
            ko_ref[:, hs] = one(k_ref[:, hs].astype(F32), kg_ref[...], 1.0).astype(BF16)
            vo_ref[:, 2 * h * hd:(2 * h + 1) * hd] = v_ref[:, hs]
            vo_ref[:, (2 * h + 1) * hd:(2 * h + 2) * hd] = jnp.ones((tr, hd), BF16)

    return pl.pallas_call(
        body, name=name, grid=(T // tr,),
        in_specs=[_row(tr, QW), _row(tr, KVW, QW // KVW), _row(tr, KVW, QW // KVW + 1), _bc(hd), _bc(hd),
                  _row(tr, hd), _row(tr, hd)],
        out_specs=[pl.BlockSpec((NH, tr, hd), lambda i: (0, i, 0)), _row(tr, KVW), _row(tr, 2 * KVW)],
        out_shape=[jax.ShapeDtypeStruct((NH, T, hd), BF16), jax.ShapeDtypeStruct((T, KVW), BF16),
                   jax.ShapeDtypeStruct((T, 2 * KVW), BF16)],
        compiler_params=_params(1))(proj, proj, proj, qg, kg, cos, sin)


def attn_prep_bwd(dqr, dkr, proj, qg, kg, cos, sin, *, name):
    T = proj.shape[0]
    QW, KVW, NH, G = _attn_dims(proj)
    hd = ATTN_HEAD_DIM
    tr = _tile(T, 256)

    def body(dq_ref, dk_ref, q_ref, k_ref, qg_ref, kg_ref, c_ref, s_ref, oq_ref, ok_ref, dqg_ref, dkg_ref):
        @pl.when(pl.program_id(0) == 0)
        def _():
            dqg_ref[...] = jnp.zeros_like(dqg_ref)
            dkg_ref[...] = jnp.zeros_like(dkg_ref)

        cv, sv = c_ref[...], s_ref[...]

        def one(d, x, gain, mult):
            d = d * mult
            dxs = d * cv - _pair_swap(d) * sv
            rstd = lax.rsqrt(jnp.mean(x * x, axis=-1, keepdims=True) + NORM_EPS)
            xn = x * rstd
            dgain = jnp.sum(dxs * xn, axis=0, keepdims=True)
            dxn = dxs * gain
            return rstd * (dxn - xn * jnp.mean(dxn * xn, axis=-1, keepdims=True)), dgain

        for h in range(NH):
            hs = slice(h * hd, (h + 1) * hd)
            dx, dgain = one(dq_ref[h].astype(F32), q_ref[:, hs].astype(F32), qg_ref[...], _q_mult())
            oq_ref[:, hs] = dx.astype(BF16)
            dqg_ref[...] += dgain
        for h in range(ATTN_KV_HEADS):
            hs = slice(h * hd, (h + 1) * hd)
            dx, dgain = one(dk_ref[:, hs].astype(F32), k_ref[:, hs].astype(F32), kg_ref[...], 1.0)
            ok_ref[:, hs] = dx.astype(BF16)
            dkg_ref[...] += dgain

    return pl.pallas_call(
        body, name=name, grid=(T // tr,),
        in_specs=[pl.BlockSpec((NH, tr, hd), lambda i: (0, i, 0)), _row(tr, KVW), _row(tr, QW),
                  _row(tr, KVW, QW // KVW), _bc(hd), _bc(hd), _row(tr, hd), _row(tr, hd)],
        out_specs=[_row(tr, QW), _row(tr, KVW), _bc(hd), _bc(hd)],
        out_shape=[jax.ShapeDtypeStruct((T, QW), BF16), jax.ShapeDtypeStruct((T, KVW), BF16),
                   jax.ShapeDtypeStruct((1, hd), F32), jax.ShapeDtypeStruct((1, hd), F32)],
        compiler_params=_params(1))(dqr, dkr, proj, proj, qg, kg, cos, sin)


FLASH_BQ = 256
FLASH_BK = 512


def flash_fwd(q, kr, vext, *, name):
    NH, T, hd = q.shape
    KV = ATTN_KV_HEADS
    G = NH // KV
    bq = _tile(T, FLASH_BQ)
    bk = _tile(T, FLASH_BK)
    nk = T // bk

    def body(q_ref, k_ref, v_ref, o_ref, lse_ref, m_scr, acc_scr):
        m_scr[...] = jnp.full_like(m_scr, -jnp.inf)
        acc_scr[...] = jnp.zeros_like(acc_scr)

        def chunk(c, carry):
            off = pl.multiple_of(c * bk, bk)
            kc = k_ref[pl.ds(off, bk), :]
            vc = v_ref[pl.ds(off, bk), :]
            for g in range(G):
                s = _dot_nt(q_ref[g], kc)
                m_old = m_scr[g]
                m_new = jnp.maximum(m_old, jnp.max(s, axis=-1, keepdims=True))
                p = jnp.exp2(s - m_new)
                acc_scr[g] = jnp.exp2(m_old - m_new) * acc_scr[g] + _dot(p.astype(BF16), vc)
                m_scr[g] = m_new
            return carry

        lax.fori_loop(0, nk, chunk, 0)
        for g in range(G):
            a = acc_scr[g]
            l = a[:, hd:]
            o_ref[g] = a[:, :hd] / l
            lse_ref[g] = m_scr[g] + jnp.log2(l[:, 0:1])

    return pl.pallas_call(
        body, name=name, grid=(KV, T // bq),
        in_specs=[pl.BlockSpec((G, bq, hd), lambda h, i: (h, i, 0)), pl.BlockSpec((T, hd), lambda h, i: (0, h)),
                  pl.BlockSpec((T, 2 * hd), lambda h, i: (0, h))],
        out_specs=[pl.BlockSpec((G, bq, hd), lambda h, i: (h, i, 0)),
                   pl.BlockSpec((G, bq, 1), lambda h, i: (h, i, 0))],
        out_shape=[jax.ShapeDtypeStruct((NH, T, hd), F32), jax.ShapeDtypeStruct((NH, T, 1), F32)],
        scratch_shapes=[pltpu.VMEM((G, bq, 1), F32), pltpu.VMEM((G, bq, 2 * hd), F32)],
        compiler_params=_params(2))(q, kr, vext)


def flash_bwd(q, kr, proj, v_off, do, lse, delta, *, name):
    NH, T, hd = q.shape
    KV = ATTN_KV_HEADS
    G = NH // KV
    bq = _tile(T, FLASH_BQ)
    bk = _tile(T, FLASH_BK)
    nk = T // bk
    nq = T // bq
    vb = v_off // hd

    def body(q_ref, k_ref, v_ref, do_ref, lse_ref, dl_ref, dq_ref, dk_ref, dv_ref, dq_scr, dk_scr, dv_scr):
        i = pl.program_id(1)

        @pl.when(i == 0)
        def _():
            dk_scr[...] = jnp.zeros_like(dk_scr)
            dv_scr[...] = jnp.zeros_like(dv_scr)

        dq_scr[...] = jnp.zeros_like(dq_scr)

        def chunk(c, carry):
            off = pl.multiple_of(c * bk, bk)
            kc = k_ref[pl.ds(off, bk), :]
            vc = v_ref[pl.ds(off, bk), :]
            dk_c = jnp.zeros((bk, hd), F32)
            dv_c = jnp.zeros((bk, hd), F32)
            for g in range(G):
                qv = q_ref[g]
                dov = do_ref[g]
                p = jnp.exp2(_dot_nt(qv, kc) - lse_ref[g])
                ds = (p * (_dot_nt(dov, vc) - dl_ref[g])).astype(BF16)
                dq_scr[g] += _dot(ds, kc)
                dv_c = dv_c + _dot_tn(p.astype(BF16), dov)
                dk_c = dk_c + _dot_tn(ds, qv)
            dk_scr[pl.ds(off, bk), :] += dk_c
            dv_scr[pl.ds(off, bk), :] += dv_c
            return carry

        lax.fori_loop(0, nk, chunk, 0)
        dq_ref[...] = dq_scr[...].astype(BF16)

        @pl.when(i == nq - 1)
        def _():
            dk_ref[...] = dk_scr[...].astype(BF16)
            dv_ref[...] = (dv_scr[...] * LOG2E).astype(BF16)

    qspec = pl.BlockSpec((G, bq, hd), lambda h, i: (h, i, 0))
    cspec = pl.BlockSpec((G, bq, 1), lambda h, i: (h, i, 0))
    kspec = pl.BlockSpec((T, hd), lambda h, i: (0, h))
    return pl.pallas_call(
        body, name=name, grid=(KV, nq),
        in_specs=[qspec, kspec, pl.BlockSpec((T, hd), lambda h, i: (0, vb + h)), qspec, cspec, cspec],
        out_specs=[qspec, kspec, kspec],
        out_shape=[jax.ShapeDtypeStruct((NH, T, hd), BF16), jax.ShapeDtypeStruct((T, KV * hd), BF16),
                   jax.ShapeDtypeStruct((T, KV * hd), BF16)],
        scratch_shapes=[pltpu.VMEM((G, bq, hd), F32), pltpu.VMEM((T, hd), F32), pltpu.VMEM((T, hd), F32)],
        compiler_params=_params(2))(q, kr, proj, do, lse, delta)


def attn_gate(o, proj, g_off, *, name):
    NH, T, hd = o.shape
    W = NH * hd
    cw = _col_tile(W, g_off)
    hc = cw // hd
    gb = g_off // cw
    tr = _tile(T, 256)

    def body(o_ref, g_ref, y_ref):
        for h in range(hc):
            hs = slice(h * hd, (h + 1) * hd)
            y_ref[:, hs] = (o_ref[h] * _silu(g_ref[:, hs].astype(F32))).astype(BF16)

    return pl.pallas_call(
        body, name=name, grid=(W // cw, T // tr),
        in_specs=[pl.BlockSpec((hc, tr, hd), lambda j, i: (j, i, 0)), pl.BlockSpec((tr, cw), lambda j, i: (i, gb + j))],
        out_specs=pl.BlockSpec((tr, cw), lambda j, i: (i, j)),
        out_shape=jax.ShapeDtypeStruct((T, W), BF16), compiler_params=_params(2))(o, proj)


def attn_gate_bwd(dy2, o, proj, g_off, *, name):
    NH, T, hd = o.shape
    W = NH * hd
    cw = _col_tile(W, g_off)
    hc = cw // hd
    gb = g_off // cw
    tr = _tile(T, 256)

    def body(d_ref, o_ref, g_ref, do_ref, dg_ref, dl_ref):
        for h in range(hc):
            hs = slice(h * hd, (h + 1) * hd)
            d = d_ref[:, hs].astype(F32)
            gv = g_ref[:, hs].astype(F32)
            ov = o_ref[h]
            dov = d * _silu(gv) * LN2
            do_ref[h] = dov.astype(BF16)
            dg_ref[:, hs] = (d * ov * _dsilu(gv)).astype(BF16)
            dl_ref[h] = jnp.sum(dov * ov, axis=-1, keepdims=True)

    return pl.pallas_call(
        body, name=name, grid=(W // cw, T // tr),
        in_specs=[pl.BlockSpec((tr, cw), lambda j, i: (i, j)), pl.BlockSpec((hc, tr, hd), lambda j, i: (j, i, 0)),
                  pl.BlockSpec((tr, cw), lambda j, i: (i, gb + j))],
        out_specs=[pl.BlockSpec((hc, tr, hd), lambda j, i: (j, i, 0)), pl.BlockSpec((tr, cw), lambda j, i: (i, j)),
                   pl.BlockSpec((hc, tr, 1), lambda j, i: (j, i, 0))],
        out_shape=[jax.ShapeDtypeStruct((NH, T, hd), BF16), jax.ShapeDtypeStruct((T, W), BF16),
                   jax.ShapeDtypeStruct((NH, T, 1), F32)], compiler_params=_params(2))(dy2, o, proj)


def outer_silu(c_t, dm, *, name):
    K, B = c_t.shape
    N = dm.shape[1]
    tk = _tile(K, 256)

    def body(c_ref, d_ref, o_ref):
        s = _silu(c_ref[...])
        acc = jnp.zeros((tk, N), F32)
        for b in range(B):
            acc = acc + s[:, b:b + 1] * d_ref[b:b + 1, :]
        o_ref[...] = acc

    return pl.pallas_call(
        body, name=name, grid=(K // tk,),
        in_specs=[pl.BlockSpec((tk, B), lambda i: (i, 0)), pl.BlockSpec((B, N), lambda i: (0, 0))],
        out_specs=pl.BlockSpec((tk, N), lambda i: (i, 0)),
        out_shape=jax.ShapeDtypeStruct((K, N), F32), compiler_params=_params(1))(c_t, dm)


def sum_slots(x, *, name):
    S, R, C = x.shape
    tr = _tile(R, 512)

    def body(x_ref, o_ref):
        acc = x_ref[0].astype(F32)
        for s in range(1, S):
            acc = acc + x_ref[s].astype(F32)
        o_ref[...] = acc

    return pl.pallas_call(
        body, name=name, grid=(R // tr,), in_specs=[pl.BlockSpec((S, tr, C), lambda i: (0, i, 0))],
        out_specs=pl.BlockSpec((tr, C), lambda i: (i, 0)),
        out_shape=jax.ShapeDtypeStruct((R, C), F32), compiler_params=_params(1))(x)


def adamw(w, g, m, v, *, name):
    R, C = w.shape
    tr = _tile(R, 512) if R % 8 == 0 else R

    def body(w_ref, g_ref, m_ref, v_ref, d_ref, nm_ref, nv_ref):
        gv = g_ref[...]
        mn = ADAM_B1 * m_ref[...] + (1.0 - ADAM_B1) * gv
        vn = ADAM_B2 * v_ref[...] + (1.0 - ADAM_B2) * jnp.square(gv)
        m_hat = mn / (1.0 - ADAM_B1 ** ADAM_STEP)
        v_hat = vn / (1.0 - ADAM_B2 ** ADAM_STEP)
        d_ref[...] = -ADAM_LR * (m_hat / (jnp.sqrt(v_hat) + ADAM_EPS) + ADAM_WD * w_ref[...])
        nm_ref[...] = mn
        nv_ref[...] = vn

    blk = pl.BlockSpec((tr, C), lambda i: (i, 0))
    sh = jax.ShapeDtypeStruct((R, C), F32)
    return pl.pallas_call(
        body, name=name, grid=(R // tr,), in_specs=[blk] * 4, out_specs=[blk] * 3, out_shape=[sh] * 3,
        compiler_params=_params(1))(w, g, m, v)


def _adamw_nd(w, g, m, v, name):
    shp = w.shape
    if w.ndim == 1:
        two = (1, shp[0])
    else:
        two = (math.prod(shp[:-1]), shp[-1])
    d, nm, nv = adamw(w.reshape(two), g.reshape(two), m.reshape(two), v.reshape(two), name=name)
    return d.reshape(shp), nm.reshape(shp), nv.reshape(shp)


def _pad_cols(w, n):
    return jnp.pad(w, ((0, 0), (0, n - w.shape[1])))


def _pad_rows(w, n):
    return jnp.pad(w, ((0, n - w.shape[0]), (0, 0)))


def kernel(x, c, w_mod, b_mod, pool_w_in, pool_w_grp, pool_scale, pool_w_out, gla_w_in, gla_fwd_w1, gla_fwd_w2, gla_fwd_b, gla_bwd_w1, gla_bwd_w2, gla_bwd_b, gla_norm_g, gla_w_out, attn_w_in, attn_q_norm_g, attn_k_norm_g, attn_w_out, final_norm_g, loss_target, m_w_mod, m_b_mod, m_pool_w_in, m_pool_w_grp, m_pool_scale, m_pool_w_out, m_gla_w_in, m_gla_fwd_w1, m_gla_fwd_w2, m_gla_fwd_b, m_gla_bwd_w1, m_gla_bwd_w2, m_gla_bwd_b, m_gla_norm_g, m_gla_w_out, m_attn_w_in, m_attn_q_norm_g, m_attn_k_norm_g, m_attn_w_out, m_final_norm_g, v_w_mod, v_b_mod, v_pool_w_in, v_pool_w_grp, v_pool_scale, v_pool_w_out, v_gla_w_in, v_gla_fwd_w1, v_gla_fwd_w2, v_gla_fwd_b, v_gla_bwd_w1, v_gla_bwd_w2, v_gla_bwd_b, v_gla_norm_g, v_gla_w_out, v_attn_w_in, v_attn_q_norm_g, v_attn_k_norm_g, v_attn_w_out, v_final_norm_g):
    W = dict(w_mod=w_mod, b_mod=b_mod, pool_w_in=pool_w_in, pool_w_grp=pool_w_grp, pool_scale=pool_scale,
             pool_w_out=pool_w_out, gla_w_in=gla_w_in, gla_fwd_w1=gla_fwd_w1, gla_fwd_w2=gla_fwd_w2,
             gla_fwd_b=gla_fwd_b, gla_bwd_w1=gla_bwd_w1, gla_bwd_w2=gla_bwd_w2, gla_bwd_b=gla_bwd_b,
             gla_norm_g=gla_norm_g, gla_w_out=gla_w_out, attn_w_in=attn_w_in, attn_q_norm_g=attn_q_norm_g,
             attn_k_norm_g=attn_k_norm_g, attn_w_out=attn_w_out, final_norm_g=final_norm_g)
    M = dict(w_mod=m_w_mod, b_mod=m_b_mod, pool_w_in=m_pool_w_in, pool_w_grp=m_pool_w_grp, pool_scale=m_pool_scale,
             pool_w_out=m_pool_w_out, gla_w_in=m_gla_w_in, gla_fwd_w1=m_gla_fwd_w1, gla_fwd_w2=m_gla_fwd_w2,
             gla_fwd_b=m_gla_fwd_b, gla_bwd_w1=m_gla_bwd_w1, gla_bwd_w2=m_gla_bwd_w2, gla_bwd_b=m_gla_bwd_b,
             gla_norm_g=m_gla_norm_g, gla_w_out=m_gla_w_out, attn_w_in=m_attn_w_in, attn_q_norm_g=m_attn_q_norm_g,
             attn_k_norm_g=m_attn_k_norm_g, attn_w_out=m_attn_w_out, final_norm_g=m_final_norm_g)
    V = dict(w_mod=v_w_mod, b_mod=v_b_mod, pool_w_in=v_pool_w_in, pool_w_grp=v_pool_w_grp, pool_scale=v_pool_scale,
             pool_w_out=v_pool_w_out, gla_w_in=v_gla_w_in, gla_fwd_w1=v_gla_fwd_w1, gla_fwd_w2=v_gla_fwd_w2,
             gla_fwd_b=v_gla_fwd_b, gla_bwd_w1=v_gla_bwd_w1, gla_bwd_w2=v_gla_bwd_w2, gla_bwd_b=v_gla_bwd_b,
             gla_norm_g=v_gla_norm_g, gla_w_out=v_gla_w_out, attn_w_in=v_attn_w_in, attn_q_norm_g=v_attn_q_norm_g,
             attn_k_norm_g=v_attn_k_norm_g, attn_w_out=v_attn_w_out, final_norm_g=v_final_norm_g)

    me = _my_id()
    T, D = x.shape[1], x.shape[2]
    x0 = x.reshape(T, D)
    target = loss_target.reshape(T, D)
    BW = D
    n_pool, n_gla, n_attn = pool_w_in.shape[0], gla_w_in.shape[0], attn_w_in.shape[0]
    LR = GLA_LOWRANK
    LRP = LANES
    KWg = gla_fwd_w2.shape[2] * NDEV
    Dm = w_mod.shape[2]

    small_shard_names = ['pool_scale', 'gla_fwd_w1', 'gla_fwd_w2', 'gla_bwd_w1', 'gla_bwd_w2']
    small_items = [c] + [W[n] for n in small_shard_names]
    small_shapes = [a.shape for a in small_items]
    g1 = exchange(GATHER, _pack(small_items, F32), name="gather_small")
    c_all_, ps_all, fw1_all, fw2_all, bw1_all, bw2_all = _unpack(g1, small_shapes, lead=NDEV)
    c_all = c_all_.reshape(NDEV, D)
    pool_scale_full = jnp.transpose(ps_all, (1, 0, 2)).reshape(n_pool, BW)
    w1_full = {'f': fw1_all.transpose(1, 0, 2, 3).reshape(n_gla, D, LR),
               'b': bw1_all.transpose(1, 0, 2, 3).reshape(n_gla, D, LR)}
    w2_full = {'f': fw2_all.transpose(1, 2, 0, 3).reshape(n_gla, LR, KWg),
               'b': bw2_all.transpose(1, 2, 0, 3).reshape(n_gla, LR, KWg)}

    c16 = _pad_rows(c_all, 16)
    b_slab = lax.dynamic_slice_in_dim(b_mod, me * Dm, Dm, axis=1)
    mod_parts = [mm(c16, w_mod[i], name=f"mod_fwd{i}", out_dtype=F32, a_silu=True, tm=16, tn=Dm,
                    bias=b_slab[i:i + 1])[:NDEV] for i in range(DEPTH)]
    mod_slab = jnp.stack(mod_parts)
    g2 = exchange(GATHER, _pack([mod_slab], F32), name="gather_mod")
    (mod_all,) = _unpack(g2, [mod_slab.shape], lead=NDEV)
    mod_mine = lax.dynamic_index_in_dim(mod_all, me, axis=2, keepdims=False)
    mod_mine = mod_mine.transpose(1, 0, 2).reshape(DEPTH, NDEV * Dm)
    shift = [mod_mine[i:i + 1, 0:D] for i in range(DEPTH)]
    scale = [mod_mine[i:i + 1, D:2 * D] for i in range(DEPTH)]
    gate = [mod_mine[i:i + 1, 2 * D:3 * D] for i in range(DEPTH)]

    layer_names = {0: ['pool_w_in', 'pool_w_grp', 'pool_w_out'], 1: ['gla_w_in', 'gla_w_out'],
                   2: ['attn_w_in', 'attn_w_out']}

    def layer_shards(i):
        return [W[n][i // N_MIXERS] for n in layer_names[i % N_MIXERS]]

    def layer_weights(i, gathered):
        parts = _unpack(gathered, [a.shape for a in layer_shards(i)], lead=NDEV)
        out = dict(w_in=parts[0], w_out=parts[-1].reshape(BW, D))
        if i % N_MIXERS == 0:
            g = parts[1]
            out['w_grp'] = g.transpose(1, 0, 2, 3).reshape(g.shape[1], -1, g.shape[3])
        return out

    gathered = exchange(GATHER, _pack(layer_shards(0), BF16), name="gather_w0")

    cos, sin = _rope_tables(T)

    xs = [x0]
    saved = []
    xi = x0
    for i in range(DEPTH):
        kind, j = i % N_MIXERS, i // N_MIXERS
        h = prenorm(xi, scale[i], shift[i], name=f"prenorm{i}")
        lw = layer_weights(i, gathered)
        sv = dict(h=h, lw=lw)
        in_name = ("pool_in", "gla_in", "attn_in")[kind] + str(i)
        if i + 1 < DEPTH:
            proj, gathered = mm(h, lw['w_in'], b_split=True, name=in_name,
                                side=(GATHER, _pack(layer_shards(i + 1), BF16)))
        else:
            proj = mm(h, lw['w_in'], b_split=True, name=in_name)
        if kind == 0:
            ug = proj
            pooled = band(ug, BW, transpose=False, name=f"pool_band{i}")
            zz = gmm_nn(pooled, lw['w_grp'], name=f"pool_grp{i}")
            y2 = gated(zz, ug, BW, pool_scale_full[j:j + 1], name=f"pool_gate{i}")
            sv.update(ug=ug, pooled=pooled, z=zz)
        elif kind == 1:
            sv.update(proj=proj)
            for dname in ('f', 'b'):
                w1p = _pad_cols(w1_full[dname][j], LRP)
                w2p = _pad_rows(w2_full[dname][j], LRP)
                bias = (gla_fwd_b if dname == 'f' else gla_bwd_b)[j:j + 1]
                hw1 = mm(h, w1p, name=f"gla_w1{dname}{i}", tn=LRP)
                zg = mm(hw1, w2p, name=f"gla_w2{dname}{i}", out_dtype=F32, tk=LRP)
                o_d, st_d = gla_fwd(proj, zg, bias, reverse=(dname == 'b'), name=f"gla_scan_{dname}{i}")
                sv.update({f"hw1{dname}": hw1, f"z{dname}": zg, f"o{dname}": o_d, f"st{dname}": st_d,
                           f"w1p{dname}": w1p, f"w2p{dname}": w2p, f"bias{dname}": bias})
            y2 = gla_out(sv['of'], sv['ob'], proj, 2 * BW, gla_norm_g[j:j + 1], name=f"gla_out{i}")
        else:
            QW, KVW, _, _ = _attn_dims(proj)
            qr, kr, vext = attn_prep(proj, attn_q_norm_g[j:j + 1], attn_k_norm_g[j:j + 1], cos, sin,
                                     name=f"attn_prep{i}")
            o, lse = flash_fwd(qr, kr, vext, name=f"attn_flash{i}")
            y2 = attn_gate(o, proj, QW + 2 * KVW, name=f"attn_gate{i}")
            sv.update(proj=proj, qr=qr, kr=kr, o=o, lse=lse)
        y = mm(y2, lw['w_out'], name=f"out_proj{i}")
        sv.update(y2=y2, y=y)
        saved.append(sv)
        xi = resid(xi, y, gate[i], name=f"resid{i}")
        xs.append(xi)

    loss_part, dx, d_final_g = loss_head(xi, final_norm_g.reshape(1, D), target, name="loss_head")

    d_mod = [None] * DEPTH
    layer_grads = [None] * DEPTH
    received = [None] * DEPTH
    small_grads = {}

    def in_dx(i, dproj, w_in, **kw):
        name = ("pool_in_dx", "gla_in_dx", "attn_in_dx")[i % N_MIXERS] + str(i)
        if i + 1 < DEPTH:
            dh, received[i + 1] = mm(dproj, w_in, tb=True, b_split=True, name=name,
                                     side=(SCATTER, _pack(layer_grads[i + 1], BF16, lead=NDEV)), **kw)
            return dh
        return mm(dproj, w_in, tb=True, b_split=True, name=name, **kw)

    for i in reversed(range(DEPTH)):
        kind, j = i % N_MIXERS, i // N_MIXERS
        sv = saved[i]
        h, lw = sv['h'], sv['lw']
        dy, d_gate = resid_bwd(dx, sv['y'], gate[i], name=f"resid_bwd{i}")
        dy2 = mm(dy, lw['w_out'], tb=True, name=f"out_proj_dx{i}")
        g_wout = mm(sv['y2'], dy, ta=True, name=f"out_proj_dw{i}").reshape(NDEV, BW // NDEV, D)
        if kind == 0:
            ug = sv['ug']
            dz, dg, d_ps = gated_bwd(dy2, sv['z'], ug, BW, pool_scale_full[j:j + 1], name=f"pool_gate_bwd{i}")
            dpooled = gmm_nn(dz, lw['w_grp'], tb=True, name=f"pool_grp_dx{i}")
            g_grp = gmm_tn(sv['pooled'], dz, len(POOL_WINDOWS), name=f"pool_grp_dw{i}")
            du = band(dpooled, BW, transpose=True, name=f"pool_band_bwd{i}")
            dproj = jnp.concatenate([du, dg], axis=1)
            dh = in_dx(i, dproj, lw['w_in'])
            g_win = mm(h, dproj, ta=True, out_split=pool_w_in.shape[2], name=f"pool_in_dw{i}")
            Gp, Cg = g_grp.shape[0], g_grp.shape[1]
            g_grp = g_grp.astype(BF16).reshape(Gp, NDEV, Cg // NDEV, Cg).transpose(1, 0, 2, 3)
            layer_grads[i] = [g_win, g_grp, g_wout]
            small_grads.setdefault('pool_scale', [None] * n_pool)[j] = d_ps
        elif kind == 1:
            proj = sv['proj']
            do, dg, d_ng = gla_out_bwd(dy2, sv['of'], sv['ob'], proj, 2 * BW, gla_norm_g[j:j + 1],
                                       name=f"gla_out_bwd{i}")
            prev = None
            dh_acc = None
            for dname in ('f', 'b'):
                dq, dk, dv, dzg, dbias = gla_bwd(proj, sv[f"z{dname}"], sv[f"bias{dname}"], sv[f"st{dname}"], do,
                                                 prev, reverse=(dname == 'b'), name=f"gla_scan_bwd_{dname}{i}")
                prev = (dq, dk, dv)
                dhw1 = mm(dzg, sv[f"w2p{dname}"], tb=True, name=f"gla_w2{dname}_dx{i}", tn=LRP)
                g_w2 = mm(sv[f"hw1{dname}"], dzg, ta=True, out_dtype=F32, name=f"gla_w2{dname}_dw{i}", tm=LRP)
                g_w1 = mm(h, dhw1, ta=True, out_dtype=F32, name=f"gla_w1{dname}_dw{i}", tn=LRP)
                dh_acc = mm(dhw1, sv[f"w1p{dname}"], tb=True, add=dh_acc, name=f"gla_w1{dname}_dx{i}", tk=LRP)
                key = 'gla_fwd' if dname == 'f' else 'gla_bwd'
                small_grads[key + '_w1'] = g_w1[:, :LR]
                small_grads[key + '_w2'] = g_w2[:LR]
                small_grads[key + '_b'] = dbias
            dproj = jnp.concatenate([prev[0], prev[1], prev[2], dg], axis=1)
            dh = in_dx(i, dproj, lw['w_in'], add=dh_acc)
            layer_grads[i] = [mm(h, dproj, ta=True, out_split=gla_w_in.shape[2], name=f"gla_in_dw{i}"), g_wout]
            small_grads['gla_norm_g'] = d_ng
        else:
            proj = sv['proj']
            QW, KVW, _, _ = _attn_dims(proj)
            do, dg, delta = attn_gate_bwd(dy2, sv['o'], proj, QW + 2 * KVW, name=f"attn_gate_bwd{i}")
            dqr, dkr, dv = flash_bwd(sv['qr'], sv['kr'], proj, QW + KVW, do, sv['lse'], delta,
                                     name=f"attn_flash_bwd{i}")
            dq, dk, d_qg, d_kg = attn_prep_bwd(dqr, dkr, proj, attn_q_norm_g[j:j + 1], attn_k_norm_g[j:j + 1],
                                               cos, sin, name=f"attn_prep_bwd{i}")
            dproj = jnp.concatenate([dq, dk, dv, dg], axis=1)
            dh = in_dx(i, dproj, lw['w_in'])
            layer_grads[i] = [mm(h, dproj, ta=True, out_split=attn_w_in.shape[2], name=f"attn_in_dw{i}"), g_wout]
            small_grads['attn_q_norm_g'] = d_qg
            small_grads['attn_k_norm_g'] = d_kg
        dx, d_scale, d_shift = prenorm_bwd(xs[i], dh, dx, scale[i], name=f"prenorm_bwd{i}")
        d_mod[i] = jnp.concatenate([d_shift, d_scale, d_gate], axis=1)
    grad_x = dx.reshape(1, T, D)

    received[0] = exchange(SCATTER, _pack(layer_grads[0], BF16, lead=NDEV), name="exchange_g0")
    per_name = {}
    for i in range(DEPTH):
        summed = sum_slots(received[i], name=f"sum_g{i}")
        parts = _unpack(summed, [a.shape for a in layer_shards(i)])
        for n, g in zip(layer_names[i % N_MIXERS], parts):
            per_name.setdefault(n, []).append(g)
    big_g = {n: jnp.stack(gs) for n, gs in per_name.items()}

    small_order = ['b_mod', 'gla_fwd_b', 'gla_bwd_b', 'gla_norm_g', 'attn_q_norm_g', 'attn_k_norm_g', 'final_norm_g',
                   'pool_scale', 'gla_fwd_w1', 'gla_fwd_w2', 'gla_bwd_w1', 'gla_bwd_w2']
    small_grads['b_mod'] = jnp.concatenate(d_mod, axis=0)
    small_grads['final_norm_g'] = d_final_g
    small_grads['pool_scale'] = jnp.concatenate(small_grads['pool_scale'], axis=0)
    part_items = [jnp.pad(loss_part.reshape(1), (0, LANES - 1))] + [small_grads[n] for n in small_order]
    part_shapes = [a.shape for a in part_items]
    g4 = exchange(GATHER, _pack(part_items, F32), name="gather_parts")
    tot = dict(zip(['loss'] + small_order, _unpack(sum_slots(g4, name="sum_parts"), part_shapes)))
    loss = tot['loss'][0]
    d_mod_all = _unpack(g4, part_shapes, lead=NDEV)[1]

    grads = {}
    grads.update(big_g)
    grads['b_mod'] = tot['b_mod']
    for n in ('gla_fwd_b', 'gla_bwd_b', 'gla_norm_g', 'attn_q_norm_g', 'attn_k_norm_g'):
        grads[n] = tot[n].reshape(W[n].shape)
    grads['final_norm_g'] = tot['final_norm_g'].reshape(D)
    ps_n = pool_scale.shape[1]
    grads['pool_scale'] = lax.dynamic_slice_in_dim(tot['pool_scale'], me * ps_n, ps_n, axis=1)
    rows = gla_fwd_w1.shape[1]
    cols = gla_fwd_w2.shape[2]
    for key in ('gla_fwd', 'gla_bwd'):
        grads[key + '_w1'] = lax.dynamic_slice_in_dim(tot[key + '_w1'], me * rows, rows, axis=0).reshape(1, rows, LR)
        grads[key + '_w2'] = lax.dynamic_slice_in_dim(tot[key + '_w2'], me * cols, cols, axis=1).reshape(1, LR, cols)

    c_t = c_all.T
    dm_slab = lax.dynamic_slice_in_dim(d_mod_all, me * Dm, Dm, axis=2)
    grads['w_mod'] = jnp.stack([outer_silu(c_t, dm_slab[:, i], name=f"mod_dw{i}") for i in range(DEPTH)])

    deltas, new_m, new_v = {}, {}, {}
    for n in WEIGHTS:
        deltas[n], new_m[n], new_v[n] = _adamw_nd(W[n], grads[n], M[n], V[n], name=f"adamw_{n}")

    return (loss, grad_x, *[grads[n] for n in WEIGHTS], *[deltas[n] for n in WEIGHTS],
            *[new_m[n] for n in WEIGHTS], *[new_v[n] for n in WEIGHTS])
```

```python
import functools
import math

import jax
import jax.numpy as jnp
from jax import lax
from jax.experimental import pallas as pl
from jax.experimental.pallas import tpu as pltpu

F32 = jnp.float32
BF16 = jnp.bfloat16
NDEV = 8
LANES = 128
VMEM_LIMIT = 56 * 1024 * 1024

D_MODEL = 2048
DEPTH = 4
N_MIXERS = 3
GRID_W = 64
NORM_EPS = 1e-6
POOL_WINDOWS = (2, 4, 8, 16)
GLA_HEADS = 4
GLA_LOWRANK = 16
GLA_TAU = 16.0
GLA_CHUNK = 64
ATTN_HEAD_DIM = 128
ATTN_KV_HEADS = 4
ROPE_THETA = 10000.0
ADAM_LR = 0.001
ADAM_B1 = 0.9
ADAM_B2 = 0.999
ADAM_EPS = 1e-08
ADAM_WD = 0.01
ADAM_STEP = 10

WEIGHTS = ['w_mod', 'b_mod', 'pool_w_in', 'pool_w_grp', 'pool_scale', 'pool_w_out', 'gla_w_in', 'gla_fwd_w1',
           'gla_fwd_w2', 'gla_fwd_b', 'gla_bwd_w1', 'gla_bwd_w2', 'gla_bwd_b', 'gla_norm_g', 'gla_w_out',
           'attn_w_in', 'attn_q_norm_g', 'attn_k_norm_g', 'attn_w_out', 'final_norm_g']


def _params(n_axes=0):
    sem = ("arbitrary",) * n_axes if n_axes else None
    return pltpu.CompilerParams(dimension_semantics=sem, vmem_limit_bytes=VMEM_LIMIT)


def _silu(g):
    return g * jax.nn.sigmoid(g)


def _dsilu(g):
    s = jax.nn.sigmoid(g)
    return s * (1.0 + g * (1.0 - s))


def _dot(a, b):
    return jnp.dot(a, b, preferred_element_type=F32)


def _dot_nt(a, b):
    return lax.dot_general(a, b, (((1,), (1,)), ((), ())), preferred_element_type=F32)


def _dot_tn(a, b):
    return lax.dot_general(a, b, (((0,), (0,)), ((), ())), preferred_element_type=F32)


def _tile(n, pref):
    if n <= pref:
        return n
    for step in (128, 16, 8):
        t = pref - pref % step
        while t >= step:
            if n % t == 0:
                return t
            t -= step
    raise ValueError((n, pref))


def _peer(k):
    x, y, c = lax.axis_index("x"), lax.axis_index("y"), lax.axis_index("c")
    px = 1 - x if k & 4 else x
    py = 1 - y if k & 2 else y
    pc = 1 - c if k & 1 else c
    return (px, py, pc), 4 * px + 2 * py + pc


def _my_id():
    return 4 * lax.axis_index("x") + 2 * lax.axis_index("y") + lax.axis_index("c")


GATHER = "gather"
SCATTER = "scatter"

EXCHANGE_SCRATCH = [pltpu.SemaphoreType.DMA((NDEV - 1,)), pltpu.SemaphoreType.DMA((NDEV - 1,)),
                    pltpu.SemaphoreType.DMA]


def _exchange_copies(kind, x_ref, o_ref, send_sems, recv_sems, local_sem, incoming):
    me = _my_id()
    copies = []
    if not incoming:
        copies.append(pltpu.make_async_copy(x_ref if kind == GATHER else x_ref.at[me], o_ref.at[me], local_sem))
    for k in range(1, NDEV):
        peer, pid = _peer(k)
        copies.append(pltpu.make_async_remote_copy(
            src_ref=x_ref if kind == GATHER else x_ref.at[pid], dst_ref=o_ref.at[pid if incoming else me],
            send_sem=send_sems.at[k - 1], recv_sem=recv_sems.at[k - 1], device_id=peer,
            device_id_type=pl.DeviceIdType.MESH))
    return copies


def _exchange_start(*refs):
    for cp in _exchange_copies(*refs, incoming=False):
        cp.start()


def _exchange_wait(*refs):
    for cp in _exchange_copies(*refs, incoming=True):
        cp.wait_recv()
    local, *sends = _exchange_copies(*refs, incoming=False)
    for cp in sends:
        cp.wait_send()
    local.wait()


def exchange(kind, x, name):
    R, C = x.shape[-2:]

    def body(x_ref, o_ref, send_sems, recv_sems, local_sem):
        _exchange_start(kind, x_ref, o_ref, send_sems, recv_sems, local_sem)
        _exchange_wait(kind, x_ref, o_ref, send_sems, recv_sems, local_sem)

    return pl.pallas_call(
        body, name=name,
        out_shape=jax.ShapeDtypeStruct((NDEV, R, C), x.dtype),
        in_specs=[pl.BlockSpec(memory_space=pl.ANY)],
        out_specs=pl.BlockSpec(memory_space=pl.ANY),
        scratch_shapes=EXCHANGE_SCRATCH,
    )(x)


def _pack(arrs, dtype, lead=None):
    unit = 16 * LANES
    if lead is None:
        flat = [a.astype(dtype).reshape(-1) for a in arrs]
        n = sum(f.shape[0] for f in flat)
        pad = (-n) % unit
        if pad:
            flat.append(jnp.zeros((pad,), dtype))
        return jnp.concatenate(flat).reshape(-1, LANES)
    flat = [a.astype(dtype).reshape(lead, -1) for a in arrs]
    n = sum(f.shape[1] for f in flat)
    pad = (-n) % unit
    if pad:
        flat.append(jnp.zeros((lead, pad), dtype))
    return jnp.concatenate(flat, axis=1).reshape(lead, -1, LANES)


def _unpack(buf, shapes, lead=None):
    out = []
    off = 0
    if lead is None:
        flat = buf.reshape(-1)
        for s in shapes:
            n = math.prod(s)
            out.append(flat[off:off + n].reshape(s))
            off += n
        return out
    flat = buf.reshape(lead, -1)
    for s in shapes:
        n = math.prod(s)
        out.append(flat[:, off:off + n].reshape((lead,) + tuple(s)))
        off += n
    return out


def mm(a, b, *, name, ta=False, tb=False, b_split=False, out_split=0, out_dtype=BF16, tm=2048, tn=1024, tk=None,
       bias=None, add=None, a_silu=False, side=None):
    K, M = a.shape if ta else a.shape[::-1]
    if b_split:
        S, d1, n = b.shape
        if tb:
            N, Kb = d1, S * n
        else:
            Kb, N = d1, S * n
    else:
        Kb, N = b.shape[::-1] if tb else b.shape
    assert Kb == K, (a.shape, b.shape, ta, tb, b_split)
    if tk is None:
        tk = 1024 if ta else 2048
    tm = _tile(M, tm)
    if b_split and not tb:
        tn = _tile(n, tn)
    elif out_split:
        tn = _tile(out_split, tn)
    else:
        tn = _tile(N, tn)
    if b_split and tb:
        tk = _tile(n, 1024)
    else:
        tk = _tile(K, tk)
    nk = K // tk

    a_spec = pl.BlockSpec((tk, tm), lambda i, j, k: (k, i)) if ta else pl.BlockSpec((tm, tk), lambda i, j, k: (i, k))
    if b_split and not tb:
        per = n // tn
        b_spec = pl.BlockSpec((None, tk, tn), lambda i, j, k: (j // per, k, j % per))
    elif b_split and tb:
        per = n // tk
        b_spec = pl.BlockSpec((None, tn, tk), lambda i, j, k: (k // per, j, k % per))
    elif tb:
        b_spec = pl.BlockSpec((tn, tk), lambda i, j, k: (j, k))
    else:
        b_spec = pl.BlockSpec((tk, tn), lambda i, j, k: (k, j))
    if out_split:
        per_o = out_split // tn
        o_spec = pl.BlockSpec((None, tm, tn), lambda i, j, k: (j // per_o, i, j % per_o))
        o_shape = jax.ShapeDtypeStruct((N // out_split, M, out_split), out_dtype)
    else:
        o_spec = pl.BlockSpec((tm, tn), lambda i, j, k: (i, j))
        o_shape = jax.ShapeDtypeStruct((M, N), out_dtype)
    ins = [a, b]
    in_specs = [a_spec, b_spec]
    if bias is not None:
        ins.append(bias)
        in_specs.append(pl.BlockSpec((1, tn), lambda i, j, k: (0, j)))
    if add is not None:
        ins.append(add)
        in_specs.append(pl.BlockSpec((tm, tn), lambda i, j, k: (i, j)))
    has_bias, has_add, has_side = bias is not None, add is not None, side is not None
    out_specs, out_shapes = [o_spec], [o_shape]
    scratch = [pltpu.VMEM((tm, tn) if nk > 1 else (8, LANES), F32)]
    if has_side:
        side_kind, side_x = side
        ins.append(side_x)
        in_specs.append(pl.BlockSpec(memory_space=pl.ANY))
        out_specs.append(pl.BlockSpec(memory_space=pl.ANY))
        out_shapes.append(jax.ShapeDtypeStruct((NDEV,) + tuple(side_x.shape[-2:]), side_x.dtype))
        scratch += EXCHANGE_SCRATCH
    gm, gn = M // tm, N // tn

    def body(*refs):
        a_ref, b_ref = refs[0], refs[1]
        pos = 2
        bias_ref = add_ref = None
        if has_bias:
            bias_ref = refs[pos]
            pos += 1
        if has_add:
            add_ref = refs[pos]
            pos += 1
        if has_side:
            side_refs = (side_kind, refs[pos], refs[pos + 2]) + tuple(refs[pos + 4:pos + 7])
            o_ref, acc_ref = refs[pos + 1], refs[pos + 3]
        else:
            o_ref, acc_ref = refs[pos], refs[pos + 1]
        k = pl.program_id(2)
        if has_side:
            i_, j_ = pl.program_id(0), pl.program_id(1)

            @pl.when((i_ == 0) & (j_ == 0) & (k == 0))
            def _():
                _exchange_start(*side_refs)

        def finish(r):
            if has_bias:
                r = r + bias_ref[...]
            if has_add:
                r = r + add_ref[...].astype(F32)
            o_ref[...] = r.astype(out_dtype)

        av = a_ref[...]
        if a_silu:
            av = _silu(av.astype(F32))
        av = av.astype(BF16)
        bv = b_ref[...].astype(BF16)
        dn = (((0 if ta else 1,), (1 if tb else 0,)), ((), ()))
        part = lax.dot_general(av, bv, dn, preferred_element_type=F32)
        if nk == 1:
            finish(part)
        else:
            @pl.when(k == 0)
            def _():
                acc_ref[...] = part

            @pl.when(k > 0)
            def _():
                acc_ref[...] += part

            @pl.when(k == nk - 1)
            def _():
                finish(acc_ref[...])

        if has_side:
            @pl.when((i_ == gm - 1) & (j_ == gn - 1) & (k == nk - 1))
            def _():
                _exchange_wait(*side_refs)

    res = pl.pallas_call(
        body, name=name, grid=(gm, gn, nk), in_specs=in_specs, out_specs=out_specs, out_shape=out_shapes,
        scratch_shapes=scratch, compiler_params=_params(3))(*ins)
    return (res[0], res[1]) if has_side else res[0]


def gmm_nn(a, w, *, name, tb=False, out_dtype=BF16, tm=512):
    T = a.shape[0]
    G = w.shape[0]
    Kg = a.shape[1] // G
    Ng = w.shape[1] if tb else w.shape[2]
    tm = _tile(T, tm)

    def body(a_ref, w_ref, o_ref):
        wv = w_ref[...].astype(BF16)
        av = a_ref[...].astype(BF16)
        r = _dot_nt(av, wv) if tb else _dot(av, wv)
        o_ref[...] = r.astype(out_dtype)

    return pl.pallas_call(
        body, name=name, grid=(G, T // tm),
        in_specs=[pl.BlockSpec((tm, Kg), lambda g, i: (i, g)),
                  pl.BlockSpec((None,) + tuple(w.shape[1:]), lambda g, i: (g, 0, 0))],
        out_specs=pl.BlockSpec((tm, Ng), lambda g, i: (i, g)),
        out_shape=jax.ShapeDtypeStruct((T, G * Ng), out_dtype), compiler_params=_params(2))(a, w)


def gmm_tn(a, b, G, *, name, tk=512):
    T = a.shape[0]
    Kg = a.shape[1] // G
    Ng = b.shape[1] // G
    tk = _tile(T, tk)

    def body(a_ref, b_ref, o_ref):
        @pl.when(pl.program_id(1) == 0)
        def _():
            o_ref[...] = jnp.zeros_like(o_ref)

        o_ref[...] += _dot_tn(a_ref[...].astype(BF16), b_ref[...].astype(BF16))

    return pl.pallas_call(
        body, name=name, grid=(G, T // tk),
        in_specs=[pl.BlockSpec((tk, Kg), lambda g, i: (i, g)), pl.BlockSpec((tk, Ng), lambda g, i: (i, g))],
        out_specs=pl.BlockSpec((None, Kg, Ng), lambda g, i: (g, 0, 0)),
        out_shape=jax.ShapeDtypeStruct((G, Kg, Ng), F32), compiler_params=_params(2))(a, b)


def _row(tr, w, cb=0):
    return pl.BlockSpec((tr, w), lambda i: (i, cb))


def _bc(w):
    return pl.BlockSpec((1, w), lambda i: (0, 0))


def prenorm(x, scale, shift, *, name):
    T, D = x.shape
    tr = _tile(T, 256)

    def body(x_ref, sc_ref, sh_ref, h_ref):
        xv = x_ref[...]
        rstd = lax.rsqrt(jnp.mean(xv * xv, axis=-1, keepdims=True) + NORM_EPS)
        h_ref[...] = ((xv * rstd) * (1.0 + sc_ref[...]) + sh_ref[...]).astype(BF16)

    return pl.pallas_call(
        body, name=name, grid=(T // tr,), in_specs=[_row(tr, D), _bc(D), _bc(D)], out_specs=_row(tr, D),
        out_shape=jax.ShapeDtypeStruct((T, D), BF16), compiler_params=_params(1))(x, scale, shift)


def prenorm_bwd(x, dh, dxn, scale, *, name):
    T, D = x.shape
    tr = _tile(T, 256)

    def body(x_ref, dh_ref, dxn_ref, sc_ref, dx_ref, dsc_ref, dsh_ref):
        @pl.when(pl.program_id(0) == 0)
        def _():
            dsc_ref[...] = jnp.zeros_like(dsc_ref)
            dsh_ref[...] = jnp.zeros_like(dsh_ref)

        xv = x_ref[...]
        dhv = dh_ref[...].astype(F32)
        rstd = lax.rsqrt(jnp.mean(xv * xv, axis=-1, keepdims=True) + NORM_EPS)
        r = xv * rstd
        dsc_ref[...] += jnp.sum(dhv * r, axis=0, keepdims=True)
        dsh_ref[...] += jnp.sum(dhv, axis=0, keepdims=True)
        dr = dhv * (1.0 + sc_ref[...])
        dx_ref[...] = dxn_ref[...] + rstd * (dr - r * jnp.mean(dr * r, axis=-1, keepdims=True))

    return pl.pallas_call(
        body, name=name, grid=(T // tr,), in_specs=[_row(tr, D), _row(tr, D), _row(tr, D), _bc(D)],
        out_specs=[_row(tr, D), _bc(D), _bc(D)],
        out_shape=[jax.ShapeDtypeStruct((T, D), F32), jax.ShapeDtypeStruct((1, D), F32),
                   jax.ShapeDtypeStruct((1, D), F32)], compiler_params=_params(1))(x, dh, dxn, scale)


def resid(x, y, gate, *, name):
    T, D = x.shape
    tr = _tile(T, 256)

    def body(x_ref, y_ref, g_ref, o_ref):
        o_ref[...] = x_ref[...] + g_ref[...] * y_ref[...].astype(F32)

    return pl.pallas_call(
        body, name=name, grid=(T // tr,), in_specs=[_row(tr, D), _row(tr, D), _bc(D)], out_specs=_row(tr, D),
        out_shape=jax.ShapeDtypeStruct((T, D), F32), compiler_params=_params(1))(x, y, gate)


def resid_bwd(dxn, y, gate, *, name):
    T, D = dxn.shape
    tr = _tile(T, 256)

    def body(d_ref, y_ref, g_ref, dy_ref, dg_ref):
        @pl.when(pl.program_id(0) == 0)
        def _():
            dg_ref[...] = jnp.zeros_like(dg_ref)

        d = d_ref[...]
        dy_ref[...] = (d * g_ref[...]).astype(BF16)
        dg_ref[...] += jnp.sum(d * y_ref[...].astype(F32), axis=0, keepdims=True)

    return pl.pallas_call(
        body, name=name, grid=(T // tr,), in_specs=[_row(tr, D), _row(tr, D), _bc(D)],
        out_specs=[_row(tr, D), _bc(D)],
        out_shape=[jax.ShapeDtypeStruct((T, D), BF16), jax.ShapeDtypeStruct((1, D), F32)],
        compiler_params=_params(1))(dxn, y, gate)


def loss_head(x, g, target, *, name):
    T, D = x.shape
    tr = _tile(T, 256)

    def body(x_ref, g_ref, t_ref, loss_ref, dx_ref, dg_ref):
        @pl.when(pl.program_id(0) == 0)
        def _():
            loss_ref[...] = jnp.zeros_like(loss_ref)
            dg_ref[...] = jnp.zeros_like(dg_ref)

        xv = x_ref[...]
        gv = g_ref[...]
        rstd = lax.rsqrt(jnp.mean(xv * xv, axis=-1, keepdims=True) + NORM_EPS)
        r = xv * rstd
        e = r * gv - t_ref[...]
        loss_ref[...] += 0.5 * jnp.sum(jnp.mean(e * e, axis=-1, keepdims=True), axis=0, keepdims=True)
        dout = e * (1.0 / D)
        dg_ref[...] += jnp.sum(dout * r, axis=0, keepdims=True)
        dr = dout * gv
        dx_ref[...] = rstd * (dr - r * jnp.mean(dr * r, axis=-1, keepdims=True))

    return pl.pallas_call(
        body, name=name, grid=(T // tr,), in_specs=[_row(tr, D), _bc(D), _row(tr, D)],
        out_specs=[pl.BlockSpec((1, 1), lambda i: (0, 0)), _row(tr, D), _bc(D)],
        out_shape=[jax.ShapeDtypeStruct((1, 1), F32), jax.ShapeDtypeStruct((T, D), F32),
                   jax.ShapeDtypeStruct((1, D), F32)], compiler_params=_params(1))(x, g, target)


def _col_tile(W, off):
    cw = math.gcd(W, off) if off else W
    cw = math.gcd(cw, 1024) if cw > 1024 else cw
    return cw


def gated(a, proj, g_off, scale, *, name):
    T, W = a.shape
    cw = _col_tile(W, g_off)
    gb = g_off // cw
    tr = _tile(T, 256)

    def body(a_ref, g_ref, s_ref, o_ref):
        o_ref[...] = (a_ref[...].astype(F32) * s_ref[...] * _silu(g_ref[...].astype(F32))).astype(BF16)

    return pl.pallas_call(
        body, name=name, grid=(W // cw, T // tr),
        in_specs=[pl.BlockSpec((tr, cw), lambda j, i: (i, j)), pl.BlockSpec((tr, cw), lambda j, i: (i, gb + j)),
                  pl.BlockSpec((1, cw), lambda j, i: (0, j))],
        out_specs=pl.BlockSpec((tr, cw), lambda j, i: (i, j)),
        out_shape=jax.ShapeDtypeStruct((T, W), BF16), compiler_params=_params(2))(a, proj, scale)


def gated_bwd(dy2, a, proj, g_off, scale, *, name):
    T, W = a.shape
    cw = _col_tile(W, g_off)
    gb = g_off // cw
    tr = _tile(T, 256)

    def body(d_ref, a_ref, g_ref, s_ref, da_ref, dg_ref, ds_ref):
        @pl.when(pl.program_id(1) == 0)
        def _():
            ds_ref[...] = jnp.zeros_like(ds_ref)

        d = d_ref[...].astype(F32)
        av = a_ref[...].astype(F32)
        gv = g_ref[...].astype(F32)
        sv = s_ref[...]
        dsg = d * _silu(gv)
        da_ref[...] = (dsg * sv).astype(BF16)
        dg_ref[...] = (d * av * sv * _dsilu(gv)).astype(BF16)
        ds_ref[...] += jnp.sum(dsg * av, axis=0, keepdims=True)

    blk = pl.BlockSpec((tr, cw), lambda j, i: (i, j))
    return pl.pallas_call(
        body, name=name, grid=(W // cw, T // tr),
        in_specs=[blk, blk, pl.BlockSpec((tr, cw), lambda j, i: (i, gb + j)),
                  pl.BlockSpec((1, cw), lambda j, i: (0, j))],
        out_specs=[blk, blk, pl.BlockSpec((1, cw), lambda j, i: (0, j))],
        out_shape=[jax.ShapeDtypeStruct((T, W), BF16), jax.ShapeDtypeStruct((T, W), BF16),
                   jax.ShapeDtypeStruct((1, W), F32)], compiler_params=_params(2))(dy2, a, proj, scale)


HALO = 16


def band(u, W, *, transpose, name):
    T = u.shape[0]
    R = _tile(T, 256)
    nb = T // R
    G = len(POOL_WINDOWS)
    Cg = W // G
    hal = min(HALO, R)

    def body(p_ref, c_ref, n_ref, o_ref):
        i = pl.program_id(0)
        out_pos = lax.broadcasted_iota(jnp.int32, (R, 1), 0) + i * R
        parts = ((p_ref, i * R - hal, hal, R - hal), (c_ref, i * R, R, 0), (n_ref, (i + 1) * R, hal, 0))
        for gi, w in enumerate(POOL_WINDOWS):
            half = w // 2
            cols = slice(gi * Cg, (gi + 1) * Cg)
            acc = jnp.zeros((R, Cg), F32)
            for ref, base, n, r0 in parts:
                src_pos = lax.broadcasted_iota(jnp.int32, (1, n), 1) + base
                valid = (src_pos >= 0) & (src_pos < T)
                src = ref[r0:r0 + n, cols]
                if not transpose:
                    m = (src_pos >= out_pos - half) & (src_pos < out_pos + half) & valid
                else:
                    m = (out_pos >= src_pos - half) & (out_pos < src_pos + half) & valid
                    sp = lax.broadcasted_iota(jnp.int32, (n, 1), 0) + base
                    cnt = jnp.minimum(sp + half, T) - jnp.maximum(sp - half, 0)
                    src = (src.astype(F32) / jnp.maximum(cnt, 1).astype(F32)).astype(BF16)
                acc = acc + _dot(m.astype(BF16), src.astype(BF16))
            if not transpose:
                cnt = jnp.minimum(out_pos + half, T) - jnp.maximum(out_pos - half, 0)
                acc = acc / cnt.astype(F32)
            o_ref[:, cols] = (acc - c_ref[:, cols].astype(F32)).astype(BF16)

    return pl.pallas_call(
        body, name=name, grid=(nb,),
        in_specs=[pl.BlockSpec((R, W), lambda i: (jnp.maximum(i - 1, 0), 0)), pl.BlockSpec((R, W), lambda i: (i, 0)),
                  pl.BlockSpec((R, W), lambda i: (jnp.minimum(i + 1, nb - 1), 0))],
        out_specs=pl.BlockSpec((R, W), lambda i: (i, 0)),
        out_shape=jax.ShapeDtypeStruct((T, W), BF16), compiler_params=_params(1))(u, u, u)


def _log_sigmoid(x):
    return jnp.minimum(x, 0.0) - jnp.log1p(jnp.exp(-jnp.abs(x)))


def _split3(x):
    hi = x.astype(BF16)
    r1 = x - hi.astype(F32)
    md = r1.astype(BF16)
    lo = (r1 - md.astype(F32)).astype(BF16)
    return hi, md, lo


def _tri_sum(tri, x):
    hi, md, lo = _split3(x)
    return _dot(tri, hi) + _dot(tri, md) + _dot(tri, lo)


def _gla_masks(C, reverse):
    row = lax.broadcasted_iota(jnp.int32, (C, C), 0)
    col = lax.broadcasted_iota(jnp.int32, (C, C), 1)
    if not reverse:
        return (col <= row), (col >= row), (col <= row)
    return (col >= row), (col <= row), (col > row)


def _gla_dims(proj):
    VW = proj.shape[1] // 3
    KW = VW // 2
    return KW, VW, KW // GLA_HEADS, VW // GLA_HEADS


def gla_fwd(proj, z, bias, *, reverse, name):
    T = proj.shape[0]
    KW, VW, DK, DV = _gla_dims(proj)
    H = GLA_HEADS
    C = _tile(T, GLA_CHUNK)
    NC = T // C
    cidx = (lambda i: NC - 1 - i) if reverse else (lambda i: i)
    last = 0 if reverse else C - 1

    def body(q_ref, k_ref, v_ref, z_ref, b_ref, o_ref, s_ref, S_scr):
        @pl.when(pl.program_id(0) == 0)
        def _():
            S_scr[...] = jnp.zeros_like(S_scr)

        cum, _, amask = _gla_masks(C, reverse)
        la = _log_sigmoid(z_ref[...] + b_ref[...]) * (1.0 / GLA_TAU)
        b = _tri_sum(cum.astype(BF16), la)
        for h in range(H):
            ks = slice(h * DK, (h + 1) * DK)
            vs = slice(h * DV, (h + 1) * DV)
            bh = b[:, ks]
            mid = bh[C // 2:C // 2 + 1, :]
            bl = bh[last:last + 1, :]
            q = q_ref[:, ks].astype(F32) * (DK ** -0.5)
            k = k_ref[:, ks].astype(F32)
            v = v_ref[:, vs]
            qe = (q * jnp.exp(bh)).astype(BF16)
            qt = (q * jnp.exp(bh - mid)).astype(BF16)
            kt = (k * jnp.exp(mid - bh)).astype(BF16)
            kd = (k * jnp.exp(bl - bh)).astype(BF16)
            St = S_scr[h]
            Sb = St.astype(BF16)
            s_ref[0, h] = Sb
            A = jnp.where(amask, _dot_nt(qt, kt), 0.0).astype(BF16)
            o_ref[:, vs] = _dot_nt(qe, Sb) + _dot(A, v)
            S_scr[h] = St * jnp.exp(bl) + _dot_tn(v, kd)

    return pl.pallas_call(
        body, name=name, grid=(NC,),
        in_specs=[pl.BlockSpec((C, KW), lambda i: (cidx(i), 0)), pl.BlockSpec((C, KW), lambda i: (cidx(i), 1)),
                  pl.BlockSpec((C, VW), lambda i: (cidx(i), 1)), pl.BlockSpec((C, KW), lambda i: (cidx(i), 0)),
                  pl.BlockSpec((1, KW), lambda i: (0, 0))],
        out_specs=[pl.BlockSpec((C, VW), lambda i: (cidx(i), 0)),
                   pl.BlockSpec((1, H, DV, DK), lambda i: (cidx(i), 0, 0, 0))],
        out_shape=[jax.ShapeDtypeStruct((T, VW), F32), jax.ShapeDtypeStruct((NC, H, DV, DK), BF16)],
        scratch_shapes=[pltpu.VMEM((H, DV, DK), F32)], compiler_params=_params(1))(proj, proj, proj, z, bias)


def gla_bwd(proj, z, bias, states, do, prev, *, reverse, name):
    T = proj.shape[0]
    KW, VW, DK, DV = _gla_dims(proj)
    H = GLA_HEADS
    C = _tile(T, GLA_CHUNK)
    NC = T // C
    cidx = (lambda i: i) if reverse else (lambda i: NC - 1 - i)
    last = 0 if reverse else C - 1
    has_prev = prev is not None
    odt = BF16 if has_prev else F32

    def body(*refs):
        q_ref, k_ref, v_ref, z_ref, b_ref, s_ref, do_ref = refs[:7]
        pos = 7
        if has_prev:
            pq_ref, pk_ref, pv_ref = refs[7:10]
            pos = 10
        dq_ref, dk_ref, dv_ref, dz_ref, db_ref, dS_scr = refs[pos:pos + 6]

        @pl.when(pl.program_id(0) == 0)
        def _():
            dS_scr[...] = jnp.zeros_like(dS_scr)
            db_ref[...] = jnp.zeros_like(db_ref)

        cum, cum_t, amask = _gla_masks(C, reverse)
        xg = z_ref[...] + b_ref[...]
        la = _log_sigmoid(xg) * (1.0 / GLA_TAU)
        b = _tri_sum(cum.astype(BF16), la)
        cum_t_bf = cum_t.astype(BF16)
        for h in range(H):
            ks = slice(h * DK, (h + 1) * DK)
            vs = slice(h * DV, (h + 1) * DV)
            bh = b[:, ks]
            mid = bh[C // 2:C // 2 + 1, :]
            bl = bh[last:last + 1, :]
            q = q_ref[:, ks].astype(F32) * (DK ** -0.5)
            k = k_ref[:, ks].astype(F32)
            v = v_ref[:, vs]
            dov = do_ref[:, vs]
            e_b = jnp.exp(bh)
            e_up = jnp.exp(bh - mid)
            e_dn = jnp.exp(mid - bh)
            e_l = jnp.exp(bl - bh)
            e_bl = jnp.exp(bl)
            qe = (q * e_b).astype(BF16)
            qt = (q * e_up).astype(BF16)
            kt = (k * e_dn).astype(BF16)
            kd = (k * e_l).astype(BF16)
            Sb = s_ref[0, h]
            dSt = dS_scr[h]
            dSb = dSt.astype(BF16)
            A = jnp.where(amask, _dot_nt(qt, kt), 0.0).astype(BF16)
            dA = jnp.where(amask, _dot_nt(dov, v), 0.0).astype(BF16)
            dv = _dot_tn(A, dov) + _dot_nt(kd, dSb)
            dq = _dot(dA, kt) * e_up + _dot(dov, Sb) * e_b
            dk_state = _dot(v, dSb) * e_l
            dk = _dot_tn(dA, qt) * e_dn + dk_state
            d_b = dq * q - dk * k
            d_bl = (jnp.sum(dk_state * k, axis=0, keepdims=True)
                    + e_bl * jnp.sum(dSt * Sb.astype(F32), axis=0, keepdims=True))
            d_la = _tri_sum(cum_t_bf, d_b) + d_bl
            dz = d_la * (1.0 / GLA_TAU) * jax.nn.sigmoid(-xg[:, ks])
            dq = dq * (DK ** -0.5)
            if has_prev:
                dq = dq + pq_ref[:, ks]
                dk = dk + pk_ref[:, ks]
                dv = dv + pv_ref[:, vs]
            dq_ref[:, ks] = dq.astype(odt)
            dk_ref[:, ks] = dk.astype(odt)
            dv_ref[:, vs] = dv.astype(odt)
            dz_ref[:, ks] = dz.astype(BF16)
            db_ref[:, ks] += jnp.sum(dz, axis=0, keepdims=True)
            dS_scr[h] = dSt * e_bl + _dot_tn(dov, qe)

    kspec = lambda cb: pl.BlockSpec((C, KW), lambda i: (cidx(i), cb))
    vspec = lambda cb: pl.BlockSpec((C, VW), lambda i: (cidx(i), cb))
    ins = [proj, proj, proj, z, bias, states, do]
    in_specs = [kspec(0), kspec(1), vspec(1), kspec(0), pl.BlockSpec((1, KW), lambda i: (0, 0)),
                pl.BlockSpec((1, H, DV, DK), lambda i: (cidx(i), 0, 0, 0)), vspec(0)]
    if has_prev:
        ins += list(prev)
        in_specs += [kspec(0), kspec(0), vspec(0)]
    return pl.pallas_call(
        body, name=name, grid=(NC,), in_specs=in_specs,
        out_specs=[kspec(0), kspec(0), vspec(0), kspec(0), pl.BlockSpec((1, KW), lambda i: (0, 0))],
        out_shape=[jax.ShapeDtypeStruct((T, KW), odt), jax.ShapeDtypeStruct((T, KW), odt),
                   jax.ShapeDtypeStruct((T, VW), odt), jax.ShapeDtypeStruct((T, KW), BF16),
                   jax.ShapeDtypeStruct((1, KW), F32)],
        scratch_shapes=[pltpu.VMEM((H, DV, DK), F32)], compiler_params=_params(1))(*ins)


def gla_out(o_f, o_b, proj, g_off, norm_g, *, name):
    T, VW = o_f.shape
    H = GLA_HEADS
    DV = VW // H
    gb = g_off // VW
    tr = _tile(T, 256)

    def body(f_ref, b_ref, g_ref, n_ref, y_ref):
        for h in range(H):
            vs = slice(h * DV, (h + 1) * DV)
            o = f_ref[:, vs] + b_ref[:, vs]
            rstd = lax.rsqrt(jnp.mean(o * o, axis=-1, keepdims=True) + NORM_EPS)
            y_ref[:, vs] = (o * rstd * n_ref[...] * _silu(g_ref[:, vs].astype(F32))).astype(BF16)

    return pl.pallas_call(
        body, name=name, grid=(T // tr,),
        in_specs=[_row(tr, VW), _row(tr, VW), _row(tr, VW, gb), _bc(DV)], out_specs=_row(tr, VW),
        out_shape=jax.ShapeDtypeStruct((T, VW), BF16), compiler_params=_params(1))(o_f, o_b, proj, norm_g)


def gla_out_bwd(dy2, o_f, o_b, proj, g_off, norm_g, *, name):
    T, VW = o_f.shape
    H = GLA_HEADS
    DV = VW // H
    gb = g_off // VW
    tr = _tile(T, 256)

    def body(d_ref, f_ref, b_ref, g_ref, n_ref, do_ref, dg_ref, dn_ref):
        @pl.when(pl.program_id(0) == 0)
        def _():
            dn_ref[...] = jnp.zeros_like(dn_ref)

        nv = n_ref[...]
        for h in range(H):
            vs = slice(h * DV, (h + 1) * DV)
            o = f_ref[:, vs] + b_ref[:, vs]
            rstd = lax.rsqrt(jnp.mean(o * o, axis=-1, keepdims=True) + NORM_EPS)
            r = o * rstd
            gv = g_ref[:, vs].astype(F32)
            d = d_ref[:, vs].astype(F32)
            dg_ref[:, vs] = (d * r * nv * _dsilu(gv)).astype(BF16)
            dn_o = d * _silu(gv)
            dn_ref[...] += jnp.sum(dn_o * r, axis=0, keepdims=True)
            dr = dn_o * nv
            do_ref[:, vs] = (rstd * (dr - r * jnp.mean(dr * r, axis=-1, keepdims=True))).astype(BF16)

    return pl.pallas_call(
        body, name=name, grid=(T // tr,),
        in_specs=[_row(tr, VW), _row(tr, VW), _row(tr, VW), _row(tr, VW, gb), _bc(DV)],
        out_specs=[_row(tr, VW), _row(tr, VW), _bc(DV)],
        out_shape=[jax.ShapeDtypeStruct((T, VW), BF16), jax.ShapeDtypeStruct((T, VW), BF16),
                   jax.ShapeDtypeStruct((1, DV), F32)], compiler_params=_params(1))(dy2, o_f, o_b, proj, norm_g)


def _rope_tables(T):
    hd = ATTN_HEAD_DIM
    axis_dim = hd // 2
    rows = T // GRID_W
    t = jnp.arange(T)
    row = (t // GRID_W - rows // 2).astype(F32)
    col = (t % GRID_W - GRID_W // 2).astype(F32)
    inv = ROPE_THETA ** (-jnp.arange(0, axis_dim, 2, dtype=F32) / axis_dim)
    ang = jnp.concatenate([row[:, None] * inv, col[:, None] * inv], axis=-1)
    cos = jnp.repeat(jnp.cos(ang), 2, axis=-1)
    sin = jnp.repeat(jnp.sin(ang), 2, axis=-1)
    sign = jnp.where(jnp.arange(hd) % 2 == 0, -1.0, 1.0).astype(F32)
    return cos, sin * sign


def _pair_swap(x):
    n = x.shape[-1]
    lane = lax.broadcasted_iota(jnp.int32, x.shape, x.ndim - 1)
    return jnp.where(lane % 2 == 0, pltpu.roll(x, n - 1, x.ndim - 1), pltpu.roll(x, 1, x.ndim - 1))


def _attn_dims(proj):
    hd = ATTN_HEAD_DIM
    kvw = ATTN_KV_HEADS * hd
    qw = (proj.shape[1] - 2 * kvw) // 2
    return qw, kvw, qw // hd, (qw // hd) // ATTN_KV_HEADS


LOG2E = 1.4426950408889634
LN2 = 0.6931471805599453


def _q_mult():
    return ATTN_HEAD_DIM ** -0.5 * LOG2E


def attn_prep(proj, qg, kg, cos, sin, *, name):
    T = proj.shape[0]
    QW, KVW, NH, G = _attn_dims(proj)
    hd = ATTN_HEAD_DIM
    KV = ATTN_KV_HEADS
    assert QW % KVW == 0
    tr = _tile(T, 256)

    def body(q_ref, k_ref, v_ref, qg_ref, kg_ref, c_ref, s_ref, qo_ref, ko_ref, vo_ref):
        cv, sv = c_ref[...], s_ref[...]

        def one(x, gain, mult):
            rstd = lax.rsqrt(jnp.mean(x * x, axis=-1, keepdims=True) + NORM_EPS)
            xs = x * rstd * gain
            return (xs * cv + _pair_swap(xs) * sv) * mult

        for h in range(NH):
            hs = slice(h * hd, (h + 1) * hd)
            qo_ref[h] = one(q_ref[:, hs].astype(F32), qg_ref[...], _q_mult()).astype(BF16)
        for h in range(KV):
            hs = slice(h * hd, (h + 1) * hd)
            ko_ref[:, hs] = one(k_ref[:, hs].astype(F32), kg_ref[...], 1.0).astype(BF16)
            vo_ref[:, 2 * h * hd:(2 * h + 1) * hd] = v_ref[:, hs]
            vo_ref[:, (2 * h + 1) * hd:(2 * h + 2) * hd] = jnp.ones((tr, hd), BF16)

    return pl.pallas_call(
        body, name=name, grid=(T // tr,),
        in_specs=[_row(tr, QW), _row(tr, KVW, QW // KVW), _row(tr, KVW, QW // KVW + 1), _bc(hd), _bc(hd),
                  _row(tr, hd), _row(tr, hd)],
        out_specs=[pl.BlockSpec((NH, tr, hd), lambda i: (0, i, 0)), _row(tr, KVW), _row(tr, 2 * KVW)],
        out_shape=[jax.ShapeDtypeStruct((NH, T, hd), BF16), jax.ShapeDtypeStruct((T, KVW), BF16),
                   jax.ShapeDtypeStruct((T, 2 * KVW), BF16)],
        compiler_params=_params(1))(proj, proj, proj, qg, kg, cos, sin)


def attn_prep_bwd(dqr, dkr, proj, qg, kg, cos, sin, *, name):
    T = proj.shape[0]
    QW, KVW, NH, G = _attn_dims(proj)
    hd = ATTN_HEAD_DIM
    tr = _tile(T, 256)

    def body(dq_ref, dk_ref, q_ref, k_ref, qg_ref, kg_ref, c_ref, s_ref, oq_ref, ok_ref, dqg_ref, dkg_ref):
        @pl.when(pl.program_id(0) == 0)
        def _():
            dqg_ref[...] = jnp.zeros_like(dqg_ref)
            dkg_ref[...] = jnp.zeros_like(dkg_ref)

        cv, sv = c_ref[...], s_ref[...]

        def one(d, x, gain, mult):
            d = d * mult
            dxs = d * cv - _pair_swap(d) * sv
            rstd = lax.rsqrt(jnp.mean(x * x, axis=-1, keepdims=True) + NORM_EPS)
            xn = x * rstd
            dgain = jnp.sum(dxs * xn, axis=0, keepdims=True)
            dxn = dxs * gain
            return rstd * (dxn - xn * jnp.mean(dxn * xn, axis=-1, keepdims=True)), dgain

        for h in range(NH):
            hs = slice(h * hd, (h + 1) * hd)
            dx, dgain = one(dq_ref[h].astype(F32), q_ref[:, hs].astype(F32), qg_ref[...], _q_mult())
            oq_ref[:, hs] = dx.astype(BF16)
            dqg_ref[...] += dgain
        for h in range(ATTN_KV_HEADS):
            hs = slice(h * hd, (h + 1) * hd)
            dx, dgain = one(dk_ref[:, hs].astype(F32), k_ref[:, hs].astype(F32), kg_ref[...], 1.0)
            ok_ref[:, hs] = dx.astype(BF16)
            dkg_ref[...] += dgain

    return pl.pallas_call(
        body, name=name, grid=(T // tr,),
        in_specs=[pl.BlockSpec((NH, tr, hd), lambda i: (0, i, 0)), _row(tr, KVW), _row(tr, QW),
                  _row(tr, KVW, QW // KVW), _bc(hd), _bc(hd), _row(tr, hd), _row(tr, hd)],
        out_specs=[_row(tr, QW), _row(tr, KVW), _bc(hd), _bc(hd)],
        out_shape=[jax.ShapeDtypeStruct((T, QW), BF16), jax.ShapeDtypeStruct((T, KVW), BF16),
                   jax.ShapeDtypeStruct((1, hd), F32), jax.ShapeDtypeStruct((1, hd), F32)],
        compiler_params=_params(1))(dqr, dkr, proj, proj, qg, kg, cos, sin)


FLASH_BQ = 256
FLASH_BK = 512


def flash_fwd(q, kr, vext, *, name):
    NH, T, hd = q.shape
    KV = ATTN_KV_HEADS
    G = NH // KV
    bq = _tile(T, FLASH_BQ)
    bk = _tile(T, FLASH_BK)
    nk = T // bk

    def body(q_ref, k_ref, v_ref, o_ref, lse_ref, m_scr, acc_scr, sa_scr, sb_scr):
        m_scr[...] = jnp.full_like(m_scr, -jnp.inf)
        acc_scr[...] = jnp.zeros_like(acc_scr)

        def scores(c, s_scr):
            kc = k_ref[pl.ds(pl.multiple_of(c * bk, bk), bk), :]
            for g in range(G):
                s_scr[g] = _dot_nt(q_ref[g], kc)

        def consume(c, s_scr):
            vc = v_ref[pl.ds(pl.multiple_of(c * bk, bk), bk), :]
            for g in range(G):
                s = s_scr[g]
                m_old = m_scr[g]
                m_new = jnp.maximum(m_old, jnp.max(s, axis=-1, keepdims=True))
                p = jnp.exp2(s - m_new)
                acc_scr[g] = jnp.exp2(m_old - m_new) * acc_scr[g] + _dot(p.astype(BF16), vc)
                m_scr[g] = m_new

        scores(0, sa_scr)
        if nk % 2 == 0:
            def pair(t, carry):
                scores(2 * t + 1, sb_scr)
                consume(2 * t, sa_scr)
                scores(2 * t + 2, sa_scr)
                consume(2 * t + 1, sb_scr)
                return carry

            lax.fori_loop(0, nk // 2 - 1, pair, 0)
            scores(nk - 1, sb_scr)
            consume(nk - 2, sa_scr)
            consume(nk - 1, sb_scr)
        else:
            def single(c, carry):
                consume(c, sa_scr)
                scores(c + 1, sa_scr)
                return carry

            lax.fori_loop(0, nk - 1, single, 0)
            consume(nk - 1, sa_scr)
        for g in range(G):
            a = acc_scr[g]
            l = a[:, hd:]
            o_ref[g] = a[:, :hd] / l
            lse_ref[g] = m_scr[g] + jnp.log2(l[:, 0:1])

    return pl.pallas_call(
        body, name=name, grid=(KV, T // bq),
        in_specs=[pl.BlockSpec((G, bq, hd), lambda h, i: (h, i, 0)), pl.BlockSpec((T, hd), lambda h, i: (0, h)),
                  pl.BlockSpec((T, 2 * hd), lambda h, i: (0, h))],
        out_specs=[pl.BlockSpec((G, bq, hd), lambda h, i: (h, i, 0)),
                   pl.BlockSpec((G, bq, 1), lambda h, i: (h, i, 0))],
        out_shape=[jax.ShapeDtypeStruct((NH, T, hd), F32), jax.ShapeDtypeStruct((NH, T, 1), F32)],
        scratch_shapes=[pltpu.VMEM((G, bq, 1), F32), pltpu.VMEM((G, bq, 2 * hd), F32),
                        pltpu.VMEM((G, bq, bk), F32), pltpu.VMEM((G, bq, bk), F32)],
        compiler_params=_params(2))(q, kr, vext)


def flash_bwd(q, kr, proj, v_off, do, lse, delta, *, name):
    NH, T, hd = q.shape
    KV = ATTN_KV_HEADS
    G = NH // KV
    bq = _tile(T, FLASH_BQ)
    bk = _tile(T, FLASH_BK)
    nk = T // bk
    nq = T // bq
    vb = v_off // hd

    def body(q_ref, k_ref, v_ref, do_ref, lse_ref, dl_ref, dq_ref, dk_ref, dv_ref, dq_scr, dk_scr, dv_scr,
             sa_scr, pa_scr, sb_scr, pb_scr):
        i = pl.program_id(1)

        @pl.when(i == 0)
        def _():
            dk_scr[...] = jnp.zeros_like(dk_scr)
            dv_scr[...] = jnp.zeros_like(dv_scr)

        dq_scr[...] = jnp.zeros_like(dq_scr)

        def scores(c, s_scr, dp_scr):
            off = pl.multiple_of(c * bk, bk)
            kc = k_ref[pl.ds(off, bk), :]
            vc = v_ref[pl.ds(off, bk), :]
            for g in range(G):
                s_scr[g] = _dot_nt(q_ref[g], kc)
                dp_scr[g] = _dot_nt(do_ref[g], vc)

        def consume(c, s_scr, dp_scr):
            off = pl.multiple_of(c * bk, bk)
            kc = k_ref[pl.ds(off, bk), :]
            dk_c = jnp.zeros((bk, hd), F32)
            dv_c = jnp.zeros((bk, hd), F32)
            for g in range(G):
                p = jnp.exp2(s_scr[g] - lse_ref[g])
                ds = (p * (dp_scr[g] - dl_ref[g])).astype(BF16)
                dq_scr[g] += _dot(ds, kc)
                dv_c = dv_c + _dot_tn(p.astype(BF16), do_ref[g])
                dk_c = dk_c + _dot_tn(ds, q_ref[g])
            dk_scr[pl.ds(off, bk), :] += dk_c
            dv_scr[pl.ds(off, bk), :] += dv_c

        scores(0, sa_scr, pa_scr)
        if nk % 2 == 0:
            def pair(t, carry):
                scores(2 * t + 1, sb_scr, pb_scr)
                consume(2 * t, sa_scr, pa_scr)
                scores(2 * t + 2, sa_scr, pa_scr)
                consume(2 * t + 1, sb_scr, pb_scr)
                return carry

            lax.fori_loop(0, nk // 2 - 1, pair, 0)
            scores(nk - 1, sb_scr, pb_scr)
            consume(nk - 2, sa_scr, pa_scr)
            consume(nk - 1, sb_scr, pb_scr)
        else:
            def single(c, carry):
                consume(c, sa_scr, pa_scr)
                scores(c + 1, sa_scr, pa_scr)
                return carry

            lax.fori_loop(0, nk - 1, single, 0)
            consume(nk - 1, sa_scr, pa_scr)
        dq_ref[...] = dq_scr[...].astype(BF16)

        @pl.when(i == nq - 1)
        def _():
            dk_ref[...] = dk_scr[...].astype(BF16)
            dv_ref[...] = (dv_scr[...] * LOG2E).astype(BF16)

    qspec = pl.BlockSpec((G, bq, hd), lambda h, i: (h, i, 0))
    cspec = pl.BlockSpec((G, bq, 1), lambda h, i: (h, i, 0))
    kspec = pl.BlockSpec((T, hd), lambda h, i: (0, h))
    return pl.pallas_call(
        body, name=name, grid=(KV, nq),
        in_specs=[qspec, kspec, pl.BlockSpec((T, hd), lambda h, i: (0, vb + h)), qspec, cspec, cspec],
        out_specs=[qspec, kspec, kspec],
        out_shape=[jax.ShapeDtypeStruct((NH, T, hd), BF16), jax.ShapeDtypeStruct((T, KV * hd), BF16),
                   jax.ShapeDtypeStruct((T, KV * hd), BF16)],
        scratch_shapes=[pltpu.VMEM((G, bq, hd), F32), pltpu.VMEM((T, hd), F32), pltpu.VMEM((T, hd), F32)]
        + [pltpu.VMEM((G, bq, bk), F32)] * 4,
        compiler_params=_params(2))(q, kr, proj, do, lse, delta)


def attn_gate(o, proj, g_off, *, name):
    NH, T, hd = o.shape
    W = NH * hd
    cw = _col_tile(W, g_off)
    hc = cw // hd
    gb = g_off // cw
    tr = _tile(T, 256)

    def body(o_ref, g_ref, y_ref):
        for h in range(hc):
            hs = slice(h * hd, (h + 1) * hd)
            y_ref[:, hs] = (o_ref[h] * _silu(g_ref[:, hs].astype(F32))).astype(BF16)

    return pl.pallas_call(
        body, name=name, grid=(W // cw, T // tr),
        in_specs=[pl.BlockSpec((hc, tr, hd), lambda j, i: (j, i, 0)), pl.BlockSpec((tr, cw), lambda j, i: (i, gb + j))],
        out_specs=pl.BlockSpec((tr, cw), lambda j, i: (i, j)),
        out_shape=jax.ShapeDtypeStruct((T, W), BF16), compiler_params=_params(2))(o, proj)


def attn_gate_bwd(dy2, o, proj, g_off, *, name):
    NH, T, hd = o.shape
    W = NH * hd
    cw = _col_tile(W, g_off)
    hc = cw // hd
    gb = g_off // cw
    tr = _tile(T, 256)

    def body(d_ref, o_ref, g_ref, do_ref, dg_ref, dl_ref):
        for h in range(hc):
            hs = slice(h * hd, (h + 1) * hd)
            d = d_ref[:, hs].astype(F32)
            gv = g_ref[:, hs].astype(F32)
            ov = o_ref[h]
            dov = d * _silu(gv) * LN2
            do_ref[h] = dov.astype(BF16)
            dg_ref[:, hs] = (d * ov * _dsilu(gv)).astype(BF16)
            dl_ref[h] = jnp.sum(dov * ov, axis=-1, keepdims=True)

    return pl.pallas_call(
        body, name=name, grid=(W // cw, T // tr),
        in_specs=[pl.BlockSpec((tr, cw), lambda j, i: (i, j)), pl.BlockSpec((hc, tr, hd), lambda j, i: (j, i, 0)),
                  pl.BlockSpec((tr, cw), lambda j, i: (i, gb + j))],
        out_specs=[pl.BlockSpec((hc, tr, hd), lambda j, i: (j, i, 0)), pl.BlockSpec((tr, cw), lambda j, i: (i, j)),
                   pl.BlockSpec((hc, tr, 1), lambda j, i: (j, i, 0))],
        out_shape=[jax.ShapeDtypeStruct((NH, T, hd), BF16), jax.ShapeDtypeStruct((T, W), BF16),
                   jax.ShapeDtypeStruct((NH, T, 1), F32)], compiler_params=_params(2))(dy2, o, proj)


def outer_silu(c_t, dm, *, name):
    K, B = c_t.shape
    N = dm.shape[1]
    tk = _tile(K, 256)

    def body(c_ref, d_ref, o_ref):
        s = _silu(c_ref[...])
        acc = jnp.zeros((tk, N), F32)
        for b in range(B):
            acc = acc + s[:, b:b + 1] * d_ref[b:b + 1, :]
        o_ref[...] = acc

    return pl.pallas_call(
        body, name=name, grid=(K // tk,),
        in_specs=[pl.BlockSpec((tk, B), lambda i: (i, 0)), pl.BlockSpec((B, N), lambda i: (0, 0))],
        out_specs=pl.BlockSpec((tk, N), lambda i: (i, 0)),
        out_shape=jax.ShapeDtypeStruct((K, N), F32), compiler_params=_params(1))(c_t, dm)


def sum_slots(x, *, name):
    S, R, C = x.shape
    tr = _tile(R, 512)

    def body(x_ref, o_ref):
        acc = x_ref[0].astype(F32)
        for s in range(1, S):
            acc = acc + x_ref[s].astype(F32)
        o_ref[...] = acc

    return pl.pallas_call(
        body, name=name, grid=(R // tr,), in_specs=[pl.BlockSpec((S, tr, C), lambda i: (0, i, 0))],
        out_specs=pl.BlockSpec((tr, C), lambda i: (i, 0)),
        out_shape=jax.ShapeDtypeStruct((R, C), F32), compiler_params=_params(1))(x)


def adamw(w, g, m, v, *, name):
    R, C = w.shape
    tr = _tile(R, 512) if R % 8 == 0 else R

    def body(w_ref, g_ref, m_ref, v_ref, d_ref, nm_ref, nv_ref):
        gv = g_ref[...]
        mn = ADAM_B1 * m_ref[...] + (1.0 - ADAM_B1) * gv
        vn = ADAM_B2 * v_ref[...] + (1.0 - ADAM_B2) * jnp.square(gv)
        m_hat = mn / (1.0 - ADAM_B1 ** ADAM_STEP)
        v_hat = vn / (1.0 - ADAM_B2 ** ADAM_STEP)
        d_ref[...] = -ADAM_LR * (m_hat / (jnp.sqrt(v_hat) + ADAM_EPS) + ADAM_WD * w_ref[...])
        nm_ref[...] = mn
        nv_ref[...] = vn

    blk = pl.BlockSpec((tr, C), lambda i: (i, 0))
    sh = jax.ShapeDtypeStruct((R, C), F32)
    return pl.pallas_call(
        body, name=name, grid=(R // tr,), in_specs=[blk] * 4, out_specs=[blk] * 3, out_shape=[sh] * 3,
        compiler_params=_params(1))(w, g, m, v)


def _adamw_nd(w, g, m, v, name):
    shp = w.shape
    if w.ndim == 1:
        two = (1, shp[0])
    else:
        two = (math.prod(shp[:-1]), shp[-1])
    d, nm, nv = adamw(w.reshape(two), g.reshape(two), m.reshape(two), v.reshape(two), name=name)
    return d.reshape(shp), nm.reshape(shp), nv.reshape(shp)


IN_DX_TILES = dict(tm=1024, tn=512, tk=8192)


def _pad_cols(w, n):
    return jnp.pad(w, ((0, 0), (0, n - w.shape[1])))


def _pad_rows(w, n):
    return jnp.pad(w, ((0, n - w.shape[0]), (0, 0)))


def kernel(x, c, w_mod, b_mod, pool_w_in, pool_w_grp, pool_scale, pool_w_out, gla_w_in, gla_fwd_w1, gla_fwd_w2, gla_fwd_b, gla_bwd_w1, gla_bwd_w2, gla_bwd_b, gla_norm_g, gla_w_out, attn_w_in, attn_q_norm_g, attn_k_norm_g, attn_w_out, final_norm_g, loss_target, m_w_mod, m_b_mod, m_pool_w_in, m_pool_w_grp, m_pool_scale, m_pool_w_out, m_gla_w_in, m_gla_fwd_w1, m_gla_fwd_w2, m_gla_fwd_b, m_gla_bwd_w1, m_gla_bwd_w2, m_gla_bwd_b, m_gla_norm_g, m_gla_w_out, m_attn_w_in, m_attn_q_norm_g, m_attn_k_norm_g, m_attn_w_out, m_final_norm_g, v_w_mod, v_b_mod, v_pool_w_in, v_pool_w_grp, v_pool_scale, v_pool_w_out, v_gla_w_in, v_gla_fwd_w1, v_gla_fwd_w2, v_gla_fwd_b, v_gla_bwd_w1, v_gla_bwd_w2, v_gla_bwd_b, v_gla_norm_g, v_gla_w_out, v_attn_w_in, v_attn_q_norm_g, v_attn_k_norm_g, v_attn_w_out, v_final_norm_g):
    W = dict(w_mod=w_mod, b_mod=b_mod, pool_w_in=pool_w_in, pool_w_grp=pool_w_grp, pool_scale=pool_scale,
             pool_w_out=pool_w_out, gla_w_in=gla_w_in, gla_fwd_w1=gla_fwd_w1, gla_fwd_w2=gla_fwd_w2,
             gla_fwd_b=gla_fwd_b, gla_bwd_w1=gla_bwd_w1, gla_bwd_w2=gla_bwd_w2, gla_bwd_b=gla_bwd_b,
             gla_norm_g=gla_norm_g, gla_w_out=gla_w_out, attn_w_in=attn_w_in, attn_q_norm_g=attn_q_norm_g,
             attn_k_norm_g=attn_k_norm_g, attn_w_out=attn_w_out, final_norm_g=final_norm_g)
    M = dict(w_mod=m_w_mod, b_mod=m_b_mod, pool_w_in=m_pool_w_in, pool_w_grp=m_pool_w_grp, pool_scale=m_pool_scale,
             pool_w_out=m_pool_w_out, gla_w_in=m_gla_w_in, gla_fwd_w1=m_gla_fwd_w1, gla_fwd_w2=m_gla_fwd_w2,
             gla_fwd_b=m_gla_fwd_b, gla_bwd_w1=m_gla_bwd_w1, gla_bwd_w2=m_gla_bwd_w2, gla_bwd_b=m_gla_bwd_b,
             gla_norm_g=m_gla_norm_g, gla_w_out=m_gla_w_out, attn_w_in=m_attn_w_in, attn_q_norm_g=m_attn_q_norm_g,
             attn_k_norm_g=m_attn_k_norm_g, attn_w_out=m_attn_w_out, final_norm_g=m_final_norm_g)
    V = dict(w_mod=v_w_mod, b_mod=v_b_mod, pool_w_in=v_pool_w_in, pool_w_grp=v_pool_w_grp, pool_scale=v_pool_scale,
             pool_w_out=v_pool_w_out, gla_w_in=v_gla_w_in, gla_fwd_w1=v_gla_fwd_w1, gla_fwd_w2=v_gla_fwd_w2,
             gla_fwd_b=v_gla_fwd_b, gla_bwd_w1=v_gla_bwd_w1, gla_bwd_w2=v_gla_bwd_w2, gla_bwd_b=v_gla_bwd_b,
             gla_norm_g=v_gla_norm_g, gla_w_out=v_gla_w_out, attn_w_in=v_attn_w_in, attn_q_norm_g=v_attn_q_norm_g,
             attn_k_norm_g=v_attn_k_norm_g, attn_w_out=v_attn_w_out, final_norm_g=v_final_norm_g)

    me = _my_id()
    T, D = x.shape[1], x.shape[2]
    x0 = x.reshape(T, D)
    target = loss_target.reshape(T, D)
    BW = D
    n_pool, n_gla, n_attn = pool_w_in.shape[0], gla_w_in.shape[0], attn_w_in.shape[0]
    LR = GLA_LOWRANK
    LRP = LANES
    KWg = gla_fwd_w2.shape[2] * NDEV
    Dm = w_mod.shape[2]

    small_shard_names = ['pool_scale', 'gla_fwd_w1', 'gla_fwd_w2', 'gla_bwd_w1', 'gla_bwd_w2']
    small_items = [c] + [W[n] for n in small_shard_names]
    small_shapes = [a.shape for a in small_items]
    g1 = exchange(GATHER, _pack(small_items, F32), name="gather_small")
    c_all_, ps_all, fw1_all, fw2_all, bw1_all, bw2_all = _unpack(g1, small_shapes, lead=NDEV)
    c_all = c_all_.reshape(NDEV, D)
    pool_scale_full = jnp.transpose(ps_all, (1, 0, 2)).reshape(n_pool, BW)
    w1_full = {'f': fw1_all.transpose(1, 0, 2, 3).reshape(n_gla, D, LR),
               'b': bw1_all.transpose(1, 0, 2, 3).reshape(n_gla, D, LR)}
    w2_full = {'f': fw2_all.transpose(1, 2, 0, 3).reshape(n_gla, LR, KWg),
               'b': bw2_all.transpose(1, 2, 0, 3).reshape(n_gla, LR, KWg)}

    c16 = _pad_rows(c_all, 16)
    b_slab = lax.dynamic_slice_in_dim(b_mod, me * Dm, Dm, axis=1)
    mod_parts = [mm(c16, w_mod[i], name=f"mod_fwd{i}", out_dtype=F32, a_silu=True, tm=16, tn=Dm,
                    bias=b_slab[i:i + 1])[:NDEV] for i in range(DEPTH)]
    mod_slab = jnp.stack(mod_parts)
    g2 = exchange(GATHER, _pack([mod_slab], F32), name="gather_mod")
    (mod_all,) = _unpack(g2, [mod_slab.shape], lead=NDEV)
    mod_mine = lax.dynamic_index_in_dim(mod_all, me, axis=2, keepdims=False)
    mod_mine = mod_mine.transpose(1, 0, 2).reshape(DEPTH, NDEV * Dm)
    shift = [mod_mine[i:i + 1, 0:D] for i in range(DEPTH)]
    scale = [mod_mine[i:i + 1, D:2 * D] for i in range(DEPTH)]
    gate = [mod_mine[i:i + 1, 2 * D:3 * D] for i in range(DEPTH)]

    layer_names = {0: ['pool_w_in', 'pool_w_grp', 'pool_w_out'], 1: ['gla_w_in', 'gla_w_out'],
                   2: ['attn_w_in', 'attn_w_out']}

    def layer_shards(i):
        return [W[n][i // N_MIXERS] for n in layer_names[i % N_MIXERS]]

    def layer_weights(i, gathered):
        parts = _unpack(gathered, [a.shape for a in layer_shards(i)], lead=NDEV)
        w_out = parts[-1].reshape(BW, D)
        out = dict(w_in=parts[0], w_out=w_out, w_in_t=parts[0].transpose(0, 2, 1).reshape(-1, D), w_out_t=w_out.T)
        if i % N_MIXERS == 0:
            g = parts[1]
            out['w_grp'] = g.transpose(1, 0, 2, 3).reshape(g.shape[1], -1, g.shape[3])
        return out

    gathered = exchange(GATHER, _pack(layer_shards(0), BF16), name="gather_w0")

    cos, sin = _rope_tables(T)

    xs = [x0]
    saved = []
    xi = x0
    for i in range(DEPTH):
        kind, j = i % N_MIXERS, i // N_MIXERS
        h = prenorm(xi, scale[i], shift[i], name=f"prenorm{i}")
        lw = layer_weights(i, gathered)
        sv = dict(h=h, lw=lw)
        in_name = ("pool_in", "gla_in", "attn_in")[kind] + str(i)
        if i + 1 < DEPTH:
            proj, gathered = mm(h, lw['w_in'], b_split=True, name=in_name,
                                side=(GATHER, _pack(layer_shards(i + 1), BF16)))
        else:
            proj = mm(h, lw['w_in'], b_split=True, name=in_name)
        if kind == 0:
            ug = proj
            pooled = band(ug, BW, transpose=False, name=f"pool_band{i}")
            zz = gmm_nn(pooled, lw['w_grp'], name=f"pool_grp{i}")
            y2 = gated(zz, ug, BW, pool_scale_full[j:j + 1], name=f"pool_gate{i}")
            sv.update(ug=ug, pooled=pooled, z=zz)
        elif kind == 1:
            sv.update(proj=proj)
            for dname in ('f', 'b'):
                w1p = _pad_cols(w1_full[dname][j], LRP)
                w2p = _pad_rows(w2_full[dname][j], LRP)
                bias = (gla_fwd_b if dname == 'f' else gla_bwd_b)[j:j + 1]
                hw1 = mm(h, w1p, name=f"gla_w1{dname}{i}", tn=LRP)
                zg = mm(hw1, w2p, name=f"gla_w2{dname}{i}", out_dtype=F32, tk=LRP)
                o_d, st_d = gla_fwd(proj, zg, bias, reverse=(dname == 'b'), name=f"gla_scan_{dname}{i}")
                sv.update({f"hw1{dname}": hw1, f"z{dname}": zg, f"o{dname}": o_d, f"st{dname}": st_d,
                           f"w1p{dname}": w1p, f"w2p{dname}": w2p, f"bias{dname}": bias})
            y2 = gla_out(sv['of'], sv['ob'], proj, 2 * BW, gla_norm_g[j:j + 1], name=f"gla_out{i}")
        else:
            QW, KVW, _, _ = _attn_dims(proj)
            qr, kr, vext = attn_prep(proj, attn_q_norm_g[j:j + 1], attn_k_norm_g[j:j + 1], cos, sin,
                                     name=f"attn_prep{i}")
            o, lse = flash_fwd(qr, kr, vext, name=f"attn_flash{i}")
            y2 = attn_gate(o, proj, QW + 2 * KVW, name=f"attn_gate{i}")
            sv.update(proj=proj, qr=qr, kr=kr, o=o, lse=lse)
        y = mm(y2, lw['w_out'], name=f"out_proj{i}")
        sv.update(y2=y2, y=y)
        saved.append(sv)
        xi = resid(xi, y, gate[i], name=f"resid{i}")
        xs.append(xi)

    loss_part, dx, d_final_g = loss_head(xi, final_norm_g.reshape(1, D), target, name="loss_head")

    d_mod = [None] * DEPTH
    layer_grads = [None] * DEPTH
    received = [None] * DEPTH
    small_grads = {}

    def in_dx(i, dproj, w_in, **kw):
        name = ("pool_in_dx", "gla_in_dx", "attn_in_dx")[i % N_MIXERS] + str(i)
        if i + 1 < DEPTH:
            dh, received[i + 1] = mm(dproj, w_in, name=name, **IN_DX_TILES,
                                     side=(SCATTER, _pack(layer_grads[i + 1], BF16, lead=NDEV)), **kw)
            return dh
        return mm(dproj, w_in, name=name, **IN_DX_TILES, **kw)

    for i in reversed(range(DEPTH)):
        kind, j = i % N_MIXERS, i // N_MIXERS
        sv = saved[i]
        h, lw = sv['h'], sv['lw']
        dy, d_gate = resid_bwd(dx, sv['y'], gate[i], name=f"resid_bwd{i}")
        h_t = h.T
        dy2 = mm(dy, lw['w_out_t'], name=f"out_proj_dx{i}")
        g_wout = mm(sv['y2'].T, dy, name=f"out_proj_dw{i}").reshape(NDEV, BW // NDEV, D)
        if kind == 0:
            ug = sv['ug']
            dz, dg, d_ps = gated_bwd(dy2, sv['z'], ug, BW, pool_scale_full[j:j + 1], name=f"pool_gate_bwd{i}")
            dpooled = gmm_nn(dz, lw['w_grp'], tb=True, name=f"pool_grp_dx{i}")
            g_grp = gmm_tn(sv['pooled'], dz, len(POOL_WINDOWS), name=f"pool_grp_dw{i}")
            du = band(dpooled, BW, transpose=True, name=f"pool_band_bwd{i}")
            dproj = jnp.concatenate([du, dg], axis=1)
            dh = in_dx(i, dproj, lw['w_in_t'])
            g_win = mm(h_t, dproj, out_split=pool_w_in.shape[2], name=f"pool_in_dw{i}")
            Gp, Cg = g_grp.shape[0], g_grp.shape[1]
            g_grp = g_grp.astype(BF16).reshape(Gp, NDEV, Cg // NDEV, Cg).transpose(1, 0, 2, 3)
            layer_grads[i] = [g_win, g_grp, g_wout]
            small_grads.setdefault('pool_scale', [None] * n_pool)[j] = d_ps
        elif kind == 1:
            proj = sv['proj']
            do, dg, d_ng = gla_out_bwd(dy2, sv['of'], sv['ob'], proj, 2 * BW, gla_norm_g[j:j + 1],
                                       name=f"gla_out_bwd{i}")
            prev = None
            dh_acc = None
            for dname in ('f', 'b'):
                dq, dk, dv, dzg, dbias = gla_bwd(proj, sv[f"z{dname}"], sv[f"bias{dname}"], sv[f"st{dname}"], do,
                                                 prev, reverse=(dname == 'b'), name=f"gla_scan_bwd_{dname}{i}")
                prev = (dq, dk, dv)
                dhw1 = mm(dzg, sv[f"w2p{dname}"], tb=True, name=f"gla_w2{dname}_dx{i}", tn=LRP)
                g_w2 = mm(sv[f"hw1{dname}"], dzg, ta=True, out_dtype=F32, name=f"gla_w2{dname}_dw{i}", tm=LRP)
                g_w1 = mm(h_t, dhw1, out_dtype=F32, name=f"gla_w1{dname}_dw{i}", tn=LRP)
                dh_acc = mm(dhw1, sv[f"w1p{dname}"], tb=True, add=dh_acc, name=f"gla_w1{dname}_dx{i}", tk=LRP)
                key = 'gla_fwd' if dname == 'f' else 'gla_bwd'
                small_grads[key + '_w1'] = g_w1[:, :LR]
                small_grads[key + '_w2'] = g_w2[:LR]
                small_grads[key + '_b'] = dbias
            dproj = jnp.concatenate([prev[0], prev[1], prev[2], dg], axis=1)
            dh = in_dx(i, dproj, lw['w_in_t'], add=dh_acc)
            layer_grads[i] = [mm(h_t, dproj, out_split=gla_w_in.shape[2], name=f"gla_in_dw{i}"), g_wout]
            small_grads['gla_norm_g'] = d_ng
        else:
            proj = sv['proj']
            QW, KVW, _, _ = _attn_dims(proj)
            do, dg, delta = attn_gate_bwd(dy2, sv['o'], proj, QW + 2 * KVW, name=f"attn_gate_bwd{i}")
            dqr, dkr, dv = flash_bwd(sv['qr'], sv['kr'], proj, QW + KVW, do, sv['lse'], delta,
                                     name=f"attn_flash_bwd{i}")
            dq, dk, d_qg, d_kg = attn_prep_bwd(dqr, dkr, proj, attn_q_norm_g[j:j + 1], attn_k_norm_g[j:j + 1],
                                               cos, sin, name=f"attn_prep_bwd{i}")
            dproj = jnp.concatenate([dq, dk, dv, dg], axis=1)
            dh = in_dx(i, dproj, lw['w_in_t'])
            layer_grads[i] = [mm(h_t, dproj, out_split=attn_w_in.shape[2], name=f"attn_in_dw{i}"), g_wout]
            small_grads['attn_q_norm_g'] = d_qg
            small_grads['attn_k_norm_g'] = d_kg
        dx, d_scale, d_shift = prenorm_bwd(xs[i], dh, dx, scale[i], name=f"prenorm_bwd{i}")
        d_mod[i] = jnp.concatenate([d_shift, d_scale, d_gate], axis=1)
    grad_x = dx.reshape(1, T, D)

    received[0] = exchange(SCATTER, _pack(layer_grads[0], BF16, lead=NDEV), name="exchange_g0")
    per_name = {}
    for i in range(DEPTH):
        summed = sum_slots(received[i], name=f"sum_g{i}")
        parts = _unpack(summed, [a.shape for a in layer_shards(i)])
        for n, g in zip(layer_names[i % N_MIXERS], parts):
            per_name.setdefault(n, []).append(g)
    big_g = {n: jnp.stack(gs) for n, gs in per_name.items()}

    small_order = ['b_mod', 'gla_fwd_b', 'gla_bwd_b', 'gla_norm_g', 'attn_q_norm_g', 'attn_k_norm_g', 'final_norm_g',
                   'pool_scale', 'gla_fwd_w1', 'gla_fwd_w2', 'gla_bwd_w1', 'gla_bwd_w2']
    small_grads['b_mod'] = jnp.concatenate(d_mod, axis=0)
    small_grads['final_norm_g'] = d_final_g
    small_grads['pool_scale'] = jnp.concatenate(small_grads['pool_scale'], axis=0)
    part_items = [jnp.pad(loss_part.reshape(1), (0, LANES - 1))] + [small_grads[n] for n in small_order]
    part_shapes = [a.shape for a in part_items]
    g4 = exchange(GATHER, _pack(part_items, F32), name="gather_parts")
    tot = dict(zip(['loss'] + small_order, _unpack(sum_slots(g4, name="sum_parts"), part_shapes)))
    loss = tot['loss'][0]
    d_mod_all = _unpack(g4, part_shapes, lead=NDEV)[1]

    grads = {}
    grads.update(big_g)
    grads['b_mod'] = tot['b_mod']
    for n in ('gla_fwd_b', 'gla_bwd_b', 'gla_norm_g', 'attn_q_norm_g', 'attn_k_norm_g'):
        grads[n] = tot[n].reshape(W[n].shape)
    grads['final_norm_g'] = tot['final_norm_g'].reshape(D)
    ps_n = pool_scale.shape[1]
    grads['pool_scale'] = lax.dynamic_slice_in_dim(tot['pool_scale'], me * ps_n, ps_n, axis=1)
    rows = gla_fwd_w1.shape[1]
    cols = gla_fwd_w2.shape[2]
    for key in ('gla_fwd', 'gla_bwd'):
        grads[key + '_w1'] = lax.dynamic_slice_in_dim(tot[key + '_w1'], me * rows, rows, axis=0).reshape(1, rows, LR)
        grads[key + '_w2'] = lax.dynamic_slice_in_dim(tot[key + '_w2'], me * cols, cols, axis=1).reshape(1, LR, cols)

    c_t = c_all.T
    dm_slab = lax.dynamic_slice_in_dim(d_mod_all, me * Dm, Dm, axis=2)
    grads['w_mod'] = jnp.stack([outer_silu(c_t, dm_slab[:, i], name=f"mod_dw{i}") for i in range(DEPTH)])

    deltas, new_m, new_v = {}, {}, {}
    for n in WEIGHTS:
        deltas[n], new_m[n], new_v[n] = _adamw_nd(W[n], grads[n], M[n], V[n], name=f"adamw_{n}")

    return (loss, grad_x, *[grads[n] for n in WEIGHTS], *[deltas[n] for n in WEIGHTS],
            *[new_m[n] for n in WEIGHTS], *[new_v[n] for n in WEIGHTS])
```

```python
import functools
import math

import jax
import jax.numpy as jnp
from jax import lax
from jax.experimental import pallas as pl
from jax.experimental.pallas import tpu as pltpu

F32 = jnp.float32
BF16 = jnp.bfloat16
NDEV = 8
LANES = 128
VMEM_LIMIT = 56 * 1024 * 1024

D_MODEL = 2048
DEPTH = 4
N_MIXERS = 3
GRID_W = 64
NORM_EPS = 1e-6
POOL_WINDOWS = (2, 4, 8, 16)
GLA_HEADS = 4
GLA_LOWRANK = 16
GLA_TAU = 16.0
GLA_CHUNK = 64
ATTN_HEAD_DIM = 128
ATTN_KV_HEADS = 4
ROPE_THETA = 10000.0
ADAM_LR = 0.001
ADAM_B1 = 0.9
ADAM_B2 = 0.999
ADAM_EPS = 1e-08
ADAM_WD = 0.01
ADAM_STEP = 10

WEIGHTS = ['w_mod', 'b_mod', 'pool_w_in', 'pool_w_grp', 'pool_scale', 'pool_w_out', 'gla_w_in', 'gla_fwd_w1',
           'gla_fwd_w2', 'gla_fwd_b', 'gla_bwd_w1', 'gla_bwd_w2', 'gla_bwd_b', 'gla_norm_g', 'gla_w_out',
           'attn_w_in', 'attn_q_norm_g', 'attn_k_norm_g', 'attn_w_out', 'final_norm_g']


def _params(n_axes=0):
    sem = ("arbitrary",) * n_axes if n_axes else None
    return pltpu.CompilerParams(dimension_semantics=sem, vmem_limit_bytes=VMEM_LIMIT)


def _silu(g):
    return g * jax.nn.sigmoid(g)


def _dsilu(g):
    s = jax.nn.sigmoid(g)
    return s * (1.0 + g * (1.0 - s))


def _dot(a, b):
    return jnp.dot(a, b, preferred_element_type=F32)


def _dot_nt(a, b):
    return lax.dot_general(a, b, (((1,), (1,)), ((), ())), preferred_element_type=F32)


def _dot_tn(a, b):
    return lax.dot_general(a, b, (((0,), (0,)), ((), ())), preferred_element_type=F32)


def _tile(n, pref):
    if n <= pref:
        return n
    for step in (128, 16, 8):
        t = pref - pref % step
        while t >= step:
            if n % t == 0:
                return t
            t -= step
    raise ValueError((n, pref))


def _peer(k):
    x, y, c = lax.axis_index("x"), lax.axis_index("y"), lax.axis_index("c")
    px = 1 - x if k & 4 else x
    py = 1 - y if k & 2 else y
    pc = 1 - c if k & 1 else c
    return (px, py, pc), 4 * px + 2 * py + pc


def _my_id():
    return 4 * lax.axis_index("x") + 2 * lax.axis_index("y") + lax.axis_index("c")


NCHIP = NDEV // 2
GATHER = "gather"
GATHER_VIA_SIBLING = "g2"
SIBLING_ROWS = "d2d"
CHIP_ROWS = "ici"
ICI_FLIPS = (4, 2, 6)

EXCHANGE_SCRATCH = [pltpu.SemaphoreType.DMA((NDEV - 1,)), pltpu.SemaphoreType.DMA((NDEV - 1,)),
                    pltpu.SemaphoreType.DMA]


def _exchange_slots(kind):
    return NCHIP if kind in (SIBLING_ROWS, CHIP_ROWS) else NDEV


def _exchange_plan(kind, x_ref, o_ref):
    me = _my_id()
    c = lax.axis_index("c")
    chip = me // 2
    sib, sib_id = _peer(1)
    if kind == GATHER:
        sends = [(x_ref, o_ref.at[me], _peer(k)[0], o_ref.at[_peer(k)[1]]) for k in range(1, NDEV)]
        return (x_ref, o_ref.at[me]), sends
    if kind == GATHER_VIA_SIBLING:
        sends = [(x_ref, o_ref.at[me], sib, o_ref.at[sib_id])]
        sends += [(x_ref, o_ref.at[me], _peer(k)[0], o_ref.at[_peer(k)[1]]) for k in ICI_FLIPS]
        sends += [(o_ref.at[_peer(k)[1]], o_ref.at[_peer(k)[1]], sib, o_ref.at[_peer(k ^ 1)[1]]) for k in ICI_FLIPS]
        return (x_ref, o_ref.at[me]), sends
    if kind == SIBLING_ROWS:
        return None, [(x_ref.at[2 * q + (1 - c)], o_ref.at[q], sib, o_ref.at[q]) for q in range(NCHIP)]
    assert kind == CHIP_ROWS, kind
    sends = [(x_ref.at[chip ^ (k >> 1)], o_ref.at[chip], _peer(k)[0], o_ref.at[chip ^ (k >> 1)]) for k in ICI_FLIPS]
    return (x_ref.at[chip], o_ref.at[chip]), sends


def _exchange_copy(n, send, send_sems, recv_sems, incoming):
    src, dst, peer, lands = send
    return pltpu.make_async_remote_copy(
        src_ref=src, dst_ref=lands if incoming else dst, send_sem=send_sems.at[n], recv_sem=recv_sems.at[n],
        device_id=peer, device_id_type=pl.DeviceIdType.MESH)


def _exchange_start(kind, x_ref, o_ref, send_sems, recv_sems, local_sem):
    local, sends = _exchange_plan(kind, x_ref, o_ref)
    if local is not None:
        pltpu.make_async_copy(*local, local_sem).start()
    first = sends[:4] if kind == GATHER_VIA_SIBLING else sends
    for n, send in enumerate(first):
        _exchange_copy(n, send, send_sems, recv_sems, False).start()


def _exchange_wait(kind, x_ref, o_ref, send_sems, recv_sems, local_sem):
    local, sends = _exchange_plan(kind, x_ref, o_ref)
    order = list(range(len(sends)))
    if kind == GATHER_VIA_SIBLING:
        for j in range(3):
            _exchange_copy(1 + j, sends[1 + j], send_sems, recv_sems, True).wait_recv()
            _exchange_copy(4 + j, sends[4 + j], send_sems, recv_sems, False).start()
        order = [0, 4, 5, 6]
    for n in order:
        _exchange_copy(n, sends[n], send_sems, recv_sems, True).wait_recv()
    for n, send in enumerate(sends):
        _exchange_copy(n, send, send_sems, recv_sems, False).wait_send()
    if local is not None:
        pltpu.make_async_copy(*local, local_sem).wait()


def _exchange_out(kind, x):
    return jax.ShapeDtypeStruct((_exchange_slots(kind),) + tuple(x.shape[-2:]), x.dtype)


def exchange(kind, x, name):
    def body(x_ref, o_ref, send_sems, recv_sems, local_sem):
        _exchange_start(kind, x_ref, o_ref, send_sems, recv_sems, local_sem)
        _exchange_wait(kind, x_ref, o_ref, send_sems, recv_sems, local_sem)

    return pl.pallas_call(
        body, name=name, out_shape=_exchange_out(kind, x),
        in_specs=[pl.BlockSpec(memory_space=pl.ANY)],
        out_specs=pl.BlockSpec(memory_space=pl.ANY),
        scratch_shapes=EXCHANGE_SCRATCH,
    )(x)


def _pack(arrs, dtype, lead=None):
    unit = 16 * LANES
    if lead is None:
        flat = [a.astype(dtype).reshape(-1) for a in arrs]
        n = sum(f.shape[0] for f in flat)
        pad = (-n) % unit
        if pad:
            flat.append(jnp.zeros((pad,), dtype))
        return jnp.concatenate(flat).reshape(-1, LANES)
    flat = [a.astype(dtype).reshape(lead, -1) for a in arrs]
    n = sum(f.shape[1] for f in flat)
    pad = (-n) % unit
    if pad:
        flat.append(jnp.zeros((lead, pad), dtype))
    return jnp.concatenate(flat, axis=1).reshape(lead, -1, LANES)


def _unpack(buf, shapes, lead=None):
    out = []
    off = 0
    if lead is None:
        flat = buf.reshape(-1)
        for s in shapes:
            n = math.prod(s)
            out.append(flat[off:off + n].reshape(s))
            off += n
        return out
    flat = buf.reshape(lead, -1)
    for s in shapes:
        n = math.prod(s)
        out.append(flat[:, off:off + n].reshape((lead,) + tuple(s)))
        off += n
    return out


def mm(a, b, *, name, ta=False, tb=False, b_split=False, out_split=0, out_dtype=BF16, tm=2048, tn=1024, tk=None,
       bias=None, add=None, a_silu=False, side=None):
    K, M = a.shape if ta else a.shape[::-1]
    if b_split:
        S, d1, n = b.shape
        if tb:
            N, Kb = d1, S * n
        else:
            Kb, N = d1, S * n
    else:
        Kb, N = b.shape[::-1] if tb else b.shape
    assert Kb == K, (a.shape, b.shape, ta, tb, b_split)
    if tk is None:
        tk = 1024 if ta else 2048
    tm = _tile(M, tm)
    if b_split and not tb:
        tn = _tile(n, tn)
    elif out_split:
        tn = _tile(out_split, tn)
    else:
        tn = _tile(N, tn)
    if b_split and tb:
        tk = _tile(n, 1024)
    else:
        tk = _tile(K, tk)
    nk = K // tk

    a_spec = pl.BlockSpec((tk, tm), lambda i, j, k: (k, i)) if ta else pl.BlockSpec((tm, tk), lambda i, j, k: (i, k))
    if b_split and not tb:
        per = n // tn
        b_spec = pl.BlockSpec((None, tk, tn), lambda i, j, k: (j // per, k, j % per))
    elif b_split and tb:
        per = n // tk
        b_spec = pl.BlockSpec((None, tn, tk), lambda i, j, k: (k // per, j, k % per))
    elif tb:
        b_spec = pl.BlockSpec((tn, tk), lambda i, j, k: (j, k))
    else:
        b_spec = pl.BlockSpec((tk, tn), lambda i, j, k: (k, j))
    if out_split:
        per_o = out_split // tn
        o_spec = pl.BlockSpec((None, tm, tn), lambda i, j, k: (j // per_o, i, j % per_o))
        o_shape = jax.ShapeDtypeStruct((N // out_split, M, out_split), out_dtype)
    else:
        o_spec = pl.BlockSpec((tm, tn), lambda i, j, k: (i, j))
        o_shape = jax.ShapeDtypeStruct((M, N), out_dtype)
    ins = [a, b]
    in_specs = [a_spec, b_spec]
    if bias is not None:
        ins.append(bias)
        in_specs.append(pl.BlockSpec((1, tn), lambda i, j, k: (0, j)))
    if add is not None:
        ins.append(add)
        in_specs.append(pl.BlockSpec((tm, tn), lambda i, j, k: (i, j)))
    has_bias, has_add, has_side = bias is not None, add is not None, side is not None
    out_specs, out_shapes = [o_spec], [o_shape]
    scratch = [pltpu.VMEM((tm, tn) if nk > 1 else (8, LANES), F32)]
    if has_side:
        side_kind, side_x = side
        ins.append(side_x)
        in_specs.append(pl.BlockSpec(memory_space=pl.ANY))
        out_specs.append(pl.BlockSpec(memory_space=pl.ANY))
        out_shapes.append(_exchange_out(side_kind, side_x))
        scratch += EXCHANGE_SCRATCH
    gm, gn = M // tm, N // tn

    def body(*refs):
        a_ref, b_ref = refs[0], refs[1]
        pos = 2
        bias_ref = add_ref = None
        if has_bias:
            bias_ref = refs[pos]
            pos += 1
        if has_add:
            add_ref = refs[pos]
            pos += 1
        if has_side:
            side_refs = (side_kind, refs[pos], refs[pos + 2]) + tuple(refs[pos + 4:pos + 7])
            o_ref, acc_ref = refs[pos + 1], refs[pos + 3]
        else:
            o_ref, acc_ref = refs[pos], refs[pos + 1]
        k = pl.program_id(2)
        if has_side:
            i_, j_ = pl.program_id(0), pl.program_id(1)

            @pl.when((i_ == 0) & (j_ == 0) & (k == 0))
            def _():
                _exchange_start(*side_refs)

        def finish(r):
            if has_bias:
                r = r + bias_ref[...]
            if has_add:
                r = r + add_ref[...].astype(F32)
            o_ref[...] = r.astype(out_dtype)

        av = a_ref[...]
        if a_silu:
            av = _silu(av.astype(F32))
        av = av.astype(BF16)
        bv = b_ref[...].astype(BF16)
        dn = (((0 if ta else 1,), (1 if tb else 0,)), ((), ()))
        part = lax.dot_general(av, bv, dn, preferred_element_type=F32)
        if nk == 1:
            finish(part)
        else:
            @pl.when(k == 0)
            def _():
                acc_ref[...] = part

            @pl.when(k > 0)
            def _():
                acc_ref[...] += part

            @pl.when(k == nk - 1)
            def _():
                finish(acc_ref[...])

        if has_side:
            @pl.when((i_ == gm - 1) & (j_ == gn - 1) & (k == nk - 1))
            def _():
                _exchange_wait(*side_refs)

    res = pl.pallas_call(
        body, name=name, grid=(gm, gn, nk), in_specs=in_specs, out_specs=out_specs, out_shape=out_shapes,
        scratch_shapes=scratch, compiler_params=_params(3))(*ins)
    return (res[0], res[1]) if has_side else res[0]


def gmm_nn(a, w, *, name, tb=False, out_dtype=BF16, tm=512):
    T = a.shape[0]
    G = w.shape[0]
    Kg = a.shape[1] // G
    Ng = w.shape[1] if tb else w.shape[2]
    tm = _tile(T, tm)

    def body(a_ref, w_ref, o_ref):
        wv = w_ref[...].astype(BF16)
        av = a_ref[...].astype(BF16)
        r = _dot_nt(av, wv) if tb else _dot(av, wv)
        o_ref[...] = r.astype(out_dtype)

    return pl.pallas_call(
        body, name=name, grid=(G, T // tm),
        in_specs=[pl.BlockSpec((tm, Kg), lambda g, i: (i, g)),
                  pl.BlockSpec((None,) + tuple(w.shape[1:]), lambda g, i: (g, 0, 0))],
        out_specs=pl.BlockSpec((tm, Ng), lambda g, i: (i, g)),
        out_shape=jax.ShapeDtypeStruct((T, G * Ng), out_dtype), compiler_params=_params(2))(a, w)


def gmm_tn(a, b, G, *, name, tk=512):
    T = a.shape[0]
    Kg = a.shape[1] // G
    Ng = b.shape[1] // G
    tk = _tile(T, tk)

    def body(a_ref, b_ref, o_ref):
        @pl.when(pl.program_id(1) == 0)
        def _():
            o_ref[...] = jnp.zeros_like(o_ref)

        o_ref[...] += _dot_tn(a_ref[...].astype(BF16), b_ref[...].astype(BF16))

    return pl.pallas_call(
        body, name=name, grid=(G, T // tk),
        in_specs=[pl.BlockSpec((tk, Kg), lambda g, i: (i, g)), pl.BlockSpec((tk, Ng), lambda g, i: (i, g))],
        out_specs=pl.BlockSpec((None, Kg, Ng), lambda g, i: (g, 0, 0)),
        out_shape=jax.ShapeDtypeStruct((G, Kg, Ng), F32), compiler_params=_params(2))(a, b)


def _row(tr, w, cb=0):
    return pl.BlockSpec((tr, w), lambda i: (i, cb))


def _bc(w):
    return pl.BlockSpec((1, w), lambda i: (0, 0))


def prenorm(x, scale, shift, *, name):
    T, D = x.shape
    tr = _tile(T, 256)

    def body(x_ref, sc_ref, sh_ref, h_ref, ht_ref):
        xv = x_ref[...]
        rstd = lax.rsqrt(jnp.mean(xv * xv, axis=-1, keepdims=True) + NORM_EPS)
        hv = ((xv * rstd) * (1.0 + sc_ref[...]) + sh_ref[...]).astype(BF16)
        h_ref[...] = hv
        ht_ref[...] = hv.T

    return pl.pallas_call(
        body, name=name, grid=(T // tr,), in_specs=[_row(tr, D), _bc(D), _bc(D)],
        out_specs=[_row(tr, D), pl.BlockSpec((D, tr), lambda i: (0, i))],
        out_shape=[jax.ShapeDtypeStruct((T, D), BF16), jax.ShapeDtypeStruct((D, T), BF16)],
        compiler_params=_params(1))(x, scale, shift)


def prenorm_bwd(x, dh, dxn, scale, *, name):
    T, D = x.shape
    tr = _tile(T, 256)

    def body(x_ref, dh_ref, dxn_ref, sc_ref, dx_ref, dsc_ref, dsh_ref):
        @pl.when(pl.program_id(0) == 0)
        def _():
            dsc_ref[...] = jnp.zeros_like(dsc_ref)
            dsh_ref[...] = jnp.zeros_like(dsh_ref)

        xv = x_ref[...]
        dhv = dh_ref[...].astype(F32)
        rstd = lax.rsqrt(jnp.mean(xv * xv, axis=-1, keepdims=True) + NORM_EPS)
        r = xv * rstd
        dsc_ref[...] += jnp.sum(dhv * r, axis=0, keepdims=True)
        dsh_ref[...] += jnp.sum(dhv, axis=0, keepdims=True)
        dr = dhv * (1.0 + sc_ref[...])
        dx_ref[...] = dxn_ref[...] + rstd * (dr - r * jnp.mean(dr * r, axis=-1, keepdims=True))

    return pl.pallas_call(
        body, name=name, grid=(T // tr,), in_specs=[_row(tr, D), _row(tr, D), _row(tr, D), _bc(D)],
        out_specs=[_row(tr, D), _bc(D), _bc(D)],
        out_shape=[jax.ShapeDtypeStruct((T, D), F32), jax.ShapeDtypeStruct((1, D), F32),
                   jax.ShapeDtypeStruct((1, D), F32)], compiler_params=_params(1))(x, dh, dxn, scale)


def resid(x, y, gate, *, name):
    T, D = x.shape
    tr = _tile(T, 256)

    def body(x_ref, y_ref, g_ref, o_ref):
        o_ref[...] = x_ref[...] + g_ref[...] * y_ref[...].astype(F32)

    return pl.pallas_call(
        body, name=name, grid=(T // tr,), in_specs=[_row(tr, D), _row(tr, D), _bc(D)], out_specs=_row(tr, D),
        out_shape=jax.ShapeDtypeStruct((T, D), F32), compiler_params=_params(1))(x, y, gate)


def resid_bwd(dxn, y, gate, *, name):
    T, D = dxn.shape
    tr = _tile(T, 256)

    def body(d_ref, y_ref, g_ref, dy_ref, dg_ref):
        @pl.when(pl.program_id(0) == 0)
        def _():
            dg_ref[...] = jnp.zeros_like(dg_ref)

        d = d_ref[...]
        dy_ref[...] = (d * g_ref[...]).astype(BF16)
        dg_ref[...] += jnp.sum(d * y_ref[...].astype(F32), axis=0, keepdims=True)

    return pl.pallas_call(
        body, name=name, grid=(T // tr,), in_specs=[_row(tr, D), _row(tr, D), _bc(D)],
        out_specs=[_row(tr, D), _bc(D)],
        out_shape=[jax.ShapeDtypeStruct((T, D), BF16), jax.ShapeDtypeStruct((1, D), F32)],
        compiler_params=_params(1))(dxn, y, gate)


def loss_head(x, g, target, *, name):
    T, D = x.shape
    tr = _tile(T, 256)

    def body(x_ref, g_ref, t_ref, loss_ref, dx_ref, dg_ref):
        @pl.when(pl.program_id(0) == 0)
        def _():
            loss_ref[...] = jnp.zeros_like(loss_ref)
            dg_ref[...] = jnp.zeros_like(dg_ref)

        xv = x_ref[...]
        gv = g_ref[...]
        rstd = lax.rsqrt(jnp.mean(xv * xv, axis=-1, keepdims=True) + NORM_EPS)
        r = xv * rstd
        e = r * gv - t_ref[...]
        loss_ref[...] += 0.5 * jnp.sum(jnp.mean(e * e, axis=-1, keepdims=True), axis=0, keepdims=True)
        dout = e * (1.0 / D)
        dg_ref[...] += jnp.sum(dout * r, axis=0, keepdims=True)
        dr = dout * gv
        dx_ref[...] = rstd * (dr - r * jnp.mean(dr * r, axis=-1, keepdims=True))

    return pl.pallas_call(
        body, name=name, grid=(T // tr,), in_specs=[_row(tr, D), _bc(D), _row(tr, D)],
        out_specs=[pl.BlockSpec((1, 1), lambda i: (0, 0)), _row(tr, D), _bc(D)],
        out_shape=[jax.ShapeDtypeStruct((1, 1), F32), jax.ShapeDtypeStruct((T, D), F32),
                   jax.ShapeDtypeStruct((1, D), F32)], compiler_params=_params(1))(x, g, target)


def _col_tile(W, off):
    cw = math.gcd(W, off) if off else W
    cw = math.gcd(cw, 1024) if cw > 1024 else cw
    return cw


def gated(a, proj, g_off, scale, *, name):
    T, W = a.shape
    cw = _col_tile(W, g_off)
    gb = g_off // cw
    tr = _tile(T, 256)

    def body(a_ref, g_ref, s_ref, o_ref, ot_ref):
        y = (a_ref[...].astype(F32) * s_ref[...] * _silu(g_ref[...].astype(F32))).astype(BF16)
        o_ref[...] = y
        ot_ref[...] = y.T

    return pl.pallas_call(
        body, name=name, grid=(W // cw, T // tr),
        in_specs=[pl.BlockSpec((tr, cw), lambda j, i: (i, j)), pl.BlockSpec((tr, cw), lambda j, i: (i, gb + j)),
                  pl.BlockSpec((1, cw), lambda j, i: (0, j))],
        out_specs=[pl.BlockSpec((tr, cw), lambda j, i: (i, j)), pl.BlockSpec((cw, tr), lambda j, i: (j, i))],
        out_shape=[jax.ShapeDtypeStruct((T, W), BF16), jax.ShapeDtypeStruct((W, T), BF16)],
        compiler_params=_params(2))(a, proj, scale)


def gated_bwd(dy2, a, proj, g_off, scale, *, name):
    T, W = a.shape
    cw = _col_tile(W, g_off)
    gb = g_off // cw
    tr = _tile(T, 256)

    def body(d_ref, a_ref, g_ref, s_ref, da_ref, dg_ref, ds_ref):
        @pl.when(pl.program_id(1) == 0)
        def _():
            ds_ref[...] = jnp.zeros_like(ds_ref)

        d = d_ref[...].astype(F32)
        av = a_ref[...].astype(F32)
        gv = g_ref[...].astype(F32)
        sv = s_ref[...]
        dsg = d * _silu(gv)
        da_ref[...] = (dsg * sv).astype(BF16)
        dg_ref[...] = (d * av * sv * _dsilu(gv)).astype(BF16)
        ds_ref[...] += jnp.sum(dsg * av, axis=0, keepdims=True)

    blk = pl.BlockSpec((tr, cw), lambda j, i: (i, j))
    return pl.pallas_call(
        body, name=name, grid=(W // cw, T // tr),
        in_specs=[blk, blk, pl.BlockSpec((tr, cw), lambda j, i: (i, gb + j)),
                  pl.BlockSpec((1, cw), lambda j, i: (0, j))],
        out_specs=[blk, blk, pl.BlockSpec((1, cw), lambda j, i: (0, j))],
        out_shape=[jax.ShapeDtypeStruct((T, W), BF16), jax.ShapeDtypeStruct((T, W), BF16),
                   jax.ShapeDtypeStruct((1, W), F32)], compiler_params=_params(2))(dy2, a, proj, scale)


HALO = 16


def band(u, W, *, transpose, name):
    T = u.shape[0]
    R = _tile(T, 256)
    nb = T // R
    G = len(POOL_WINDOWS)
    Cg = W // G
    hal = min(HALO, R)

    def body(p_ref, c_ref, n_ref, o_ref):
        i = pl.program_id(0)
        out_pos = lax.broadcasted_iota(jnp.int32, (R, 1), 0) + i * R
        parts = ((p_ref, i * R - hal, hal, R - hal), (c_ref, i * R, R, 0), (n_ref, (i + 1) * R, hal, 0))
        for gi, w in enumerate(POOL_WINDOWS):
            half = w // 2
            cols = slice(gi * Cg, (gi + 1) * Cg)
            acc = jnp.zeros((R, Cg), F32)
            for ref, base, n, r0 in parts:
                src_pos = lax.broadcasted_iota(jnp.int32, (1, n), 1) + base
                valid = (src_pos >= 0) & (src_pos < T)
                src = ref[r0:r0 + n, cols]
                if not transpose:
                    m = (src_pos >= out_pos - half) & (src_pos < out_pos + half) & valid
                else:
                    m = (out_pos >= src_pos - half) & (out_pos < src_pos + half) & valid
                    sp = lax.broadcasted_iota(jnp.int32, (n, 1), 0) + base
                    cnt = jnp.minimum(sp + half, T) - jnp.maximum(sp - half, 0)
                    src = (src.astype(F32) / jnp.maximum(cnt, 1).astype(F32)).astype(BF16)
                acc = acc + _dot(m.astype(BF16), src.astype(BF16))
            if not transpose:
                cnt = jnp.minimum(out_pos + half, T) - jnp.maximum(out_pos - half, 0)
                acc = acc / cnt.astype(F32)
            o_ref[:, cols] = (acc - c_ref[:, cols].astype(F32)).astype(BF16)

    return pl.pallas_call(
        body, name=name, grid=(nb,),
        in_specs=[pl.BlockSpec((R, W), lambda i: (jnp.maximum(i - 1, 0), 0)), pl.BlockSpec((R, W), lambda i: (i, 0)),
                  pl.BlockSpec((R, W), lambda i: (jnp.minimum(i + 1, nb - 1), 0))],
        out_specs=pl.BlockSpec((R, W), lambda i: (i, 0)),
        out_shape=jax.ShapeDtypeStruct((T, W), BF16), compiler_params=_params(1))(u, u, u)


def _log_sigmoid(x):
    return jnp.minimum(x, 0.0) - jnp.log1p(jnp.exp(-jnp.abs(x)))


def _split3(x):
    hi = x.astype(BF16)
    r1 = x - hi.astype(F32)
    md = r1.astype(BF16)
    lo = (r1 - md.astype(F32)).astype(BF16)
    return hi, md, lo


def _tri_sum(tri, x):
    hi, md, lo = _split3(x)
    return _dot(tri, hi) + _dot(tri, md) + _dot(tri, lo)


def _gla_masks(C, reverse):
    row = lax.broadcasted_iota(jnp.int32, (C, C), 0)
    col = lax.broadcasted_iota(jnp.int32, (C, C), 1)
    if not reverse:
        return (col <= row), (col >= row), (col <= row)
    return (col >= row), (col <= row), (col > row)


def _gla_dims(proj):
    VW = proj.shape[1] // 3
    KW = VW // 2
    return KW, VW, KW // GLA_HEADS, VW // GLA_HEADS


def gla_fwd(proj, z, bias, *, reverse, name):
    T = proj.shape[0]
    KW, VW, DK, DV = _gla_dims(proj)
    H = GLA_HEADS
    C = _tile(T, GLA_CHUNK)
    NC = T // C
    cidx = (lambda i: NC - 1 - i) if reverse else (lambda i: i)
    last = 0 if reverse else C - 1

    def body(q_ref, k_ref, v_ref, z_ref, b_ref, o_ref, s_ref, S_scr):
        @pl.when(pl.program_id(0) == 0)
        def _():
            S_scr[...] = jnp.zeros_like(S_scr)

        cum, _, amask = _gla_masks(C, reverse)
        la = _log_sigmoid(z_ref[...] + b_ref[...]) * (1.0 / GLA_TAU)
        b = _tri_sum(cum.astype(BF16), la)
        for h in range(H):
            ks = slice(h * DK, (h + 1) * DK)
            vs = slice(h * DV, (h + 1) * DV)
            bh = b[:, ks]
            mid = bh[C // 2:C // 2 + 1, :]
            bl = bh[last:last + 1, :]
            q = q_ref[:, ks].astype(F32) * (DK ** -0.5)
            k = k_ref[:, ks].astype(F32)
            v = v_ref[:, vs]
            qe = (q * jnp.exp(bh)).astype(BF16)
            qt = (q * jnp.exp(bh - mid)).astype(BF16)
            kt = (k * jnp.exp(mid - bh)).astype(BF16)
            kd = (k * jnp.exp(bl - bh)).astype(BF16)
            St = S_scr[h]
            Sb = St.astype(BF16)
            s_ref[0, h] = Sb
            A = jnp.where(amask, _dot_nt(qt, kt), 0.0).astype(BF16)
            o_ref[:, vs] = _dot_nt(qe, Sb) + _dot(A, v)
            S_scr[h] = St * jnp.exp(bl) + _dot_tn(v, kd)

    return pl.pallas_call(
        body, name=name, grid=(NC,),
        in_specs=[pl.BlockSpec((C, KW), lambda i: (cidx(i), 0)), pl.BlockSpec((C, KW), lambda i: (cidx(i), 1)),
                  pl.BlockSpec((C, VW), lambda i: (cidx(i), 1)), pl.BlockSpec((C, KW), lambda i: (cidx(i), 0)),
                  pl.BlockSpec((1, KW), lambda i: (0, 0))],
        out_specs=[pl.BlockSpec((C, VW), lambda i: (cidx(i), 0)),
                   pl.BlockSpec((1, H, DV, DK), lambda i: (cidx(i), 0, 0, 0))],
        out_shape=[jax.ShapeDtypeStruct((T, VW), F32), jax.ShapeDtypeStruct((NC, H, DV, DK), BF16)],
        scratch_shapes=[pltpu.VMEM((H, DV, DK), F32)], compiler_params=_params(1))(proj, proj, proj, z, bias)


def gla_bwd(proj, z, bias, states, do, prev, *, reverse, name):
    T = proj.shape[0]
    KW, VW, DK, DV = _gla_dims(proj)
    H = GLA_HEADS
    C = _tile(T, GLA_CHUNK)
    NC = T // C
    cidx = (lambda i: i) if reverse else (lambda i: NC - 1 - i)
    last = 0 if reverse else C - 1
    has_prev = prev is not None
    odt = BF16 if has_prev else F32

    def body(*refs):
        q_ref, k_ref, v_ref, z_ref, b_ref, s_ref, do_ref = refs[:7]
        pos = 7
        if has_prev:
            pq_ref, pk_ref, pv_ref = refs[7:10]
            pos = 10
        dq_ref, dk_ref, dv_ref, dz_ref, db_ref, dS_scr = refs[pos:pos + 6]

        @pl.when(pl.program_id(0) == 0)
        def _():
            dS_scr[...] = jnp.zeros_like(dS_scr)
            db_ref[...] = jnp.zeros_like(db_ref)

        cum, cum_t, amask = _gla_masks(C, reverse)
        xg = z_ref[...] + b_ref[...]
        la = _log_sigmoid(xg) * (1.0 / GLA_TAU)
        b = _tri_sum(cum.astype(BF16), la)
        cum_t_bf = cum_t.astype(BF16)
        for h in range(H):
            ks = slice(h * DK, (h + 1) * DK)
            vs = slice(h * DV, (h + 1) * DV)
            bh = b[:, ks]
            mid = bh[C // 2:C // 2 + 1, :]
            bl = bh[last:last + 1, :]
            q = q_ref[:, ks].astype(F32) * (DK ** -0.5)
            k = k_ref[:, ks].astype(F32)
            v = v_ref[:, vs]
            dov = do_ref[:, vs]
            e_b = jnp.exp(bh)
            e_up = jnp.exp(bh - mid)
            e_dn = jnp.exp(mid - bh)
            e_l = jnp.exp(bl - bh)
            e_bl = jnp.exp(bl)
            qe = (q * e_b).astype(BF16)
            qt = (q * e_up).astype(BF16)
            kt = (k * e_dn).astype(BF16)
            kd = (k * e_l).astype(BF16)
            Sb = s_ref[0, h]
            dSt = dS_scr[h]
            dSb = dSt.astype(BF16)
            A = jnp.where(amask, _dot_nt(qt, kt), 0.0).astype(BF16)
            dA = jnp.where(amask, _dot_nt(dov, v), 0.0).astype(BF16)
            dv = _dot_tn(A, dov) + _dot_nt(kd, dSb)
            dq = _dot(dA, kt) * e_up + _dot(dov, Sb) * e_b
            dk_state = _dot(v, dSb) * e_l
            dk = _dot_tn(dA, qt) * e_dn + dk_state
            d_b = dq * q - dk * k
            d_bl = (jnp.sum(dk_state * k, axis=0, keepdims=True)
                    + e_bl * jnp.sum(dSt * Sb.astype(F32), axis=0, keepdims=True))
            d_la = _tri_sum(cum_t_bf, d_b) + d_bl
            dz = d_la * (1.0 / GLA_TAU) * jax.nn.sigmoid(-xg[:, ks])
            dq = dq * (DK ** -0.5)
            if has_prev:
                dq = dq + pq_ref[:, ks]
                dk = dk + pk_ref[:, ks]
                dv = dv + pv_ref[:, vs]
            dq_ref[:, ks] = dq.astype(odt)
            dk_ref[:, ks] = dk.astype(odt)
            dv_ref[:, vs] = dv.astype(odt)
            dz_ref[:, ks] = dz.astype(BF16)
            db_ref[:, ks] += jnp.sum(dz, axis=0, keepdims=True)
            dS_scr[h] = dSt * e_bl + _dot_tn(dov, qe)

    kspec = lambda cb: pl.BlockSpec((C, KW), lambda i: (cidx(i), cb))
    vspec = lambda cb: pl.BlockSpec((C, VW), lambda i: (cidx(i), cb))
    ins = [proj, proj, proj, z, bias, states, do]
    in_specs = [kspec(0), kspec(1), vspec(1), kspec(0), pl.BlockSpec((1, KW), lambda i: (0, 0)),
                pl.BlockSpec((1, H, DV, DK), lambda i: (cidx(i), 0, 0, 0)), vspec(0)]
    if has_prev:
        ins += list(prev)
        in_specs += [kspec(0), kspec(0), vspec(0)]
    return pl.pallas_call(
        body, name=name, grid=(NC,), in_specs=in_specs,
        out_specs=[kspec(0), kspec(0), vspec(0), kspec(0), pl.BlockSpec((1, KW), lambda i: (0, 0))],
        out_shape=[jax.ShapeDtypeStruct((T, KW), odt), jax.ShapeDtypeStruct((T, KW), odt),
                   jax.ShapeDtypeStruct((T, VW), odt), jax.ShapeDtypeStruct((T, KW), BF16),
                   jax.ShapeDtypeStruct((1, KW), F32)],
        scratch_shapes=[pltpu.VMEM((H, DV, DK), F32)], compiler_params=_params(1))(*ins)


def gla_out(o_f, o_b, proj, g_off, norm_g, *, name):
    T, VW = o_f.shape
    H = GLA_HEADS
    DV = VW // H
    gb = g_off // VW
    tr = _tile(T, 256)

    def body(f_ref, b_ref, g_ref, n_ref, y_ref, yt_ref):
        for h in range(H):
            vs = slice(h * DV, (h + 1) * DV)
            o = f_ref[:, vs] + b_ref[:, vs]
            rstd = lax.rsqrt(jnp.mean(o * o, axis=-1, keepdims=True) + NORM_EPS)
            y = (o * rstd * n_ref[...] * _silu(g_ref[:, vs].astype(F32))).astype(BF16)
            y_ref[:, vs] = y
            yt_ref[vs, :] = y.T

    return pl.pallas_call(
        body, name=name, grid=(T // tr,),
        in_specs=[_row(tr, VW), _row(tr, VW), _row(tr, VW, gb), _bc(DV)],
        out_specs=[_row(tr, VW), pl.BlockSpec((VW, tr), lambda i: (0, i))],
        out_shape=[jax.ShapeDtypeStruct((T, VW), BF16), jax.ShapeDtypeStruct((VW, T), BF16)],
        compiler_params=_params(1))(o_f, o_b, proj, norm_g)


def gla_out_bwd(dy2, o_f, o_b, proj, g_off, norm_g, *, name):
    T, VW = o_f.shape
    H = GLA_HEADS
    DV = VW // H
    gb = g_off // VW
    tr = _tile(T, 256)

    def body(d_ref, f_ref, b_ref, g_ref, n_ref, do_ref, dg_ref, dn_ref):
        @pl.when(pl.program_id(0) == 0)
        def _():
            dn_ref[...] = jnp.zeros_like(dn_ref)

        nv = n_ref[...]
        for h in range(H):
            vs = slice(h * DV, (h + 1) * DV)
            o = f_ref[:, vs] + b_ref[:, vs]
            rstd = lax.rsqrt(jnp.mean(o * o, axis=-1, keepdims=True) + NORM_EPS)
            r = o * rstd
            gv = g_ref[:, vs].astype(F32)
            d = d_ref[:, vs].astype(F32)
            dg_ref[:, vs] = (d * r * nv * _dsilu(gv)).astype(BF16)
            dn_o = d * _silu(gv)
            dn_ref[...] += jnp.sum(dn_o * r, axis=0, keepdims=True)
            dr = dn_o * nv
            do_ref[:, vs] = (rstd * (dr - r * jnp.mean(dr * r, axis=-1, keepdims=True))).astype(BF16)

    return pl.pallas_call(
        body, name=name, grid=(T // tr,),
        in_specs=[_row(tr, VW), _row(tr, VW), _row(tr, VW), _row(tr, VW, gb), _bc(DV)],
        out_specs=[_row(tr, VW), _row(tr, VW), _bc(DV)],
        out_shape=[jax.ShapeDtypeStruct((T, VW), BF16), jax.ShapeDtypeStruct((T, VW), BF16),
                   jax.ShapeDtypeStruct((1, DV), F32)], compiler_params=_params(1))(dy2, o_f, o_b, proj, norm_g)


def _rope_tables(T):
    hd = ATTN_HEAD_DIM
    axis_dim = hd // 2
    rows = T // GRID_W
    t = jnp.arange(T)
    row = (t // GRID_W - rows // 2).astype(F32)
    col = (t % GRID_W - GRID_W // 2).astype(F32)
    inv = ROPE_THETA ** (-jnp.arange(0, axis_dim, 2, dtype=F32) / axis_dim)
    ang = jnp.concatenate([row[:, None] * inv, col[:, None] * inv], axis=-1)
    cos = jnp.repeat(jnp.cos(ang), 2, axis=-1)
    sin = jnp.repeat(jnp.sin(ang), 2, axis=-1)
    sign = jnp.where(jnp.arange(hd) % 2 == 0, -1.0, 1.0).astype(F32)
    return cos, sin * sign


def _pair_swap(x):
    n = x.shape[-1]
    lane = lax.broadcasted_iota(jnp.int32, x.shape, x.ndim - 1)
    return jnp.where(lane % 2 == 0, pltpu.roll(x, n - 1, x.ndim - 1), pltpu.roll(x, 1, x.ndim - 1))


def _attn_dims(proj):
    hd = ATTN_HEAD_DIM
    kvw = ATTN_KV_HEADS * hd
    qw = (proj.shape[1] - 2 * kvw) // 2
    return qw, kvw, qw // hd, (qw // hd) // ATTN_KV_HEADS


LOG2E = 1.4426950408889634
LN2 = 0.6931471805599453


def _q_mult():
    return ATTN_HEAD_DIM ** -0.5 * LOG2E


def attn_prep(proj, qg, kg, cos, sin, *, name):
    T = proj.shape[0]
    QW, KVW, NH, G = _attn_dims(proj)
    hd = ATTN_HEAD_DIM
    KV = ATTN_KV_HEADS
    assert QW % KVW == 0
    tr = _tile(T, 256)

    def body(q_ref, k_ref, v_ref, qg_ref, kg_ref, c_ref, s_ref, qo_ref, ko_ref, vo_ref):
        cv, sv = c_ref[...], s_ref[...]

        def one(x, gain, mult):
            rstd = lax.rsqrt(jnp.mean(x * x, axis=-1, keepdims=True) + NORM_EPS)
            xs = x * rstd * gain
            return (xs * cv + _pair_swap(xs) * sv) * mult

        for h in range(NH):
            hs = slice(h * hd, (h + 1) * hd)
            qo_ref[h] = one(q_ref[:, hs].astype(F32), qg_ref[...], _q_mult()).astype(BF16)
        for h in range(KV):
            hs = slice(h * hd, (h + 1) * hd)
            ko_ref[:, hs] = one(k_ref[:, hs].astype(F32), kg_ref[...], 1.0).astype(BF16)
            vo_ref[:, 2 * h * hd:(2 * h + 1) * hd] = v_ref[:, hs]
            vo_ref[:, (2 * h + 1) * hd:(2 * h + 2) * hd] = jnp.ones((tr, hd), BF16)

    return pl.pallas_call(
        body, name=name, grid=(T // tr,),
        in_specs=[_row(tr, QW), _row(tr, KVW, QW // KVW), _row(tr, KVW, QW // KVW + 1), _bc(hd), _bc(hd),
                  _row(tr, hd), _row(tr, hd)],
        out_specs=[pl.BlockSpec((NH, tr, hd), lambda i: (0, i, 0)), _row(tr, KVW), _row(tr, 2 * KVW)],
        out_shape=[jax.ShapeDtypeStruct((NH, T, hd), BF16), jax.ShapeDtypeStruct((T, KVW), BF16),
                   jax.ShapeDtypeStruct((T, 2 * KVW), BF16)],
        compiler_params=_params(1))(proj, proj, proj, qg, kg, cos, sin)


def attn_prep_bwd(dqr, dkr, proj, qg, kg, cos, sin, *, name):
    T = proj.shape[0]
    QW, KVW, NH, G = _attn_dims(proj)
    hd = ATTN_HEAD_DIM
    tr = _tile(T, 256)

    def body(dq_ref, dk_ref, q_ref, k_ref, qg_ref, kg_ref, c_ref, s_ref, oq_ref, ok_ref, dqg_ref, dkg_ref):
        @pl.when(pl.program_id(0) == 0)
        def _():
            dqg_ref[...] = jnp.zeros_like(dqg_ref)
            dkg_ref[...] = jnp.zeros_like(dkg_ref)

        cv, sv = c_ref[...], s_ref[...]

        def one(d, x, gain, mult):
            d = d * mult
            dxs = d * cv - _pair_swap(d) * sv
            rstd = lax.rsqrt(jnp.mean(x * x, axis=-1, keepdims=True) + NORM_EPS)
            xn = x * rstd
            dgain = jnp.sum(dxs * xn, axis=0, keepdims=True)
            dxn = dxs * gain
            return rstd * (dxn - xn * jnp.mean(dxn * xn, axis=-1, keepdims=True)), dgain

        for h in range(NH):
            hs = slice(h * hd, (h + 1) * hd)
            dx, dgain = one(dq_ref[h].astype(F32), q_ref[:, hs].astype(F32), qg_ref[...], _q_mult())
            oq_ref[:, hs] = dx.astype(BF16)
            dqg_ref[...] += dgain
        for h in range(ATTN_KV_HEADS):
            hs = slice(h * hd, (h + 1) * hd)
            dx, dgain = one(dk_ref[:, hs].astype(F32), k_ref[:, hs].astype(F32), kg_ref[...], 1.0)
            ok_ref[:, hs] = dx.astype(BF16)
            dkg_ref[...] += dgain

    return pl.pallas_call(
        body, name=name, grid=(T // tr,),
        in_specs=[pl.BlockSpec((NH, tr, hd), lambda i: (0, i, 0)), _row(tr, KVW), _row(tr, QW),
                  _row(tr, KVW, QW // KVW), _bc(hd), _bc(hd), _row(tr, hd), _row(tr, hd)],
        out_specs=[_row(tr, QW), _row(tr, KVW), _bc(hd), _bc(hd)],
        out_shape=[jax.ShapeDtypeStruct((T, QW), BF16), jax.ShapeDtypeStruct((T, KVW), BF16),
                   jax.ShapeDtypeStruct((1, hd), F32), jax.ShapeDtypeStruct((1, hd), F32)],
        compiler_params=_params(1))(dqr, dkr, proj, proj, qg, kg, cos, sin)


FLASH_BQ = 256
FLASH_BK = 512


def flash_fwd(q, kr, vext, *, name):
    NH, T, hd = q.shape
    KV = ATTN_KV_HEADS
    G = NH // KV
    bq = _tile(T, FLASH_BQ)
    bk = _tile(T, FLASH_BK)
    nk = T // bk

    def body(q_ref, k_ref, v_ref, o_ref, lse_ref, m_scr, acc_scr, sa_scr, sb_scr):
        m_scr[...] = jnp.full_like(m_scr, -jnp.inf)
        acc_scr[...] = jnp.zeros_like(acc_scr)

        def scores(c, s_scr):
            kc = k_ref[pl.ds(pl.multiple_of(c * bk, bk), bk), :]
            for g in range(G):
                s_scr[g] = _dot_nt(q_ref[g], kc)

        def consume(c, s_scr):
            vc = v_ref[pl.ds(pl.multiple_of(c * bk, bk), bk), :]
            for g in range(G):
                s = s_scr[g]
                m_old = m_scr[g]
                m_new = jnp.maximum(m_old, jnp.max(s, axis=-1, keepdims=True))
                p = jnp.exp2(s - m_new)
                acc_scr[g] = jnp.exp2(m_old - m_new) * acc_scr[g] + _dot(p.astype(BF16), vc)
                m_scr[g] = m_new

        scores(0, sa_scr)
        if nk % 2 == 0:
            def pair(t, carry):
                scores(2 * t + 1, sb_scr)
                consume(2 * t, sa_scr)
                scores(2 * t + 2, sa_scr)
                consume(2 * t + 1, sb_scr)
                return carry

            lax.fori_loop(0, nk // 2 - 1, pair, 0)
            scores(nk - 1, sb_scr)
            consume(nk - 2, sa_scr)
            consume(nk - 1, sb_scr)
        else:
            def single(c, carry):
                consume(c, sa_scr)
                scores(c + 1, sa_scr)
                return carry

            lax.fori_loop(0, nk - 1, single, 0)
            consume(nk - 1, sa_scr)
        for g in range(G):
            a = acc_scr[g]
            l = a[:, hd:]
            o_ref[g] = a[:, :hd] / l
            lse_ref[g] = m_scr[g] + jnp.log2(l[:, 0:1])

    return pl.pallas_call(
        body, name=name, grid=(KV, T // bq),
        in_specs=[pl.BlockSpec((G, bq, hd), lambda h, i: (h, i, 0)), pl.BlockSpec((T, hd), lambda h, i: (0, h)),
                  pl.BlockSpec((T, 2 * hd), lambda h, i: (0, h))],
        out_specs=[pl.BlockSpec((G, bq, hd), lambda h, i: (h, i, 0)),
                   pl.BlockSpec((G, bq, 1), lambda h, i: (h, i, 0))],
        out_shape=[jax.ShapeDtypeStruct((NH, T, hd), F32), jax.ShapeDtypeStruct((NH, T, 1), F32)],
        scratch_shapes=[pltpu.VMEM((G, bq, 1), F32), pltpu.VMEM((G, bq, 2 * hd), F32),
                        pltpu.VMEM((G, bq, bk), F32), pltpu.VMEM((G, bq, bk), F32)],
        compiler_params=_params(2))(q, kr, vext)


def flash_bwd(q, kr, proj, v_off, do, lse, delta, *, name):
    NH, T, hd = q.shape
    KV = ATTN_KV_HEADS
    G = NH // KV
    bq = _tile(T, FLASH_BQ)
    bk = _tile(T, FLASH_BK)
    nk = T // bk
    nq = T // bq
    vb = v_off // hd

    def body(q_ref, k_ref, v_ref, do_ref, lse_ref, dl_ref, dq_ref, dk_ref, dv_ref, dq_scr, dk_scr, dv_scr,
             sa_scr, pa_scr, sb_scr, pb_scr):
        i = pl.program_id(1)

        @pl.when(i == 0)
        def _():
            dk_scr[...] = jnp.zeros_like(dk_scr)
            dv_scr[...] = jnp.zeros_like(dv_scr)

        dq_scr[...] = jnp.zeros_like(dq_scr)

        def scores(c, s_scr, dp_scr):
            off = pl.multiple_of(c * bk, bk)
            kc = k_ref[pl.ds(off, bk), :]
            vc = v_ref[pl.ds(off, bk), :]
            for g in range(G):
                s_scr[g] = _dot_nt(q_ref[g], kc)
                dp_scr[g] = _dot_nt(do_ref[g], vc)

        def consume(c, s_scr, dp_scr):
            off = pl.multiple_of(c * bk, bk)
            kc = k_ref[pl.ds(off, bk), :]
            dk_c = jnp.zeros((bk, hd), F32)
            dv_c = jnp.zeros((bk, hd), F32)
            for g in range(G):
                p = jnp.exp2(s_scr[g] - lse_ref[g])
                ds = (p * (dp_scr[g] - dl_ref[g])).astype(BF16)
                dq_scr[g] += _dot(ds, kc)
                dv_c = dv_c + _dot_tn(p.astype(BF16), do_ref[g])
                dk_c = dk_c + _dot_tn(ds, q_ref[g])
            dk_scr[pl.ds(off, bk), :] += dk_c
            dv_scr[pl.ds(off, bk), :] += dv_c

        scores(0, sa_scr, pa_scr)
        if nk % 2 == 0:
            def pair(t, carry):
                scores(2 * t + 1, sb_scr, pb_scr)
                consume(2 * t, sa_scr, pa_scr)
                scores(2 * t + 2, sa_scr, pa_scr)
                consume(2 * t + 1, sb_scr, pb_scr)
                return carry

            lax.fori_loop(0, nk // 2 - 1, pair, 0)
            scores(nk - 1, sb_scr, pb_scr)
            consume(nk - 2, sa_scr, pa_scr)
            consume(nk - 1, sb_scr, pb_scr)
        else:
            def single(c, carry):
                consume(c, sa_scr, pa_scr)
                scores(c + 1, sa_scr, pa_scr)
                return carry

            lax.fori_loop(0, nk - 1, single, 0)
            consume(nk - 1, sa_scr, pa_scr)
        dq_ref[...] = dq_scr[...].astype(BF16)

        @pl.when(i == nq - 1)
        def _():
            dk_ref[...] = dk_scr[...].astype(BF16)
            dv_ref[...] = (dv_scr[...] * LOG2E).astype(BF16)

    qspec = pl.BlockSpec((G, bq, hd), lambda h, i: (h, i, 0))
    cspec = pl.BlockSpec((G, bq, 1), lambda h, i: (h, i, 0))
    kspec = pl.BlockSpec((T, hd), lambda h, i: (0, h))
    return pl.pallas_call(
        body, name=name, grid=(KV, nq),
        in_specs=[qspec, kspec, pl.BlockSpec((T, hd), lambda h, i: (0, vb + h)), qspec, cspec, cspec],
        out_specs=[qspec, kspec, kspec],
        out_shape=[jax.ShapeDtypeStruct((NH, T, hd), BF16), jax.ShapeDtypeStruct((T, KV * hd), BF16),
                   jax.ShapeDtypeStruct((T, KV * hd), BF16)],
        scratch_shapes=[pltpu.VMEM((G, bq, hd), F32), pltpu.VMEM((T, hd), F32), pltpu.VMEM((T, hd), F32)]
        + [pltpu.VMEM((G, bq, bk), F32)] * 4,
        compiler_params=_params(2))(q, kr, proj, do, lse, delta)


def attn_gate(o, proj, g_off, *, name):
    NH, T, hd = o.shape
    W = NH * hd
    cw = _col_tile(W, g_off)
    hc = cw // hd
    gb = g_off // cw
    tr = _tile(T, 256)

    def body(o_ref, g_ref, y_ref, yt_ref):
        for h in range(hc):
            hs = slice(h * hd, (h + 1) * hd)
            y = (o_ref[h] * _silu(g_ref[:, hs].astype(F32))).astype(BF16)
            y_ref[:, hs] = y
            yt_ref[hs, :] = y.T

    return pl.pallas_call(
        body, name=name, grid=(W // cw, T // tr),
        in_specs=[pl.BlockSpec((hc, tr, hd), lambda j, i: (j, i, 0)), pl.BlockSpec((tr, cw), lambda j, i: (i, gb + j))],
        out_specs=[pl.BlockSpec((tr, cw), lambda j, i: (i, j)), pl.BlockSpec((cw, tr), lambda j, i: (j, i))],
        out_shape=[jax.ShapeDtypeStruct((T, W), BF16), jax.ShapeDtypeStruct((W, T), BF16)],
        compiler_params=_params(2))(o, proj)


def attn_gate_bwd(dy2, o, proj, g_off, *, name):
    NH, T, hd = o.shape
    W = NH * hd
    cw = _col_tile(W, g_off)
    hc = cw // hd
    gb = g_off // cw
    tr = _tile(T, 256)

    def body(d_ref, o_ref, g_ref, do_ref, dg_ref, dl_ref):
        for h in range(hc):
            hs = slice(h * hd, (h + 1) * hd)
            d = d_ref[:, hs].astype(F32)
            gv = g_ref[:, hs].astype(F32)
            ov = o_ref[h]
            dov = d * _silu(gv) * LN2
            do_ref[h] = dov.astype(BF16)
            dg_ref[:, hs] = (d * ov * _dsilu(gv)).astype(BF16)
            dl_ref[h] = jnp.sum(dov * ov, axis=-1, keepdims=True)

    return pl.pallas_call(
        body, name=name, grid=(W // cw, T // tr),
        in_specs=[pl.BlockSpec((tr, cw), lambda j, i: (i, j)), pl.BlockSpec((hc, tr, hd), lambda j, i: (j, i, 0)),
                  pl.BlockSpec((tr, cw), lambda j, i: (i, gb + j))],
        out_specs=[pl.BlockSpec((hc, tr, hd), lambda j, i: (j, i, 0)), pl.BlockSpec((tr, cw), lambda j, i: (i, j)),
                   pl.BlockSpec((hc, tr, 1), lambda j, i: (j, i, 0))],
        out_shape=[jax.ShapeDtypeStruct((NH, T, hd), BF16), jax.ShapeDtypeStruct((T, W), BF16),
                   jax.ShapeDtypeStruct((NH, T, 1), F32)], compiler_params=_params(2))(dy2, o, proj)


def outer_silu(c_t, dm, *, name):
    K, B = c_t.shape
    N = dm.shape[1]
    tk = _tile(K, 256)

    def body(c_ref, d_ref, o_ref):
        s = _silu(c_ref[...])
        acc = jnp.zeros((tk, N), F32)
        for b in range(B):
            acc = acc + s[:, b:b + 1] * d_ref[b:b + 1, :]
        o_ref[...] = acc

    return pl.pallas_call(
        body, name=name, grid=(K // tk,),
        in_specs=[pl.BlockSpec((tk, B), lambda i: (i, 0)), pl.BlockSpec((B, N), lambda i: (0, 0))],
        out_specs=pl.BlockSpec((tk, N), lambda i: (i, 0)),
        out_shape=jax.ShapeDtypeStruct((K, N), F32), compiler_params=_params(1))(c_t, dm)


def add_core_rows(core, a, b, *, name):
    S, R, C = b.shape
    tr = _tile(R, 2048)

    def body(core_ref, a_ref, b_ref, o_ref):
        o_ref[...] = (a_ref[...].astype(F32) + b_ref[...].astype(F32)).astype(BF16)

    blk = pl.BlockSpec((None, tr, C), lambda q, i, core_ref: (q, i, 0))
    grid_spec = pltpu.PrefetchScalarGridSpec(
        num_scalar_prefetch=1, grid=(S, R // tr),
        in_specs=[pl.BlockSpec((None, tr, C), lambda q, i, core_ref: (2 * q + core_ref[0], i, 0)), blk],
        out_specs=blk)
    return pl.pallas_call(
        body, name=name, grid_spec=grid_spec, out_shape=jax.ShapeDtypeStruct((S, R, C), BF16),
        compiler_params=_params(2))(core, a, b)


def sum_slots(x, *, name):
    S, R, C = x.shape
    tr = _tile(R, 512)

    def body(x_ref, o_ref):
        acc = x_ref[0].astype(F32)
        for s in range(1, S):
            acc = acc + x_ref[s].astype(F32)
        o_ref[...] = acc

    return pl.pallas_call(
        body, name=name, grid=(R // tr,), in_specs=[pl.BlockSpec((S, tr, C), lambda i: (0, i, 0))],
        out_specs=pl.BlockSpec((tr, C), lambda i: (i, 0)),
        out_shape=jax.ShapeDtypeStruct((R, C), F32), compiler_params=_params(1))(x)


def adamw(w, g, m, v, *, name):
    R, C = w.shape
    tr = _tile(R, 512) if R % 8 == 0 else R

    def body(w_ref, g_ref, m_ref, v_ref, d_ref, nm_ref, nv_ref):
        gv = g_ref[...]
        mn = ADAM_B1 * m_ref[...] + (1.0 - ADAM_B1) * gv
        vn = ADAM_B2 * v_ref[...] + (1.0 - ADAM_B2) * jnp.square(gv)
        m_hat = mn / (1.0 - ADAM_B1 ** ADAM_STEP)
        v_hat = vn / (1.0 - ADAM_B2 ** ADAM_STEP)
        d_ref[...] = -ADAM_LR * (m_hat / (jnp.sqrt(v_hat) + ADAM_EPS) + ADAM_WD * w_ref[...])
        nm_ref[...] = mn
        nv_ref[...] = vn

    blk = pl.BlockSpec((tr, C), lambda i: (i, 0))
    sh = jax.ShapeDtypeStruct((R, C), F32)
    return pl.pallas_call(
        body, name=name, grid=(R // tr,), in_specs=[blk] * 4, out_specs=[blk] * 3, out_shape=[sh] * 3,
        compiler_params=_params(1))(w, g, m, v)


def _adamw_nd(w, g, m, v, name):
    shp = w.shape
    if w.ndim == 1:
        two = (1, shp[0])
    else:
        two = (math.prod(shp[:-1]), shp[-1])
    d, nm, nv = adamw(w.reshape(two), g.reshape(two), m.reshape(two), v.reshape(two), name=name)
    return d.reshape(shp), nm.reshape(shp), nv.reshape(shp)


IN_DX_TILES = dict(tm=1024, tn=512, tk=8192)


def _pad_cols(w, n):
    return jnp.pad(w, ((0, 0), (0, n - w.shape[1])))


def _pad_rows(w, n):
    return jnp.pad(w, ((0, n - w.shape[0]), (0, 0)))


def kernel(x, c, w_mod, b_mod, pool_w_in, pool_w_grp, pool_scale, pool_w_out, gla_w_in, gla_fwd_w1, gla_fwd_w2, gla_fwd_b, gla_bwd_w1, gla_bwd_w2, gla_bwd_b, gla_norm_g, gla_w_out, attn_w_in, attn_q_norm_g, attn_k_norm_g, attn_w_out, final_norm_g, loss_target, m_w_mod, m_b_mod, m_pool_w_in, m_pool_w_grp, m_pool_scale, m_pool_w_out, m_gla_w_in, m_gla_fwd_w1, m_gla_fwd_w2, m_gla_fwd_b, m_gla_bwd_w1, m_gla_bwd_w2, m_gla_bwd_b, m_gla_norm_g, m_gla_w_out, m_attn_w_in, m_attn_q_norm_g, m_attn_k_norm_g, m_attn_w_out, m_final_norm_g, v_w_mod, v_b_mod, v_pool_w_in, v_pool_w_grp, v_pool_scale, v_pool_w_out, v_gla_w_in, v_gla_fwd_w1, v_gla_fwd_w2, v_gla_fwd_b, v_gla_bwd_w1, v_gla_bwd_w2, v_gla_bwd_b, v_gla_norm_g, v_gla_w_out, v_attn_w_in, v_attn_q_norm_g, v_attn_k_norm_g, v_attn_w_out, v_final_norm_g):
    W = dict(w_mod=w_mod, b_mod=b_mod, pool_w_in=pool_w_in, pool_w_grp=pool_w_grp, pool_scale=pool_scale,
             pool_w_out=pool_w_out, gla_w_in=gla_w_in, gla_fwd_w1=gla_fwd_w1, gla_fwd_w2=gla_fwd_w2,
             gla_fwd_b=gla_fwd_b, gla_bwd_w1=gla_bwd_w1, gla_bwd_w2=gla_bwd_w2, gla_bwd_b=gla_bwd_b,
             gla_norm_g=gla_norm_g, gla_w_out=gla_w_out, attn_w_in=attn_w_in, attn_q_norm_g=attn_q_norm_g,
             attn_k_norm_g=attn_k_norm_g, attn_w_out=attn_w_out, final_norm_g=final_norm_g)
    M = dict(w_mod=m_w_mod, b_mod=m_b_mod, pool_w_in=m_pool_w_in, pool_w_grp=m_pool_w_grp, pool_scale=m_pool_scale,
             pool_w_out=m_pool_w_out, gla_w_in=m_gla_w_in, gla_fwd_w1=m_gla_fwd_w1, gla_fwd_w2=m_gla_fwd_w2,
             gla_fwd_b=m_gla_fwd_b, gla_bwd_w1=m_gla_bwd_w1, gla_bwd_w2=m_gla_bwd_w2, gla_bwd_b=m_gla_bwd_b,
             gla_norm_g=m_gla_norm_g, gla_w_out=m_gla_w_out, attn_w_in=m_attn_w_in, attn_q_norm_g=m_attn_q_norm_g,
             attn_k_norm_g=m_attn_k_norm_g, attn_w_out=m_attn_w_out, final_norm_g=m_final_norm_g)
    V = dict(w_mod=v_w_mod, b_mod=v_b_mod, pool_w_in=v_pool_w_in, pool_w_grp=v_pool_w_grp, pool_scale=v_pool_scale,
             pool_w_out=v_pool_w_out, gla_w_in=v_gla_w_in, gla_fwd_w1=v_gla_fwd_w1, gla_fwd_w2=v_gla_fwd_w2,
             gla_fwd_b=v_gla_fwd_b, gla_bwd_w1=v_gla_bwd_w1, gla_bwd_w2=v_gla_bwd_w2, gla_bwd_b=v_gla_bwd_b,
             gla_norm_g=v_gla_norm_g, gla_w_out=v_gla_w_out, attn_w_in=v_attn_w_in, attn_q_norm_g=v_attn_q_norm_g,
             attn_k_norm_g=v_attn_k_norm_g, attn_w_out=v_attn_w_out, final_norm_g=v_final_norm_g)

    me = _my_id()
    T, D = x.shape[1], x.shape[2]
    x0 = x.reshape(T, D)
    target = loss_target.reshape(T, D)
    BW = D
    n_pool, n_gla, n_attn = pool_w_in.shape[0], gla_w_in.shape[0], attn_w_in.shape[0]
    LR = GLA_LOWRANK
    LRP = LANES
    KWg = gla_fwd_w2.shape[2] * NDEV
    Dm = w_mod.shape[2]

    small_shard_names = ['pool_scale', 'gla_fwd_w1', 'gla_fwd_w2', 'gla_bwd_w1', 'gla_bwd_w2']
    small_items = [c] + [W[n] for n in small_shard_names]
    small_shapes = [a.shape for a in small_items]
    g1 = exchange(GATHER, _pack(small_items, F32), name="gather_small")
    c_all_, ps_all, fw1_all, fw2_all, bw1_all, bw2_all = _unpack(g1, small_shapes, lead=NDEV)
    c_all = c_all_.reshape(NDEV, D)
    pool_scale_full = jnp.transpose(ps_all, (1, 0, 2)).reshape(n_pool, BW)
    w1_full = {'f': fw1_all.transpose(1, 0, 2, 3).reshape(n_gla, D, LR),
               'b': bw1_all.transpose(1, 0, 2, 3).reshape(n_gla, D, LR)}
    w2_full = {'f': fw2_all.transpose(1, 2, 0, 3).reshape(n_gla, LR, KWg),
               'b': bw2_all.transpose(1, 2, 0, 3).reshape(n_gla, LR, KWg)}

    c16 = _pad_rows(c_all, 16)
    b_slab = lax.dynamic_slice_in_dim(b_mod, me * Dm, Dm, axis=1)
    mod_parts = [mm(c16, w_mod[i], name=f"mod_fwd{i}", out_dtype=F32, a_silu=True, tm=16, tn=Dm,
                    bias=b_slab[i:i + 1])[:NDEV] for i in range(DEPTH)]
    mod_slab = jnp.stack(mod_parts)
    g2 = exchange(GATHER, _pack([mod_slab], F32), name="gather_mod")
    (mod_all,) = _unpack(g2, [mod_slab.shape], lead=NDEV)
    mod_mine = lax.dynamic_index_in_dim(mod_all, me, axis=2, keepdims=False)
    mod_mine = mod_mine.transpose(1, 0, 2).reshape(DEPTH, NDEV * Dm)
    shift = [mod_mine[i:i + 1, 0:D] for i in range(DEPTH)]
    scale = [mod_mine[i:i + 1, D:2 * D] for i in range(DEPTH)]
    gate = [mod_mine[i:i + 1, 2 * D:3 * D] for i in range(DEPTH)]

    layer_names = {0: ['pool_w_in', 'pool_w_grp', 'pool_w_out'], 1: ['gla_w_in', 'gla_w_out'],
                   2: ['attn_w_in', 'attn_w_out']}

    def layer_shards(i):
        return [W[n][i // N_MIXERS] for n in layer_names[i % N_MIXERS]]

    def layer_weights(i, gathered):
        parts = _unpack(gathered, [a.shape for a in layer_shards(i)], lead=NDEV)
        w_out = parts[-1].reshape(BW, D)
        out = dict(w_in=parts[0], w_out=w_out, w_in_t=parts[0].transpose(0, 2, 1).reshape(-1, D), w_out_t=w_out.T)
        if i % N_MIXERS == 0:
            g = parts[1]
            out['w_grp'] = g.transpose(1, 0, 2, 3).reshape(g.shape[1], -1, g.shape[3])
        return out

    gathered = exchange(GATHER_VIA_SIBLING, _pack(layer_shards(0), BF16), name="gather_w0")

    cos, sin = _rope_tables(T)

    xs = [x0]
    saved = []
    xi = x0
    for i in range(DEPTH):
        kind, j = i % N_MIXERS, i // N_MIXERS
        h, h_t = prenorm(xi, scale[i], shift[i], name=f"prenorm{i}")
        lw = layer_weights(i, gathered)
        sv = dict(h=h, h_t=h_t, lw=lw)
        in_name = ("pool_in", "gla_in", "attn_in")[kind] + str(i)
        if i + 1 < DEPTH:
            proj, gathered = mm(h, lw['w_in'], b_split=True, name=in_name,
                                side=(GATHER_VIA_SIBLING, _pack(layer_shards(i + 1), BF16)))
        else:
            proj = mm(h, lw['w_in'], b_split=True, name=in_name)
        if kind == 0:
            ug = proj
            pooled = band(ug, BW, transpose=False, name=f"pool_band{i}")
            zz = gmm_nn(pooled, lw['w_grp'], name=f"pool_grp{i}")
            y2, y2_t = gated(zz, ug, BW, pool_scale_full[j:j + 1], name=f"pool_gate{i}")
            sv.update(ug=ug, pooled=pooled, z=zz)
        elif kind == 1:
            sv.update(proj=proj)
            for dname in ('f', 'b'):
                w1p = _pad_cols(w1_full[dname][j], LRP)
                w2p = _pad_rows(w2_full[dname][j], LRP)
                bias = (gla_fwd_b if dname == 'f' else gla_bwd_b)[j:j + 1]
                hw1 = mm(h, w1p, name=f"gla_w1{dname}{i}", tn=LRP)
                zg = mm(hw1, w2p, name=f"gla_w2{dname}{i}", out_dtype=F32, tk=LRP)
                o_d, st_d = gla_fwd(proj, zg, bias, reverse=(dname == 'b'), name=f"gla_scan_{dname}{i}")
                sv.update({f"hw1{dname}": hw1, f"z{dname}": zg, f"o{dname}": o_d, f"st{dname}": st_d,
                           f"w1p{dname}": w1p, f"w2p{dname}": w2p, f"bias{dname}": bias})
            y2, y2_t = gla_out(sv['of'], sv['ob'], proj, 2 * BW, gla_norm_g[j:j + 1], name=f"gla_out{i}")
        else:
            QW, KVW, _, _ = _attn_dims(proj)
            qr, kr, vext = attn_prep(proj, attn_q_norm_g[j:j + 1], attn_k_norm_g[j:j + 1], cos, sin,
                                     name=f"attn_prep{i}")
            o, lse = flash_fwd(qr, kr, vext, name=f"attn_flash{i}")
            y2, y2_t = attn_gate(o, proj, QW + 2 * KVW, name=f"attn_gate{i}")
            sv.update(proj=proj, qr=qr, kr=kr, o=o, lse=lse)
        y = mm(y2, lw['w_out'], name=f"out_proj{i}")
        sv.update(y2_t=y2_t, y=y)
        saved.append(sv)
        xi = resid(xi, y, gate[i], name=f"resid{i}")
        xs.append(xi)

    loss_part, dx, d_final_g = loss_head(xi, final_norm_g.reshape(1, D), target, name="loss_head")

    d_mod = [None] * DEPTH
    layer_grads = [None] * DEPTH
    received = [None] * DEPTH
    small_grads = {}

    def in_dx(i, dproj, w_in, **kw):
        name = ("pool_in_dx", "gla_in_dx", "attn_in_dx")[i % N_MIXERS] + str(i)
        if i + 1 < DEPTH:
            dh, received[i + 1] = mm(dproj, w_in, name=name, **IN_DX_TILES, side=(CHIP_ROWS, chip_sums[i + 1]), **kw)
            return dh
        return mm(dproj, w_in, name=name, **IN_DX_TILES, **kw)

    def chip_sum(i, packed, from_sibling):
        core = lax.axis_index("c").astype(jnp.int32).reshape(1)
        return add_core_rows(core, packed, from_sibling, name=f"chip_sum_g{i}")

    chip_sums = [None] * DEPTH
    for i in reversed(range(DEPTH)):
        kind, j = i % N_MIXERS, i // N_MIXERS
        sv = saved[i]
        h_t, lw = sv['h_t'], sv['lw']
        dy, d_gate = resid_bwd(dx, sv['y'], gate[i], name=f"resid_bwd{i}")
        if i + 1 < DEPTH:
            packed = _pack(layer_grads[i + 1], BF16, lead=NDEV)
            dy2, from_sibling = mm(dy, lw['w_out_t'], name=f"out_proj_dx{i}", side=(SIBLING_ROWS, packed))
            chip_sums[i + 1] = chip_sum(i + 1, packed, from_sibling)
        else:
            dy2 = mm(dy, lw['w_out_t'], name=f"out_proj_dx{i}")
        g_wout = mm(sv['y2_t'], dy, name=f"out_proj_dw{i}").reshape(NDEV, BW // NDEV, D)
        if kind == 0:
            ug = sv['ug']
            dz, dg, d_ps = gated_bwd(dy2, sv['z'], ug, BW, pool_scale_full[j:j + 1], name=f"pool_gate_bwd{i}")
            dpooled = gmm_nn(dz, lw['w_grp'], tb=True, name=f"pool_grp_dx{i}")
            g_grp = gmm_tn(sv['pooled'], dz, len(POOL_WINDOWS), name=f"pool_grp_dw{i}")
            du = band(dpooled, BW, transpose=True, name=f"pool_band_bwd{i}")
            dproj = jnp.concatenate([du, dg], axis=1)
            dh = in_dx(i, dproj, lw['w_in_t'])
            g_win = mm(h_t, dproj, out_split=pool_w_in.shape[2], name=f"pool_in_dw{i}")
            Gp, Cg = g_grp.shape[0], g_grp.shape[1]
            g_grp = g_grp.astype(BF16).reshape(Gp, NDEV, Cg // NDEV, Cg).transpose(1, 0, 2, 3)
            layer_grads[i] = [g_win, g_grp, g_wout]
            small_grads.setdefault('pool_scale', [None] * n_pool)[j] = d_ps
        elif kind == 1:
            proj = sv['proj']
            do, dg, d_ng = gla_out_bwd(dy2, sv['of'], sv['ob'], proj, 2 * BW, gla_norm_g[j:j + 1],
                                       name=f"gla_out_bwd{i}")
            prev = None
            dh_acc = None
            for dname in ('f', 'b'):
                dq, dk, dv, dzg, dbias = gla_bwd(proj, sv[f"z{dname}"], sv[f"bias{dname}"], sv[f"st{dname}"], do,
                                                 prev, reverse=(dname == 'b'), name=f"gla_scan_bwd_{dname}{i}")
                prev = (dq, dk, dv)
                dhw1 = mm(dzg, sv[f"w2p{dname}"], tb=True, name=f"gla_w2{dname}_dx{i}", tn=LRP)
                g_w2 = mm(sv[f"hw1{dname}"], dzg, ta=True, out_dtype=F32, name=f"gla_w2{dname}_dw{i}", tm=LRP)
                g_w1 = mm(h_t, dhw1, out_dtype=F32, name=f"gla_w1{dname}_dw{i}", tn=LRP)
                dh_acc = mm(dhw1, sv[f"w1p{dname}"], tb=True, add=dh_acc, name=f"gla_w1{dname}_dx{i}", tk=LRP)
                key = 'gla_fwd' if dname == 'f' else 'gla_bwd'
                small_grads[key + '_w1'] = g_w1[:, :LR]
                small_grads[key + '_w2'] = g_w2[:LR]
                small_grads[key + '_b'] = dbias
            dproj = jnp.concatenate([prev[0], prev[1], prev[2], dg], axis=1)
            dh = in_dx(i, dproj, lw['w_in_t'], add=dh_acc)
            layer_grads[i] = [mm(h_t, dproj, out_split=gla_w_in.shape[2], name=f"gla_in_dw{i}"), g_wout]
            small_grads['gla_norm_g'] = d_ng
        else:
            proj = sv['proj']
            QW, KVW, _, _ = _attn_dims(proj)
            do, dg, delta = attn_gate_bwd(dy2, sv['o'], proj, QW + 2 * KVW, name=f"attn_gate_bwd{i}")
            dqr, dkr, dv = flash_bwd(sv['qr'], sv['kr'], proj, QW + KVW, do, sv['lse'], delta,
                                     name=f"attn_flash_bwd{i}")
            dq, dk, d_qg, d_kg = attn_prep_bwd(dqr, dkr, proj, attn_q_norm_g[j:j + 1], attn_k_norm_g[j:j + 1],
                                               cos, sin, name=f"attn_prep_bwd{i}")
            dproj = jnp.concatenate([dq, dk, dv, dg], axis=1)
            dh = in_dx(i, dproj, lw['w_in_t'])
            layer_grads[i] = [mm(h_t, dproj, out_split=attn_w_in.shape[2], name=f"attn_in_dw{i}"), g_wout]
            small_grads['attn_q_norm_g'] = d_qg
            small_grads['attn_k_norm_g'] = d_kg
        dx, d_scale, d_shift = prenorm_bwd(xs[i], dh, dx, scale[i], name=f"prenorm_bwd{i}")
        d_mod[i] = jnp.concatenate([d_shift, d_scale, d_gate], axis=1)
    grad_x = dx.reshape(1, T, D)

    packed = _pack(layer_grads[0], BF16, lead=NDEV)
    chip_sums[0] = chip_sum(0, packed, exchange(SIBLING_ROWS, packed, name="exchange_g0_sibling"))
    received[0] = exchange(CHIP_ROWS, chip_sums[0], name="exchange_g0_chips")
    per_name = {}
    for i in range(DEPTH):
        summed = sum_slots(received[i], name=f"sum_g{i}")
        parts = _unpack(summed, [a.shape for a in layer_shards(i)])
        for n, g in zip(layer_names[i % N_MIXERS], parts):
            per_name.setdefault(n, []).append(g)
    big_g = {n: jnp.stack(gs) for n, gs in per_name.items()}

    small_order = ['b_mod', 'gla_fwd_b', 'gla_bwd_b', 'gla_norm_g', 'attn_q_norm_g', 'attn_k_norm_g', 'final_norm_g',
                   'pool_scale', 'gla_fwd_w1', 'gla_fwd_w2', 'gla_bwd_w1', 'gla_bwd_w2']
    small_grads['b_mod'] = jnp.concatenate(d_mod, axis=0)
    small_grads['final_norm_g'] = d_final_g
    small_grads['pool_scale'] = jnp.concatenate(small_grads['pool_scale'], axis=0)
    part_items = [jnp.pad(loss_part.reshape(1), (0, LANES - 1))] + [small_grads[n] for n in small_order]
    part_shapes = [a.shape for a in part_items]
    g4 = exchange(GATHER, _pack(part_items, F32), name="gather_parts")
    tot = dict(zip(['loss'] + small_order, _unpack(sum_slots(g4, name="sum_parts"), part_shapes)))
    loss = tot['loss'][0]
    d_mod_all = _unpack(g4, part_shapes, lead=NDEV)[1]

    grads = {}
    grads.update(big_g)
    grads['b_mod'] = tot['b_mod']
    for n in ('gla_fwd_b', 'gla_bwd_b', 'gla_norm_g', 'attn_q_norm_g', 'attn_k_norm_g'):
        grads[n] = tot[n].reshape(W[n].shape)
    grads['final_norm_g'] = tot['final_norm_g'].reshape(D)
    ps_n = pool_scale.shape[1]
    grads['pool_scale'] = lax.dynamic_slice_in_dim(tot['pool_scale'], me * ps_n, ps_n, axis=1)
    rows = gla_fwd_w1.shape[1]
    cols = gla_fwd_w2.shape[2]
    for key in ('gla_fwd', 'gla_bwd'):
        grads[key + '_w1'] = lax.dynamic_slice_in_dim(tot[key + '_w1'], me * rows, rows, axis=0).reshape(1, rows, LR)
        grads[key + '_w2'] = lax.dynamic_slice_in_dim(tot[key + '_w2'], me * cols, cols, axis=1).reshape(1, LR, cols)

    c_t = c_all.T
    dm_slab = lax.dynamic_slice_in_dim(d_mod_all, me * Dm, Dm, axis=2)
    grads['w_mod'] = jnp.stack([outer_silu(c_t, dm_slab[:, i], name=f"mod_dw{i}") for i in range(DEPTH)])

    deltas, new_m, new_v = {}, {}, {}
    for n in WEIGHTS:
        deltas[n], new_m[n], new_v[n] = _adamw_nd(W[n], grads[n], M[n], V[n], name=f"adamw_{n}")

    return (loss, grad_x, *[grads[n] for n in WEIGHTS], *[deltas[n] for n in WEIGHTS],
            *[new_m[n] for n in WEIGHTS], *[new_v[n] for n in WEIGHTS])
```

```python
import functools
import math

import jax
import jax.numpy as jnp
from jax import lax
from jax.experimental import pallas as pl
from jax.experimental.pallas import tpu as pltpu

F32 = jnp.float32
BF16 = jnp.bfloat16
NDEV = 8
LANES = 128
VMEM_LIMIT = 56 * 1024 * 1024

D_MODEL = 2048
DEPTH = 4
N_MIXERS = 3
GRID_W = 64
NORM_EPS = 1e-6
POOL_WINDOWS = (2, 4, 8, 16)
GLA_HEADS = 4
GLA_LOWRANK = 16
GLA_TAU = 16.0
GLA_CHUNK = 128
ATTN_HEAD_DIM = 128
ATTN_KV_HEADS = 4
ROPE_THETA = 10000.0
ADAM_LR = 0.001
ADAM_B1 = 0.9
ADAM_B2 = 0.999
ADAM_EPS = 1e-08
ADAM_WD = 0.01
ADAM_STEP = 10

WEIGHTS = ['w_mod', 'b_mod', 'pool_w_in', 'pool_w_grp', 'pool_scale', 'pool_w_out', 'gla_w_in', 'gla_fwd_w1',
           'gla_fwd_w2', 'gla_fwd_b', 'gla_bwd_w1', 'gla_bwd_w2', 'gla_bwd_b', 'gla_norm_g', 'gla_w_out',
           'attn_w_in', 'attn_q_norm_g', 'attn_k_norm_g', 'attn_w_out', 'final_norm_g']


def _params(n_axes=0):
    sem = ("arbitrary",) * n_axes if n_axes else None
    return pltpu.CompilerParams(dimension_semantics=sem, vmem_limit_bytes=VMEM_LIMIT)


def _silu(g):
    return g * jax.nn.sigmoid(g)


def _dsilu(g):
    s = jax.nn.sigmoid(g)
    return s * (1.0 + g * (1.0 - s))


def _dot(a, b):
    return jnp.dot(a, b, preferred_element_type=F32)


def _dot_nt(a, b):
    return lax.dot_general(a, b, (((1,), (1,)), ((), ())), preferred_element_type=F32)


def _dot_tn(a, b):
    return lax.dot_general(a, b, (((0,), (0,)), ((), ())), preferred_element_type=F32)


def _tile(n, pref):
    if n <= pref:
        return n
    for step in (128, 16, 8):
        t = pref - pref % step
        while t >= step:
            if n % t == 0:
                return t
            t -= step
    raise ValueError((n, pref))


def _peer(k):
    x, y, c = lax.axis_index("x"), lax.axis_index("y"), lax.axis_index("c")
    px = 1 - x if k & 4 else x
    py = 1 - y if k & 2 else y
    pc = 1 - c if k & 1 else c
    return (px, py, pc), 4 * px + 2 * py + pc


def _my_id():
    return 4 * lax.axis_index("x") + 2 * lax.axis_index("y") + lax.axis_index("c")


NCHIP = NDEV // 2
GATHER = "gather"
GATHER_VIA_SIBLING = "g2"
SIBLING_ROWS = "d2d"
CHIP_ROWS = "ici"
ICI_FLIPS = (4, 2, 6)

EXCHANGE_SCRATCH = [pltpu.SemaphoreType.DMA((NDEV - 1,)), pltpu.SemaphoreType.DMA((NDEV - 1,)),
                    pltpu.SemaphoreType.DMA]


def _exchange_slots(kind):
    return NCHIP if kind in (SIBLING_ROWS, CHIP_ROWS) else NDEV


def _exchange_plan(kind, x_ref, o_ref):
    me = _my_id()
    c = lax.axis_index("c")
    chip = me // 2
    sib, sib_id = _peer(1)
    if kind == GATHER:
        sends = [(x_ref, o_ref.at[me], _peer(k)[0], o_ref.at[_peer(k)[1]]) for k in range(1, NDEV)]
        return (x_ref, o_ref.at[me]), sends
    if kind == GATHER_VIA_SIBLING:
        sends = [(x_ref, o_ref.at[me], sib, o_ref.at[sib_id])]
        sends += [(x_ref, o_ref.at[me], _peer(k)[0], o_ref.at[_peer(k)[1]]) for k in ICI_FLIPS]
        sends += [(o_ref.at[_peer(k)[1]], o_ref.at[_peer(k)[1]], sib, o_ref.at[_peer(k ^ 1)[1]]) for k in ICI_FLIPS]
        return (x_ref, o_ref.at[me]), sends
    if kind == SIBLING_ROWS:
        return None, [(x_ref.at[2 * q + (1 - c)], o_ref.at[q], sib, o_ref.at[q]) for q in range(NCHIP)]
    assert kind == CHIP_ROWS, kind
    sends = [(x_ref.at[chip ^ (k >> 1)], o_ref.at[chip], _peer(k)[0], o_ref.at[chip ^ (k >> 1)]) for k in ICI_FLIPS]
    return (x_ref.at[chip], o_ref.at[chip]), sends


def _exchange_copy(n, send, send_sems, recv_sems, incoming):
    src, dst, peer, lands = send
    return pltpu.make_async_remote_copy(
        src_ref=src, dst_ref=lands if incoming else dst, send_sem=send_sems.at[n], recv_sem=recv_sems.at[n],
        device_id=peer, device_id_type=pl.DeviceIdType.MESH)


def _exchange_start(kind, x_ref, o_ref, send_sems, recv_sems, local_sem):
    local, sends = _exchange_plan(kind, x_ref, o_ref)
    if local is not None:
        pltpu.make_async_copy(*local, local_sem).start()
    first = sends[:4] if kind == GATHER_VIA_SIBLING else sends
    for n, send in enumerate(first):
        _exchange_copy(n, send, send_sems, recv_sems, False).start()


def _exchange_wait(kind, x_ref, o_ref, send_sems, recv_sems, local_sem):
    local, sends = _exchange_plan(kind, x_ref, o_ref)
    order = list(range(len(sends)))
    if kind == GATHER_VIA_SIBLING:
        for j in range(3):
            _exchange_copy(1 + j, sends[1 + j], send_sems, recv_sems, True).wait_recv()
            _exchange_copy(4 + j, sends[4 + j], send_sems, recv_sems, False).start()
        order = [0, 4, 5, 6]
    for n in order:
        _exchange_copy(n, sends[n], send_sems, recv_sems, True).wait_recv()
    for n, send in enumerate(sends):
        _exchange_copy(n, send, send_sems, recv_sems, False).wait_send()
    if local is not None:
        pltpu.make_async_copy(*local, local_sem).wait()


def _exchange_out(kind, x):
    return jax.ShapeDtypeStruct((_exchange_slots(kind),) + tuple(x.shape[-2:]), x.dtype)


def exchange(kind, x, name):
    def body(x_ref, o_ref, send_sems, recv_sems, local_sem):
        _exchange_start(kind, x_ref, o_ref, send_sems, recv_sems, local_sem)
        _exchange_wait(kind, x_ref, o_ref, send_sems, recv_sems, local_sem)

    return pl.pallas_call(
        body, name=name, out_shape=_exchange_out(kind, x),
        in_specs=[pl.BlockSpec(memory_space=pl.ANY)],
        out_specs=pl.BlockSpec(memory_space=pl.ANY),
        scratch_shapes=EXCHANGE_SCRATCH,
    )(x)


def _pack(arrs, dtype, lead=None):
    unit = 16 * LANES
    if lead is None:
        flat = [a.astype(dtype).reshape(-1) for a in arrs]
        n = sum(f.shape[0] for f in flat)
        pad = (-n) % unit
        if pad:
            flat.append(jnp.zeros((pad,), dtype))
        return jnp.concatenate(flat).reshape(-1, LANES)
    flat = [a.astype(dtype).reshape(lead, -1) for a in arrs]
    n = sum(f.shape[1] for f in flat)
    pad = (-n) % unit
    if pad:
        flat.append(jnp.zeros((lead, pad), dtype))
    return jnp.concatenate(flat, axis=1).reshape(lead, -1, LANES)


def _unpack(buf, shapes, lead=None):
    out = []
    off = 0
    if lead is None:
        flat = buf.reshape(-1)
        for s in shapes:
            n = math.prod(s)
            out.append(flat[off:off + n].reshape(s))
            off += n
        return out
    flat = buf.reshape(lead, -1)
    for s in shapes:
        n = math.prod(s)
        out.append(flat[:, off:off + n].reshape((lead,) + tuple(s)))
        off += n
    return out


def mm(a, b, *, name, ta=False, tb=False, b_split=False, out_split=0, out_dtype=BF16, tm=2048, tn=1024, tk=None,
       bias=None, add=None, a_silu=False, a_scale=None, col_scale=None, side=None):
    K, M = a.shape if ta else a.shape[::-1]
    if b_split:
        S, d1, n = b.shape
        if tb:
            N, Kb = d1, S * n
        else:
            Kb, N = d1, S * n
    else:
        Kb, N = b.shape[::-1] if tb else b.shape
    assert Kb == K, (a.shape, b.shape, ta, tb, b_split)
    if tk is None:
        tk = 1024 if ta else 2048
    tm = _tile(M, tm)
    if b_split and not tb:
        tn = _tile(n, tn)
    elif out_split:
        tn = _tile(out_split, tn)
    else:
        tn = _tile(N, tn)
    if b_split and tb:
        tk = _tile(n, 1024)
    else:
        tk = _tile(K, tk)
    nk = K // tk

    a_spec = pl.BlockSpec((tk, tm), lambda i, j, k: (k, i)) if ta else pl.BlockSpec((tm, tk), lambda i, j, k: (i, k))
    if b_split and not tb:
        per = n // tn
        b_spec = pl.BlockSpec((None, tk, tn), lambda i, j, k: (j // per, k, j % per))
    elif b_split and tb:
        per = n // tk
        b_spec = pl.BlockSpec((None, tn, tk), lambda i, j, k: (k // per, j, k % per))
    elif tb:
        b_spec = pl.BlockSpec((tn, tk), lambda i, j, k: (j, k))
    else:
        b_spec = pl.BlockSpec((tk, tn), lambda i, j, k: (k, j))
    if out_split:
        per_o = out_split // tn
        o_spec = pl.BlockSpec((None, tm, tn), lambda i, j, k: (j // per_o, i, j % per_o))
        o_shape = jax.ShapeDtypeStruct((N // out_split, M, out_split), out_dtype)
    else:
        o_spec = pl.BlockSpec((tm, tn), lambda i, j, k: (i, j))
        o_shape = jax.ShapeDtypeStruct((M, N), out_dtype)
    ins = [a, b]
    in_specs = [a_spec, b_spec]
    if bias is not None:
        ins.append(bias)
        in_specs.append(pl.BlockSpec((1, tn), lambda i, j, k: (0, j)))
    if add is not None:
        ins.append(add)
        in_specs.append(pl.BlockSpec((tm, tn), lambda i, j, k: (i, j)))
    if a_scale is not None:
        assert not ta
        ins.append(a_scale)
        in_specs.append(pl.BlockSpec((1, tk), lambda i, j, k: (0, k)))
    if col_scale is not None:
        ins.append(col_scale)
        in_specs.append(pl.BlockSpec((1, tn), lambda i, j, k: (0, j)))
    has_bias, has_add, has_side = bias is not None, add is not None, side is not None
    has_ascale, has_cscale = a_scale is not None, col_scale is not None
    out_specs, out_shapes = [o_spec], [o_shape]
    scratch = [pltpu.VMEM((tm, tn) if nk > 1 else (8, LANES), F32)]
    if has_side:
        side_kind, side_x = side
        ins.append(side_x)
        in_specs.append(pl.BlockSpec(memory_space=pl.ANY))
        out_specs.append(pl.BlockSpec(memory_space=pl.ANY))
        out_shapes.append(_exchange_out(side_kind, side_x))
        scratch += EXCHANGE_SCRATCH
    gm, gn = M // tm, N // tn

    def body(*refs):
        a_ref, b_ref = refs[0], refs[1]
        pos = 2
        bias_ref = add_ref = None
        if has_bias:
            bias_ref = refs[pos]
            pos += 1
        if has_add:
            add_ref = refs[pos]
            pos += 1
        if has_ascale:
            ascale_ref = refs[pos]
            pos += 1
        if has_cscale:
            cscale_ref = refs[pos]
            pos += 1
        if has_side:
            side_refs = (side_kind, refs[pos], refs[pos + 2]) + tuple(refs[pos + 4:pos + 7])
            o_ref, acc_ref = refs[pos + 1], refs[pos + 3]
        else:
            o_ref, acc_ref = refs[pos], refs[pos + 1]
        k = pl.program_id(2)
        if has_side:
            i_, j_ = pl.program_id(0), pl.program_id(1)

            @pl.when((i_ == 0) & (j_ == 0) & (k == 0))
            def _():
                _exchange_start(*side_refs)

        def finish(r):
            if has_cscale:
                r = r * cscale_ref[...]
            if has_bias:
                r = r + bias_ref[...]
            if has_add:
                r = r + add_ref[...].astype(F32)
            o_ref[...] = r.astype(out_dtype)

        av = a_ref[...]
        if a_silu:
            av = _silu(av.astype(F32))
        if has_ascale:
            av = av.astype(F32) * ascale_ref[...]
        av = av.astype(BF16)
        bv = b_ref[...].astype(BF16)
        dn = (((0 if ta else 1,), (1 if tb else 0,)), ((), ()))
        part = lax.dot_general(av, bv, dn, preferred_element_type=F32)
        if nk == 1:
            finish(part)
        else:
            @pl.when(k == 0)
            def _():
                acc_ref[...] = part

            @pl.when(k > 0)
            def _():
                acc_ref[...] += part

            @pl.when(k == nk - 1)
            def _():
                finish(acc_ref[...])

        if has_side:
            @pl.when((i_ == gm - 1) & (j_ == gn - 1) & (k == nk - 1))
            def _():
                _exchange_wait(*side_refs)

    res = pl.pallas_call(
        body, name=name, grid=(gm, gn, nk), in_specs=in_specs, out_specs=out_specs, out_shape=out_shapes,
        scratch_shapes=scratch, compiler_params=_params(3))(*ins)
    return (res[0], res[1]) if has_side else res[0]


def gmm_nn(a, w, *, name, tb=False, out_dtype=BF16, tm=2048):
    T = a.shape[0]
    G = w.shape[0]
    Kg = a.shape[1] // G
    Ng = w.shape[1] if tb else w.shape[2]
    tm = _tile(T, tm)

    def body(a_ref, w_ref, o_ref):
        wv = w_ref[...].astype(BF16)
        av = a_ref[...].astype(BF16)
        r = _dot_nt(av, wv) if tb else _dot(av, wv)
        o_ref[...] = r.astype(out_dtype)

    return pl.pallas_call(
        body, name=name, grid=(G, T // tm),
        in_specs=[pl.BlockSpec((tm, Kg), lambda g, i: (i, g)),
                  pl.BlockSpec((None,) + tuple(w.shape[1:]), lambda g, i: (g, 0, 0))],
        out_specs=pl.BlockSpec((tm, Ng), lambda g, i: (i, g)),
        out_shape=jax.ShapeDtypeStruct((T, G * Ng), out_dtype), compiler_params=_params(2))(a, w)


def gmm_tn(a, b, G, *, name, tk=2048):
    T = a.shape[0]
    Kg = a.shape[1] // G
    Ng = b.shape[1] // G
    tk = _tile(T, tk)

    def body(a_ref, b_ref, o_ref):
        @pl.when(pl.program_id(1) == 0)
        def _():
            o_ref[...] = jnp.zeros_like(o_ref)

        o_ref[...] += _dot_tn(a_ref[...].astype(BF16), b_ref[...].astype(BF16))

    return pl.pallas_call(
        body, name=name, grid=(G, T // tk),
        in_specs=[pl.BlockSpec((tk, Kg), lambda g, i: (i, g)), pl.BlockSpec((tk, Ng), lambda g, i: (i, g))],
        out_specs=pl.BlockSpec((None, Kg, Ng), lambda g, i: (g, 0, 0)),
        out_shape=jax.ShapeDtypeStruct((G, Kg, Ng), F32), compiler_params=_params(2))(a, b)


def _row(tr, w, cb=0):
    return pl.BlockSpec((tr, w), lambda i: (i, cb))


def _bc(w):
    return pl.BlockSpec((1, w), lambda i: (0, 0))


def prenorm(x, scale, shift, *, name):
    T, D = x.shape
    tr = _tile(T, 256)

    def body(x_ref, sc_ref, sh_ref, h_ref, ht_ref):
        xv = x_ref[...]
        rstd = lax.rsqrt(jnp.mean(xv * xv, axis=-1, keepdims=True) + NORM_EPS)
        hv = ((xv * rstd) * (1.0 + sc_ref[...]) + sh_ref[...]).astype(BF16)
        h_ref[...] = hv
        ht_ref[...] = hv.T

    return pl.pallas_call(
        body, name=name, grid=(T // tr,), in_specs=[_row(tr, D), _bc(D), _bc(D)],
        out_specs=[_row(tr, D), pl.BlockSpec((D, tr), lambda i: (0, i))],
        out_shape=[jax.ShapeDtypeStruct((T, D), BF16), jax.ShapeDtypeStruct((D, T), BF16)],
        compiler_params=_params(1))(x, scale, shift)


def prenorm_bwd(x, dh, dxn, scale, *, name):
    T, D = x.shape
    tr = _tile(T, 256)

    def body(x_ref, dh_ref, dxn_ref, sc_ref, dx_ref, dsc_ref, dsh_ref):
        @pl.when(pl.program_id(0) == 0)
        def _():
            dsc_ref[...] = jnp.zeros_like(dsc_ref)
            dsh_ref[...] = jnp.zeros_like(dsh_ref)

        xv = x_ref[...]
        dhv = dh_ref[...].astype(F32)
        rstd = lax.rsqrt(jnp.mean(xv * xv, axis=-1, keepdims=True) + NORM_EPS)
        r = xv * rstd
        dsc_ref[...] += jnp.sum(dhv * r, axis=0, keepdims=True)
        dsh_ref[...] += jnp.sum(dhv, axis=0, keepdims=True)
        dr = dhv * (1.0 + sc_ref[...])
        dx_ref[...] = dxn_ref[...] + rstd * (dr - r * jnp.mean(dr * r, axis=-1, keepdims=True))

    return pl.pallas_call(
        body, name=name, grid=(T // tr,), in_specs=[_row(tr, D), _row(tr, D), _row(tr, D), _bc(D)],
        out_specs=[_row(tr, D), _bc(D), _bc(D)],
        out_shape=[jax.ShapeDtypeStruct((T, D), F32), jax.ShapeDtypeStruct((1, D), F32),
                   jax.ShapeDtypeStruct((1, D), F32)], compiler_params=_params(1))(x, dh, dxn, scale)


def out_proj_grads(g, w, gate, *, name):
    K, D = g.shape
    tr = _tile(K, 256)

    def body(g_ref, w_ref, gate_ref, dw_ref, dg_ref):
        @pl.when(pl.program_id(0) == 0)
        def _():
            dg_ref[...] = jnp.zeros_like(dg_ref)

        gv = g_ref[...]
        dw_ref[...] = (gv * gate_ref[...]).astype(BF16)
        dg_ref[...] += jnp.sum(gv * w_ref[...].astype(F32), axis=0, keepdims=True)

    return pl.pallas_call(
        body, name=name, grid=(K // tr,), in_specs=[_row(tr, D), _row(tr, D), _bc(D)],
        out_specs=[_row(tr, D), _bc(D)],
        out_shape=[jax.ShapeDtypeStruct((K, D), BF16), jax.ShapeDtypeStruct((1, D), F32)],
        compiler_params=_params(1))(g, w, gate)


def loss_head(x, g, target, *, name):
    T, D = x.shape
    tr = _tile(T, 256)

    def body(x_ref, g_ref, t_ref, loss_ref, dx_ref, dg_ref):
        @pl.when(pl.program_id(0) == 0)
        def _():
            loss_ref[...] = jnp.zeros_like(loss_ref)
            dg_ref[...] = jnp.zeros_like(dg_ref)

        xv = x_ref[...]
        gv = g_ref[...]
        rstd = lax.rsqrt(jnp.mean(xv * xv, axis=-1, keepdims=True) + NORM_EPS)
        r = xv * rstd
        e = r * gv - t_ref[...]
        loss_ref[...] += 0.5 * jnp.sum(jnp.mean(e * e, axis=-1, keepdims=True), axis=0, keepdims=True)
        dout = e * (1.0 / D)
        dg_ref[...] += jnp.sum(dout * r, axis=0, keepdims=True)
        dr = dout * gv
        dx_ref[...] = rstd * (dr - r * jnp.mean(dr * r, axis=-1, keepdims=True))

    return pl.pallas_call(
        body, name=name, grid=(T // tr,), in_specs=[_row(tr, D), _bc(D), _row(tr, D)],
        out_specs=[pl.BlockSpec((1, 1), lambda i: (0, 0)), _row(tr, D), _bc(D)],
        out_shape=[jax.ShapeDtypeStruct((1, 1), F32), jax.ShapeDtypeStruct((T, D), F32),
                   jax.ShapeDtypeStruct((1, D), F32)], compiler_params=_params(1))(x, g, target)


def _col_tile(W, off):
    cw = math.gcd(W, off) if off else W
    cw = math.gcd(cw, 1024) if cw > 1024 else cw
    return cw


def gated(a, proj, g_off, scale, *, name):
    T, W = a.shape
    cw = _col_tile(W, g_off)
    gb = g_off // cw
    tr = _tile(T, 256)

    def body(a_ref, g_ref, s_ref, o_ref, ot_ref):
        y = (a_ref[...].astype(F32) * s_ref[...] * _silu(g_ref[...].astype(F32))).astype(BF16)
        o_ref[...] = y
        ot_ref[...] = y.T

    return pl.pallas_call(
        body, name=name, grid=(W // cw, T // tr),
        in_specs=[pl.BlockSpec((tr, cw), lambda j, i: (i, j)), pl.BlockSpec((tr, cw), lambda j, i: (i, gb + j)),
                  pl.BlockSpec((1, cw), lambda j, i: (0, j))],
        out_specs=[pl.BlockSpec((tr, cw), lambda j, i: (i, j)), pl.BlockSpec((cw, tr), lambda j, i: (j, i))],
        out_shape=[jax.ShapeDtypeStruct((T, W), BF16), jax.ShapeDtypeStruct((W, T), BF16)],
        compiler_params=_params(2))(a, proj, scale)


def gated_bwd(dy2, a, proj, g_off, scale, *, name):
    T, W = a.shape
    cw = _col_tile(W, g_off)
    gb = g_off // cw
    tr = _tile(T, 256)

    def body(d_ref, a_ref, g_ref, s_ref, da_ref, dg_ref, ds_ref):
        @pl.when(pl.program_id(1) == 0)
        def _():
            ds_ref[...] = jnp.zeros_like(ds_ref)

        d = d_ref[...].astype(F32)
        av = a_ref[...].astype(F32)
        gv = g_ref[...].astype(F32)
        sv = s_ref[...]
        dsg = d * _silu(gv)
        da_ref[...] = (dsg * sv).astype(BF16)
        dg_ref[...] = (d * av * sv * _dsilu(gv)).astype(BF16)
        ds_ref[...] += jnp.sum(dsg * av, axis=0, keepdims=True)

    blk = pl.BlockSpec((tr, cw), lambda j, i: (i, j))
    return pl.pallas_call(
        body, name=name, grid=(W // cw, T // tr),
        in_specs=[blk, blk, pl.BlockSpec((tr, cw), lambda j, i: (i, gb + j)),
                  pl.BlockSpec((1, cw), lambda j, i: (0, j))],
        out_specs=[blk, blk, pl.BlockSpec((1, cw), lambda j, i: (0, j))],
        out_shape=[jax.ShapeDtypeStruct((T, W), BF16), jax.ShapeDtypeStruct((T, W), BF16),
                   jax.ShapeDtypeStruct((1, W), F32)], compiler_params=_params(2))(dy2, a, proj, scale)


HALO = 16


def band(u, W, *, transpose, name):
    T = u.shape[0]
    R = _tile(T, 256)
    nb = T // R
    G = len(POOL_WINDOWS)
    Cg = W // G
    hal = min(HALO, R)

    def body(p_ref, c_ref, n_ref, o_ref):
        i = pl.program_id(0)
        out_pos = lax.broadcasted_iota(jnp.int32, (R, 1), 0) + i * R
        parts = ((p_ref, i * R - hal, hal, R - hal), (c_ref, i * R, R, 0), (n_ref, (i + 1) * R, hal, 0))
        for gi, w in enumerate(POOL_WINDOWS):
            half = w // 2
            cols = slice(gi * Cg, (gi + 1) * Cg)
            acc = jnp.zeros((R, Cg), F32)
            for ref, base, n, r0 in parts:
                src_pos = lax.broadcasted_iota(jnp.int32, (1, n), 1) + base
                valid = (src_pos >= 0) & (src_pos < T)
                src = ref[r0:r0 + n, cols]
                if not transpose:
                    m = (src_pos >= out_pos - half) & (src_pos < out_pos + half) & valid
                else:
                    m = (out_pos >= src_pos - half) & (out_pos < src_pos + half) & valid
                    sp = lax.broadcasted_iota(jnp.int32, (n, 1), 0) + base
                    cnt = jnp.minimum(sp + half, T) - jnp.maximum(sp - half, 0)
                    src = (src.astype(F32) / jnp.maximum(cnt, 1).astype(F32)).astype(BF16)
                acc = acc + _dot(m.astype(BF16), src.astype(BF16))
            if not transpose:
                cnt = jnp.minimum(out_pos + half, T) - jnp.maximum(out_pos - half, 0)
                acc = acc / cnt.astype(F32)
            o_ref[:, cols] = (acc - c_ref[:, cols].astype(F32)).astype(BF16)

    return pl.pallas_call(
        body, name=name, grid=(nb,),
        in_specs=[pl.BlockSpec((R, W), lambda i: (jnp.maximum(i - 1, 0), 0)), pl.BlockSpec((R, W), lambda i: (i, 0)),
                  pl.BlockSpec((R, W), lambda i: (jnp.minimum(i + 1, nb - 1), 0))],
        out_specs=pl.BlockSpec((R, W), lambda i: (i, 0)),
        out_shape=jax.ShapeDtypeStruct((T, W), BF16), compiler_params=_params(1))(u, u, u)


def _log_sigmoid(x):
    return jnp.minimum(x, 0.0) - jnp.log1p(jnp.exp(-jnp.abs(x)))


def _split3(x):
    hi = x.astype(BF16)
    r1 = x - hi.astype(F32)
    md = r1.astype(BF16)
    lo = (r1 - md.astype(F32)).astype(BF16)
    return hi, md, lo


def _tri_sum(tri, x):
    hi, md, lo = _split3(x)
    return _dot(tri, hi) + _dot(tri, md) + _dot(tri, lo)


def _gla_masks(C, reverse):
    row = lax.broadcasted_iota(jnp.int32, (C, C), 0)
    col = lax.broadcasted_iota(jnp.int32, (C, C), 1)
    if not reverse:
        return (col <= row), (col >= row), (col <= row)
    return (col >= row), (col <= row), (col > row)


def _gla_dims(proj):
    VW = proj.shape[1] // 3
    KW = VW // 2
    return KW, VW, KW // GLA_HEADS, VW // GLA_HEADS


def gla_fwd(proj, z, bias, *, reverse, name):
    T = proj.shape[0]
    KW, VW, DK, DV = _gla_dims(proj)
    H = GLA_HEADS
    C = _tile(T, GLA_CHUNK)
    NC = T // C
    cidx = (lambda i: NC - 1 - i) if reverse else (lambda i: i)
    last = 0 if reverse else C - 1

    def body(q_ref, k_ref, v_ref, z_ref, b_ref, o_ref, s_ref, S_scr):
        @pl.when(pl.program_id(0) == 0)
        def _():
            S_scr[...] = jnp.zeros_like(S_scr)

        cum, _, amask = _gla_masks(C, reverse)
        la = _log_sigmoid(z_ref[...] + b_ref[...]) * (1.0 / GLA_TAU)
        b = _tri_sum(cum.astype(BF16), la)
        for h in range(H):
            ks = slice(h * DK, (h + 1) * DK)
            vs = slice(h * DV, (h + 1) * DV)
            bh = b[:, ks]
            mid = bh[C // 2:C // 2 + 1, :]
            bl = bh[last:last + 1, :]
            q = q_ref[:, ks].astype(F32) * (DK ** -0.5)
            k = k_ref[:, ks].astype(F32)
            v = v_ref[:, vs]
            qe = (q * jnp.exp(bh)).astype(BF16)
            qt = (q * jnp.exp(bh - mid)).astype(BF16)
            kt = (k * jnp.exp(mid - bh)).astype(BF16)
            kd = (k * jnp.exp(bl - bh)).astype(BF16)
            St = S_scr[h]
            Sb = St.astype(BF16)
            s_ref[0, h] = Sb
            A = jnp.where(amask, _dot_nt(qt, kt), 0.0).astype(BF16)
            o_ref[:, vs] = _dot_nt(qe, Sb) + _dot(A, v)
            S_scr[h] = St * jnp.exp(bl) + _dot_tn(v, kd)

    return pl.pallas_call(
        body, name=name, grid=(NC,),
        in_specs=[pl.BlockSpec((C, KW), lambda i: (cidx(i), 0)), pl.BlockSpec((C, KW), lambda i: (cidx(i), 1)),
                  pl.BlockSpec((C, VW), lambda i: (cidx(i), 1)), pl.BlockSpec((C, KW), lambda i: (cidx(i), 0)),
                  pl.BlockSpec((1, KW), lambda i: (0, 0))],
        out_specs=[pl.BlockSpec((C, VW), lambda i: (cidx(i), 0)),
                   pl.BlockSpec((1, H, DV, DK), lambda i: (cidx(i), 0, 0, 0))],
        out_shape=[jax.ShapeDtypeStruct((T, VW), F32), jax.ShapeDtypeStruct((NC, H, DV, DK), BF16)],
        scratch_shapes=[pltpu.VMEM((H, DV, DK), F32)], compiler_params=_params(1))(proj, proj, proj, z, bias)


def gla_bwd(proj, z, bias, states, do, prev, *, reverse, name):
    T = proj.shape[0]
    KW, VW, DK, DV = _gla_dims(proj)
    H = GLA_HEADS
    C = _tile(T, GLA_CHUNK)
    NC = T // C
    cidx = (lambda i: i) if reverse else (lambda i: NC - 1 - i)
    last = 0 if reverse else C - 1
    has_prev = prev is not None
    odt = BF16 if has_prev else F32

    def body(*refs):
        q_ref, k_ref, v_ref, z_ref, b_ref, s_ref, do_ref = refs[:7]
        pos = 7
        if has_prev:
            pq_ref, pk_ref, pv_ref = refs[7:10]
            pos = 10
        dq_ref, dk_ref, dv_ref, dz_ref, db_ref, dS_scr = refs[pos:pos + 6]

        @pl.when(pl.program_id(0) == 0)
        def _():
            dS_scr[...] = jnp.zeros_like(dS_scr)
            db_ref[...] = jnp.zeros_like(db_ref)

        cum, cum_t, amask = _gla_masks(C, reverse)
        xg = z_ref[...] + b_ref[...]
        la = _log_sigmoid(xg) * (1.0 / GLA_TAU)
        b = _tri_sum(cum.astype(BF16), la)
        cum_t_bf = cum_t.astype(BF16)
        for h in range(H):
            ks = slice(h * DK, (h + 1) * DK)
            vs = slice(h * DV, (h + 1) * DV)
            bh = b[:, ks]
            mid = bh[C // 2:C // 2 + 1, :]
            bl = bh[last:last + 1, :]
            q = q_ref[:, ks].astype(F32) * (DK ** -0.5)
            k = k_ref[:, ks].astype(F32)
            v = v_ref[:, vs]
            dov = do_ref[:, vs]
            e_b = jnp.exp(bh)
            e_up = jnp.exp(bh - mid)
            e_dn = jnp.exp(mid - bh)
            e_l = jnp.exp(bl - bh)
            e_bl = jnp.exp(bl)
            qe = (q * e_b).astype(BF16)
            qt = (q * e_up).astype(BF16)
            kt = (k * e_dn).astype(BF16)
            kd = (k * e_l).astype(BF16)
            Sb = s_ref[0, h]
            dSt = dS_scr[h]
            dSb = dSt.astype(BF16)
            A = jnp.where(amask, _dot_nt(qt, kt), 0.0).astype(BF16)
            dA = jnp.where(amask, _dot_nt(dov, v), 0.0).astype(BF16)
            dv = _dot_tn(A, dov) + _dot_nt(kd, dSb)
            dq = _dot(dA, kt) * e_up + _dot(dov, Sb) * e_b
            dk_state = _dot(v, dSb) * e_l
            dk = _dot_tn(dA, qt) * e_dn + dk_state
            d_b = dq * q - dk * k
            d_bl = (jnp.sum(dk_state * k, axis=0, keepdims=True)
                    + e_bl * jnp.sum(dSt * Sb.astype(F32), axis=0, keepdims=True))
            d_la = _tri_sum(cum_t_bf, d_b) + d_bl
            dz = d_la * (1.0 / GLA_TAU) * jax.nn.sigmoid(-xg[:, ks])
            dq = dq * (DK ** -0.5)
            if has_prev:
                dq = dq + pq_ref[:, ks]
                dk = dk + pk_ref[:, ks]
                dv = dv + pv_ref[:, vs]
            dq_ref[:, ks] = dq.astype(odt)
            dk_ref[:, ks] = dk.astype(odt)
            dv_ref[:, vs] = dv.astype(odt)
            dz_ref[:, ks] = dz.astype(BF16)
            db_ref[:, ks] += jnp.sum(dz, axis=0, keepdims=True)
            dS_scr[h] = dSt * e_bl + _dot_tn(dov, qe)

    kspec = lambda cb: pl.BlockSpec((C, KW), lambda i: (cidx(i), cb))
    vspec = lambda cb: pl.BlockSpec((C, VW), lambda i: (cidx(i), cb))
    ins = [proj, proj, proj, z, bias, states, do]
    in_specs = [kspec(0), kspec(1), vspec(1), kspec(0), pl.BlockSpec((1, KW), lambda i: (0, 0)),
                pl.BlockSpec((1, H, DV, DK), lambda i: (cidx(i), 0, 0, 0)), vspec(0)]
    if has_prev:
        ins += list(prev)
        in_specs += [kspec(0), kspec(0), vspec(0)]
    return pl.pallas_call(
        body, name=name, grid=(NC,), in_specs=in_specs,
        out_specs=[kspec(0), kspec(0), vspec(0), kspec(0), pl.BlockSpec((1, KW), lambda i: (0, 0))],
        out_shape=[jax.ShapeDtypeStruct((T, KW), odt), jax.ShapeDtypeStruct((T, KW), odt),
                   jax.ShapeDtypeStruct((T, VW), odt), jax.ShapeDtypeStruct((T, KW), BF16),
                   jax.ShapeDtypeStruct((1, KW), F32)],
        scratch_shapes=[pltpu.VMEM((H, DV, DK), F32)], compiler_params=_params(1))(*ins)


def gla_out(o_f, o_b, proj, g_off, norm_g, *, name):
    T, VW = o_f.shape
    H = GLA_HEADS
    DV = VW // H
    gb = g_off // VW
    tr = _tile(T, 256)

    def body(f_ref, b_ref, g_ref, n_ref, y_ref, yt_ref):
        for h in range(H):
            vs = slice(h * DV, (h + 1) * DV)
            o = f_ref[:, vs] + b_ref[:, vs]
            rstd = lax.rsqrt(jnp.mean(o * o, axis=-1, keepdims=True) + NORM_EPS)
            y = (o * rstd * n_ref[...] * _silu(g_ref[:, vs].astype(F32))).astype(BF16)
            y_ref[:, vs] = y
            yt_ref[vs, :] = y.T

    return pl.pallas_call(
        body, name=name, grid=(T // tr,),
        in_specs=[_row(tr, VW), _row(tr, VW), _row(tr, VW, gb), _bc(DV)],
        out_specs=[_row(tr, VW), pl.BlockSpec((VW, tr), lambda i: (0, i))],
        out_shape=[jax.ShapeDtypeStruct((T, VW), BF16), jax.ShapeDtypeStruct((VW, T), BF16)],
        compiler_params=_params(1))(o_f, o_b, proj, norm_g)


def gla_out_bwd(dy2, o_f, o_b, proj, g_off, norm_g, *, name):
    T, VW = o_f.shape
    H = GLA_HEADS
    DV = VW // H
    gb = g_off // VW
    tr = _tile(T, 256)

    def body(d_ref, f_ref, b_ref, g_ref, n_ref, do_ref, dg_ref, dn_ref):
        @pl.when(pl.program_id(0) == 0)
        def _():
            dn_ref[...] = jnp.zeros_like(dn_ref)

        nv = n_ref[...]
        for h in range(H):
            vs = slice(h * DV, (h + 1) * DV)
            o = f_ref[:, vs] + b_ref[:, vs]
            rstd = lax.rsqrt(jnp.mean(o * o, axis=-1, keepdims=True) + NORM_EPS)
            r = o * rstd
            gv = g_ref[:, vs].astype(F32)
            d = d_ref[:, vs].astype(F32)
            dg_ref[:, vs] = (d * r * nv * _dsilu(gv)).astype(BF16)
            dn_o = d * _silu(gv)
            dn_ref[...] += jnp.sum(dn_o * r, axis=0, keepdims=True)
            dr = dn_o * nv
            do_ref[:, vs] = (rstd * (dr - r * jnp.mean(dr * r, axis=-1, keepdims=True))).astype(BF16)

    return pl.pallas_call(
        body, name=name, grid=(T // tr,),
        in_specs=[_row(tr, VW), _row(tr, VW), _row(tr, VW), _row(tr, VW, gb), _bc(DV)],
        out_specs=[_row(tr, VW), _row(tr, VW), _bc(DV)],
        out_shape=[jax.ShapeDtypeStruct((T, VW), BF16), jax.ShapeDtypeStruct((T, VW), BF16),
                   jax.ShapeDtypeStruct((1, DV), F32)], compiler_params=_params(1))(dy2, o_f, o_b, proj, norm_g)


def _rope_tables(T):
    hd = ATTN_HEAD_DIM
    axis_dim = hd // 2
    rows = T // GRID_W
    t = jnp.arange(T)
    row = (t // GRID_W - rows // 2).astype(F32)
    col = (t % GRID_W - GRID_W // 2).astype(F32)
    inv = ROPE_THETA ** (-jnp.arange(0, axis_dim, 2, dtype=F32) / axis_dim)
    ang = jnp.concatenate([row[:, None] * inv, col[:, None] * inv], axis=-1)
    cos = jnp.repeat(jnp.cos(ang), 2, axis=-1)
    sin = jnp.repeat(jnp.sin(ang), 2, axis=-1)
    sign = jnp.where(jnp.arange(hd) % 2 == 0, -1.0, 1.0).astype(F32)
    return cos, sin * sign


def _pair_swap(x):
    n = x.shape[-1]
    lane = lax.broadcasted_iota(jnp.int32, x.shape, x.ndim - 1)
    return jnp.where(lane % 2 == 0, pltpu.roll(x, n - 1, x.ndim - 1), pltpu.roll(x, 1, x.ndim - 1))


def _attn_dims(proj):
    hd = ATTN_HEAD_DIM
    kvw = ATTN_KV_HEADS * hd
    qw = (proj.shape[1] - 2 * kvw) // 2
    return qw, kvw, qw // hd, (qw // hd) // ATTN_KV_HEADS


LOG2E = 1.4426950408889634
LN2 = 0.6931471805599453


def _q_mult():
    return ATTN_HEAD_DIM ** -0.5 * LOG2E


def attn_prep(proj, qg, kg, cos, sin, *, name):
    T = proj.shape[0]
    QW, KVW, NH, G = _attn_dims(proj)
    hd = ATTN_HEAD_DIM
    KV = ATTN_KV_HEADS
    assert QW % KVW == 0
    tr = _tile(T, 256)

    def body(q_ref, k_ref, v_ref, qg_ref, kg_ref, c_ref, s_ref, qo_ref, ko_ref, vo_ref):
        cv, sv = c_ref[...], s_ref[...]

        def one(x, gain, mult):
            rstd = lax.rsqrt(jnp.mean(x * x, axis=-1, keepdims=True) + NORM_EPS)
            xs = x * rstd * gain
            return (xs * cv + _pair_swap(xs) * sv) * mult

        for h in range(NH):
            hs = slice(h * hd, (h + 1) * hd)
            qo_ref[h] = one(q_ref[:, hs].astype(F32), qg_ref[...], _q_mult()).astype(BF16)
        for h in range(KV):
            hs = slice(h * hd, (h + 1) * hd)
            ko_ref[:, hs] = one(k_ref[:, hs].astype(F32), kg_ref[...], 1.0).astype(BF16)
            vo_ref[:, 2 * h * hd:(2 * h + 1) * hd] = v_ref[:, hs]
            vo_ref[:, (2 * h + 1) * hd:(2 * h + 2) * hd] = jnp.ones((tr, hd), BF16)

    return pl.pallas_call(
        body, name=name, grid=(T // tr,),
        in_specs=[_row(tr, QW), _row(tr, KVW, QW // KVW), _row(tr, KVW, QW // KVW + 1), _bc(hd), _bc(hd),
                  _row(tr, hd), _row(tr, hd)],
        out_specs=[pl.BlockSpec((NH, tr, hd), lambda i: (0, i, 0)), _row(tr, KVW), _row(tr, 2 * KVW)],
        out_shape=[jax.ShapeDtypeStruct((NH, T, hd), BF16), jax.ShapeDtypeStruct((T, KVW), BF16),
                   jax.ShapeDtypeStruct((T, 2 * KVW), BF16)],
        compiler_params=_params(1))(proj, proj, proj, qg, kg, cos, sin)


def attn_prep_bwd(dqr, dkr, proj, qg, kg, cos, sin, *, name):
    T = proj.shape[0]
    QW, KVW, NH, G = _attn_dims(proj)
    hd = ATTN_HEAD_DIM
    tr = _tile(T, 256)

    def body(dq_ref, dk_ref, q_ref, k_ref, qg_ref, kg_ref, c_ref, s_ref, oq_ref, ok_ref, dqg_ref, dkg_ref):
        @pl.when(pl.program_id(0) == 0)
        def _():
            dqg_ref[...] = jnp.zeros_like(dqg_ref)
            dkg_ref[...] = jnp.zeros_like(dkg_ref)

        cv, sv = c_ref[...], s_ref[...]

        def one(d, x, gain, mult):
            d = d * mult
            dxs = d * cv - _pair_swap(d) * sv
            rstd = lax.rsqrt(jnp.mean(x * x, axis=-1, keepdims=True) + NORM_EPS)
            xn = x * rstd
            dgain = jnp.sum(dxs * xn, axis=0, keepdims=True)
            dxn = dxs * gain
            return rstd * (dxn - xn * jnp.mean(dxn * xn, axis=-1, keepdims=True)), dgain

        for h in range(NH):
            hs = slice(h * hd, (h + 1) * hd)
            dx, dgain = one(dq_ref[h].astype(F32), q_ref[:, hs].astype(F32), qg_ref[...], _q_mult())
            oq_ref[:, hs] = dx.astype(BF16)
            dqg_ref[...] += dgain
        for h in range(ATTN_KV_HEADS):
            hs = slice(h * hd, (h + 1) * hd)
            dx, dgain = one(dk_ref[:, hs].astype(F32), k_ref[:, hs].astype(F32), kg_ref[...], 1.0)
            ok_ref[:, hs] = dx.astype(BF16)
            dkg_ref[...] += dgain

    return pl.pallas_call(
        body, name=name, grid=(T // tr,),
        in_specs=[pl.BlockSpec((NH, tr, hd), lambda i: (0, i, 0)), _row(tr, KVW), _row(tr, QW),
                  _row(tr, KVW, QW // KVW), _bc(hd), _bc(hd), _row(tr, hd), _row(tr, hd)],
        out_specs=[_row(tr, QW), _row(tr, KVW), _bc(hd), _bc(hd)],
        out_shape=[jax.ShapeDtypeStruct((T, QW), BF16), jax.ShapeDtypeStruct((T, KVW), BF16),
                   jax.ShapeDtypeStruct((1, hd), F32), jax.ShapeDtypeStruct((1, hd), F32)],
        compiler_params=_params(1))(dqr, dkr, proj, proj, qg, kg, cos, sin)


FLASH_BQ = 256
FLASH_BK = 512


def flash_fwd(q, kr, vext, *, name):
    NH, T, hd = q.shape
    KV = ATTN_KV_HEADS
    G = NH // KV
    bq = _tile(T, FLASH_BQ)
    bk = _tile(T, FLASH_BK)
    nk = T // bk

    def body(q_ref, k_ref, v_ref, o_ref, lse_ref, m_scr, acc_scr, sa_scr, sb_scr):
        m_scr[...] = jnp.full_like(m_scr, -jnp.inf)
        acc_scr[...] = jnp.zeros_like(acc_scr)

        def scores(c, s_scr):
            kc = k_ref[pl.ds(pl.multiple_of(c * bk, bk), bk), :]
            for g in range(G):
                s_scr[g] = _dot_nt(q_ref[g], kc)

        def consume(c, s_scr):
            vc = v_ref[pl.ds(pl.multiple_of(c * bk, bk), bk), :]
            for g in range(G):
                s = s_scr[g]
                m_old = m_scr[g]
                m_new = jnp.maximum(m_old, jnp.max(s, axis=-1, keepdims=True))
                p = jnp.exp2(s - m_new)
                acc_scr[g] = jnp.exp2(m_old - m_new) * acc_scr[g] + _dot(p.astype(BF16), vc)
                m_scr[g] = m_new

        scores(0, sa_scr)
        if nk % 2 == 0:
            def pair(t, carry):
                scores(2 * t + 1, sb_scr)
                consume(2 * t, sa_scr)
                scores(2 * t + 2, sa_scr)
                consume(2 * t + 1, sb_scr)
                return carry

            lax.fori_loop(0, nk // 2 - 1, pair, 0)
            scores(nk - 1, sb_scr)
            consume(nk - 2, sa_scr)
            consume(nk - 1, sb_scr)
        else:
            def single(c, carry):
                consume(c, sa_scr)
                scores(c + 1, sa_scr)
                return carry

            lax.fori_loop(0, nk - 1, single, 0)
            consume(nk - 1, sa_scr)
        for g in range(G):
            a = acc_scr[g]
            l = a[:, hd:]
            o_ref[g] = a[:, :hd] / l
            lse_ref[g] = m_scr[g] + jnp.log2(l[:, 0:1])

    return pl.pallas_call(
        body, name=name, grid=(KV, T // bq),
        in_specs=[pl.BlockSpec((G, bq, hd), lambda h, i: (h, i, 0)), pl.BlockSpec((T, hd), lambda h, i: (0, h)),
                  pl.BlockSpec((T, 2 * hd), lambda h, i: (0, h))],
        out_specs=[pl.BlockSpec((G, bq, hd), lambda h, i: (h, i, 0)),
                   pl.BlockSpec((G, bq, 1), lambda h, i: (h, i, 0))],
        out_shape=[jax.ShapeDtypeStruct((NH, T, hd), F32), jax.ShapeDtypeStruct((NH, T, 1), F32)],
        scratch_shapes=[pltpu.VMEM((G, bq, 1), F32), pltpu.VMEM((G, bq, 2 * hd), F32),
                        pltpu.VMEM((G, bq, bk), F32), pltpu.VMEM((G, bq, bk), F32)],
        compiler_params=_params(2))(q, kr, vext)


def flash_bwd(q, kr, proj, v_off, do, lse, delta, *, name):
    NH, T, hd = q.shape
    KV = ATTN_KV_HEADS
    G = NH // KV
    bq = _tile(T, FLASH_BQ)
    bk = _tile(T, FLASH_BK)
    nk = T // bk
    nq = T // bq
    vb = v_off // hd

    def body(q_ref, k_ref, v_ref, do_ref, lse_ref, dl_ref, dq_ref, dk_ref, dv_ref, dq_scr, dk_scr, dv_scr,
             sa_scr, pa_scr, sb_scr, pb_scr):
        i = pl.program_id(1)

        @pl.when(i == 0)
        def _():
            dk_scr[...] = jnp.zeros_like(dk_scr)
            dv_scr[...] = jnp.zeros_like(dv_scr)

        dq_scr[...] = jnp.zeros_like(dq_scr)

        def scores(c, s_scr, dp_scr):
            off = pl.multiple_of(c * bk, bk)
            kc = k_ref[pl.ds(off, bk), :]
            vc = v_ref[pl.ds(off, bk), :]
            for g in range(G):
                s_scr[g] = _dot_nt(q_ref[g], kc)
                dp_scr[g] = _dot_nt(do_ref[g], vc)

        def consume(c, s_scr, dp_scr):
            off = pl.multiple_of(c * bk, bk)
            kc = k_ref[pl.ds(off, bk), :]
            dk_c = jnp.zeros((bk, hd), F32)
            dv_c = jnp.zeros((bk, hd), F32)
            for g in range(G):
                p = jnp.exp2(s_scr[g] - lse_ref[g])
                ds = (p * (dp_scr[g] - dl_ref[g])).astype(BF16)
                dq_scr[g] += _dot(ds, kc)
                dv_c = dv_c + _dot_tn(p.astype(BF16), do_ref[g])
                dk_c = dk_c + _dot_tn(ds, q_ref[g])
            dk_scr[pl.ds(off, bk), :] += dk_c
            dv_scr[pl.ds(off, bk), :] += dv_c

        scores(0, sa_scr, pa_scr)
        if nk % 2 == 0:
            def pair(t, carry):
                scores(2 * t + 1, sb_scr, pb_scr)
                consume(2 * t, sa_scr, pa_scr)
                scores(2 * t + 2, sa_scr, pa_scr)
                consume(2 * t + 1, sb_scr, pb_scr)
                return carry

            lax.fori_loop(0, nk // 2 - 1, pair, 0)
            scores(nk - 1, sb_scr, pb_scr)
            consume(nk - 2, sa_scr, pa_scr)
            consume(nk - 1, sb_scr, pb_scr)
        else:
            def single(c, carry):
                consume(c, sa_scr, pa_scr)
                scores(c + 1, sa_scr, pa_scr)
                return carry

            lax.fori_loop(0, nk - 1, single, 0)
            consume(nk - 1, sa_scr, pa_scr)
        dq_ref[...] = dq_scr[...].astype(BF16)

        @pl.when(i == nq - 1)
        def _():
            dk_ref[...] = dk_scr[...].astype(BF16)
            dv_ref[...] = (dv_scr[...] * LOG2E).astype(BF16)

    qspec = pl.BlockSpec((G, bq, hd), lambda h, i: (h, i, 0))
    cspec = pl.BlockSpec((G, bq, 1), lambda h, i: (h, i, 0))
    kspec = pl.BlockSpec((T, hd), lambda h, i: (0, h))
    return pl.pallas_call(
        body, name=name, grid=(KV, nq),
        in_specs=[qspec, kspec, pl.BlockSpec((T, hd), lambda h, i: (0, vb + h)), qspec, cspec, cspec],
        out_specs=[qspec, kspec, kspec],
        out_shape=[jax.ShapeDtypeStruct((NH, T, hd), BF16), jax.ShapeDtypeStruct((T, KV * hd), BF16),
                   jax.ShapeDtypeStruct((T, KV * hd), BF16)],
        scratch_shapes=[pltpu.VMEM((G, bq, hd), F32), pltpu.VMEM((T, hd), F32), pltpu.VMEM((T, hd), F32)]
        + [pltpu.VMEM((G, bq, bk), F32)] * 4,
        compiler_params=_params(2))(q, kr, proj, do, lse, delta)


def attn_gate(o, proj, g_off, *, name):
    NH, T, hd = o.shape
    W = NH * hd
    cw = _col_tile(W, g_off)
    hc = cw // hd
    gb = g_off // cw
    tr = _tile(T, 256)

    def body(o_ref, g_ref, y_ref, yt_ref):
        for h in range(hc):
            hs = slice(h * hd, (h + 1) * hd)
            y = (o_ref[h] * _silu(g_ref[:, hs].astype(F32))).astype(BF16)
            y_ref[:, hs] = y
            yt_ref[hs, :] = y.T

    return pl.pallas_call(
        body, name=name, grid=(W // cw, T // tr),
        in_specs=[pl.BlockSpec((hc, tr, hd), lambda j, i: (j, i, 0)), pl.BlockSpec((tr, cw), lambda j, i: (i, gb + j))],
        out_specs=[pl.BlockSpec((tr, cw), lambda j, i: (i, j)), pl.BlockSpec((cw, tr), lambda j, i: (j, i))],
        out_shape=[jax.ShapeDtypeStruct((T, W), BF16), jax.ShapeDtypeStruct((W, T), BF16)],
        compiler_params=_params(2))(o, proj)


def attn_gate_bwd(dy2, o, proj, g_off, *, name):
    NH, T, hd = o.shape
    W = NH * hd
    cw = _col_tile(W, g_off)
    hc = cw // hd
    gb = g_off // cw
    tr = _tile(T, 256)

    def body(d_ref, o_ref, g_ref, do_ref, dg_ref, dl_ref):
        for h in range(hc):
            hs = slice(h * hd, (h + 1) * hd)
            d = d_ref[:, hs].astype(F32)
            gv = g_ref[:, hs].astype(F32)
            ov = o_ref[h]
            dov = d * _silu(gv) * LN2
            do_ref[h] = dov.astype(BF16)
            dg_ref[:, hs] = (d * ov * _dsilu(gv)).astype(BF16)
            dl_ref[h] = jnp.sum(dov * ov, axis=-1, keepdims=True)

    return pl.pallas_call(
        body, name=name, grid=(W // cw, T // tr),
        in_specs=[pl.BlockSpec((tr, cw), lambda j, i: (i, j)), pl.BlockSpec((hc, tr, hd), lambda j, i: (j, i, 0)),
                  pl.BlockSpec((tr, cw), lambda j, i: (i, gb + j))],
        out_specs=[pl.BlockSpec((hc, tr, hd), lambda j, i: (j, i, 0)), pl.BlockSpec((tr, cw), lambda j, i: (i, j)),
                   pl.BlockSpec((hc, tr, 1), lambda j, i: (j, i, 0))],
        out_shape=[jax.ShapeDtypeStruct((NH, T, hd), BF16), jax.ShapeDtypeStruct((T, W), BF16),
                   jax.ShapeDtypeStruct((NH, T, 1), F32)], compiler_params=_params(2))(dy2, o, proj)


def outer_silu(c_t, dm, *, name):
    K, B = c_t.shape
    N = dm.shape[1]
    tk = _tile(K, 256)

    def body(c_ref, d_ref, o_ref):
        s = _silu(c_ref[...])
        acc = jnp.zeros((tk, N), F32)
        for b in range(B):
            acc = acc + s[:, b:b + 1] * d_ref[b:b + 1, :]
        o_ref[...] = acc

    return pl.pallas_call(
        body, name=name, grid=(K // tk,),
        in_specs=[pl.BlockSpec((tk, B), lambda i: (i, 0)), pl.BlockSpec((B, N), lambda i: (0, 0))],
        out_specs=pl.BlockSpec((tk, N), lambda i: (i, 0)),
        out_shape=jax.ShapeDtypeStruct((K, N), F32), compiler_params=_params(1))(c_t, dm)


def add_core_rows(core, a, b, *, name):
    S, R, C = b.shape
    tr = _tile(R, 2048)

    def body(core_ref, a_ref, b_ref, o_ref):
        o_ref[...] = (a_ref[...].astype(F32) + b_ref[...].astype(F32)).astype(BF16)

    blk = pl.BlockSpec((None, tr, C), lambda q, i, core_ref: (q, i, 0))
    grid_spec = pltpu.PrefetchScalarGridSpec(
        num_scalar_prefetch=1, grid=(S, R // tr),
        in_specs=[pl.BlockSpec((None, tr, C), lambda q, i, core_ref: (2 * q + core_ref[0], i, 0)), blk],
        out_specs=blk)
    return pl.pallas_call(
        body, name=name, grid_spec=grid_spec, out_shape=jax.ShapeDtypeStruct((S, R, C), BF16),
        compiler_params=_params(2))(core, a, b)


def sum_slots(x, *, name):
    S, R, C = x.shape
    tr = _tile(R, 512)

    def body(x_ref, o_ref):
        acc = x_ref[0].astype(F32)
        for s in range(1, S):
            acc = acc + x_ref[s].astype(F32)
        o_ref[...] = acc

    return pl.pallas_call(
        body, name=name, grid=(R // tr,), in_specs=[pl.BlockSpec((S, tr, C), lambda i: (0, i, 0))],
        out_specs=pl.BlockSpec((tr, C), lambda i: (i, 0)),
        out_shape=jax.ShapeDtypeStruct((R, C), F32), compiler_params=_params(1))(x)


def adamw(w, g, m, v, *, name):
    R, C = w.shape
    tr = _tile(R, 512) if R % 8 == 0 else R

    def body(w_ref, g_ref, m_ref, v_ref, d_ref, nm_ref, nv_ref):
        gv = g_ref[...]
        mn = ADAM_B1 * m_ref[...] + (1.0 - ADAM_B1) * gv
        vn = ADAM_B2 * v_ref[...] + (1.0 - ADAM_B2) * jnp.square(gv)
        m_hat = mn / (1.0 - ADAM_B1 ** ADAM_STEP)
        v_hat = vn / (1.0 - ADAM_B2 ** ADAM_STEP)
        d_ref[...] = -ADAM_LR * (m_hat / (jnp.sqrt(v_hat) + ADAM_EPS) + ADAM_WD * w_ref[...])
        nm_ref[...] = mn
        nv_ref[...] = vn

    blk = pl.BlockSpec((tr, C), lambda i: (i, 0))
    sh = jax.ShapeDtypeStruct((R, C), F32)
    return pl.pallas_call(
        body, name=name, grid=(R // tr,), in_specs=[blk] * 4, out_specs=[blk] * 3, out_shape=[sh] * 3,
        compiler_params=_params(1))(w, g, m, v)


def _adamw_nd(w, g, m, v, name):
    shp = w.shape
    if w.ndim == 1:
        two = (1, shp[0])
    else:
        two = (math.prod(shp[:-1]), shp[-1])
    d, nm, nv = adamw(w.reshape(two), g.reshape(two), m.reshape(two), v.reshape(two), name=name)
    return d.reshape(shp), nm.reshape(shp), nv.reshape(shp)


IN_DX_TILES = dict(tm=1024, tn=512, tk=8192)


def _pad_cols(w, n):
    return jnp.pad(w, ((0, 0), (0, n - w.shape[1])))


def _pad_rows(w, n):
    return jnp.pad(w, ((0, n - w.shape[0]), (0, 0)))


def kernel(x, c, w_mod, b_mod, pool_w_in, pool_w_grp, pool_scale, pool_w_out, gla_w_in, gla_fwd_w1, gla_fwd_w2, gla_fwd_b, gla_bwd_w1, gla_bwd_w2, gla_bwd_b, gla_norm_g, gla_w_out, attn_w_in, attn_q_norm_g, attn_k_norm_g, attn_w_out, final_norm_g, loss_target, m_w_mod, m_b_mod, m_pool_w_in, m_pool_w_grp, m_pool_scale, m_pool_w_out, m_gla_w_in, m_gla_fwd_w1, m_gla_fwd_w2, m_gla_fwd_b, m_gla_bwd_w1, m_gla_bwd_w2, m_gla_bwd_b, m_gla_norm_g, m_gla_w_out, m_attn_w_in, m_attn_q_norm_g, m_attn_k_norm_g, m_attn_w_out, m_final_norm_g, v_w_mod, v_b_mod, v_pool_w_in, v_pool_w_grp, v_pool_scale, v_pool_w_out, v_gla_w_in, v_gla_fwd_w1, v_gla_fwd_w2, v_gla_fwd_b, v_gla_bwd_w1, v_gla_bwd_w2, v_gla_bwd_b, v_gla_norm_g, v_gla_w_out, v_attn_w_in, v_attn_q_norm_g, v_attn_k_norm_g, v_attn_w_out, v_final_norm_g):
    W = dict(w_mod=w_mod, b_mod=b_mod, pool_w_in=pool_w_in, pool_w_grp=pool_w_grp, pool_scale=pool_scale,
             pool_w_out=pool_w_out, gla_w_in=gla_w_in, gla_fwd_w1=gla_fwd_w1, gla_fwd_w2=gla_fwd_w2,
             gla_fwd_b=gla_fwd_b, gla_bwd_w1=gla_bwd_w1, gla_bwd_w2=gla_bwd_w2, gla_bwd_b=gla_bwd_b,
             gla_norm_g=gla_norm_g, gla_w_out=gla_w_out, attn_w_in=attn_w_in, attn_q_norm_g=attn_q_norm_g,
             attn_k_norm_g=attn_k_norm_g, attn_w_out=attn_w_out, final_norm_g=final_norm_g)
    M = dict(w_mod=m_w_mod, b_mod=m_b_mod, pool_w_in=m_pool_w_in, pool_w_grp=m_pool_w_grp, pool_scale=m_pool_scale,
             pool_w_out=m_pool_w_out, gla_w_in=m_gla_w_in, gla_fwd_w1=m_gla_fwd_w1, gla_fwd_w2=m_gla_fwd_w2,
             gla_fwd_b=m_gla_fwd_b, gla_bwd_w1=m_gla_bwd_w1, gla_bwd_w2=m_gla_bwd_w2, gla_bwd_b=m_gla_bwd_b,
             gla_norm_g=m_gla_norm_g, gla_w_out=m_gla_w_out, attn_w_in=m_attn_w_in, attn_q_norm_g=m_attn_q_norm_g,
             attn_k_norm_g=m_attn_k_norm_g, attn_w_out=m_attn_w_out, final_norm_g=m_final_norm_g)
    V = dict(w_mod=v_w_mod, b_mod=v_b_mod, pool_w_in=v_pool_w_in, pool_w_grp=v_pool_w_grp, pool_scale=v_pool_scale,
             pool_w_out=v_pool_w_out, gla_w_in=v_gla_w_in, gla_fwd_w1=v_gla_fwd_w1, gla_fwd_w2=v_gla_fwd_w2,
             gla_fwd_b=v_gla_fwd_b, gla_bwd_w1=v_gla_bwd_w1, gla_bwd_w2=v_gla_bwd_w2, gla_bwd_b=v_gla_bwd_b,
             gla_norm_g=v_gla_norm_g, gla_w_out=v_gla_w_out, attn_w_in=v_attn_w_in, attn_q_norm_g=v_attn_q_norm_g,
             attn_k_norm_g=v_attn_k_norm_g, attn_w_out=v_attn_w_out, final_norm_g=v_final_norm_g)

    me = _my_id()
    T, D = x.shape[1], x.shape[2]
    x0 = x.reshape(T, D)
    target = loss_target.reshape(T, D)
    BW = D
    n_pool, n_gla, n_attn = pool_w_in.shape[0], gla_w_in.shape[0], attn_w_in.shape[0]
    LR = GLA_LOWRANK
    LRP = LANES
    KWg = gla_fwd_w2.shape[2] * NDEV
    Dm = w_mod.shape[2]

    small_shard_names = ['pool_scale', 'gla_fwd_w1', 'gla_fwd_w2', 'gla_bwd_w1', 'gla_bwd_w2']
    small_items = [c] + [W[n] for n in small_shard_names]
    small_shapes = [a.shape for a in small_items]
    g1 = exchange(GATHER, _pack(small_items, F32), name="gather_small")
    c_all_, ps_all, fw1_all, fw2_all, bw1_all, bw2_all = _unpack(g1, small_shapes, lead=NDEV)
    c_all = c_all_.reshape(NDEV, D)
    pool_scale_full = jnp.transpose(ps_all, (1, 0, 2)).reshape(n_pool, BW)
    w1_full = {'f': fw1_all.transpose(1, 0, 2, 3).reshape(n_gla, D, LR),
               'b': bw1_all.transpose(1, 0, 2, 3).reshape(n_gla, D, LR)}
    w2_full = {'f': fw2_all.transpose(1, 2, 0, 3).reshape(n_gla, LR, KWg),
               'b': bw2_all.transpose(1, 2, 0, 3).reshape(n_gla, LR, KWg)}

    c16 = _pad_rows(c_all, 16)
    b_slab = lax.dynamic_slice_in_dim(b_mod, me * Dm, Dm, axis=1)
    mod_parts = [mm(c16, w_mod[i], name=f"mod_fwd{i}", out_dtype=F32, a_silu=True, tm=16, tn=Dm,
                    bias=b_slab[i:i + 1])[:NDEV] for i in range(DEPTH)]
    mod_slab = jnp.stack(mod_parts)
    g2 = exchange(GATHER, _pack([mod_slab], F32), name="gather_mod")
    (mod_all,) = _unpack(g2, [mod_slab.shape], lead=NDEV)
    mod_mine = lax.dynamic_index_in_dim(mod_all, me, axis=2, keepdims=False)
    mod_mine = mod_mine.transpose(1, 0, 2).reshape(DEPTH, NDEV * Dm)
    shift = [mod_mine[i:i + 1, 0:D] for i in range(DEPTH)]
    scale = [mod_mine[i:i + 1, D:2 * D] for i in range(DEPTH)]
    gate = [mod_mine[i:i + 1, 2 * D:3 * D] for i in range(DEPTH)]

    layer_names = {0: ['pool_w_in', 'pool_w_grp', 'pool_w_out'], 1: ['gla_w_in', 'gla_w_out'],
                   2: ['attn_w_in', 'attn_w_out']}

    def layer_shards(i):
        return [W[n][i // N_MIXERS] for n in layer_names[i % N_MIXERS]]

    def layer_weights(i, gathered):
        parts = _unpack(gathered, [a.shape for a in layer_shards(i)], lead=NDEV)
        w_out = parts[-1].reshape(BW, D)
        out = dict(w_in=parts[0], w_out=w_out, w_in_t=parts[0].transpose(0, 2, 1).reshape(-1, D), w_out_t=w_out.T)
        if i % N_MIXERS == 0:
            g = parts[1]
            out['w_grp'] = g.transpose(1, 0, 2, 3).reshape(g.shape[1], -1, g.shape[3])
        return out

    gathered = exchange(GATHER_VIA_SIBLING, _pack(layer_shards(0), BF16), name="gather_w0")

    cos, sin = _rope_tables(T)

    xs = [x0]
    saved = []
    xi = x0
    for i in range(DEPTH):
        kind, j = i % N_MIXERS, i // N_MIXERS
        h, h_t = prenorm(xi, scale[i], shift[i], name=f"prenorm{i}")
        lw = layer_weights(i, gathered)
        sv = dict(h=h, h_t=h_t, lw=lw)
        in_name = ("pool_in", "gla_in", "attn_in")[kind] + str(i)
        if i + 1 < DEPTH:
            proj, gathered = mm(h, lw['w_in'], b_split=True, name=in_name,
                                side=(GATHER_VIA_SIBLING, _pack(layer_shards(i + 1), BF16)))
        else:
            proj = mm(h, lw['w_in'], b_split=True, name=in_name)
        if kind == 0:
            ug = proj
            pooled = band(ug, BW, transpose=False, name=f"pool_band{i}")
            zz = gmm_nn(pooled, lw['w_grp'], name=f"pool_grp{i}")
            y2, y2_t = gated(zz, ug, BW, pool_scale_full[j:j + 1], name=f"pool_gate{i}")
            sv.update(ug=ug, pooled=pooled, z=zz)
        elif kind == 1:
            sv.update(proj=proj)
            for dname in ('f', 'b'):
                w1p = _pad_cols(w1_full[dname][j], LRP)
                w2p = _pad_rows(w2_full[dname][j], LRP)
                bias = (gla_fwd_b if dname == 'f' else gla_bwd_b)[j:j + 1]
                hw1 = mm(h, w1p, name=f"gla_w1{dname}{i}", tn=LRP)
                zg = mm(hw1, w2p, name=f"gla_w2{dname}{i}", out_dtype=F32, tk=LRP)
                o_d, st_d = gla_fwd(proj, zg, bias, reverse=(dname == 'b'), name=f"gla_scan_{dname}{i}")
                sv.update({f"hw1{dname}": hw1, f"z{dname}": zg, f"o{dname}": o_d, f"st{dname}": st_d,
                           f"w1p{dname}": w1p, f"w2p{dname}": w2p, f"bias{dname}": bias})
            y2, y2_t = gla_out(sv['of'], sv['ob'], proj, 2 * BW, gla_norm_g[j:j + 1], name=f"gla_out{i}")
        else:
            QW, KVW, _, _ = _attn_dims(proj)
            qr, kr, vext = attn_prep(proj, attn_q_norm_g[j:j + 1], attn_k_norm_g[j:j + 1], cos, sin,
                                     name=f"attn_prep{i}")
            o, lse = flash_fwd(qr, kr, vext, name=f"attn_flash{i}")
            y2, y2_t = attn_gate(o, proj, QW + 2 * KVW, name=f"attn_gate{i}")
            sv.update(proj=proj, qr=qr, kr=kr, o=o, lse=lse)
        sv.update(y2_t=y2_t)
        saved.append(sv)
        xi = mm(y2, lw['w_out'], name=f"out_proj{i}", col_scale=gate[i], add=xi, out_dtype=F32, tm=1024)
        xs.append(xi)

    loss_part, dx, d_final_g = loss_head(xi, final_norm_g.reshape(1, D), target, name="loss_head")

    d_mod = [None] * DEPTH
    layer_grads = [None] * DEPTH
    received = [None] * DEPTH
    small_grads = {}

    def in_dx(i, dproj, w_in, **kw):
        name = ("pool_in_dx", "gla_in_dx", "attn_in_dx")[i % N_MIXERS] + str(i)
        if i + 1 < DEPTH:
            dh, received[i + 1] = mm(dproj, w_in, name=name, **IN_DX_TILES, side=(CHIP_ROWS, chip_sums[i + 1]), **kw)
            return dh
        return mm(dproj, w_in, name=name, **IN_DX_TILES, **kw)

    def chip_sum(i, packed, from_sibling):
        core = lax.axis_index("c").astype(jnp.int32).reshape(1)
        return add_core_rows(core, packed, from_sibling, name=f"chip_sum_g{i}")

    chip_sums = [None] * DEPTH
    for i in reversed(range(DEPTH)):
        kind, j = i % N_MIXERS, i // N_MIXERS
        sv = saved[i]
        h_t, lw = sv['h_t'], sv['lw']
        if i + 1 < DEPTH:
            packed = _pack(layer_grads[i + 1], BF16, lead=NDEV)
            dy2, from_sibling = mm(dx, lw['w_out_t'], name=f"out_proj_dx{i}", a_scale=gate[i], tm=1024,
                                   side=(SIBLING_ROWS, packed))
            chip_sums[i + 1] = chip_sum(i + 1, packed, from_sibling)
        else:
            dy2 = mm(dx, lw['w_out_t'], name=f"out_proj_dx{i}", a_scale=gate[i], tm=1024)
        g_raw = mm(sv['y2_t'], dx, name=f"out_proj_dw{i}", out_dtype=F32, tk=1024)
        g_wout, d_gate = out_proj_grads(g_raw, lw['w_out'], gate[i], name=f"out_proj_dgate{i}")
        g_wout = g_wout.reshape(NDEV, BW // NDEV, D)
        if kind == 0:
            ug = sv['ug']
            dz, dg, d_ps = gated_bwd(dy2, sv['z'], ug, BW, pool_scale_full[j:j + 1], name=f"pool_gate_bwd{i}")
            dpooled = gmm_nn(dz, lw['w_grp'], tb=True, name=f"pool_grp_dx{i}")
            g_grp = gmm_tn(sv['pooled'], dz, len(POOL_WINDOWS), name=f"pool_grp_dw{i}")
            du = band(dpooled, BW, transpose=True, name=f"pool_band_bwd{i}")
            dproj = jnp.concatenate([du, dg], axis=1)
            dh = in_dx(i, dproj, lw['w_in_t'])
            g_win = mm(h_t, dproj, out_split=pool_w_in.shape[2], name=f"pool_in_dw{i}")
            Gp, Cg = g_grp.shape[0], g_grp.shape[1]
            g_grp = g_grp.astype(BF16).reshape(Gp, NDEV, Cg // NDEV, Cg).transpose(1, 0, 2, 3)
            layer_grads[i] = [g_win, g_grp, g_wout]
            small_grads.setdefault('pool_scale', [None] * n_pool)[j] = d_ps
        elif kind == 1:
            proj = sv['proj']
            do, dg, d_ng = gla_out_bwd(dy2, sv['of'], sv['ob'], proj, 2 * BW, gla_norm_g[j:j + 1],
                                       name=f"gla_out_bwd{i}")
            prev = None
            dh_acc = None
            for dname in ('f', 'b'):
                dq, dk, dv, dzg, dbias = gla_bwd(proj, sv[f"z{dname}"], sv[f"bias{dname}"], sv[f"st{dname}"], do,
                                                 prev, reverse=(dname == 'b'), name=f"gla_scan_bwd_{dname}{i}")
                prev = (dq, dk, dv)
                dhw1 = mm(dzg, sv[f"w2p{dname}"], tb=True, name=f"gla_w2{dname}_dx{i}", tn=LRP)
                g_w2 = mm(sv[f"hw1{dname}"], dzg, ta=True, out_dtype=F32, name=f"gla_w2{dname}_dw{i}", tm=LRP)
                g_w1 = mm(h_t, dhw1, out_dtype=F32, name=f"gla_w1{dname}_dw{i}", tn=LRP)
                dh_acc = mm(dhw1, sv[f"w1p{dname}"], tb=True, add=dh_acc, name=f"gla_w1{dname}_dx{i}", tk=LRP)
                key = 'gla_fwd' if dname == 'f' else 'gla_bwd'
                small_grads[key + '_w1'] = g_w1[:, :LR]
                small_grads[key + '_w2'] = g_w2[:LR]
                small_grads[key + '_b'] = dbias
            dproj = jnp.concatenate([prev[0], prev[1], prev[2], dg], axis=1)
            dh = in_dx(i, dproj, lw['w_in_t'], add=dh_acc)
            layer_grads[i] = [mm(h_t, dproj, out_split=gla_w_in.shape[2], name=f"gla_in_dw{i}"), g_wout]
            small_grads['gla_norm_g'] = d_ng
        else:
            proj = sv['proj']
            QW, KVW, _, _ = _attn_dims(proj)
            do, dg, delta = attn_gate_bwd(dy2, sv['o'], proj, QW + 2 * KVW, name=f"attn_gate_bwd{i}")
            dqr, dkr, dv = flash_bwd(sv['qr'], sv['kr'], proj, QW + KVW, do, sv['lse'], delta,
                                     name=f"attn_flash_bwd{i}")
            dq, dk, d_qg, d_kg = attn_prep_bwd(dqr, dkr, proj, attn_q_norm_g[j:j + 1], attn_k_norm_g[j:j + 1],
                                               cos, sin, name=f"attn_prep_bwd{i}")
            dproj = jnp.concatenate([dq, dk, dv, dg], axis=1)
            dh = in_dx(i, dproj, lw['w_in_t'])
            layer_grads[i] = [mm(h_t, dproj, out_split=attn_w_in.shape[2], name=f"attn_in_dw{i}"), g_wout]
            small_grads['attn_q_norm_g'] = d_qg
            small_grads['attn_k_norm_g'] = d_kg
        dx, d_scale, d_shift = prenorm_bwd(xs[i], dh, dx, scale[i], name=f"prenorm_bwd{i}")
        d_mod[i] = jnp.concatenate([d_shift, d_scale, d_gate], axis=1)
    grad_x = dx.reshape(1, T, D)

    packed = _pack(layer_grads[0], BF16, lead=NDEV)
    chip_sums[0] = chip_sum(0, packed, exchange(SIBLING_ROWS, packed, name="exchange_g0_sibling"))
    received[0] = exchange(CHIP_ROWS, chip_sums[0], name="exchange_g0_chips")
    per_name = {}
    for i in range(DEPTH):
        summed = sum_slots(received[i], name=f"sum_g{i}")
        parts = _unpack(summed, [a.shape for a in layer_shards(i)])
        for n, g in zip(layer_names[i % N_MIXERS], parts):
            per_name.setdefault(n, []).append(g)
    big_g = {n: jnp.stack(gs) for n, gs in per_name.items()}

    small_order = ['b_mod', 'gla_fwd_b', 'gla_bwd_b', 'gla_norm_g', 'attn_q_norm_g', 'attn_k_norm_g', 'final_norm_g',
                   'pool_scale', 'gla_fwd_w1', 'gla_fwd_w2', 'gla_bwd_w1', 'gla_bwd_w2']
    small_grads['b_mod'] = jnp.concatenate(d_mod, axis=0)
    small_grads['final_norm_g'] = d_final_g
    small_grads['pool_scale'] = jnp.concatenate(small_grads['pool_scale'], axis=0)
    part_items = [jnp.pad(loss_part.reshape(1), (0, LANES - 1))] + [small_grads[n] for n in small_order]
    part_shapes = [a.shape for a in part_items]
    g4 = exchange(GATHER, _pack(part_items, F32), name="gather_parts")
    tot = dict(zip(['loss'] + small_order, _unpack(sum_slots(g4, name="sum_parts"), part_shapes)))
    loss = tot['loss'][0]
    d_mod_all = _unpack(g4, part_shapes, lead=NDEV)[1]

    grads = {}
    grads.update(big_g)
    grads['b_mod'] = tot['b_mod']
    for n in ('gla_fwd_b', 'gla_bwd_b', 'gla_norm_g', 'attn_q_norm_g', 'attn_k_norm_g'):
        grads[n] = tot[n].reshape(W[n].shape)
    grads['final_norm_g'] = tot['final_norm_g'].reshape(D)
    ps_n = pool_scale.shape[1]
    grads['pool_scale'] = lax.dynamic_slice_in_dim(tot['pool_scale'], me * ps_n, ps_n, axis=1)
    rows = gla_fwd_w1.shape[1]
    cols = gla_fwd_w2.shape[2]
    for key in ('gla_fwd', 'gla_bwd'):
        grads[key + '_w1'] = lax.dynamic_slice_in_dim(tot[key + '_w1'], me * rows, rows, axis=0).reshape(1, rows, LR)
        grads[key + '_w2'] = lax.dynamic_slice_in_dim(tot[key + '_w2'], me * cols, cols, axis=1).reshape(1, LR, cols)

    c_t = c_all.T
    dm_slab = lax.dynamic_slice_in_dim(d_mod_all, me * Dm, Dm, axis=2)
    grads['w_mod'] = jnp.stack([outer_silu(c_t, dm_slab[:, i], name=f"mod_dw{i}") for i in range(DEPTH)])

    deltas, new_m, new_v = {}, {}, {}
    for n in WEIGHTS:
        deltas[n], new_m[n], new_v[n] = _adamw_nd(W[n], grads[n], M[n], V[n], name=f"adamw_{n}")

    return (loss, grad_x, *[grads[n] for n in WEIGHTS], *[deltas[n] for n in WEIGHTS],
            *[new_m[n] for n in WEIGHTS], *[new_v[n] for n in WEIGHTS])
```

```python
import functools
import math

import jax
import jax.numpy as jnp
from jax import lax
from jax.experimental import pallas as pl
from jax.experimental.pallas import tpu as pltpu

F32 = jnp.float32
BF16 = jnp.bfloat16
NDEV = 8
LANES = 128
VMEM_LIMIT = 56 * 1024 * 1024

D_MODEL = 2048
DEPTH = 4
N_MIXERS = 3
GRID_W = 64
NORM_EPS = 1e-6
POOL_WINDOWS = (2, 4, 8, 16)
GLA_HEADS = 4
GLA_LOWRANK = 16
GLA_TAU = 16.0
GLA_CHUNK = 128
ATTN_HEAD_DIM = 128
ATTN_KV_HEADS = 4
ROPE_THETA = 10000.0
ADAM_LR = 0.001
ADAM_B1 = 0.9
ADAM_B2 = 0.999
ADAM_EPS = 1e-08
ADAM_WD = 0.01
ADAM_STEP = 10

WEIGHTS = ['w_mod', 'b_mod', 'pool_w_in', 'pool_w_grp', 'pool_scale', 'pool_w_out', 'gla_w_in', 'gla_fwd_w1',
           'gla_fwd_w2', 'gla_fwd_b', 'gla_bwd_w1', 'gla_bwd_w2', 'gla_bwd_b', 'gla_norm_g', 'gla_w_out',
           'attn_w_in', 'attn_q_norm_g', 'attn_k_norm_g', 'attn_w_out', 'final_norm_g']


def _params(n_axes=0):
    sem = ("arbitrary",) * n_axes if n_axes else None
    return pltpu.CompilerParams(dimension_semantics=sem, vmem_limit_bytes=VMEM_LIMIT)


def _silu(g):
    return g * jax.nn.sigmoid(g)


def _dsilu(g):
    s = jax.nn.sigmoid(g)
    return s * (1.0 + g * (1.0 - s))


def _dot(a, b):
    return jnp.dot(a, b, preferred_element_type=F32)


def _dot_nt(a, b):
    return lax.dot_general(a, b, (((1,), (1,)), ((), ())), preferred_element_type=F32)


def _dot_tn(a, b):
    return lax.dot_general(a, b, (((0,), (0,)), ((), ())), preferred_element_type=F32)


def _tile(n, pref):
    if n <= pref:
        return n
    for step in (128, 16, 8):
        t = pref - pref % step
        while t >= step:
            if n % t == 0:
                return t
            t -= step
    raise ValueError((n, pref))


def _peer(k):
    x, y, c = lax.axis_index("x"), lax.axis_index("y"), lax.axis_index("c")
    px = 1 - x if k & 4 else x
    py = 1 - y if k & 2 else y
    pc = 1 - c if k & 1 else c
    return (px, py, pc), 4 * px + 2 * py + pc


def _my_id():
    return 4 * lax.axis_index("x") + 2 * lax.axis_index("y") + lax.axis_index("c")


NCHIP = NDEV // 2
GATHER = "gather"
GATHER_VIA_SIBLING = "g2"
SIBLING_ROWS = "d2d"
CHIP_ROWS = "ici"
ICI_FLIPS = (4, 2, 6)

EXCHANGE_SCRATCH = [pltpu.SemaphoreType.DMA((NDEV - 1,)), pltpu.SemaphoreType.DMA((NDEV - 1,)),
                    pltpu.SemaphoreType.DMA]


def _exchange_slots(kind):
    return NCHIP if kind in (SIBLING_ROWS, CHIP_ROWS) else NDEV


def _exchange_plan(kind, x_ref, o_ref):
    me = _my_id()
    c = lax.axis_index("c")
    chip = me // 2
    sib, sib_id = _peer(1)
    if kind == GATHER:
        sends = [(x_ref, o_ref.at[me], _peer(k)[0], o_ref.at[_peer(k)[1]]) for k in range(1, NDEV)]
        return (x_ref, o_ref.at[me]), sends
    if kind == GATHER_VIA_SIBLING:
        sends = [(x_ref, o_ref.at[me], sib, o_ref.at[sib_id])]
        sends += [(x_ref, o_ref.at[me], _peer(k)[0], o_ref.at[_peer(k)[1]]) for k in ICI_FLIPS]
        sends += [(o_ref.at[_peer(k)[1]], o_ref.at[_peer(k)[1]], sib, o_ref.at[_peer(k ^ 1)[1]]) for k in ICI_FLIPS]
        return (x_ref, o_ref.at[me]), sends
    if kind == SIBLING_ROWS:
        return None, [(x_ref.at[2 * q + (1 - c)], o_ref.at[q], sib, o_ref.at[q]) for q in range(NCHIP)]
    assert kind == CHIP_ROWS, kind
    sends = [(x_ref.at[chip ^ (k >> 1)], o_ref.at[chip], _peer(k)[0], o_ref.at[chip ^ (k >> 1)]) for k in ICI_FLIPS]
    return (x_ref.at[chip], o_ref.at[chip]), sends


def _exchange_copy(n, send, send_sems, recv_sems, incoming):
    src, dst, peer, lands = send
    return pltpu.make_async_remote_copy(
        src_ref=src, dst_ref=lands if incoming else dst, send_sem=send_sems.at[n], recv_sem=recv_sems.at[n],
        device_id=peer, device_id_type=pl.DeviceIdType.MESH)


def _exchange_start(kind, x_ref, o_ref, send_sems, recv_sems, local_sem):
    local, sends = _exchange_plan(kind, x_ref, o_ref)
    if local is not None:
        pltpu.make_async_copy(*local, local_sem).start()
    first = sends[:4] if kind == GATHER_VIA_SIBLING else sends
    for n, send in enumerate(first):
        _exchange_copy(n, send, send_sems, recv_sems, False).start()


def _exchange_wait(kind, x_ref, o_ref, send_sems, recv_sems, local_sem):
    local, sends = _exchange_plan(kind, x_ref, o_ref)
    order = list(range(len(sends)))
    if kind == GATHER_VIA_SIBLING:
        for j in range(3):
            _exchange_copy(1 + j, sends[1 + j], send_sems, recv_sems, True).wait_recv()
            _exchange_copy(4 + j, sends[4 + j], send_sems, recv_sems, False).start()
        order = [0, 4, 5, 6]
    for n in order:
        _exchange_copy(n, sends[n], send_sems, recv_sems, True).wait_recv()
    for n, send in enumerate(sends):
        _exchange_copy(n, send, send_sems, recv_sems, False).wait_send()
    if local is not None:
        pltpu.make_async_copy(*local, local_sem).wait()


def _exchange_out(kind, x):
    return jax.ShapeDtypeStruct((_exchange_slots(kind),) + tuple(x.shape[-2:]), x.dtype)


def exchange(kind, x, name):
    def body(x_ref, o_ref, send_sems, recv_sems, local_sem):
        _exchange_start(kind, x_ref, o_ref, send_sems, recv_sems, local_sem)
        _exchange_wait(kind, x_ref, o_ref, send_sems, recv_sems, local_sem)

    return pl.pallas_call(
        body, name=name, out_shape=_exchange_out(kind, x),
        in_specs=[pl.BlockSpec(memory_space=pl.ANY)],
        out_specs=pl.BlockSpec(memory_space=pl.ANY),
        scratch_shapes=EXCHANGE_SCRATCH,
    )(x)


def _pack(arrs, dtype, lead=None):
    unit = 16 * LANES
    if lead is None:
        flat = [a.astype(dtype).reshape(-1) for a in arrs]
        n = sum(f.shape[0] for f in flat)
        pad = (-n) % unit
        if pad:
            flat.append(jnp.zeros((pad,), dtype))
        return jnp.concatenate(flat).reshape(-1, LANES)
    flat = [a.astype(dtype).reshape(lead, -1) for a in arrs]
    n = sum(f.shape[1] for f in flat)
    pad = (-n) % unit
    if pad:
        flat.append(jnp.zeros((lead, pad), dtype))
    return jnp.concatenate(flat, axis=1).reshape(lead, -1, LANES)


def _unpack(buf, shapes, lead=None):
    out = []
    off = 0
    if lead is None:
        flat = buf.reshape(-1)
        for s in shapes:
            n = math.prod(s)
            out.append(flat[off:off + n].reshape(s))
            off += n
        return out
    flat = buf.reshape(lead, -1)
    for s in shapes:
        n = math.prod(s)
        out.append(flat[:, off:off + n].reshape((lead,) + tuple(s)))
        off += n
    return out


def mm(a, b, *, name, ta=False, tb=False, b_split=False, out_split=0, out_dtype=BF16, tm=2048, tn=1024, tk=None,
       bias=None, add=None, a_silu=False, a_scale=None, col_scale=None, side=None):
    K, M = a.shape if ta else a.shape[::-1]
    if b_split:
        S, d1, n = b.shape
        if tb:
            N, Kb = d1, S * n
        else:
            Kb, N = d1, S * n
    else:
        Kb, N = b.shape[::-1] if tb else b.shape
    assert Kb == K, (a.shape, b.shape, ta, tb, b_split)
    if tk is None:
        tk = 1024 if ta else 2048
    tm = _tile(M, tm)
    if b_split and not tb:
        tn = _tile(n, tn)
    elif out_split:
        tn = _tile(out_split, tn)
    else:
        tn = _tile(N, tn)
    if b_split and tb:
        tk = _tile(n, 1024)
    else:
        tk = _tile(K, tk)
    nk = K // tk

    a_spec = pl.BlockSpec((tk, tm), lambda i, j, k: (k, i)) if ta else pl.BlockSpec((tm, tk), lambda i, j, k: (i, k))
    if b_split and not tb:
        per = n // tn
        b_spec = pl.BlockSpec((None, tk, tn), lambda i, j, k: (j // per, k, j % per))
    elif b_split and tb:
        per = n // tk
        b_spec = pl.BlockSpec((None, tn, tk), lambda i, j, k: (k // per, j, k % per))
    elif tb:
        b_spec = pl.BlockSpec((tn, tk), lambda i, j, k: (j, k))
    else:
        b_spec = pl.BlockSpec((tk, tn), lambda i, j, k: (k, j))
    if out_split:
        per_o = out_split // tn
        o_spec = pl.BlockSpec((None, tm, tn), lambda i, j, k: (j // per_o, i, j % per_o))
        o_shape = jax.ShapeDtypeStruct((N // out_split, M, out_split), out_dtype)
    else:
        o_spec = pl.BlockSpec((tm, tn), lambda i, j, k: (i, j))
        o_shape = jax.ShapeDtypeStruct((M, N), out_dtype)
    ins = [a, b]
    in_specs = [a_spec, b_spec]
    if bias is not None:
        ins.append(bias)
        in_specs.append(pl.BlockSpec((1, tn), lambda i, j, k: (0, j)))
    if add is not None:
        ins.append(add)
        in_specs.append(pl.BlockSpec((tm, tn), lambda i, j, k: (i, j)))
    if a_scale is not None:
        assert not ta
        ins.append(a_scale)
        in_specs.append(pl.BlockSpec((1, tk), lambda i, j, k: (0, k)))
    if col_scale is not None:
        ins.append(col_scale)
        in_specs.append(pl.BlockSpec((1, tn), lambda i, j, k: (0, j)))
    has_bias, has_add, has_side = bias is not None, add is not None, side is not None
    has_ascale, has_cscale = a_scale is not None, col_scale is not None
    out_specs, out_shapes = [o_spec], [o_shape]
    scratch = [pltpu.VMEM((tm, tn) if nk > 1 else (8, LANES), F32)]
    if has_side:
        side_kind, side_x = side
        ins.append(side_x)
        in_specs.append(pl.BlockSpec(memory_space=pl.ANY))
        out_specs.append(pl.BlockSpec(memory_space=pl.ANY))
        out_shapes.append(_exchange_out(side_kind, side_x))
        scratch += EXCHANGE_SCRATCH
    gm, gn = M // tm, N // tn

    def body(*refs):
        a_ref, b_ref = refs[0], refs[1]
        pos = 2
        bias_ref = add_ref = None
        if has_bias:
            bias_ref = refs[pos]
            pos += 1
        if has_add:
            add_ref = refs[pos]
            pos += 1
        if has_ascale:
            ascale_ref = refs[pos]
            pos += 1
        if has_cscale:
            cscale_ref = refs[pos]
            pos += 1
        if has_side:
            side_refs = (side_kind, refs[pos], refs[pos + 2]) + tuple(refs[pos + 4:pos + 7])
            o_ref, acc_ref = refs[pos + 1], refs[pos + 3]
        else:
            o_ref, acc_ref = refs[pos], refs[pos + 1]
        k = pl.program_id(2)
        if has_side:
            i_, j_ = pl.program_id(0), pl.program_id(1)

            @pl.when((i_ == 0) & (j_ == 0) & (k == 0))
            def _():
                _exchange_start(*side_refs)

        def finish(r):
            if has_cscale:
                r = r * cscale_ref[...]
            if has_bias:
                r = r + bias_ref[...]
            if has_add:
                r = r + add_ref[...].astype(F32)
            o_ref[...] = r.astype(out_dtype)

        av = a_ref[...]
        if a_silu:
            av = _silu(av.astype(F32))
        if has_ascale:
            av = av.astype(F32) * ascale_ref[...]
        av = av.astype(BF16)
        bv = b_ref[...].astype(BF16)
        dn = (((0 if ta else 1,), (1 if tb else 0,)), ((), ()))
        part = lax.dot_general(av, bv, dn, preferred_element_type=F32)
        if nk == 1:
            finish(part)
        else:
            @pl.when(k == 0)
            def _():
                acc_ref[...] = part

            @pl.when(k > 0)
            def _():
                acc_ref[...] += part

            @pl.when(k == nk - 1)
            def _():
                finish(acc_ref[...])

        if has_side:
            @pl.when((i_ == gm - 1) & (j_ == gn - 1) & (k == nk - 1))
            def _():
                _exchange_wait(*side_refs)

    res = pl.pallas_call(
        body, name=name, grid=(gm, gn, nk), in_specs=in_specs, out_specs=out_specs, out_shape=out_shapes,
        scratch_shapes=scratch, compiler_params=_params(3))(*ins)
    return (res[0], res[1]) if has_side else res[0]


def gmm_nn(a, w, *, name, tb=False, out_dtype=BF16, tm=2048):
    T = a.shape[0]
    G = w.shape[0]
    Kg = a.shape[1] // G
    Ng = w.shape[1] if tb else w.shape[2]
    tm = _tile(T, tm)

    def body(a_ref, w_ref, o_ref):
        wv = w_ref[...].astype(BF16)
        av = a_ref[...].astype(BF16)
        r = _dot_nt(av, wv) if tb else _dot(av, wv)
        o_ref[...] = r.astype(out_dtype)

    return pl.pallas_call(
        body, name=name, grid=(G, T // tm),
        in_specs=[pl.BlockSpec((tm, Kg), lambda g, i: (i, g)),
                  pl.BlockSpec((None,) + tuple(w.shape[1:]), lambda g, i: (g, 0, 0))],
        out_specs=pl.BlockSpec((tm, Ng), lambda g, i: (i, g)),
        out_shape=jax.ShapeDtypeStruct((T, G * Ng), out_dtype), compiler_params=_params(2))(a, w)


def gmm_tn(a, b, G, *, name, tk=2048):
    T = a.shape[0]
    Kg = a.shape[1] // G
    Ng = b.shape[1] // G
    tk = _tile(T, tk)

    def body(a_ref, b_ref, o_ref):
        @pl.when(pl.program_id(1) == 0)
        def _():
            o_ref[...] = jnp.zeros_like(o_ref)

        o_ref[...] += _dot_tn(a_ref[...].astype(BF16), b_ref[...].astype(BF16))

    return pl.pallas_call(
        body, name=name, grid=(G, T // tk),
        in_specs=[pl.BlockSpec((tk, Kg), lambda g, i: (i, g)), pl.BlockSpec((tk, Ng), lambda g, i: (i, g))],
        out_specs=pl.BlockSpec((None, Kg, Ng), lambda g, i: (g, 0, 0)),
        out_shape=jax.ShapeDtypeStruct((G, Kg, Ng), F32), compiler_params=_params(2))(a, b)


def _row(tr, w, cb=0):
    return pl.BlockSpec((tr, w), lambda i: (i, cb))


def _bc(w):
    return pl.BlockSpec((1, w), lambda i: (0, 0))


def prenorm(x, scale, shift, *, name):
    T, D = x.shape
    tr = _tile(T, 256)

    def body(x_ref, sc_ref, sh_ref, h_ref, ht_ref):
        xv = x_ref[...]
        rstd = lax.rsqrt(jnp.mean(xv * xv, axis=-1, keepdims=True) + NORM_EPS)
        hv = ((xv * rstd) * (1.0 + sc_ref[...]) + sh_ref[...]).astype(BF16)
        h_ref[...] = hv
        ht_ref[...] = hv.T

    return pl.pallas_call(
        body, name=name, grid=(T // tr,), in_specs=[_row(tr, D), _bc(D), _bc(D)],
        out_specs=[_row(tr, D), pl.BlockSpec((D, tr), lambda i: (0, i))],
        out_shape=[jax.ShapeDtypeStruct((T, D), BF16), jax.ShapeDtypeStruct((D, T), BF16)],
        compiler_params=_params(1))(x, scale, shift)


def prenorm_bwd(x, dh, dxn, scale, *, name):
    T, D = x.shape
    tr = _tile(T, 256)

    def body(x_ref, dh_ref, dxn_ref, sc_ref, dx_ref, dsc_ref, dsh_ref):
        @pl.when(pl.program_id(0) == 0)
        def _():
            dsc_ref[...] = jnp.zeros_like(dsc_ref)
            dsh_ref[...] = jnp.zeros_like(dsh_ref)

        xv = x_ref[...]
        dhv = dh_ref[...].astype(F32)
        rstd = lax.rsqrt(jnp.mean(xv * xv, axis=-1, keepdims=True) + NORM_EPS)
        r = xv * rstd
        dsc_ref[...] += jnp.sum(dhv * r, axis=0, keepdims=True)
        dsh_ref[...] += jnp.sum(dhv, axis=0, keepdims=True)
        dr = dhv * (1.0 + sc_ref[...])
        dx_ref[...] = dxn_ref[...] + rstd * (dr - r * jnp.mean(dr * r, axis=-1, keepdims=True))

    return pl.pallas_call(
        body, name=name, grid=(T // tr,), in_specs=[_row(tr, D), _row(tr, D), _row(tr, D), _bc(D)],
        out_specs=[_row(tr, D), _bc(D), _bc(D)],
        out_shape=[jax.ShapeDtypeStruct((T, D), F32), jax.ShapeDtypeStruct((1, D), F32),
                   jax.ShapeDtypeStruct((1, D), F32)], compiler_params=_params(1))(x, dh, dxn, scale)


def out_proj_grads(g, w, gate, *, name):
    K, D = g.shape
    tr = _tile(K, 256)

    def body(g_ref, w_ref, gate_ref, dw_ref, dg_ref):
        @pl.when(pl.program_id(0) == 0)
        def _():
            dg_ref[...] = jnp.zeros_like(dg_ref)

        gv = g_ref[...]
        dw_ref[...] = (gv * gate_ref[...]).astype(BF16)
        dg_ref[...] += jnp.sum(gv * w_ref[...].astype(F32), axis=0, keepdims=True)

    return pl.pallas_call(
        body, name=name, grid=(K // tr,), in_specs=[_row(tr, D), _row(tr, D), _bc(D)],
        out_specs=[_row(tr, D), _bc(D)],
        out_shape=[jax.ShapeDtypeStruct((K, D), BF16), jax.ShapeDtypeStruct((1, D), F32)],
        compiler_params=_params(1))(g, w, gate)


def loss_head(x, g, target, *, name):
    T, D = x.shape
    tr = _tile(T, 256)

    def body(x_ref, g_ref, t_ref, loss_ref, dx_ref, dg_ref):
        @pl.when(pl.program_id(0) == 0)
        def _():
            loss_ref[...] = jnp.zeros_like(loss_ref)
            dg_ref[...] = jnp.zeros_like(dg_ref)

        xv = x_ref[...]
        gv = g_ref[...]
        rstd = lax.rsqrt(jnp.mean(xv * xv, axis=-1, keepdims=True) + NORM_EPS)
        r = xv * rstd
        e = r * gv - t_ref[...]
        loss_ref[...] += 0.5 * jnp.sum(jnp.mean(e * e, axis=-1, keepdims=True), axis=0, keepdims=True)
        dout = e * (1.0 / D)
        dg_ref[...] += jnp.sum(dout * r, axis=0, keepdims=True)
        dr = dout * gv
        dx_ref[...] = rstd * (dr - r * jnp.mean(dr * r, axis=-1, keepdims=True))

    return pl.pallas_call(
        body, name=name, grid=(T // tr,), in_specs=[_row(tr, D), _bc(D), _row(tr, D)],
        out_specs=[pl.BlockSpec((1, 1), lambda i: (0, 0)), _row(tr, D), _bc(D)],
        out_shape=[jax.ShapeDtypeStruct((1, 1), F32), jax.ShapeDtypeStruct((T, D), F32),
                   jax.ShapeDtypeStruct((1, D), F32)], compiler_params=_params(1))(x, g, target)


def _col_tile(W, off):
    cw = math.gcd(W, off) if off else W
    cw = math.gcd(cw, 1024) if cw > 1024 else cw
    return cw


def gated(a, proj, g_off, scale, *, name):
    T, W = a.shape
    cw = _col_tile(W, g_off)
    gb = g_off // cw
    tr = _tile(T, 256)

    def body(a_ref, g_ref, s_ref, o_ref, ot_ref):
        y = (a_ref[...].astype(F32) * s_ref[...] * _silu(g_ref[...].astype(F32))).astype(BF16)
        o_ref[...] = y
        ot_ref[...] = y.T

    return pl.pallas_call(
        body, name=name, grid=(W // cw, T // tr),
        in_specs=[pl.BlockSpec((tr, cw), lambda j, i: (i, j)), pl.BlockSpec((tr, cw), lambda j, i: (i, gb + j)),
                  pl.BlockSpec((1, cw), lambda j, i: (0, j))],
        out_specs=[pl.BlockSpec((tr, cw), lambda j, i: (i, j)), pl.BlockSpec((cw, tr), lambda j, i: (j, i))],
        out_shape=[jax.ShapeDtypeStruct((T, W), BF16), jax.ShapeDtypeStruct((W, T), BF16)],
        compiler_params=_params(2))(a, proj, scale)


def gated_bwd(dy2, a, proj, g_off, scale, *, name):
    T, W = a.shape
    cw = _col_tile(W, g_off)
    gb = g_off // cw
    tr = _tile(T, 256)

    def body(d_ref, a_ref, g_ref, s_ref, da_ref, dg_ref, ds_ref):
        @pl.when(pl.program_id(1) == 0)
        def _():
            ds_ref[...] = jnp.zeros_like(ds_ref)

        d = d_ref[...].astype(F32)
        av = a_ref[...].astype(F32)
        gv = g_ref[...].astype(F32)
        sv = s_ref[...]
        dsg = d * _silu(gv)
        da_ref[...] = (dsg * sv).astype(BF16)
        dg_ref[...] = (d * av * sv * _dsilu(gv)).astype(BF16)
        ds_ref[...] += jnp.sum(dsg * av, axis=0, keepdims=True)

    blk = pl.BlockSpec((tr, cw), lambda j, i: (i, j))
    return pl.pallas_call(
        body, name=name, grid=(W // cw, T // tr),
        in_specs=[blk, blk, pl.BlockSpec((tr, cw), lambda j, i: (i, gb + j)),
                  pl.BlockSpec((1, cw), lambda j, i: (0, j))],
        out_specs=[blk, pl.BlockSpec((tr, cw), lambda j, i: (i, gb + j)), pl.BlockSpec((1, cw), lambda j, i: (0, j))],
        out_shape=[jax.ShapeDtypeStruct((T, W), BF16), jax.ShapeDtypeStruct(proj.shape, BF16),
                   jax.ShapeDtypeStruct((1, W), F32)], compiler_params=_params(2))(dy2, a, proj, scale)


HALO = 16


def band(u, W, *, transpose, name, into=None):
    T = u.shape[0]
    R = _tile(T, 256)
    nb = T // R
    G = len(POOL_WINDOWS)
    Cg = W // G
    hal = min(HALO, R)

    def body(p_ref, c_ref, n_ref, *rest):
        o_ref = rest[-1]
        i = pl.program_id(0)
        out_pos = lax.broadcasted_iota(jnp.int32, (R, 1), 0) + i * R
        parts = ((p_ref, i * R - hal, hal, R - hal), (c_ref, i * R, R, 0), (n_ref, (i + 1) * R, hal, 0))
        for gi, w in enumerate(POOL_WINDOWS):
            half = w // 2
            cols = slice(gi * Cg, (gi + 1) * Cg)
            acc = jnp.zeros((R, Cg), F32)
            for ref, base, n, r0 in parts:
                src_pos = lax.broadcasted_iota(jnp.int32, (1, n), 1) + base
                valid = (src_pos >= 0) & (src_pos < T)
                src = ref[r0:r0 + n, cols]
                if not transpose:
                    m = (src_pos >= out_pos - half) & (src_pos < out_pos + half) & valid
                else:
                    m = (out_pos >= src_pos - half) & (out_pos < src_pos + half) & valid
                    sp = lax.broadcasted_iota(jnp.int32, (n, 1), 0) + base
                    cnt = jnp.minimum(sp + half, T) - jnp.maximum(sp - half, 0)
                    src = (src.astype(F32) / jnp.maximum(cnt, 1).astype(F32)).astype(BF16)
                acc = acc + _dot(m.astype(BF16), src.astype(BF16))
            if not transpose:
                cnt = jnp.minimum(out_pos + half, T) - jnp.maximum(out_pos - half, 0)
                acc = acc / cnt.astype(F32)
            o_ref[:, cols] = (acc - c_ref[:, cols].astype(F32)).astype(BF16)

    in_specs = [pl.BlockSpec((R, W), lambda i: (jnp.maximum(i - 1, 0), 0)), pl.BlockSpec((R, W), lambda i: (i, 0)),
                pl.BlockSpec((R, W), lambda i: (jnp.minimum(i + 1, nb - 1), 0))]
    ins, out_cols, aliases = [u, u, u], W, {}
    if into is not None:
        ins.append(into)
        in_specs.append(pl.BlockSpec(memory_space=pl.ANY))
        out_cols, aliases = into.shape[1], {3: 0}
    return pl.pallas_call(
        body, name=name, grid=(nb,), in_specs=in_specs, out_specs=pl.BlockSpec((R, W), lambda i: (i, 0)),
        out_shape=jax.ShapeDtypeStruct((T, out_cols), BF16), input_output_aliases=aliases,
        compiler_params=_params(1))(*ins)


def _log_sigmoid(x):
    return jnp.minimum(x, 0.0) - jnp.log1p(jnp.exp(-jnp.abs(x)))


def _split3(x):
    hi = x.astype(BF16)
    r1 = x - hi.astype(F32)
    md = r1.astype(BF16)
    lo = (r1 - md.astype(F32)).astype(BF16)
    return hi, md, lo


def _tri_sum(tri, x):
    hi, md, lo = _split3(x)
    return _dot(tri, hi) + _dot(tri, md) + _dot(tri, lo)


def _gla_masks(C, reverse):
    row = lax.broadcasted_iota(jnp.int32, (C, C), 0)
    col = lax.broadcasted_iota(jnp.int32, (C, C), 1)
    if not reverse:
        return (col <= row), (col >= row), (col <= row)
    return (col >= row), (col <= row), (col > row)


def _gla_dims(proj):
    VW = proj.shape[1] // 3
    KW = VW // 2
    return KW, VW, KW // GLA_HEADS, VW // GLA_HEADS


def gla_fwd(proj, z, bias, *, reverse, name):
    T = proj.shape[0]
    KW, VW, DK, DV = _gla_dims(proj)
    H = GLA_HEADS
    C = _tile(T, GLA_CHUNK)
    NC = T // C
    cidx = (lambda i: NC - 1 - i) if reverse else (lambda i: i)
    last = 0 if reverse else C - 1

    def body(q_ref, k_ref, v_ref, z_ref, b_ref, o_ref, s_ref, S_scr):
        @pl.when(pl.program_id(0) == 0)
        def _():
            S_scr[...] = jnp.zeros_like(S_scr)

        cum, _, amask = _gla_masks(C, reverse)
        la = _log_sigmoid(z_ref[...] + b_ref[...]) * (1.0 / GLA_TAU)
        b = _tri_sum(cum.astype(BF16), la)
        for h in range(H):
            ks = slice(h * DK, (h + 1) * DK)
            vs = slice(h * DV, (h + 1) * DV)
            bh = b[:, ks]
            mid = bh[C // 2:C // 2 + 1, :]
            bl = bh[last:last + 1, :]
            q = q_ref[:, ks].astype(F32) * (DK ** -0.5)
            k = k_ref[:, ks].astype(F32)
            v = v_ref[:, vs]
            qe = (q * jnp.exp(bh)).astype(BF16)
            qt = (q * jnp.exp(bh - mid)).astype(BF16)
            kt = (k * jnp.exp(mid - bh)).astype(BF16)
            kd = (k * jnp.exp(bl - bh)).astype(BF16)
            St = S_scr[h]
            Sb = St.astype(BF16)
            s_ref[0, h] = Sb
            A = jnp.where(amask, _dot_nt(qt, kt), 0.0).astype(BF16)
            o_ref[:, vs] = _dot_nt(qe, Sb) + _dot(A, v)
            S_scr[h] = St * jnp.exp(bl) + _dot_tn(v, kd)

    return pl.pallas_call(
        body, name=name, grid=(NC,),
        in_specs=[pl.BlockSpec((C, KW), lambda i: (cidx(i), 0)), pl.BlockSpec((C, KW), lambda i: (cidx(i), 1)),
                  pl.BlockSpec((C, VW), lambda i: (cidx(i), 1)), pl.BlockSpec((C, KW), lambda i: (cidx(i), 0)),
                  pl.BlockSpec((1, KW), lambda i: (0, 0))],
        out_specs=[pl.BlockSpec((C, VW), lambda i: (cidx(i), 0)),
                   pl.BlockSpec((1, H, DV, DK), lambda i: (cidx(i), 0, 0, 0))],
        out_shape=[jax.ShapeDtypeStruct((T, VW), F32), jax.ShapeDtypeStruct((NC, H, DV, DK), BF16)],
        scratch_shapes=[pltpu.VMEM((H, DV, DK), F32)], compiler_params=_params(1))(proj, proj, proj, z, bias)


def gla_bwd(proj, z, bias, states, do, prev, *, reverse, name, into=None):
    T = proj.shape[0]
    KW, VW, DK, DV = _gla_dims(proj)
    H = GLA_HEADS
    C = _tile(T, GLA_CHUNK)
    NC = T // C
    cidx = (lambda i: i) if reverse else (lambda i: NC - 1 - i)
    last = 0 if reverse else C - 1
    has_prev = prev is not None
    merged = into is not None
    assert has_prev or not merged
    odt = BF16 if has_prev else F32

    def body(*refs):
        q_ref, k_ref, v_ref, z_ref, b_ref, s_ref, do_ref = refs[:7]
        pos = 7
        if has_prev:
            pq_ref, pk_ref, pv_ref = refs[7:10]
            pos = 10
        if merged:
            dp_ref, dz_ref, db_ref, dS_scr = refs[pos + 1:pos + 5]
        else:
            dq_ref, dk_ref, dv_ref, dz_ref, db_ref, dS_scr = refs[pos:pos + 6]

        @pl.when(pl.program_id(0) == 0)
        def _():
            dS_scr[...] = jnp.zeros_like(dS_scr)
            db_ref[...] = jnp.zeros_like(db_ref)

        cum, cum_t, amask = _gla_masks(C, reverse)
        xg = z_ref[...] + b_ref[...]
        la = _log_sigmoid(xg) * (1.0 / GLA_TAU)
        b = _tri_sum(cum.astype(BF16), la)
        cum_t_bf = cum_t.astype(BF16)
        for h in range(H):
            ks = slice(h * DK, (h + 1) * DK)
            vs = slice(h * DV, (h + 1) * DV)
            bh = b[:, ks]
            mid = bh[C // 2:C // 2 + 1, :]
            bl = bh[last:last + 1, :]
            q = q_ref[:, ks].astype(F32) * (DK ** -0.5)
            k = k_ref[:, ks].astype(F32)
            v = v_ref[:, vs]
            dov = do_ref[:, vs]
            e_b = jnp.exp(bh)
            e_up = jnp.exp(bh - mid)
            e_dn = jnp.exp(mid - bh)
            e_l = jnp.exp(bl - bh)
            e_bl = jnp.exp(bl)
            qe = (q * e_b).astype(BF16)
            qt = (q * e_up).astype(BF16)
            kt = (k * e_dn).astype(BF16)
            kd = (k * e_l).astype(BF16)
            Sb = s_ref[0, h]
            dSt = dS_scr[h]
            dSb = dSt.astype(BF16)
            A = jnp.where(amask, _dot_nt(qt, kt), 0.0).astype(BF16)
            dA = jnp.where(amask, _dot_nt(dov, v), 0.0).astype(BF16)
            dv = _dot_tn(A, dov) + _dot_nt(kd, dSb)
            dq = _dot(dA, kt) * e_up + _dot(dov, Sb) * e_b
            dk_state = _dot(v, dSb) * e_l
            dk = _dot_tn(dA, qt) * e_dn + dk_state
            d_b = dq * q - dk * k
            d_bl = (jnp.sum(dk_state * k, axis=0, keepdims=True)
                    + e_bl * jnp.sum(dSt * Sb.astype(F32), axis=0, keepdims=True))
            d_la = _tri_sum(cum_t_bf, d_b) + d_bl
            dz = d_la * (1.0 / GLA_TAU) * jax.nn.sigmoid(-xg[:, ks])
            dq = dq * (DK ** -0.5)
            if has_prev:
                dq = dq + pq_ref[:, ks]
                dk = dk + pk_ref[:, ks]
                dv = dv + pv_ref[:, vs]
            if merged:
                dp_ref[:, ks] = dq.astype(BF16)
                dp_ref[:, slice(KW + h * DK, KW + (h + 1) * DK)] = dk.astype(BF16)
                dp_ref[:, slice(2 * KW + h * DV, 2 * KW + (h + 1) * DV)] = dv.astype(BF16)
            else:
                dq_ref[:, ks] = dq.astype(odt)
                dk_ref[:, ks] = dk.astype(odt)
                dv_ref[:, vs] = dv.astype(odt)
            dz_ref[:, ks] = dz.astype(BF16)
            db_ref[:, ks] += jnp.sum(dz, axis=0, keepdims=True)
            dS_scr[h] = dSt * e_bl + _dot_tn(dov, qe)

    kspec = lambda cb: pl.BlockSpec((C, KW), lambda i: (cidx(i), cb))
    vspec = lambda cb: pl.BlockSpec((C, VW), lambda i: (cidx(i), cb))
    ins = [proj, proj, proj, z, bias, states, do]
    in_specs = [kspec(0), kspec(1), vspec(1), kspec(0), pl.BlockSpec((1, KW), lambda i: (0, 0)),
                pl.BlockSpec((1, H, DV, DK), lambda i: (cidx(i), 0, 0, 0)), vspec(0)]
    if has_prev:
        ins += list(prev)
        in_specs += [kspec(0), kspec(0), vspec(0)]
    tail_specs = [kspec(0), pl.BlockSpec((1, KW), lambda i: (0, 0))]
    tail_shapes = [jax.ShapeDtypeStruct((T, KW), BF16), jax.ShapeDtypeStruct((1, KW), F32)]
    if merged:
        ins.append(into)
        in_specs.append(pl.BlockSpec(memory_space=pl.ANY))
        out_specs = [pl.BlockSpec((C, 2 * KW + VW), lambda i: (cidx(i), 0))] + tail_specs
        out_shapes = [jax.ShapeDtypeStruct(into.shape, BF16)] + tail_shapes
        aliases = {len(ins) - 1: 0}
    else:
        out_specs = [kspec(0), kspec(0), vspec(0)] + tail_specs
        out_shapes = [jax.ShapeDtypeStruct((T, KW), odt), jax.ShapeDtypeStruct((T, KW), odt),
                      jax.ShapeDtypeStruct((T, VW), odt)] + tail_shapes
        aliases = {}
    return pl.pallas_call(
        body, name=name, grid=(NC,), in_specs=in_specs, out_specs=out_specs, out_shape=out_shapes,
        input_output_aliases=aliases, scratch_shapes=[pltpu.VMEM((H, DV, DK), F32)],
        compiler_params=_params(1))(*ins)


def gla_out(o_f, o_b, proj, g_off, norm_g, *, name):
    T, VW = o_f.shape
    H = GLA_HEADS
    DV = VW // H
    gb = g_off // VW
    tr = _tile(T, 256)

    def body(f_ref, b_ref, g_ref, n_ref, y_ref, yt_ref):
        for h in range(H):
            vs = slice(h * DV, (h + 1) * DV)
            o = f_ref[:, vs] + b_ref[:, vs]
            rstd = lax.rsqrt(jnp.mean(o * o, axis=-1, keepdims=True) + NORM_EPS)
            y = (o * rstd * n_ref[...] * _silu(g_ref[:, vs].astype(F32))).astype(BF16)
            y_ref[:, vs] = y
            yt_ref[vs, :] = y.T

    return pl.pallas_call(
        body, name=name, grid=(T // tr,),
        in_specs=[_row(tr, VW), _row(tr, VW), _row(tr, VW, gb), _bc(DV)],
        out_specs=[_row(tr, VW), pl.BlockSpec((VW, tr), lambda i: (0, i))],
        out_shape=[jax.ShapeDtypeStruct((T, VW), BF16), jax.ShapeDtypeStruct((VW, T), BF16)],
        compiler_params=_params(1))(o_f, o_b, proj, norm_g)


def gla_out_bwd(dy2, o_f, o_b, proj, g_off, norm_g, *, name):
    T, VW = o_f.shape
    H = GLA_HEADS
    DV = VW // H
    gb = g_off // VW
    tr = _tile(T, 256)

    def body(d_ref, f_ref, b_ref, g_ref, n_ref, do_ref, dg_ref, dn_ref):
        @pl.when(pl.program_id(0) == 0)
        def _():
            dn_ref[...] = jnp.zeros_like(dn_ref)

        nv = n_ref[...]
        for h in range(H):
            vs = slice(h * DV, (h + 1) * DV)
            o = f_ref[:, vs] + b_ref[:, vs]
            rstd = lax.rsqrt(jnp.mean(o * o, axis=-1, keepdims=True) + NORM_EPS)
            r = o * rstd
            gv = g_ref[:, vs].astype(F32)
            d = d_ref[:, vs].astype(F32)
            dg_ref[:, vs] = (d * r * nv * _dsilu(gv)).astype(BF16)
            dn_o = d * _silu(gv)
            dn_ref[...] += jnp.sum(dn_o * r, axis=0, keepdims=True)
            dr = dn_o * nv
            do_ref[:, vs] = (rstd * (dr - r * jnp.mean(dr * r, axis=-1, keepdims=True))).astype(BF16)

    return pl.pallas_call(
        body, name=name, grid=(T // tr,),
        in_specs=[_row(tr, VW), _row(tr, VW), _row(tr, VW), _row(tr, VW, gb), _bc(DV)],
        out_specs=[_row(tr, VW), _row(tr, VW, gb), _bc(DV)],
        out_shape=[jax.ShapeDtypeStruct((T, VW), BF16), jax.ShapeDtypeStruct(proj.shape, BF16),
                   jax.ShapeDtypeStruct((1, DV), F32)], compiler_params=_params(1))(dy2, o_f, o_b, proj, norm_g)


def _rope_tables(T):
    hd = ATTN_HEAD_DIM
    axis_dim = hd // 2
    rows = T // GRID_W
    t = jnp.arange(T)
    row = (t // GRID_W - rows // 2).astype(F32)
    col = (t % GRID_W - GRID_W // 2).astype(F32)
    inv = ROPE_THETA ** (-jnp.arange(0, axis_dim, 2, dtype=F32) / axis_dim)
    ang = jnp.concatenate([row[:, None] * inv, col[:, None] * inv], axis=-1)
    cos = jnp.repeat(jnp.cos(ang), 2, axis=-1)
    sin = jnp.repeat(jnp.sin(ang), 2, axis=-1)
    sign = jnp.where(jnp.arange(hd) % 2 == 0, -1.0, 1.0).astype(F32)
    return cos, sin * sign


def _pair_swap(x):
    n = x.shape[-1]
    lane = lax.broadcasted_iota(jnp.int32, x.shape, x.ndim - 1)
    return jnp.where(lane % 2 == 0, pltpu.roll(x, n - 1, x.ndim - 1), pltpu.roll(x, 1, x.ndim - 1))


def _attn_dims(proj):
    hd = ATTN_HEAD_DIM
    kvw = ATTN_KV_HEADS * hd
    qw = (proj.shape[1] - 2 * kvw) // 2
    return qw, kvw, qw // hd, (qw // hd) // ATTN_KV_HEADS


LOG2E = 1.4426950408889634
LN2 = 0.6931471805599453


def _q_mult():
    return ATTN_HEAD_DIM ** -0.5 * LOG2E


def attn_prep(proj, qg, kg, cos, sin, *, name):
    T = proj.shape[0]
    QW, KVW, NH, G = _attn_dims(proj)
    hd = ATTN_HEAD_DIM
    KV = ATTN_KV_HEADS
    assert QW % KVW == 0
    tr = _tile(T, 256)

    def body(q_ref, k_ref, v_ref, qg_ref, kg_ref, c_ref, s_ref, qo_ref, ko_ref, vo_ref):
        cv, sv = c_ref[...], s_ref[...]

        def one(x, gain, mult):
            rstd = lax.rsqrt(jnp.mean(x * x, axis=-1, keepdims=True) + NORM_EPS)
            xs = x * rstd * gain
            return (xs * cv + _pair_swap(xs) * sv) * mult

        for h in range(NH):
            hs = slice(h * hd, (h + 1) * hd)
            qo_ref[h] = one(q_ref[:, hs].astype(F32), qg_ref[...], _q_mult()).astype(BF16)
        for h in range(KV):
            hs = slice(h * hd, (h + 1) * hd)
            ko_ref[:, hs] = one(k_ref[:, hs].astype(F32), kg_ref[...], 1.0).astype(BF16)
            vo_ref[:, 2 * h * hd:(2 * h + 1) * hd] = v_ref[:, hs]
            vo_ref[:, (2 * h + 1) * hd:(2 * h + 2) * hd] = jnp.ones((tr, hd), BF16)

    return pl.pallas_call(
        body, name=name, grid=(T // tr,),
        in_specs=[_row(tr, QW), _row(tr, KVW, QW // KVW), _row(tr, KVW, QW // KVW + 1), _bc(hd), _bc(hd),
                  _row(tr, hd), _row(tr, hd)],
        out_specs=[pl.BlockSpec((NH, tr, hd), lambda i: (0, i, 0)), _row(tr, KVW), _row(tr, 2 * KVW)],
        out_shape=[jax.ShapeDtypeStruct((NH, T, hd), BF16), jax.ShapeDtypeStruct((T, KVW), BF16),
                   jax.ShapeDtypeStruct((T, 2 * KVW), BF16)],
        compiler_params=_params(1))(proj, proj, proj, qg, kg, cos, sin)


def attn_prep_bwd(dqr, dkr, proj, qg, kg, cos, sin, into, *, name):
    T = proj.shape[0]
    QW, KVW, NH, G = _attn_dims(proj)
    hd = ATTN_HEAD_DIM
    tr = _tile(T, 256)

    def body(dq_ref, dk_ref, q_ref, k_ref, qg_ref, kg_ref, c_ref, s_ref, into_ref, o_ref, dqg_ref, dkg_ref):
        @pl.when(pl.program_id(0) == 0)
        def _():
            dqg_ref[...] = jnp.zeros_like(dqg_ref)
            dkg_ref[...] = jnp.zeros_like(dkg_ref)

        cv, sv = c_ref[...], s_ref[...]

        def one(d, x, gain, mult):
            d = d * mult
            dxs = d * cv - _pair_swap(d) * sv
            rstd = lax.rsqrt(jnp.mean(x * x, axis=-1, keepdims=True) + NORM_EPS)
            xn = x * rstd
            dgain = jnp.sum(dxs * xn, axis=0, keepdims=True)
            dxn = dxs * gain
            return rstd * (dxn - xn * jnp.mean(dxn * xn, axis=-1, keepdims=True)), dgain

        for h in range(NH):
            hs = slice(h * hd, (h + 1) * hd)
            dx, dgain = one(dq_ref[h].astype(F32), q_ref[:, hs].astype(F32), qg_ref[...], _q_mult())
            o_ref[:, hs] = dx.astype(BF16)
            dqg_ref[...] += dgain
        for h in range(ATTN_KV_HEADS):
            hs = slice(h * hd, (h + 1) * hd)
            dx, dgain = one(dk_ref[:, hs].astype(F32), k_ref[:, hs].astype(F32), kg_ref[...], 1.0)
            o_ref[:, slice(QW + h * hd, QW + (h + 1) * hd)] = dx.astype(BF16)
            dkg_ref[...] += dgain

    return pl.pallas_call(
        body, name=name, grid=(T // tr,),
        in_specs=[pl.BlockSpec((NH, tr, hd), lambda i: (0, i, 0)), _row(tr, KVW), _row(tr, QW),
                  _row(tr, KVW, QW // KVW), _bc(hd), _bc(hd), _row(tr, hd), _row(tr, hd),
                  pl.BlockSpec(memory_space=pl.ANY)],
        out_specs=[_row(tr, QW + KVW), _bc(hd), _bc(hd)],
        out_shape=[jax.ShapeDtypeStruct(into.shape, BF16), jax.ShapeDtypeStruct((1, hd), F32),
                   jax.ShapeDtypeStruct((1, hd), F32)],
        input_output_aliases={8: 0},
        compiler_params=_params(1))(dqr, dkr, proj, proj, qg, kg, cos, sin, into)


FLASH_BQ = 256
FLASH_BK = 512


def flash_fwd(q, kr, vext, *, name):
    NH, T, hd = q.shape
    KV = ATTN_KV_HEADS
    G = NH // KV
    bq = _tile(T, FLASH_BQ)
    bk = _tile(T, FLASH_BK)
    nk = T // bk

    def body(q_ref, k_ref, v_ref, o_ref, lse_ref, m_scr, acc_scr, sa_scr, sb_scr):
        m_scr[...] = jnp.full_like(m_scr, -jnp.inf)
        acc_scr[...] = jnp.zeros_like(acc_scr)

        def scores(c, s_scr):
            kc = k_ref[pl.ds(pl.multiple_of(c * bk, bk), bk), :]
            for g in range(G):
                s_scr[g] = _dot_nt(q_ref[g], kc)

        def consume(c, s_scr):
            vc = v_ref[pl.ds(pl.multiple_of(c * bk, bk), bk), :]
            for g in range(G):
                s = s_scr[g]
                m_old = m_scr[g]
                m_new = jnp.maximum(m_old, jnp.max(s, axis=-1, keepdims=True))
                p = jnp.exp2(s - m_new)
                acc_scr[g] = jnp.exp2(m_old - m_new) * acc_scr[g] + _dot(p.astype(BF16), vc)
                m_scr[g] = m_new

        scores(0, sa_scr)
        if nk % 2 == 0:
            def pair(t, carry):
                scores(2 * t + 1, sb_scr)
                consume(2 * t, sa_scr)
                scores(2 * t + 2, sa_scr)
                consume(2 * t + 1, sb_scr)
                return carry

            lax.fori_loop(0, nk // 2 - 1, pair, 0)
            scores(nk - 1, sb_scr)
            consume(nk - 2, sa_scr)
            consume(nk - 1, sb_scr)
        else:
            def single(c, carry):
                consume(c, sa_scr)
                scores(c + 1, sa_scr)
                return carry

            lax.fori_loop(0, nk - 1, single, 0)
            consume(nk - 1, sa_scr)
        for g in range(G):
            a = acc_scr[g]
            l = a[:, hd:]
            o_ref[g] = a[:, :hd] / l
            lse_ref[g] = m_scr[g] + jnp.log2(l[:, 0:1])

    return pl.pallas_call(
        body, name=name, grid=(KV, T // bq),
        in_specs=[pl.BlockSpec((G, bq, hd), lambda h, i: (h, i, 0)), pl.BlockSpec((T, hd), lambda h, i: (0, h)),
                  pl.BlockSpec((T, 2 * hd), lambda h, i: (0, h))],
        out_specs=[pl.BlockSpec((G, bq, hd), lambda h, i: (h, i, 0)),
                   pl.BlockSpec((G, bq, 1), lambda h, i: (h, i, 0))],
        out_shape=[jax.ShapeDtypeStruct((NH, T, hd), F32), jax.ShapeDtypeStruct((NH, T, 1), F32)],
        scratch_shapes=[pltpu.VMEM((G, bq, 1), F32), pltpu.VMEM((G, bq, 2 * hd), F32),
                        pltpu.VMEM((G, bq, bk), F32), pltpu.VMEM((G, bq, bk), F32)],
        compiler_params=_params(2))(q, kr, vext)


def flash_bwd(q, kr, proj, v_off, do, lse, delta, into, *, name):
    NH, T, hd = q.shape
    KV = ATTN_KV_HEADS
    G = NH // KV
    bq = _tile(T, FLASH_BQ)
    bk = _tile(T, FLASH_BK)
    nk = T // bk
    nq = T // bq
    vb = v_off // hd

    def body(q_ref, k_ref, v_ref, do_ref, lse_ref, dl_ref, into_ref, dq_ref, dk_ref, dv_ref, dq_scr, dk_scr, dv_scr,
             sa_scr, pa_scr, sb_scr, pb_scr):
        i = pl.program_id(1)

        @pl.when(i == 0)
        def _():
            dk_scr[...] = jnp.zeros_like(dk_scr)
            dv_scr[...] = jnp.zeros_like(dv_scr)

        dq_scr[...] = jnp.zeros_like(dq_scr)

        def scores(c, s_scr, dp_scr):
            off = pl.multiple_of(c * bk, bk)
            kc = k_ref[pl.ds(off, bk), :]
            vc = v_ref[pl.ds(off, bk), :]
            for g in range(G):
                s_scr[g] = _dot_nt(q_ref[g], kc)
                dp_scr[g] = _dot_nt(do_ref[g], vc)

        def consume(c, s_scr, dp_scr):
            off = pl.multiple_of(c * bk, bk)
            kc = k_ref[pl.ds(off, bk), :]
            dk_c = jnp.zeros((bk, hd), F32)
            dv_c = jnp.zeros((bk, hd), F32)
            for g in range(G):
                p = jnp.exp2(s_scr[g] - lse_ref[g])
                ds = (p * (dp_scr[g] - dl_ref[g])).astype(BF16)
                dq_scr[g] += _dot(ds, kc)
                dv_c = dv_c + _dot_tn(p.astype(BF16), do_ref[g])
                dk_c = dk_c + _dot_tn(ds, q_ref[g])
            dk_scr[pl.ds(off, bk), :] += dk_c
            dv_scr[pl.ds(off, bk), :] += dv_c

        scores(0, sa_scr, pa_scr)
        if nk % 2 == 0:
            def pair(t, carry):
                scores(2 * t + 1, sb_scr, pb_scr)
                consume(2 * t, sa_scr, pa_scr)
                scores(2 * t + 2, sa_scr, pa_scr)
                consume(2 * t + 1, sb_scr, pb_scr)
                return carry

            lax.fori_loop(0, nk // 2 - 1, pair, 0)
            scores(nk - 1, sb_scr, pb_scr)
            consume(nk - 2, sa_scr, pa_scr)
            consume(nk - 1, sb_scr, pb_scr)
        else:
            def single(c, carry):
                consume(c, sa_scr, pa_scr)
                scores(c + 1, sa_scr, pa_scr)
                return carry

            lax.fori_loop(0, nk - 1, single, 0)
            consume(nk - 1, sa_scr, pa_scr)
        dq_ref[...] = dq_scr[...].astype(BF16)

        @pl.when(i == nq - 1)
        def _():
            dk_ref[...] = dk_scr[...].astype(BF16)
            dv_ref[...] = (dv_scr[...] * LOG2E).astype(BF16)

    qspec = pl.BlockSpec((G, bq, hd), lambda h, i: (h, i, 0))
    cspec = pl.BlockSpec((G, bq, 1), lambda h, i: (h, i, 0))
    kspec = pl.BlockSpec((T, hd), lambda h, i: (0, h))
    vspec = pl.BlockSpec((T, hd), lambda h, i: (0, vb + h))
    return pl.pallas_call(
        body, name=name, grid=(KV, nq),
        in_specs=[qspec, kspec, vspec, qspec, cspec, cspec, pl.BlockSpec(memory_space=pl.ANY)],
        out_specs=[qspec, kspec, vspec],
        out_shape=[jax.ShapeDtypeStruct((NH, T, hd), BF16), jax.ShapeDtypeStruct((T, KV * hd), BF16),
                   jax.ShapeDtypeStruct(into.shape, BF16)],
        input_output_aliases={6: 2},
        scratch_shapes=[pltpu.VMEM((G, bq, hd), F32), pltpu.VMEM((T, hd), F32), pltpu.VMEM((T, hd), F32)]
        + [pltpu.VMEM((G, bq, bk), F32)] * 4,
        compiler_params=_params(2))(q, kr, proj, do, lse, delta, into)


def attn_gate(o, proj, g_off, *, name):
    NH, T, hd = o.shape
    W = NH * hd
    cw = _col_tile(W, g_off)
    hc = cw // hd
    gb = g_off // cw
    tr = _tile(T, 256)

    def body(o_ref, g_ref, y_ref, yt_ref):
        for h in range(hc):
            hs = slice(h * hd, (h + 1) * hd)
            y = (o_ref[h] * _silu(g_ref[:, hs].astype(F32))).astype(BF16)
            y_ref[:, hs] = y
            yt_ref[hs, :] = y.T

    return pl.pallas_call(
        body, name=name, grid=(W // cw, T // tr),
        in_specs=[pl.BlockSpec((hc, tr, hd), lambda j, i: (j, i, 0)), pl.BlockSpec((tr, cw), lambda j, i: (i, gb + j))],
        out_specs=[pl.BlockSpec((tr, cw), lambda j, i: (i, j)), pl.BlockSpec((cw, tr), lambda j, i: (j, i))],
        out_shape=[jax.ShapeDtypeStruct((T, W), BF16), jax.ShapeDtypeStruct((W, T), BF16)],
        compiler_params=_params(2))(o, proj)


def attn_gate_bwd(dy2, o, proj, g_off, *, name):
    NH, T, hd = o.shape
    W = NH * hd
    cw = _col_tile(W, g_off)
    hc = cw // hd
    gb = g_off // cw
    tr = _tile(T, 256)

    def body(d_ref, o_ref, g_ref, do_ref, dg_ref, dl_ref):
        for h in range(hc):
            hs = slice(h * hd, (h + 1) * hd)
            d = d_ref[:, hs].astype(F32)
            gv = g_ref[:, hs].astype(F32)
            ov = o_ref[h]
            dov = d * _silu(gv) * LN2
            do_ref[h] = dov.astype(BF16)
            dg_ref[:, hs] = (d * ov * _dsilu(gv)).astype(BF16)
            dl_ref[h] = jnp.sum(dov * ov, axis=-1, keepdims=True)

    return pl.pallas_call(
        body, name=name, grid=(W // cw, T // tr),
        in_specs=[pl.BlockSpec((tr, cw), lambda j, i: (i, j)), pl.BlockSpec((hc, tr, hd), lambda j, i: (j, i, 0)),
                  pl.BlockSpec((tr, cw), lambda j, i: (i, gb + j))],
        out_specs=[pl.BlockSpec((hc, tr, hd), lambda j, i: (j, i, 0)), pl.BlockSpec((tr, cw), lambda j, i: (i, gb + j)),
                   pl.BlockSpec((hc, tr, 1), lambda j, i: (j, i, 0))],
        out_shape=[jax.ShapeDtypeStruct((NH, T, hd), BF16), jax.ShapeDtypeStruct(proj.shape, BF16),
                   jax.ShapeDtypeStruct((NH, T, 1), F32)], compiler_params=_params(2))(dy2, o, proj)


def outer_silu(c_t, dm, *, name):
    K, B = c_t.shape
    N = dm.shape[1]
    tk = _tile(K, 256)

    def body(c_ref, d_ref, o_ref):
        s = _silu(c_ref[...])
        acc = jnp.zeros((tk, N), F32)
        for b in range(B):
            acc = acc + s[:, b:b + 1] * d_ref[b:b + 1, :]
        o_ref[...] = acc

    return pl.pallas_call(
        body, name=name, grid=(K // tk,),
        in_specs=[pl.BlockSpec((tk, B), lambda i: (i, 0)), pl.BlockSpec((B, N), lambda i: (0, 0))],
        out_specs=pl.BlockSpec((tk, N), lambda i: (i, 0)),
        out_shape=jax.ShapeDtypeStruct((K, N), F32), compiler_params=_params(1))(c_t, dm)


def add_core_rows(core, a, b, *, name):
    S, R, C = b.shape
    tr = _tile(R, 2048)

    def body(core_ref, a_ref, b_ref, o_ref):
        o_ref[...] = (a_ref[...].astype(F32) + b_ref[...].astype(F32)).astype(BF16)

    blk = pl.BlockSpec((None, tr, C), lambda q, i, core_ref: (q, i, 0))
    grid_spec = pltpu.PrefetchScalarGridSpec(
        num_scalar_prefetch=1, grid=(S, R // tr),
        in_specs=[pl.BlockSpec((None, tr, C), lambda q, i, core_ref: (2 * q + core_ref[0], i, 0)), blk],
        out_specs=blk)
    return pl.pallas_call(
        body, name=name, grid_spec=grid_spec, out_shape=jax.ShapeDtypeStruct((S, R, C), BF16),
        compiler_params=_params(2))(core, a, b)


def sum_slots(x, *, name):
    S, R, C = x.shape
    tr = _tile(R, 512)

    def body(x_ref, o_ref):
        acc = x_ref[0].astype(F32)
        for s in range(1, S):
            acc = acc + x_ref[s].astype(F32)
        o_ref[...] = acc

    return pl.pallas_call(
        body, name=name, grid=(R // tr,), in_specs=[pl.BlockSpec((S, tr, C), lambda i: (0, i, 0))],
        out_specs=pl.BlockSpec((tr, C), lambda i: (i, 0)),
        out_shape=jax.ShapeDtypeStruct((R, C), F32), compiler_params=_params(1))(x)


def adamw(w, g, m, v, *, name):
    R, C = w.shape
    tr = _tile(R, 512) if R % 8 == 0 else R

    def body(w_ref, g_ref, m_ref, v_ref, d_ref, nm_ref, nv_ref):
        gv = g_ref[...]
        mn = ADAM_B1 * m_ref[...] + (1.0 - ADAM_B1) * gv
        vn = ADAM_B2 * v_ref[...] + (1.0 - ADAM_B2) * jnp.square(gv)
        m_hat = mn / (1.0 - ADAM_B1 ** ADAM_STEP)
        v_hat = vn / (1.0 - ADAM_B2 ** ADAM_STEP)
        d_ref[...] = -ADAM_LR * (m_hat / (jnp.sqrt(v_hat) + ADAM_EPS) + ADAM_WD * w_ref[...])
        nm_ref[...] = mn
        nv_ref[...] = vn

    blk = pl.BlockSpec((tr, C), lambda i: (i, 0))
    sh = jax.ShapeDtypeStruct((R, C), F32)
    return pl.pallas_call(
        body, name=name, grid=(R // tr,), in_specs=[blk] * 4, out_specs=[blk] * 3, out_shape=[sh] * 3,
        compiler_params=_params(1))(w, g, m, v)


def _adamw_nd(w, g, m, v, name):
    shp = w.shape
    if w.ndim == 1:
        two = (1, shp[0])
    else:
        two = (math.prod(shp[:-1]), shp[-1])
    d, nm, nv = adamw(w.reshape(two), g.reshape(two), m.reshape(two), v.reshape(two), name=name)
    return d.reshape(shp), nm.reshape(shp), nv.reshape(shp)


IN_DX_TILES = dict(tm=1024, tn=512, tk=8192)


def _pad_cols(w, n):
    return jnp.pad(w, ((0, 0), (0, n - w.shape[1])))


def _pad_rows(w, n):
    return jnp.pad(w, ((0, n - w.shape[0]), (0, 0)))


def kernel(x, c, w_mod, b_mod, pool_w_in, pool_w_grp, pool_scale, pool_w_out, gla_w_in, gla_fwd_w1, gla_fwd_w2, gla_fwd_b, gla_bwd_w1, gla_bwd_w2, gla_bwd_b, gla_norm_g, gla_w_out, attn_w_in, attn_q_norm_g, attn_k_norm_g, attn_w_out, final_norm_g, loss_target, m_w_mod, m_b_mod, m_pool_w_in, m_pool_w_grp, m_pool_scale, m_pool_w_out, m_gla_w_in, m_gla_fwd_w1, m_gla_fwd_w2, m_gla_fwd_b, m_gla_bwd_w1, m_gla_bwd_w2, m_gla_bwd_b, m_gla_norm_g, m_gla_w_out, m_attn_w_in, m_attn_q_norm_g, m_attn_k_norm_g, m_attn_w_out, m_final_norm_g, v_w_mod, v_b_mod, v_pool_w_in, v_pool_w_grp, v_pool_scale, v_pool_w_out, v_gla_w_in, v_gla_fwd_w1, v_gla_fwd_w2, v_gla_fwd_b, v_gla_bwd_w1, v_gla_bwd_w2, v_gla_bwd_b, v_gla_norm_g, v_gla_w_out, v_attn_w_in, v_attn_q_norm_g, v_attn_k_norm_g, v_attn_w_out, v_final_norm_g):
    W = dict(w_mod=w_mod, b_mod=b_mod, pool_w_in=pool_w_in, pool_w_grp=pool_w_grp, pool_scale=pool_scale,
             pool_w_out=pool_w_out, gla_w_in=gla_w_in, gla_fwd_w1=gla_fwd_w1, gla_fwd_w2=gla_fwd_w2,
             gla_fwd_b=gla_fwd_b, gla_bwd_w1=gla_bwd_w1, gla_bwd_w2=gla_bwd_w2, gla_bwd_b=gla_bwd_b,
             gla_norm_g=gla_norm_g, gla_w_out=gla_w_out, attn_w_in=attn_w_in, attn_q_norm_g=attn_q_norm_g,
             attn_k_norm_g=attn_k_norm_g, attn_w_out=attn_w_out, final_norm_g=final_norm_g)
    M = dict(w_mod=m_w_mod, b_mod=m_b_mod, pool_w_in=m_pool_w_in, pool_w_grp=m_pool_w_grp, pool_scale=m_pool_scale,
             pool_w_out=m_pool_w_out, gla_w_in=m_gla_w_in, gla_fwd_w1=m_gla_fwd_w1, gla_fwd_w2=m_gla_fwd_w2,
             gla_fwd_b=m_gla_fwd_b, gla_bwd_w1=m_gla_bwd_w1, gla_bwd_w2=m_gla_bwd_w2, gla_bwd_b=m_gla_bwd_b,
             gla_norm_g=m_gla_norm_g, gla_w_out=m_gla_w_out, attn_w_in=m_attn_w_in, attn_q_norm_g=m_attn_q_norm_g,
             attn_k_norm_g=m_attn_k_norm_g, attn_w_out=m_attn_w_out, final_norm_g=m_final_norm_g)
    V = dict(w_mod=v_w_mod, b_mod=v_b_mod, pool_w_in=v_pool_w_in, pool_w_grp=v_pool_w_grp, pool_scale=v_pool_scale,
             pool_w_out=v_pool_w_out, gla_w_in=v_gla_w_in, gla_fwd_w1=v_gla_fwd_w1, gla_fwd_w2=v_gla_fwd_w2,
             gla_fwd_b=v_gla_fwd_b, gla_bwd_w1=v_gla_bwd_w1, gla_bwd_w2=v_gla_bwd_w2, gla_bwd_b=v_gla_bwd_b,
             gla_norm_g=v_gla_norm_g, gla_w_out=v_gla_w_out, attn_w_in=v_attn_w_in, attn_q_norm_g=v_attn_q_norm_g,
             attn_k_norm_g=v_attn_k_norm_g, attn_w_out=v_attn_w_out, final_norm_g=v_final_norm_g)

    me = _my_id()
    T, D = x.shape[1], x.shape[2]
    x0 = x.reshape(T, D)
    target = loss_target.reshape(T, D)
    BW = D
    n_pool, n_gla, n_attn = pool_w_in.shape[0], gla_w_in.shape[0], attn_w_in.shape[0]
    LR = GLA_LOWRANK
    LRP = LANES
    KWg = gla_fwd_w2.shape[2] * NDEV
    Dm = w_mod.shape[2]

    small_shard_names = ['pool_scale', 'gla_fwd_w1', 'gla_fwd_w2', 'gla_bwd_w1', 'gla_bwd_w2']
    small_items = [c] + [W[n] for n in small_shard_names]
    small_shapes = [a.shape for a in small_items]
    g1 = exchange(GATHER, _pack(small_items, F32), name="gather_small")
    c_all_, ps_all, fw1_all, fw2_all, bw1_all, bw2_all = _unpack(g1, small_shapes, lead=NDEV)
    c_all = c_all_.reshape(NDEV, D)
    pool_scale_full = jnp.transpose(ps_all, (1, 0, 2)).reshape(n_pool, BW)
    w1_full = {'f': fw1_all.transpose(1, 0, 2, 3).reshape(n_gla, D, LR),
               'b': bw1_all.transpose(1, 0, 2, 3).reshape(n_gla, D, LR)}
    w2_full = {'f': fw2_all.transpose(1, 2, 0, 3).reshape(n_gla, LR, KWg),
               'b': bw2_all.transpose(1, 2, 0, 3).reshape(n_gla, LR, KWg)}

    c16 = _pad_rows(c_all, 16)
    b_slab = lax.dynamic_slice_in_dim(b_mod, me * Dm, Dm, axis=1)
    mod_parts = [mm(c16, w_mod[i], name=f"mod_fwd{i}", out_dtype=F32, a_silu=True, tm=16, tn=Dm,
                    bias=b_slab[i:i + 1])[:NDEV] for i in range(DEPTH)]
    mod_slab = jnp.stack(mod_parts)
    g2 = exchange(GATHER, _pack([mod_slab], F32), name="gather_mod")
    (mod_all,) = _unpack(g2, [mod_slab.shape], lead=NDEV)
    mod_mine = lax.dynamic_index_in_dim(mod_all, me, axis=2, keepdims=False)
    mod_mine = mod_mine.transpose(1, 0, 2).reshape(DEPTH, NDEV * Dm)
    shift = [mod_mine[i:i + 1, 0:D] for i in range(DEPTH)]
    scale = [mod_mine[i:i + 1, D:2 * D] for i in range(DEPTH)]
    gate = [mod_mine[i:i + 1, 2 * D:3 * D] for i in range(DEPTH)]

    layer_names = {0: ['pool_w_in', 'pool_w_grp', 'pool_w_out'], 1: ['gla_w_in', 'gla_w_out'],
                   2: ['attn_w_in', 'attn_w_out']}

    def layer_shards(i):
        return [W[n][i // N_MIXERS] for n in layer_names[i % N_MIXERS]]

    def layer_weights(i, gathered):
        parts = _unpack(gathered, [a.shape for a in layer_shards(i)], lead=NDEV)
        w_out = parts[-1].reshape(BW, D)
        out = dict(w_in=parts[0], w_out=w_out, w_in_t=parts[0].transpose(0, 2, 1).reshape(-1, D), w_out_t=w_out.T)
        if i % N_MIXERS == 0:
            g = parts[1]
            out['w_grp'] = g.transpose(1, 0, 2, 3).reshape(g.shape[1], -1, g.shape[3])
        return out

    gathered = exchange(GATHER_VIA_SIBLING, _pack(layer_shards(0), BF16), name="gather_w0")

    cos, sin = _rope_tables(T)

    xs = [x0]
    saved = []
    xi = x0
    for i in range(DEPTH):
        kind, j = i % N_MIXERS, i // N_MIXERS
        h, h_t = prenorm(xi, scale[i], shift[i], name=f"prenorm{i}")
        lw = layer_weights(i, gathered)
        sv = dict(h=h, h_t=h_t, lw=lw)
        in_name = ("pool_in", "gla_in", "attn_in")[kind] + str(i)
        if i + 1 < DEPTH:
            proj, gathered = mm(h, lw['w_in'], b_split=True, name=in_name,
                                side=(GATHER_VIA_SIBLING, _pack(layer_shards(i + 1), BF16)))
        else:
            proj = mm(h, lw['w_in'], b_split=True, name=in_name)
        if kind == 0:
            ug = proj
            pooled = band(ug, BW, transpose=False, name=f"pool_band{i}")
            zz = gmm_nn(pooled, lw['w_grp'], name=f"pool_grp{i}")
            y2, y2_t = gated(zz, ug, BW, pool_scale_full[j:j + 1], name=f"pool_gate{i}")
            sv.update(ug=ug, pooled=pooled, z=zz)
        elif kind == 1:
            sv.update(proj=proj)
            for dname in ('f', 'b'):
                w1p = _pad_cols(w1_full[dname][j], LRP)
                w2p = _pad_rows(w2_full[dname][j], LRP)
                bias = (gla_fwd_b if dname == 'f' else gla_bwd_b)[j:j + 1]
                hw1 = mm(h, w1p, name=f"gla_w1{dname}{i}", tn=LRP)
                zg = mm(hw1, w2p, name=f"gla_w2{dname}{i}", out_dtype=F32, tk=LRP)
                o_d, st_d = gla_fwd(proj, zg, bias, reverse=(dname == 'b'), name=f"gla_scan_{dname}{i}")
                sv.update({f"hw1{dname}": hw1, f"z{dname}": zg, f"o{dname}": o_d, f"st{dname}": st_d,
                           f"w1p{dname}": w1p, f"w2p{dname}": w2p, f"bias{dname}": bias})
            y2, y2_t = gla_out(sv['of'], sv['ob'], proj, 2 * BW, gla_norm_g[j:j + 1], name=f"gla_out{i}")
        else:
            QW, KVW, _, _ = _attn_dims(proj)
            qr, kr, vext = attn_prep(proj, attn_q_norm_g[j:j + 1], attn_k_norm_g[j:j + 1], cos, sin,
                                     name=f"attn_prep{i}")
            o, lse = flash_fwd(qr, kr, vext, name=f"attn_flash{i}")
            y2, y2_t = attn_gate(o, proj, QW + 2 * KVW, name=f"attn_gate{i}")
            sv.update(proj=proj, qr=qr, kr=kr, o=o, lse=lse)
        sv.update(y2_t=y2_t)
        saved.append(sv)
        xi = mm(y2, lw['w_out'], name=f"out_proj{i}", col_scale=gate[i], add=xi, out_dtype=F32, tm=1024)
        xs.append(xi)

    loss_part, dx, d_final_g = loss_head(xi, final_norm_g.reshape(1, D), target, name="loss_head")

    d_mod = [None] * DEPTH
    layer_grads = [None] * DEPTH
    received = [None] * DEPTH
    small_grads = {}

    def in_dx(i, dproj, w_in, **kw):
        name = ("pool_in_dx", "gla_in_dx", "attn_in_dx")[i % N_MIXERS] + str(i)
        if i + 1 < DEPTH:
            dh, received[i + 1] = mm(dproj, w_in, name=name, **IN_DX_TILES, side=(CHIP_ROWS, chip_sums[i + 1]), **kw)
            return dh
        return mm(dproj, w_in, name=name, **IN_DX_TILES, **kw)

    def chip_sum(i, packed, from_sibling):
        core = lax.axis_index("c").astype(jnp.int32).reshape(1)
        return add_core_rows(core, packed, from_sibling, name=f"chip_sum_g{i}")

    chip_sums = [None] * DEPTH
    for i in reversed(range(DEPTH)):
        kind, j = i % N_MIXERS, i // N_MIXERS
        sv = saved[i]
        h_t, lw = sv['h_t'], sv['lw']
        if i + 1 < DEPTH:
            packed = _pack(layer_grads[i + 1], BF16, lead=NDEV)
            dy2, from_sibling = mm(dx, lw['w_out_t'], name=f"out_proj_dx{i}", a_scale=gate[i], tm=1024,
                                   side=(SIBLING_ROWS, packed))
            chip_sums[i + 1] = chip_sum(i + 1, packed, from_sibling)
        else:
            dy2 = mm(dx, lw['w_out_t'], name=f"out_proj_dx{i}", a_scale=gate[i], tm=1024)
        g_raw = mm(sv['y2_t'], dx, name=f"out_proj_dw{i}", out_dtype=F32, tk=1024)
        g_wout, d_gate = out_proj_grads(g_raw, lw['w_out'], gate[i], name=f"out_proj_dgate{i}")
        g_wout = g_wout.reshape(NDEV, BW // NDEV, D)
        if kind == 0:
            ug = sv['ug']
            dz, dproj, d_ps = gated_bwd(dy2, sv['z'], ug, BW, pool_scale_full[j:j + 1], name=f"pool_gate_bwd{i}")
            dpooled = gmm_nn(dz, lw['w_grp'], tb=True, name=f"pool_grp_dx{i}")
            g_grp = gmm_tn(sv['pooled'], dz, len(POOL_WINDOWS), name=f"pool_grp_dw{i}")
            dproj = band(dpooled, BW, transpose=True, name=f"pool_band_bwd{i}", into=dproj)
            dh = in_dx(i, dproj, lw['w_in_t'])
            g_win = mm(h_t, dproj, out_split=pool_w_in.shape[2], name=f"pool_in_dw{i}")
            Gp, Cg = g_grp.shape[0], g_grp.shape[1]
            g_grp = g_grp.astype(BF16).reshape(Gp, NDEV, Cg // NDEV, Cg).transpose(1, 0, 2, 3)
            layer_grads[i] = [g_win, g_grp, g_wout]
            small_grads.setdefault('pool_scale', [None] * n_pool)[j] = d_ps
        elif kind == 1:
            proj = sv['proj']
            do, dproj, d_ng = gla_out_bwd(dy2, sv['of'], sv['ob'], proj, 2 * BW, gla_norm_g[j:j + 1],
                                          name=f"gla_out_bwd{i}")
            prev = None
            dh_acc = None
            for dname in ('f', 'b'):
                scan_args = (proj, sv[f"z{dname}"], sv[f"bias{dname}"], sv[f"st{dname}"], do, prev)
                if dname == 'f':
                    *prev, dzg, dbias = gla_bwd(*scan_args, reverse=False, name=f"gla_scan_bwd_f{i}")
                else:
                    dproj, dzg, dbias = gla_bwd(*scan_args, reverse=True, name=f"gla_scan_bwd_b{i}", into=dproj)
                dhw1 = mm(dzg, sv[f"w2p{dname}"], tb=True, name=f"gla_w2{dname}_dx{i}", tn=LRP)
                g_w2 = mm(sv[f"hw1{dname}"], dzg, ta=True, out_dtype=F32, name=f"gla_w2{dname}_dw{i}", tm=LRP)
                g_w1 = mm(h_t, dhw1, out_dtype=F32, name=f"gla_w1{dname}_dw{i}", tn=LRP)
                dh_acc = mm(dhw1, sv[f"w1p{dname}"], tb=True, add=dh_acc, name=f"gla_w1{dname}_dx{i}", tk=LRP)
                key = 'gla_fwd' if dname == 'f' else 'gla_bwd'
                small_grads[key + '_w1'] = g_w1[:, :LR]
                small_grads[key + '_w2'] = g_w2[:LR]
                small_grads[key + '_b'] = dbias
            dh = in_dx(i, dproj, lw['w_in_t'], add=dh_acc)
            layer_grads[i] = [mm(h_t, dproj, out_split=gla_w_in.shape[2], name=f"gla_in_dw{i}"), g_wout]
            small_grads['gla_norm_g'] = d_ng
        else:
            proj = sv['proj']
            QW, KVW, _, _ = _attn_dims(proj)
            do, dproj, delta = attn_gate_bwd(dy2, sv['o'], proj, QW + 2 * KVW, name=f"attn_gate_bwd{i}")
            dqr, dkr, dproj = flash_bwd(sv['qr'], sv['kr'], proj, QW + KVW, do, sv['lse'], delta, dproj,
                                        name=f"attn_flash_bwd{i}")
            dproj, d_qg, d_kg = attn_prep_bwd(dqr, dkr, proj, attn_q_norm_g[j:j + 1], attn_k_norm_g[j:j + 1],
                                              cos, sin, dproj, name=f"attn_prep_bwd{i}")
            dh = in_dx(i, dproj, lw['w_in_t'])
            layer_grads[i] = [mm(h_t, dproj, out_split=attn_w_in.shape[2], name=f"attn_in_dw{i}"), g_wout]
            small_grads['attn_q_norm_g'] = d_qg
            small_grads['attn_k_norm_g'] = d_kg
        dx, d_scale, d_shift = prenorm_bwd(xs[i], dh, dx, scale[i], name=f"prenorm_bwd{i}")
        d_mod[i] = jnp.concatenate([d_shift, d_scale, d_gate], axis=1)
    grad_x = dx.reshape(1, T, D)

    packed = _pack(layer_grads[0], BF16, lead=NDEV)
    chip_sums[0] = chip_sum(0, packed, exchange(SIBLING_ROWS, packed, name="exchange_g0_sibling"))
    received[0] = exchange(CHIP_ROWS, chip_sums[0], name="exchange_g0_chips")
    per_name = {}
    for i in range(DEPTH):
        summed = sum_slots(received[i], name=f"sum_g{i}")
        parts = _unpack(summed, [a.shape for a in layer_shards(i)])
        for n, g in zip(layer_names[i % N_MIXERS], parts):
            per_name.setdefault(n, []).append(g)
    big_g = {n: jnp.stack(gs) for n, gs in per_name.items()}

    small_order = ['b_mod', 'gla_fwd_b', 'gla_bwd_b', 'gla_norm_g', 'attn_q_norm_g', 'attn_k_norm_g', 'final_norm_g',
                   'pool_scale', 'gla_fwd_w1', 'gla_fwd_w2', 'gla_bwd_w1', 'gla_bwd_w2']
    small_grads['b_mod'] = jnp.concatenate(d_mod, axis=0)
    small_grads['final_norm_g'] = d_final_g
    small_grads['pool_scale'] = jnp.concatenate(small_grads['pool_scale'], axis=0)
    part_items = [jnp.pad(loss_part.reshape(1), (0, LANES - 1))] + [small_grads[n] for n in small_order]
    part_shapes = [a.shape for a in part_items]
    g4 = exchange(GATHER, _pack(part_items, F32), name="gather_parts")
    tot = dict(zip(['loss'] + small_order, _unpack(sum_slots(g4, name="sum_parts"), part_shapes)))
    loss = tot['loss'][0]
    d_mod_all = _unpack(g4, part_shapes, lead=NDEV)[1]

    grads = {}
    grads.update(big_g)
    grads['b_mod'] = tot['b_mod']
    for n in ('gla_fwd_b', 'gla_bwd_b', 'gla_norm_g', 'attn_q_norm_g', 'attn_k_norm_g'):
        grads[n] = tot[n].reshape(W[n].shape)
    grads['final_norm_g'] = tot['final_norm_g'].reshape(D)
    ps_n = pool_scale.shape[1]
    grads['pool_scale'] = lax.dynamic_slice_in_dim(tot['pool_scale'], me * ps_n, ps_n, axis=1)
    rows = gla_fwd_w1.shape[1]
    cols = gla_fwd_w2.shape[2]
    for key in ('gla_fwd', 'gla_bwd'):
        grads[key + '_w1'] = lax.dynamic_slice_in_dim(tot[key + '_w1'], me * rows, rows, axis=0).reshape(1, rows, LR)
        grads[key + '_w2'] = lax.dynamic_slice_in_dim(tot[key + '_w2'], me * cols, cols, axis=1).reshape(1, LR, cols)

    c_t = c_all.T
    dm_slab = lax.dynamic_slice_in_dim(d_mod_all, me * Dm, Dm, axis=2)
    grads['w_mod'] = jnp.stack([outer_silu(c_t, dm_slab[:, i], name=f"mod_dw{i}") for i in range(DEPTH)])

    deltas, new_m, new_v = {}, {}, {}
    for n in WEIGHTS:
        deltas[n], new_m[n], new_v[n] = _adamw_nd(W[n], grads[n], M[n], V[n], name=f"adamw_{n}")

    return (loss, grad_x, *[grads[n] for n in WEIGHTS], *[deltas[n] for n in WEIGHTS],
            *[new_m[n] for n in WEIGHTS], *[new_v[n] for n in WEIGHTS])
```

```python
import functools
import math

import jax
import jax.numpy as jnp
from jax import lax
from jax.experimental import pallas as pl
from jax.experimental.pallas import tpu as pltpu

F32 = jnp.float32
BF16 = jnp.bfloat16
NDEV = 8
LANES = 128
VMEM_LIMIT = 56 * 1024 * 1024

D_MODEL = 2048
DEPTH = 4
N_MIXERS = 3
GRID_W = 64
NORM_EPS = 1e-6
POOL_WINDOWS = (2, 4, 8, 16)
GLA_HEADS = 4
GLA_LOWRANK = 16
GLA_TAU = 16.0
GLA_CHUNK = 128
ATTN_HEAD_DIM = 128
ATTN_KV_HEADS = 4
ROPE_THETA = 10000.0
ADAM_LR = 0.001
ADAM_B1 = 0.9
ADAM_B2 = 0.999
ADAM_EPS = 1e-08
ADAM_WD = 0.01
ADAM_STEP = 10

WEIGHTS = ['w_mod', 'b_mod', 'pool_w_in', 'pool_w_grp', 'pool_scale', 'pool_w_out', 'gla_w_in', 'gla_fwd_w1',
           'gla_fwd_w2', 'gla_fwd_b', 'gla_bwd_w1', 'gla_bwd_w2', 'gla_bwd_b', 'gla_norm_g', 'gla_w_out',
           'attn_w_in', 'attn_q_norm_g', 'attn_k_norm_g', 'attn_w_out', 'final_norm_g']


def _params(n_axes=0):
    sem = ("arbitrary",) * n_axes if n_axes else None
    return pltpu.CompilerParams(dimension_semantics=sem, vmem_limit_bytes=VMEM_LIMIT)


def _silu(g):
    return g * jax.nn.sigmoid(g)


def _dsilu(g):
    s = jax.nn.sigmoid(g)
    return s * (1.0 + g * (1.0 - s))


def _dot(a, b):
    return jnp.dot(a, b, preferred_element_type=F32)


def _dot_nt(a, b):
    return lax.dot_general(a, b, (((1,), (1,)), ((), ())), preferred_element_type=F32)


def _dot_tn(a, b):
    return lax.dot_general(a, b, (((0,), (0,)), ((), ())), preferred_element_type=F32)


def _tile(n, pref):
    if n <= pref:
        return n
    for step in (128, 16, 8):
        t = pref - pref % step
        while t >= step:
            if n % t == 0:
                return t
            t -= step
    raise ValueError((n, pref))


def _peer(k):
    x, y, c = lax.axis_index("x"), lax.axis_index("y"), lax.axis_index("c")
    px = 1 - x if k & 4 else x
    py = 1 - y if k & 2 else y
    pc = 1 - c if k & 1 else c
    return (px, py, pc), 4 * px + 2 * py + pc


def _my_id():
    return 4 * lax.axis_index("x") + 2 * lax.axis_index("y") + lax.axis_index("c")


NCHIP = NDEV // 2
GATHER = "gather"
GATHER_VIA_SIBLING = "g2"
SIBLING_ROWS = "d2d"
CHIP_ROWS = "ici"
ICI_FLIPS = (4, 2, 6)

EXCHANGE_SCRATCH = [pltpu.SemaphoreType.DMA((NDEV - 1,)), pltpu.SemaphoreType.DMA((NDEV - 1,)),
                    pltpu.SemaphoreType.DMA]


def _exchange_slots(kind):
    return NCHIP if kind in (SIBLING_ROWS, CHIP_ROWS) else NDEV


def _exchange_plan(kind, x_ref, o_ref):
    me = _my_id()
    c = lax.axis_index("c")
    chip = me // 2
    sib, sib_id = _peer(1)
    if kind == GATHER:
        sends = [(x_ref, o_ref.at[me], _peer(k)[0], o_ref.at[_peer(k)[1]]) for k in range(1, NDEV)]
        return (x_ref, o_ref.at[me]), sends
    if kind == GATHER_VIA_SIBLING:
        sends = [(x_ref, o_ref.at[me], sib, o_ref.at[sib_id])]
        sends += [(x_ref, o_ref.at[me], _peer(k)[0], o_ref.at[_peer(k)[1]]) for k in ICI_FLIPS]
        sends += [(o_ref.at[_peer(k)[1]], o_ref.at[_peer(k)[1]], sib, o_ref.at[_peer(k ^ 1)[1]]) for k in ICI_FLIPS]
        return (x_ref, o_ref.at[me]), sends
    if kind == SIBLING_ROWS:
        return None, [(x_ref.at[2 * q + (1 - c)], o_ref.at[q], sib, o_ref.at[q]) for q in range(NCHIP)]
    assert kind == CHIP_ROWS, kind
    sends = [(x_ref.at[chip ^ (k >> 1)], o_ref.at[chip], _peer(k)[0], o_ref.at[chip ^ (k >> 1)]) for k in ICI_FLIPS]
    return (x_ref.at[chip], o_ref.at[chip]), sends


def _exchange_copy(n, send, send_sems, recv_sems, incoming):
    src, dst, peer, lands = send
    return pltpu.make_async_remote_copy(
        src_ref=src, dst_ref=lands if incoming else dst, send_sem=send_sems.at[n], recv_sem=recv_sems.at[n],
        device_id=peer, device_id_type=pl.DeviceIdType.MESH)


def _exchange_start(kind, x_ref, o_ref, send_sems, recv_sems, local_sem):
    local, sends = _exchange_plan(kind, x_ref, o_ref)
    if local is not None:
        pltpu.make_async_copy(*local, local_sem).start()
    first = sends[:4] if kind == GATHER_VIA_SIBLING else sends
    for n, send in enumerate(first):
        _exchange_copy(n, send, send_sems, recv_sems, False).start()


def _exchange_wait(kind, x_ref, o_ref, send_sems, recv_sems, local_sem):
    local, sends = _exchange_plan(kind, x_ref, o_ref)
    order = list(range(len(sends)))
    if kind == GATHER_VIA_SIBLING:
        for j in range(3):
            _exchange_copy(1 + j, sends[1 + j], send_sems, recv_sems, True).wait_recv()
            _exchange_copy(4 + j, sends[4 + j], send_sems, recv_sems, False).start()
        order = [0, 4, 5, 6]
    for n in order:
        _exchange_copy(n, sends[n], send_sems, recv_sems, True).wait_recv()
    for n, send in enumerate(sends):
        _exchange_copy(n, send, send_sems, recv_sems, False).wait_send()
    if local is not None:
        pltpu.make_async_copy(*local, local_sem).wait()


def _exchange_out(kind, x):
    return jax.ShapeDtypeStruct((_exchange_slots(kind),) + tuple(x.shape[-2:]), x.dtype)


def exchange(kind, x, name):
    def body(x_ref, o_ref, send_sems, recv_sems, local_sem):
        _exchange_start(kind, x_ref, o_ref, send_sems, recv_sems, local_sem)
        _exchange_wait(kind, x_ref, o_ref, send_sems, recv_sems, local_sem)

    return pl.pallas_call(
        body, name=name, out_shape=_exchange_out(kind, x),
        in_specs=[pl.BlockSpec(memory_space=pl.ANY)],
        out_specs=pl.BlockSpec(memory_space=pl.ANY),
        scratch_shapes=EXCHANGE_SCRATCH,
    )(x)


def _pack(arrs, dtype, lead=None):
    unit = 16 * LANES
    if lead is None:
        flat = [a.astype(dtype).reshape(-1) for a in arrs]
        n = sum(f.shape[0] for f in flat)
        pad = (-n) % unit
        if pad:
            flat.append(jnp.zeros((pad,), dtype))
        return jnp.concatenate(flat).reshape(-1, LANES)
    flat = [a.astype(dtype).reshape(lead, -1) for a in arrs]
    n = sum(f.shape[1] for f in flat)
    pad = (-n) % unit
    if pad:
        flat.append(jnp.zeros((lead, pad), dtype))
    return jnp.concatenate(flat, axis=1).reshape(lead, -1, LANES)


def _unpack(buf, shapes, lead=None):
    out = []
    off = 0
    if lead is None:
        flat = buf.reshape(-1)
        for s in shapes:
            n = math.prod(s)
            out.append(flat[off:off + n].reshape(s))
            off += n
        return out
    flat = buf.reshape(lead, -1)
    for s in shapes:
        n = math.prod(s)
        out.append(flat[:, off:off + n].reshape((lead,) + tuple(s)))
        off += n
    return out


def mm(a, b, *, name, ta=False, tb=False, b_split=False, out_split=0, out_dtype=BF16, tm=2048, tn=1024, tk=None,
       bias=None, add=None, a_silu=False, a_scale=None, col_scale=None, side=None):
    K, M = a.shape if ta else a.shape[::-1]
    if b_split:
        S, d1, n = b.shape
        if tb:
            N, Kb = d1, S * n
        else:
            Kb, N = d1, S * n
    else:
        Kb, N = b.shape[::-1] if tb else b.shape
    assert Kb == K, (a.shape, b.shape, ta, tb, b_split)
    if tk is None:
        tk = 1024 if ta else 2048
    tm = _tile(M, tm)
    if b_split and not tb:
        tn = _tile(n, tn)
    elif out_split:
        tn = _tile(out_split, tn)
    else:
        tn = _tile(N, tn)
    if b_split and tb:
        tk = _tile(n, 1024)
    else:
        tk = _tile(K, tk)
    nk = K // tk

    a_spec = pl.BlockSpec((tk, tm), lambda i, j, k: (k, i)) if ta else pl.BlockSpec((tm, tk), lambda i, j, k: (i, k))
    if b_split and not tb:
        per = n // tn
        b_spec = pl.BlockSpec((None, tk, tn), lambda i, j, k: (j // per, k, j % per))
    elif b_split and tb:
        per = n // tk
        b_spec = pl.BlockSpec((None, tn, tk), lambda i, j, k: (k // per, j, k % per))
    elif tb:
        b_spec = pl.BlockSpec((tn, tk), lambda i, j, k: (j, k))
    else:
        b_spec = pl.BlockSpec((tk, tn), lambda i, j, k: (k, j))
    if out_split:
        per_o = out_split // tn
        o_spec = pl.BlockSpec((None, tm, tn), lambda i, j, k: (j // per_o, i, j % per_o))
        o_shape = jax.ShapeDtypeStruct((N // out_split, M, out_split), out_dtype)
    else:
        o_spec = pl.BlockSpec((tm, tn), lambda i, j, k: (i, j))
        o_shape = jax.ShapeDtypeStruct((M, N), out_dtype)
    ins = [a, b]
    in_specs = [a_spec, b_spec]
    if bias is not None:
        ins.append(bias)
        in_specs.append(pl.BlockSpec((1, tn), lambda i, j, k: (0, j)))
    if add is not None:
        ins.append(add)
        in_specs.append(pl.BlockSpec((tm, tn), lambda i, j, k: (i, j)))
    if a_scale is not None:
        assert not ta
        ins.append(a_scale)
        in_specs.append(pl.BlockSpec((1, tk), lambda i, j, k: (0, k)))
    if col_scale is not None:
        ins.append(col_scale)
        in_specs.append(pl.BlockSpec((1, tn), lambda i, j, k: (0, j)))
    has_bias, has_add, has_side = bias is not None, add is not None, side is not None
    has_ascale, has_cscale = a_scale is not None, col_scale is not None
    out_specs, out_shapes = [o_spec], [o_shape]
    scratch = [pltpu.VMEM((tm, tn) if nk > 1 else (8, LANES), F32)]
    if has_side:
        side_kind, side_x = side
        ins.append(side_x)
        in_specs.append(pl.BlockSpec(memory_space=pl.ANY))
        out_specs.append(pl.BlockSpec(memory_space=pl.ANY))
        out_shapes.append(_exchange_out(side_kind, side_x))
        scratch += EXCHANGE_SCRATCH
    gm, gn = M // tm, N // tn

    def body(*refs):
        a_ref, b_ref = refs[0], refs[1]
        pos = 2
        bias_ref = add_ref = None
        if has_bias:
            bias_ref = refs[pos]
            pos += 1
        if has_add:
            add_ref = refs[pos]
            pos += 1
        if has_ascale:
            ascale_ref = refs[pos]
            pos += 1
        if has_cscale:
            cscale_ref = refs[pos]
            pos += 1
        if has_side:
            side_refs = (side_kind, refs[pos], refs[pos + 2]) + tuple(refs[pos + 4:pos + 7])
            o_ref, acc_ref = refs[pos + 1], refs[pos + 3]
        else:
            o_ref, acc_ref = refs[pos], refs[pos + 1]
        k = pl.program_id(2)
        if has_side:
            i_, j_ = pl.program_id(0), pl.program_id(1)

            @pl.when((i_ == 0) & (j_ == 0) & (k == 0))
            def _():
                _exchange_start(*side_refs)

        def finish(r):
            if has_cscale:
                r = r * cscale_ref[...]
            if has_bias:
                r = r + bias_ref[...]
            if has_add:
                r = r + add_ref[...].astype(F32)
            o_ref[...] = r.astype(out_dtype)

        av = a_ref[...]
        if a_silu:
            av = _silu(av.astype(F32))
        if has_ascale:
            av = av.astype(F32) * ascale_ref[...]
        av = av.astype(BF16)
        bv = b_ref[...].astype(BF16)
        dn = (((0 if ta else 1,), (1 if tb else 0,)), ((), ()))
        part = lax.dot_general(av, bv, dn, preferred_element_type=F32)
        if nk == 1:
            finish(part)
        else:
            @pl.when(k == 0)
            def _():
                acc_ref[...] = part

            @pl.when(k > 0)
            def _():
                acc_ref[...] += part

            @pl.when(k == nk - 1)
            def _():
                finish(acc_ref[...])

        if has_side:
            @pl.when((i_ == gm - 1) & (j_ == gn - 1) & (k == nk - 1))
            def _():
                _exchange_wait(*side_refs)

    res = pl.pallas_call(
        body, name=name, grid=(gm, gn, nk), in_specs=in_specs, out_specs=out_specs, out_shape=out_shapes,
        scratch_shapes=scratch, compiler_params=_params(3))(*ins)
    return (res[0], res[1]) if has_side else res[0]


def gmm_nn(a, w, *, name, tb=False, out_dtype=BF16, tm=2048):
    T = a.shape[0]
    G = w.shape[0]
    Kg = a.shape[1] // G
    Ng = w.shape[1] if tb else w.shape[2]
    tm = _tile(T, tm)

    def body(a_ref, w_ref, o_ref):
        wv = w_ref[...].astype(BF16)
        av = a_ref[...].astype(BF16)
        r = _dot_nt(av, wv) if tb else _dot(av, wv)
        o_ref[...] = r.astype(out_dtype)

    return pl.pallas_call(
        body, name=name, grid=(G, T // tm),
        in_specs=[pl.BlockSpec((tm, Kg), lambda g, i: (i, g)),
                  pl.BlockSpec((None,) + tuple(w.shape[1:]), lambda g, i: (g, 0, 0))],
        out_specs=pl.BlockSpec((tm, Ng), lambda g, i: (i, g)),
        out_shape=jax.ShapeDtypeStruct((T, G * Ng), out_dtype), compiler_params=_params(2))(a, w)


def gmm_tn(a, b, G, *, name, tk=2048):
    T = a.shape[0]
    Kg = a.shape[1] // G
    Ng = b.shape[1] // G
    tk = _tile(T, tk)

    def body(a_ref, b_ref, o_ref):
        @pl.when(pl.program_id(1) == 0)
        def _():
            o_ref[...] = jnp.zeros_like(o_ref)

        o_ref[...] += _dot_tn(a_ref[...].astype(BF16), b_ref[...].astype(BF16))

    return pl.pallas_call(
        body, name=name, grid=(G, T // tk),
        in_specs=[pl.BlockSpec((tk, Kg), lambda g, i: (i, g)), pl.BlockSpec((tk, Ng), lambda g, i: (i, g))],
        out_specs=pl.BlockSpec((None, Kg, Ng), lambda g, i: (g, 0, 0)),
        out_shape=jax.ShapeDtypeStruct((G, Kg, Ng), F32), compiler_params=_params(2))(a, b)


def _row(tr, w, cb=0):
    return pl.BlockSpec((tr, w), lambda i: (i, cb))


def _bc(w):
    return pl.BlockSpec((1, w), lambda i: (0, 0))


def prenorm(x, scale, shift, *, name):
    T, D = x.shape
    tr = _tile(T, 256)

    def body(x_ref, sc_ref, sh_ref, h_ref, ht_ref):
        xv = x_ref[...]
        rstd = lax.rsqrt(jnp.mean(xv * xv, axis=-1, keepdims=True) + NORM_EPS)
        hv = ((xv * rstd) * (1.0 + sc_ref[...]) + sh_ref[...]).astype(BF16)
        h_ref[...] = hv
        ht_ref[...] = hv.T

    return pl.pallas_call(
        body, name=name, grid=(T // tr,), in_specs=[_row(tr, D), _bc(D), _bc(D)],
        out_specs=[_row(tr, D), pl.BlockSpec((D, tr), lambda i: (0, i))],
        out_shape=[jax.ShapeDtypeStruct((T, D), BF16), jax.ShapeDtypeStruct((D, T), BF16)],
        compiler_params=_params(1))(x, scale, shift)


def prenorm_bwd(x, dh, dxn, scale, *, name):
    T, D = x.shape
    tr = _tile(T, 256)

    def body(x_ref, dh_ref, dxn_ref, sc_ref, dx_ref, dsc_ref, dsh_ref):
        @pl.when(pl.program_id(0) == 0)
        def _():
            dsc_ref[...] = jnp.zeros_like(dsc_ref)
            dsh_ref[...] = jnp.zeros_like(dsh_ref)

        xv = x_ref[...]
        dhv = dh_ref[...].astype(F32)
        rstd = lax.rsqrt(jnp.mean(xv * xv, axis=-1, keepdims=True) + NORM_EPS)
        r = xv * rstd
        dsc_ref[...] += jnp.sum(dhv * r, axis=0, keepdims=True)
        dsh_ref[...] += jnp.sum(dhv, axis=0, keepdims=True)
        dr = dhv * (1.0 + sc_ref[...])
        dx_ref[...] = dxn_ref[...] + rstd * (dr - r * jnp.mean(dr * r, axis=-1, keepdims=True))

    return pl.pallas_call(
        body, name=name, grid=(T // tr,), in_specs=[_row(tr, D), _row(tr, D), _row(tr, D), _bc(D)],
        out_specs=[_row(tr, D), _bc(D), _bc(D)],
        out_shape=[jax.ShapeDtypeStruct((T, D), F32), jax.ShapeDtypeStruct((1, D), F32),
                   jax.ShapeDtypeStruct((1, D), F32)], compiler_params=_params(1))(x, dh, dxn, scale)


def out_proj_grads(g, w, gate, *, name):
    K, D = g.shape
    tr = _tile(K, 256)

    def body(g_ref, w_ref, gate_ref, dw_ref, dg_ref):
        @pl.when(pl.program_id(0) == 0)
        def _():
            dg_ref[...] = jnp.zeros_like(dg_ref)

        gv = g_ref[...]
        dw_ref[...] = (gv * gate_ref[...]).astype(BF16)
        dg_ref[...] += jnp.sum(gv * w_ref[...].astype(F32), axis=0, keepdims=True)

    return pl.pallas_call(
        body, name=name, grid=(K // tr,), in_specs=[_row(tr, D), _row(tr, D), _bc(D)],
        out_specs=[_row(tr, D), _bc(D)],
        out_shape=[jax.ShapeDtypeStruct((K, D), BF16), jax.ShapeDtypeStruct((1, D), F32)],
        compiler_params=_params(1))(g, w, gate)


def loss_head(x, g, target, *, name):
    T, D = x.shape
    tr = _tile(T, 256)

    def body(x_ref, g_ref, t_ref, loss_ref, dx_ref, dg_ref):
        @pl.when(pl.program_id(0) == 0)
        def _():
            loss_ref[...] = jnp.zeros_like(loss_ref)
            dg_ref[...] = jnp.zeros_like(dg_ref)

        xv = x_ref[...]
        gv = g_ref[...]
        rstd = lax.rsqrt(jnp.mean(xv * xv, axis=-1, keepdims=True) + NORM_EPS)
        r = xv * rstd
        e = r * gv - t_ref[...]
        loss_ref[...] += 0.5 * jnp.sum(jnp.mean(e * e, axis=-1, keepdims=True), axis=0, keepdims=True)
        dout = e * (1.0 / D)
        dg_ref[...] += jnp.sum(dout * r, axis=0, keepdims=True)
        dr = dout * gv
        dx_ref[...] = rstd * (dr - r * jnp.mean(dr * r, axis=-1, keepdims=True))

    return pl.pallas_call(
        body, name=name, grid=(T // tr,), in_specs=[_row(tr, D), _bc(D), _row(tr, D)],
        out_specs=[pl.BlockSpec((1, 1), lambda i: (0, 0)), _row(tr, D), _bc(D)],
        out_shape=[jax.ShapeDtypeStruct((1, 1), F32), jax.ShapeDtypeStruct((T, D), F32),
                   jax.ShapeDtypeStruct((1, D), F32)], compiler_params=_params(1))(x, g, target)


def _col_tile(W, off):
    cw = math.gcd(W, off) if off else W
    cw = math.gcd(cw, 1024) if cw > 1024 else cw
    return cw


def gated(a, proj, g_off, scale, *, name):
    T, W = a.shape
    cw = _col_tile(W, g_off)
    gb = g_off // cw
    tr = _tile(T, 256)

    def body(a_ref, g_ref, s_ref, o_ref, ot_ref):
        y = (a_ref[...].astype(F32) * s_ref[...] * _silu(g_ref[...].astype(F32))).astype(BF16)
        o_ref[...] = y
        ot_ref[...] = y.T

    return pl.pallas_call(
        body, name=name, grid=(W // cw, T // tr),
        in_specs=[pl.BlockSpec((tr, cw), lambda j, i: (i, j)), pl.BlockSpec((tr, cw), lambda j, i: (i, gb + j)),
                  pl.BlockSpec((1, cw), lambda j, i: (0, j))],
        out_specs=[pl.BlockSpec((tr, cw), lambda j, i: (i, j)), pl.BlockSpec((cw, tr), lambda j, i: (j, i))],
        out_shape=[jax.ShapeDtypeStruct((T, W), BF16), jax.ShapeDtypeStruct((W, T), BF16)],
        compiler_params=_params(2))(a, proj, scale)


def gated_bwd(dy2, a, proj, g_off, scale, *, name):
    T, W = a.shape
    cw = _col_tile(W, g_off)
    gb = g_off // cw
    tr = _tile(T, 256)

    def body(d_ref, a_ref, g_ref, s_ref, da_ref, dg_ref, ds_ref):
        @pl.when(pl.program_id(1) == 0)
        def _():
            ds_ref[...] = jnp.zeros_like(ds_ref)

        d = d_ref[...].astype(F32)
        av = a_ref[...].astype(F32)
        gv = g_ref[...].astype(F32)
        sv = s_ref[...]
        dsg = d * _silu(gv)
        da_ref[...] = (dsg * sv).astype(BF16)
        dg_ref[...] = (d * av * sv * _dsilu(gv)).astype(BF16)
        ds_ref[...] += jnp.sum(dsg * av, axis=0, keepdims=True)

    blk = pl.BlockSpec((tr, cw), lambda j, i: (i, j))
    return pl.pallas_call(
        body, name=name, grid=(W // cw, T // tr),
        in_specs=[blk, blk, pl.BlockSpec((tr, cw), lambda j, i: (i, gb + j)),
                  pl.BlockSpec((1, cw), lambda j, i: (0, j))],
        out_specs=[blk, pl.BlockSpec((tr, cw), lambda j, i: (i, gb + j)), pl.BlockSpec((1, cw), lambda j, i: (0, j))],
        out_shape=[jax.ShapeDtypeStruct((T, W), BF16), jax.ShapeDtypeStruct(proj.shape, BF16),
                   jax.ShapeDtypeStruct((1, W), F32)], compiler_params=_params(2))(dy2, a, proj, scale)


HALO = 16


def band(u, W, *, transpose, name, into=None):
    T = u.shape[0]
    R = _tile(T, 256)
    nb = T // R
    G = len(POOL_WINDOWS)
    Cg = W // G
    hal = min(HALO, R)

    def body(p_ref, c_ref, n_ref, *rest):
        o_ref = rest[-1]
        i = pl.program_id(0)
        out_pos = lax.broadcasted_iota(jnp.int32, (R, 1), 0) + i * R
        parts = ((p_ref, i * R - hal, hal, R - hal), (c_ref, i * R, R, 0), (n_ref, (i + 1) * R, hal, 0))
        for gi, w in enumerate(POOL_WINDOWS):
            half = w // 2
            cols = slice(gi * Cg, (gi + 1) * Cg)
            acc = jnp.zeros((R, Cg), F32)
            for ref, base, n, r0 in parts:
                src_pos = lax.broadcasted_iota(jnp.int32, (1, n), 1) + base
                valid = (src_pos >= 0) & (src_pos < T)
                src = ref[r0:r0 + n, cols]
                if not transpose:
                    m = (src_pos >= out_pos - half) & (src_pos < out_pos + half) & valid
                else:
                    m = (out_pos >= src_pos - half) & (out_pos < src_pos + half) & valid
                    sp = lax.broadcasted_iota(jnp.int32, (n, 1), 0) + base
                    cnt = jnp.minimum(sp + half, T) - jnp.maximum(sp - half, 0)
                    src = (src.astype(F32) / jnp.maximum(cnt, 1).astype(F32)).astype(BF16)
                acc = acc + _dot(m.astype(BF16), src.astype(BF16))
            if not transpose:
                cnt = jnp.minimum(out_pos + half, T) - jnp.maximum(out_pos - half, 0)
                acc = acc / cnt.astype(F32)
            o_ref[:, cols] = (acc - c_ref[:, cols].astype(F32)).astype(BF16)

    in_specs = [pl.BlockSpec((R, W), lambda i: (jnp.maximum(i - 1, 0), 0)), pl.BlockSpec((R, W), lambda i: (i, 0)),
                pl.BlockSpec((R, W), lambda i: (jnp.minimum(i + 1, nb - 1), 0))]
    ins, out_cols, aliases = [u, u, u], W, {}
    if into is not None:
        ins.append(into)
        in_specs.append(pl.BlockSpec(memory_space=pl.ANY))
        out_cols, aliases = into.shape[1], {3: 0}
    return pl.pallas_call(
        body, name=name, grid=(nb,), in_specs=in_specs, out_specs=pl.BlockSpec((R, W), lambda i: (i, 0)),
        out_shape=jax.ShapeDtypeStruct((T, out_cols), BF16), input_output_aliases=aliases,
        compiler_params=_params(1))(*ins)


def _log_sigmoid(x):
    return jnp.minimum(x, 0.0) - jnp.log1p(jnp.exp(-jnp.abs(x)))


def _split3(x):
    hi = x.astype(BF16)
    r1 = x - hi.astype(F32)
    md = r1.astype(BF16)
    lo = (r1 - md.astype(F32)).astype(BF16)
    return hi, md, lo


def _tri_sum(tri, x):
    hi, md, lo = _split3(x)
    return _dot(tri, hi) + _dot(tri, md) + _dot(tri, lo)


def _gla_masks(C, reverse):
    row = lax.broadcasted_iota(jnp.int32, (C, C), 0)
    col = lax.broadcasted_iota(jnp.int32, (C, C), 1)
    if not reverse:
        return (col <= row), (col >= row), (col <= row)
    return (col >= row), (col <= row), (col > row)


def _gla_dims(proj):
    VW = proj.shape[1] // 3
    KW = VW // 2
    return KW, VW, KW // GLA_HEADS, VW // GLA_HEADS


def gla_fwd(proj, z, bias, *, reverse, name):
    T = proj.shape[0]
    KW, VW, DK, DV = _gla_dims(proj)
    H = GLA_HEADS
    C = _tile(T, GLA_CHUNK)
    NC = T // C
    cidx = (lambda i: NC - 1 - i) if reverse else (lambda i: i)
    last = 0 if reverse else C - 1

    def body(q_ref, k_ref, v_ref, z_ref, b_ref, o_ref, s_ref, S_scr):
        @pl.when(pl.program_id(0) == 0)
        def _():
            S_scr[...] = jnp.zeros_like(S_scr)

        cum, _, amask = _gla_masks(C, reverse)
        la = _log_sigmoid(z_ref[...] + b_ref[...]) * (1.0 / GLA_TAU)
        b = _tri_sum(cum.astype(BF16), la)
        for h in range(H):
            ks = slice(h * DK, (h + 1) * DK)
            vs = slice(h * DV, (h + 1) * DV)
            bh = b[:, ks]
            mid = bh[C // 2:C // 2 + 1, :]
            bl = bh[last:last + 1, :]
            q = q_ref[:, ks].astype(F32) * (DK ** -0.5)
            k = k_ref[:, ks].astype(F32)
            v = v_ref[:, vs]
            qe = (q * jnp.exp(bh)).astype(BF16)
            qt = (q * jnp.exp(bh - mid)).astype(BF16)
            kt = (k * jnp.exp(mid - bh)).astype(BF16)
            kd = (k * jnp.exp(bl - bh)).astype(BF16)
            St = S_scr[h]
            Sb = St.astype(BF16)
            s_ref[0, h] = Sb
            A = jnp.where(amask, _dot_nt(qt, kt), 0.0).astype(BF16)
            o_ref[:, vs] = _dot_nt(qe, Sb) + _dot(A, v)
            S_scr[h] = St * jnp.exp(bl) + _dot_tn(v, kd)

    return pl.pallas_call(
        body, name=name, grid=(NC,),
        in_specs=[pl.BlockSpec((C, KW), lambda i: (cidx(i), 0)), pl.BlockSpec((C, KW), lambda i: (cidx(i), 1)),
                  pl.BlockSpec((C, VW), lambda i: (cidx(i), 1)), pl.BlockSpec((C, KW), lambda i: (cidx(i), 0)),
                  pl.BlockSpec((1, KW), lambda i: (0, 0))],
        out_specs=[pl.BlockSpec((C, VW), lambda i: (cidx(i), 0)),
                   pl.BlockSpec((1, H, DV, DK), lambda i: (cidx(i), 0, 0, 0))],
        out_shape=[jax.ShapeDtypeStruct((T, VW), F32), jax.ShapeDtypeStruct((NC, H, DV, DK), BF16)],
        scratch_shapes=[pltpu.VMEM((H, DV, DK), F32)], compiler_params=_params(1))(proj, proj, proj, z, bias)


def gla_bwd(proj, z, bias, states, do, prev, *, reverse, name, into=None):
    T = proj.shape[0]
    KW, VW, DK, DV = _gla_dims(proj)
    H = GLA_HEADS
    C = _tile(T, GLA_CHUNK)
    NC = T // C
    cidx = (lambda i: i) if reverse else (lambda i: NC - 1 - i)
    last = 0 if reverse else C - 1
    has_prev = prev is not None
    merged = into is not None
    assert has_prev or not merged
    odt = BF16 if has_prev else F32

    def body(*refs):
        q_ref, k_ref, v_ref, z_ref, b_ref, s_ref, do_ref = refs[:7]
        pos = 7
        if has_prev:
            pq_ref, pk_ref, pv_ref = refs[7:10]
            pos = 10
        if merged:
            dp_ref, dz_ref, db_ref, dS_scr = refs[pos + 1:pos + 5]
        else:
            dq_ref, dk_ref, dv_ref, dz_ref, db_ref, dS_scr = refs[pos:pos + 6]

        @pl.when(pl.program_id(0) == 0)
        def _():
            dS_scr[...] = jnp.zeros_like(dS_scr)
            db_ref[...] = jnp.zeros_like(db_ref)

        cum, cum_t, amask = _gla_masks(C, reverse)
        xg = z_ref[...] + b_ref[...]
        la = _log_sigmoid(xg) * (1.0 / GLA_TAU)
        b = _tri_sum(cum.astype(BF16), la)
        cum_t_bf = cum_t.astype(BF16)
        for h in range(H):
            ks = slice(h * DK, (h + 1) * DK)
            vs = slice(h * DV, (h + 1) * DV)
            bh = b[:, ks]
            mid = bh[C // 2:C // 2 + 1, :]
            bl = bh[last:last + 1, :]
            q = q_ref[:, ks].astype(F32) * (DK ** -0.5)
            k = k_ref[:, ks].astype(F32)
            v = v_ref[:, vs]
            dov = do_ref[:, vs]
            e_b = jnp.exp(bh)
            e_up = jnp.exp(bh - mid)
            e_dn = jnp.exp(mid - bh)
            e_l = jnp.exp(bl - bh)
            e_bl = jnp.exp(bl)
            qe = (q * e_b).astype(BF16)
            qt = (q * e_up).astype(BF16)
            kt = (k * e_dn).astype(BF16)
            kd = (k * e_l).astype(BF16)
            Sb = s_ref[0, h]
            dSt = dS_scr[h]
            dSb = dSt.astype(BF16)
            A = jnp.where(amask, _dot_nt(qt, kt), 0.0).astype(BF16)
            dA = jnp.where(amask, _dot_nt(dov, v), 0.0).astype(BF16)
            dv = _dot_tn(A, dov) + _dot_nt(kd, dSb)
            dq = _dot(dA, kt) * e_up + _dot(dov, Sb) * e_b
            dk_state = _dot(v, dSb) * e_l
            dk = _dot_tn(dA, qt) * e_dn + dk_state
            d_b = dq * q - dk * k
            d_bl = (jnp.sum(dk_state * k, axis=0, keepdims=True)
                    + e_bl * jnp.sum(dSt * Sb.astype(F32), axis=0, keepdims=True))
            d_la = _tri_sum(cum_t_bf, d_b) + d_bl
            dz = d_la * (1.0 / GLA_TAU) * jax.nn.sigmoid(-xg[:, ks])
            dq = dq * (DK ** -0.5)
            if has_prev:
                dq = dq + pq_ref[:, ks]
                dk = dk + pk_ref[:, ks]
                dv = dv + pv_ref[:, vs]
            if merged:
                dp_ref[:, ks] = dq.astype(BF16)
                dp_ref[:, slice(KW + h * DK, KW + (h + 1) * DK)] = dk.astype(BF16)
                dp_ref[:, slice(2 * KW + h * DV, 2 * KW + (h + 1) * DV)] = dv.astype(BF16)
            else:
                dq_ref[:, ks] = dq.astype(odt)
                dk_ref[:, ks] = dk.astype(odt)
                dv_ref[:, vs] = dv.astype(odt)
            dz_ref[:, ks] = dz.astype(BF16)
            db_ref[:, ks] += jnp.sum(dz, axis=0, keepdims=True)
            dS_scr[h] = dSt * e_bl + _dot_tn(dov, qe)

    kspec = lambda cb: pl.BlockSpec((C, KW), lambda i: (cidx(i), cb))
    vspec = lambda cb: pl.BlockSpec((C, VW), lambda i: (cidx(i), cb))
    ins = [proj, proj, proj, z, bias, states, do]
    in_specs = [kspec(0), kspec(1), vspec(1), kspec(0), pl.BlockSpec((1, KW), lambda i: (0, 0)),
                pl.BlockSpec((1, H, DV, DK), lambda i: (cidx(i), 0, 0, 0)), vspec(0)]
    if has_prev:
        ins += list(prev)
        in_specs += [kspec(0), kspec(0), vspec(0)]
    tail_specs = [kspec(0), pl.BlockSpec((1, KW), lambda i: (0, 0))]
    tail_shapes = [jax.ShapeDtypeStruct((T, KW), BF16), jax.ShapeDtypeStruct((1, KW), F32)]
    if merged:
        ins.append(into)
        in_specs.append(pl.BlockSpec(memory_space=pl.ANY))
        out_specs = [pl.BlockSpec((C, 2 * KW + VW), lambda i: (cidx(i), 0))] + tail_specs
        out_shapes = [jax.ShapeDtypeStruct(into.shape, BF16)] + tail_shapes
        aliases = {len(ins) - 1: 0}
    else:
        out_specs = [kspec(0), kspec(0), vspec(0)] + tail_specs
        out_shapes = [jax.ShapeDtypeStruct((T, KW), odt), jax.ShapeDtypeStruct((T, KW), odt),
                      jax.ShapeDtypeStruct((T, VW), odt)] + tail_shapes
        aliases = {}
    return pl.pallas_call(
        body, name=name, grid=(NC,), in_specs=in_specs, out_specs=out_specs, out_shape=out_shapes,
        input_output_aliases=aliases, scratch_shapes=[pltpu.VMEM((H, DV, DK), F32)],
        compiler_params=_params(1))(*ins)


def gla_out(o_f, o_b, proj, g_off, norm_g, *, name):
    T, VW = o_f.shape
    H = GLA_HEADS
    DV = VW // H
    gb = g_off // VW
    tr = _tile(T, 256)

    def body(f_ref, b_ref, g_ref, n_ref, y_ref, yt_ref):
        for h in range(H):
            vs = slice(h * DV, (h + 1) * DV)
            o = f_ref[:, vs] + b_ref[:, vs]
            rstd = lax.rsqrt(jnp.mean(o * o, axis=-1, keepdims=True) + NORM_EPS)
            y = (o * rstd * n_ref[...] * _silu(g_ref[:, vs].astype(F32))).astype(BF16)
            y_ref[:, vs] = y
            yt_ref[vs, :] = y.T

    return pl.pallas_call(
        body, name=name, grid=(T // tr,),
        in_specs=[_row(tr, VW), _row(tr, VW), _row(tr, VW, gb), _bc(DV)],
        out_specs=[_row(tr, VW), pl.BlockSpec((VW, tr), lambda i: (0, i))],
        out_shape=[jax.ShapeDtypeStruct((T, VW), BF16), jax.ShapeDtypeStruct((VW, T), BF16)],
        compiler_params=_params(1))(o_f, o_b, proj, norm_g)


def gla_out_bwd(dy2, o_f, o_b, proj, g_off, norm_g, *, name):
    T, VW = o_f.shape
    H = GLA_HEADS
    DV = VW // H
    gb = g_off // VW
    tr = _tile(T, 256)

    def body(d_ref, f_ref, b_ref, g_ref, n_ref, do_ref, dg_ref, dn_ref):
        @pl.when(pl.program_id(0) == 0)
        def _():
            dn_ref[...] = jnp.zeros_like(dn_ref)

        nv = n_ref[...]
        for h in range(H):
            vs = slice(h * DV, (h + 1) * DV)
            o = f_ref[:, vs] + b_ref[:, vs]
            rstd = lax.rsqrt(jnp.mean(o * o, axis=-1, keepdims=True) + NORM_EPS)
            r = o * rstd
            gv = g_ref[:, vs].astype(F32)
            d = d_ref[:, vs].astype(F32)
            dg_ref[:, vs] = (d * r * nv * _dsilu(gv)).astype(BF16)
            dn_o = d * _silu(gv)
            dn_ref[...] += jnp.sum(dn_o * r, axis=0, keepdims=True)
            dr = dn_o * nv
            do_ref[:, vs] = (rstd * (dr - r * jnp.mean(dr * r, axis=-1, keepdims=True))).astype(BF16)

    return pl.pallas_call(
        body, name=name, grid=(T // tr,),
        in_specs=[_row(tr, VW), _row(tr, VW), _row(tr, VW), _row(tr, VW, gb), _bc(DV)],
        out_specs=[_row(tr, VW), _row(tr, VW, gb), _bc(DV)],
        out_shape=[jax.ShapeDtypeStruct((T, VW), BF16), jax.ShapeDtypeStruct(proj.shape, BF16),
                   jax.ShapeDtypeStruct((1, DV), F32)], compiler_params=_params(1))(dy2, o_f, o_b, proj, norm_g)


def _rope_tables(T):
    hd = ATTN_HEAD_DIM
    axis_dim = hd // 2
    rows = T // GRID_W
    t = jnp.arange(T)
    row = (t // GRID_W - rows // 2).astype(F32)
    col = (t % GRID_W - GRID_W // 2).astype(F32)
    inv = ROPE_THETA ** (-jnp.arange(0, axis_dim, 2, dtype=F32) / axis_dim)
    ang = jnp.concatenate([row[:, None] * inv, col[:, None] * inv], axis=-1)
    cos = jnp.repeat(jnp.cos(ang), 2, axis=-1)
    sin = jnp.repeat(jnp.sin(ang), 2, axis=-1)
    sign = jnp.where(jnp.arange(hd) % 2 == 0, -1.0, 1.0).astype(F32)
    return cos, sin * sign


def _pair_swap(x):
    n = x.shape[-1]
    lane = lax.broadcasted_iota(jnp.int32, x.shape, x.ndim - 1)
    return jnp.where(lane % 2 == 0, pltpu.roll(x, n - 1, x.ndim - 1), pltpu.roll(x, 1, x.ndim - 1))


def _attn_dims(proj):
    hd = ATTN_HEAD_DIM
    kvw = ATTN_KV_HEADS * hd
    qw = (proj.shape[1] - 2 * kvw) // 2
    return qw, kvw, qw // hd, (qw // hd) // ATTN_KV_HEADS


LOG2E = 1.4426950408889634
LN2 = 0.6931471805599453


def _q_mult():
    return ATTN_HEAD_DIM ** -0.5 * LOG2E


def attn_prep(proj, qg, kg, cos, sin, *, name):
    T = proj.shape[0]
    QW, KVW, NH, G = _attn_dims(proj)
    hd = ATTN_HEAD_DIM
    KV = ATTN_KV_HEADS
    assert QW % KVW == 0
    tr = _tile(T, 256)

    def body(q_ref, k_ref, v_ref, qg_ref, kg_ref, c_ref, s_ref, qo_ref, ko_ref, vo_ref):
        cv, sv = c_ref[...], s_ref[...]

        def one(x, gain, mult):
            rstd = lax.rsqrt(jnp.mean(x * x, axis=-1, keepdims=True) + NORM_EPS)
            xs = x * rstd * gain
            return (xs * cv + _pair_swap(xs) * sv) * mult

        for h in range(NH):
            hs = slice(h * hd, (h + 1) * hd)
            qo_ref[h] = one(q_ref[:, hs].astype(F32), qg_ref[...], _q_mult()).astype(BF16)
        for h in range(KV):
            hs = slice(h * hd, (h + 1) * hd)
            ko_ref[:, hs] = one(k_ref[:, hs].astype(F32), kg_ref[...], 1.0).astype(BF16)
            vo_ref[:, 2 * h * hd:(2 * h + 1) * hd] = v_ref[:, hs]
            vo_ref[:, (2 * h + 1) * hd:(2 * h + 2) * hd] = jnp.ones((tr, hd), BF16)

    return pl.pallas_call(
        body, name=name, grid=(T // tr,),
        in_specs=[_row(tr, QW), _row(tr, KVW, QW // KVW), _row(tr, KVW, QW // KVW + 1), _bc(hd), _bc(hd),
                  _row(tr, hd), _row(tr, hd)],
        out_specs=[pl.BlockSpec((NH, tr, hd), lambda i: (0, i, 0)), _row(tr, KVW), _row(tr, 2 * KVW)],
        out_shape=[jax.ShapeDtypeStruct((NH, T, hd), BF16), jax.ShapeDtypeStruct((T, KVW), BF16),
                   jax.ShapeDtypeStruct((T, 2 * KVW), BF16)],
        compiler_params=_params(1))(proj, proj, proj, qg, kg, cos, sin)


def attn_prep_bwd(dqr, dkr, proj, qg, kg, cos, sin, into, *, name):
    T = proj.shape[0]
    QW, KVW, NH, G = _attn_dims(proj)
    hd = ATTN_HEAD_DIM
    tr = _tile(T, 256)

    def body(dq_ref, dk_ref, q_ref, k_ref, qg_ref, kg_ref, c_ref, s_ref, into_ref, o_ref, dqg_ref, dkg_ref):
        @pl.when(pl.program_id(0) == 0)
        def _():
            dqg_ref[...] = jnp.zeros_like(dqg_ref)
            dkg_ref[...] = jnp.zeros_like(dkg_ref)

        cv, sv = c_ref[...], s_ref[...]

        def one(d, x, gain, mult):
            d = d * mult
            dxs = d * cv - _pair_swap(d) * sv
            rstd = lax.rsqrt(jnp.mean(x * x, axis=-1, keepdims=True) + NORM_EPS)
            xn = x * rstd
            dgain = jnp.sum(dxs * xn, axis=0, keepdims=True)
            dxn = dxs * gain
            return rstd * (dxn - xn * jnp.mean(dxn * xn, axis=-1, keepdims=True)), dgain

        for h in range(NH):
            hs = slice(h * hd, (h + 1) * hd)
            dx, dgain = one(dq_ref[h].astype(F32), q_ref[:, hs].astype(F32), qg_ref[...], _q_mult())
            o_ref[:, hs] = dx.astype(BF16)
            dqg_ref[...] += dgain
        for h in range(ATTN_KV_HEADS):
            hs = slice(h * hd, (h + 1) * hd)
            dx, dgain = one(dk_ref[:, hs].astype(F32), k_ref[:, hs].astype(F32), kg_ref[...], 1.0)
            o_ref[:, slice(QW + h * hd, QW + (h + 1) * hd)] = dx.astype(BF16)
            dkg_ref[...] += dgain

    return pl.pallas_call(
        body, name=name, grid=(T // tr,),
        in_specs=[pl.BlockSpec((NH, tr, hd), lambda i: (0, i, 0)), _row(tr, KVW), _row(tr, QW),
                  _row(tr, KVW, QW // KVW), _bc(hd), _bc(hd), _row(tr, hd), _row(tr, hd),
                  pl.BlockSpec(memory_space=pl.ANY)],
        out_specs=[_row(tr, QW + KVW), _bc(hd), _bc(hd)],
        out_shape=[jax.ShapeDtypeStruct(into.shape, BF16), jax.ShapeDtypeStruct((1, hd), F32),
                   jax.ShapeDtypeStruct((1, hd), F32)],
        input_output_aliases={8: 0},
        compiler_params=_params(1))(dqr, dkr, proj, proj, qg, kg, cos, sin, into)


FLASH_BQ = 256
FLASH_BK = 512


def flash_fwd(q, kr, vext, *, name):
    NH, T, hd = q.shape
    KV = ATTN_KV_HEADS
    G = NH // KV
    bq = _tile(T, FLASH_BQ)
    bk = _tile(T, 2 * FLASH_BK)
    nk = T // bk

    def body(q_ref, k_ref, v_ref, o_ref, lse_ref, m_scr, acc_scr, sa_scr, sb_scr):
        m_scr[...] = jnp.full_like(m_scr, -jnp.inf)
        acc_scr[...] = jnp.zeros_like(acc_scr)

        def scores(c, s_scr):
            kc = k_ref[pl.ds(pl.multiple_of(c * bk, bk), bk), :]
            for g in range(G):
                s_scr[g] = _dot_nt(q_ref[g], kc)

        def consume(c, s_scr):
            vc = v_ref[pl.ds(pl.multiple_of(c * bk, bk), bk), :]
            for g in range(G):
                s = s_scr[g]
                m_old = m_scr[g]
                m_new = jnp.maximum(m_old, jnp.max(s, axis=-1, keepdims=True))
                p = jnp.exp2(s - m_new)
                acc_scr[g] = jnp.exp2(m_old - m_new) * acc_scr[g] + _dot(p.astype(BF16), vc)
                m_scr[g] = m_new

        scores(0, sa_scr)
        if nk % 2 == 0:
            def pair(t, carry):
                scores(2 * t + 1, sb_scr)
                consume(2 * t, sa_scr)
                scores(2 * t + 2, sa_scr)
                consume(2 * t + 1, sb_scr)
                return carry

            lax.fori_loop(0, nk // 2 - 1, pair, 0)
            scores(nk - 1, sb_scr)
            consume(nk - 2, sa_scr)
            consume(nk - 1, sb_scr)
        else:
            def single(c, carry):
                consume(c, sa_scr)
                scores(c + 1, sa_scr)
                return carry

            lax.fori_loop(0, nk - 1, single, 0)
            consume(nk - 1, sa_scr)
        for g in range(G):
            a = acc_scr[g]
            l = a[:, hd:]
            o_ref[g] = a[:, :hd] / l
            lse_ref[g] = m_scr[g] + jnp.log2(l[:, 0:1])

    return pl.pallas_call(
        body, name=name, grid=(KV, T // bq),
        in_specs=[pl.BlockSpec((G, bq, hd), lambda h, i: (h, i, 0)), pl.BlockSpec((T, hd), lambda h, i: (0, h)),
                  pl.BlockSpec((T, 2 * hd), lambda h, i: (0, h))],
        out_specs=[pl.BlockSpec((G, bq, hd), lambda h, i: (h, i, 0)),
                   pl.BlockSpec((G, bq, 1), lambda h, i: (h, i, 0))],
        out_shape=[jax.ShapeDtypeStruct((NH, T, hd), F32), jax.ShapeDtypeStruct((NH, T, 1), F32)],
        scratch_shapes=[pltpu.VMEM((G, bq, 1), F32), pltpu.VMEM((G, bq, 2 * hd), F32),
                        pltpu.VMEM((G, bq, bk), F32), pltpu.VMEM((G, bq, bk), F32)],
        compiler_params=_params(2))(q, kr, vext)


def flash_bwd(q, kr, proj, v_off, do, lse, delta, into, *, name):
    NH, T, hd = q.shape
    KV = ATTN_KV_HEADS
    G = NH // KV
    bq = _tile(T, FLASH_BQ)
    bk = _tile(T, FLASH_BK)
    nk = T // bk
    nq = T // bq
    vb = v_off // hd

    def body(q_ref, k_ref, v_ref, do_ref, lse_ref, dl_ref, into_ref, dq_ref, dk_ref, dv_ref, dq_scr, dk_scr, dv_scr,
             sa_scr, pa_scr, sb_scr, pb_scr):
        i = pl.program_id(1)

        @pl.when(i == 0)
        def _():
            dk_scr[...] = jnp.zeros_like(dk_scr)
            dv_scr[...] = jnp.zeros_like(dv_scr)

        dq_scr[...] = jnp.zeros_like(dq_scr)

        def scores(c, s_scr, dp_scr):
            off = pl.multiple_of(c * bk, bk)
            kc = k_ref[pl.ds(off, bk), :]
            vc = v_ref[pl.ds(off, bk), :]
            for g in range(G):
                s_scr[g] = _dot_nt(q_ref[g], kc)
                dp_scr[g] = _dot_nt(do_ref[g], vc)

        def consume(c, s_scr, dp_scr):
            off = pl.multiple_of(c * bk, bk)
            kc = k_ref[pl.ds(off, bk), :]
            dk_c = jnp.zeros((bk, hd), F32)
            dv_c = jnp.zeros((bk, hd), F32)
            for g in range(G):
                p = jnp.exp2(s_scr[g] - lse_ref[g])
                ds = (p * (dp_scr[g] - dl_ref[g])).astype(BF16)
                dq_scr[g] += _dot(ds, kc)
                dv_c = dv_c + _dot_tn(p.astype(BF16), do_ref[g])
                dk_c = dk_c + _dot_tn(ds, q_ref[g])
            dk_scr[pl.ds(off, bk), :] += dk_c
            dv_scr[pl.ds(off, bk), :] += dv_c

        scores(0, sa_scr, pa_scr)
        if nk % 2 == 0:
            def pair(t, carry):
                scores(2 * t + 1, sb_scr, pb_scr)
                consume(2 * t, sa_scr, pa_scr)
                scores(2 * t + 2, sa_scr, pa_scr)
                consume(2 * t + 1, sb_scr, pb_scr)
                return carry

            lax.fori_loop(0, nk // 2 - 1, pair, 0)
            scores(nk - 1, sb_scr, pb_scr)
            consume(nk - 2, sa_scr, pa_scr)
            consume(nk - 1, sb_scr, pb_scr)
        else:
            def single(c, carry):
                consume(c, sa_scr, pa_scr)
                scores(c + 1, sa_scr, pa_scr)
                return carry

            lax.fori_loop(0, nk - 1, single, 0)
            consume(nk - 1, sa_scr, pa_scr)
        dq_ref[...] = dq_scr[...].astype(BF16)

        @pl.when(i == nq - 1)
        def _():
            dk_ref[...] = dk_scr[...].astype(BF16)
            dv_ref[...] = (dv_scr[...] * LOG2E).astype(BF16)

    qspec = pl.BlockSpec((G, bq, hd), lambda h, i: (h, i, 0))
    cspec = pl.BlockSpec((G, bq, 1), lambda h, i: (h, i, 0))
    kspec = pl.BlockSpec((T, hd), lambda h, i: (0, h))
    vspec = pl.BlockSpec((T, hd), lambda h, i: (0, vb + h))
    return pl.pallas_call(
        body, name=name, grid=(KV, nq),
        in_specs=[qspec, kspec, vspec, qspec, cspec, cspec, pl.BlockSpec(memory_space=pl.ANY)],
        out_specs=[qspec, kspec, vspec],
        out_shape=[jax.ShapeDtypeStruct((NH, T, hd), BF16), jax.ShapeDtypeStruct((T, KV * hd), BF16),
                   jax.ShapeDtypeStruct(into.shape, BF16)],
        input_output_aliases={6: 2},
        scratch_shapes=[pltpu.VMEM((G, bq, hd), F32), pltpu.VMEM((T, hd), F32), pltpu.VMEM((T, hd), F32)]
        + [pltpu.VMEM((G, bq, bk), F32)] * 4,
        compiler_params=_params(2))(q, kr, proj, do, lse, delta, into)


def attn_gate(o, proj, g_off, *, name):
    NH, T, hd = o.shape
    W = NH * hd
    cw = _col_tile(W, g_off)
    hc = cw // hd
    gb = g_off // cw
    tr = _tile(T, 256)

    def body(o_ref, g_ref, y_ref, yt_ref):
        for h in range(hc):
            hs = slice(h * hd, (h + 1) * hd)
            y = (o_ref[h] * _silu(g_ref[:, hs].astype(F32))).astype(BF16)
            y_ref[:, hs] = y
            yt_ref[hs, :] = y.T

    return pl.pallas_call(
        body, name=name, grid=(W // cw, T // tr),
        in_specs=[pl.BlockSpec((hc, tr, hd), lambda j, i: (j, i, 0)), pl.BlockSpec((tr, cw), lambda j, i: (i, gb + j))],
        out_specs=[pl.BlockSpec((tr, cw), lambda j, i: (i, j)), pl.BlockSpec((cw, tr), lambda j, i: (j, i))],
        out_shape=[jax.ShapeDtypeStruct((T, W), BF16), jax.ShapeDtypeStruct((W, T), BF16)],
        compiler_params=_params(2))(o, proj)


def attn_gate_bwd(dy2, o, proj, g_off, *, name):
    NH, T, hd = o.shape
    W = NH * hd
    cw = _col_tile(W, g_off)
    hc = cw // hd
    gb = g_off // cw
    tr = _tile(T, 256)

    def body(d_ref, o_ref, g_ref, do_ref, dg_ref, dl_ref):
        for h in range(hc):
            hs = slice(h * hd, (h + 1) * hd)
            d = d_ref[:, hs].astype(F32)
            gv = g_ref[:, hs].astype(F32)
            ov = o_ref[h]
            dov = d * _silu(gv) * LN2
            do_ref[h] = dov.astype(BF16)
            dg_ref[:, hs] = (d * ov * _dsilu(gv)).astype(BF16)
            dl_ref[h] = jnp.sum(dov * ov, axis=-1, keepdims=True)

    return pl.pallas_call(
        body, name=name, grid=(W // cw, T // tr),
        in_specs=[pl.BlockSpec((tr, cw), lambda j, i: (i, j)), pl.BlockSpec((hc, tr, hd), lambda j, i: (j, i, 0)),
                  pl.BlockSpec((tr, cw), lambda j, i: (i, gb + j))],
        out_specs=[pl.BlockSpec((hc, tr, hd), lambda j, i: (j, i, 0)), pl.BlockSpec((tr, cw), lambda j, i: (i, gb + j)),
                   pl.BlockSpec((hc, tr, 1), lambda j, i: (j, i, 0))],
        out_shape=[jax.ShapeDtypeStruct((NH, T, hd), BF16), jax.ShapeDtypeStruct(proj.shape, BF16),
                   jax.ShapeDtypeStruct((NH, T, 1), F32)], compiler_params=_params(2))(dy2, o, proj)


def outer_silu(c_t, dm, *, name):
    K, B = c_t.shape
    N = dm.shape[1]
    tk = _tile(K, 256)

    def body(c_ref, d_ref, o_ref):
        s = _silu(c_ref[...])
        acc = jnp.zeros((tk, N), F32)
        for b in range(B):
            acc = acc + s[:, b:b + 1] * d_ref[b:b + 1, :]
        o_ref[...] = acc

    return pl.pallas_call(
        body, name=name, grid=(K // tk,),
        in_specs=[pl.BlockSpec((tk, B), lambda i: (i, 0)), pl.BlockSpec((B, N), lambda i: (0, 0))],
        out_specs=pl.BlockSpec((tk, N), lambda i: (i, 0)),
        out_shape=jax.ShapeDtypeStruct((K, N), F32), compiler_params=_params(1))(c_t, dm)


def add_core_rows(core, a, b, *, name):
    S, R, C = b.shape
    tr = _tile(R, 2048)

    def body(core_ref, a_ref, b_ref, o_ref):
        o_ref[...] = (a_ref[...].astype(F32) + b_ref[...].astype(F32)).astype(BF16)

    blk = pl.BlockSpec((None, tr, C), lambda q, i, core_ref: (q, i, 0))
    grid_spec = pltpu.PrefetchScalarGridSpec(
        num_scalar_prefetch=1, grid=(S, R // tr),
        in_specs=[pl.BlockSpec((None, tr, C), lambda q, i, core_ref: (2 * q + core_ref[0], i, 0)), blk],
        out_specs=blk)
    return pl.pallas_call(
        body, name=name, grid_spec=grid_spec, out_shape=jax.ShapeDtypeStruct((S, R, C), BF16),
        compiler_params=_params(2))(core, a, b)


def sum_slots(x, *, name):
    S, R, C = x.shape
    tr = _tile(R, 512)

    def body(x_ref, o_ref):
        acc = x_ref[0].astype(F32)
        for s in range(1, S):
            acc = acc + x_ref[s].astype(F32)
        o_ref[...] = acc

    return pl.pallas_call(
        body, name=name, grid=(R // tr,), in_specs=[pl.BlockSpec((S, tr, C), lambda i: (0, i, 0))],
        out_specs=pl.BlockSpec((tr, C), lambda i: (i, 0)),
        out_shape=jax.ShapeDtypeStruct((R, C), F32), compiler_params=_params(1))(x)


def adamw(w, g, m, v, *, name):
    R, C = w.shape
    tr = _tile(R, 512) if R % 8 == 0 else R

    def body(w_ref, g_ref, m_ref, v_ref, d_ref, nm_ref, nv_ref):
        gv = g_ref[...]
        mn = ADAM_B1 * m_ref[...] + (1.0 - ADAM_B1) * gv
        vn = ADAM_B2 * v_ref[...] + (1.0 - ADAM_B2) * jnp.square(gv)
        m_hat = mn / (1.0 - ADAM_B1 ** ADAM_STEP)
        v_hat = vn / (1.0 - ADAM_B2 ** ADAM_STEP)
        d_ref[...] = -ADAM_LR * (m_hat / (jnp.sqrt(v_hat) + ADAM_EPS) + ADAM_WD * w_ref[...])
        nm_ref[...] = mn
        nv_ref[...] = vn

    blk = pl.BlockSpec((tr, C), lambda i: (i, 0))
    sh = jax.ShapeDtypeStruct((R, C), F32)
    return pl.pallas_call(
        body, name=name, grid=(R // tr,), in_specs=[blk] * 4, out_specs=[blk] * 3, out_shape=[sh] * 3,
        compiler_params=_params(1))(w, g, m, v)


def _adamw_nd(w, g, m, v, name):
    shp = w.shape
    if w.ndim == 1:
        two = (1, shp[0])
    else:
        two = (math.prod(shp[:-1]), shp[-1])
    d, nm, nv = adamw(w.reshape(two), g.reshape(two), m.reshape(two), v.reshape(two), name=name)
    return d.reshape(shp), nm.reshape(shp), nv.reshape(shp)


IN_DX_TILES = dict(tm=1024, tn=512, tk=8192)


def _pad_cols(w, n):
    return jnp.pad(w, ((0, 0), (0, n - w.shape[1])))


def _pad_rows(w, n):
    return jnp.pad(w, ((0, n - w.shape[0]), (0, 0)))


def kernel(x, c, w_mod, b_mod, pool_w_in, pool_w_grp, pool_scale, pool_w_out, gla_w_in, gla_fwd_w1, gla_fwd_w2, gla_fwd_b, gla_bwd_w1, gla_bwd_w2, gla_bwd_b, gla_norm_g, gla_w_out, attn_w_in, attn_q_norm_g, attn_k_norm_g, attn_w_out, final_norm_g, loss_target, m_w_mod, m_b_mod, m_pool_w_in, m_pool_w_grp, m_pool_scale, m_pool_w_out, m_gla_w_in, m_gla_fwd_w1, m_gla_fwd_w2, m_gla_fwd_b, m_gla_bwd_w1, m_gla_bwd_w2, m_gla_bwd_b, m_gla_norm_g, m_gla_w_out, m_attn_w_in, m_attn_q_norm_g, m_attn_k_norm_g, m_attn_w_out, m_final_norm_g, v_w_mod, v_b_mod, v_pool_w_in, v_pool_w_grp, v_pool_scale, v_pool_w_out, v_gla_w_in, v_gla_fwd_w1, v_gla_fwd_w2, v_gla_fwd_b, v_gla_bwd_w1, v_gla_bwd_w2, v_gla_bwd_b, v_gla_norm_g, v_gla_w_out, v_attn_w_in, v_attn_q_norm_g, v_attn_k_norm_g, v_attn_w_out, v_final_norm_g):
    W = dict(w_mod=w_mod, b_mod=b_mod, pool_w_in=pool_w_in, pool_w_grp=pool_w_grp, pool_scale=pool_scale,
             pool_w_out=pool_w_out, gla_w_in=gla_w_in, gla_fwd_w1=gla_fwd_w1, gla_fwd_w2=gla_fwd_w2,
             gla_fwd_b=gla_fwd_b, gla_bwd_w1=gla_bwd_w1, gla_bwd_w2=gla_bwd_w2, gla_bwd_b=gla_bwd_b,
             gla_norm_g=gla_norm_g, gla_w_out=gla_w_out, attn_w_in=attn_w_in, attn_q_norm_g=attn_q_norm_g,
             attn_k_norm_g=attn_k_norm_g, attn_w_out=attn_w_out, final_norm_g=final_norm_g)
    M = dict(w_mod=m_w_mod, b_mod=m_b_mod, pool_w_in=m_pool_w_in, pool_w_grp=m_pool_w_grp, pool_scale=m_pool_scale,
             pool_w_out=m_pool_w_out, gla_w_in=m_gla_w_in, gla_fwd_w1=m_gla_fwd_w1, gla_fwd_w2=m_gla_fwd_w2,
             gla_fwd_b=m_gla_fwd_b, gla_bwd_w1=m_gla_bwd_w1, gla_bwd_w2=m_gla_bwd_w2, gla_bwd_b=m_gla_bwd_b,
             gla_norm_g=m_gla_norm_g, gla_w_out=m_gla_w_out, attn_w_in=m_attn_w_in, attn_q_norm_g=m_attn_q_norm_g,
             attn_k_norm_g=m_attn_k_norm_g, attn_w_out=m_attn_w_out, final_norm_g=m_final_norm_g)
    V = dict(w_mod=v_w_mod, b_mod=v_b_mod, pool_w_in=v_pool_w_in, pool_w_grp=v_pool_w_grp, pool_scale=v_pool_scale,
             pool_w_out=v_pool_w_out, gla_w_in=v_gla_w_in, gla_fwd_w1=v_gla_fwd_w1, gla_fwd_w2=v_gla_fwd_w2,
             gla_fwd_b=v_gla_fwd_b, gla_bwd_w1=v_gla_bwd_w1, gla_bwd_w2=v_gla_bwd_w2, gla_bwd_b=v_gla_bwd_b,
             gla_norm_g=v_gla_norm_g, gla_w_out=v_gla_w_out, attn_w_in=v_attn_w_in, attn_q_norm_g=v_attn_q_norm_g,
             attn_k_norm_g=v_attn_k_norm_g, attn_w_out=v_attn_w_out, final_norm_g=v_final_norm_g)

    me = _my_id()
    T, D = x.shape[1], x.shape[2]
    x0 = x.reshape(T, D)
    target = loss_target.reshape(T, D)
    BW = D
    n_pool, n_gla, n_attn = pool_w_in.shape[0], gla_w_in.shape[0], attn_w_in.shape[0]
    LR = GLA_LOWRANK
    LRP = LANES
    KWg = gla_fwd_w2.shape[2] * NDEV
    Dm = w_mod.shape[2]

    small_shard_names = ['pool_scale', 'gla_fwd_w1', 'gla_fwd_w2', 'gla_bwd_w1', 'gla_bwd_w2']
    small_items = [c] + [W[n] for n in small_shard_names]
    small_shapes = [a.shape for a in small_items]
    g1 = exchange(GATHER, _pack(small_items, F32), name="gather_small")
    c_all_, ps_all, fw1_all, fw2_all, bw1_all, bw2_all = _unpack(g1, small_shapes, lead=NDEV)
    c_all = c_all_.reshape(NDEV, D)
    pool_scale_full = jnp.transpose(ps_all, (1, 0, 2)).reshape(n_pool, BW)
    w1_full = {'f': fw1_all.transpose(1, 0, 2, 3).reshape(n_gla, D, LR),
               'b': bw1_all.transpose(1, 0, 2, 3).reshape(n_gla, D, LR)}
    w2_full = {'f': fw2_all.transpose(1, 2, 0, 3).reshape(n_gla, LR, KWg),
               'b': bw2_all.transpose(1, 2, 0, 3).reshape(n_gla, LR, KWg)}

    c16 = _pad_rows(c_all, 16)
    b_slab = lax.dynamic_slice_in_dim(b_mod, me * Dm, Dm, axis=1)
    mod_parts = [mm(c16, w_mod[i], name=f"mod_fwd{i}", out_dtype=F32, a_silu=True, tm=16, tn=Dm,
                    bias=b_slab[i:i + 1])[:NDEV] for i in range(DEPTH)]
    mod_slab = jnp.stack(mod_parts)
    g2 = exchange(GATHER, _pack([mod_slab], F32), name="gather_mod")
    (mod_all,) = _unpack(g2, [mod_slab.shape], lead=NDEV)
    mod_mine = lax.dynamic_index_in_dim(mod_all, me, axis=2, keepdims=False)
    mod_mine = mod_mine.transpose(1, 0, 2).reshape(DEPTH, NDEV * Dm)
    shift = [mod_mine[i:i + 1, 0:D] for i in range(DEPTH)]
    scale = [mod_mine[i:i + 1, D:2 * D] for i in range(DEPTH)]
    gate = [mod_mine[i:i + 1, 2 * D:3 * D] for i in range(DEPTH)]

    layer_names = {0: ['pool_w_in', 'pool_w_grp', 'pool_w_out'], 1: ['gla_w_in', 'gla_w_out'],
                   2: ['attn_w_in', 'attn_w_out']}

    def layer_shards(i):
        return [W[n][i // N_MIXERS] for n in layer_names[i % N_MIXERS]]

    def layer_weights(i, gathered):
        parts = _unpack(gathered, [a.shape for a in layer_shards(i)], lead=NDEV)
        w_out = parts[-1].reshape(BW, D)
        out = dict(w_in=parts[0], w_out=w_out, w_in_t=parts[0].transpose(0, 2, 1).reshape(-1, D), w_out_t=w_out.T)
        if i % N_MIXERS == 0:
            g = parts[1]
            out['w_grp'] = g.transpose(1, 0, 2, 3).reshape(g.shape[1], -1, g.shape[3])
        return out

    gathered = exchange(GATHER_VIA_SIBLING, _pack(layer_shards(0), BF16), name="gather_w0")

    cos, sin = _rope_tables(T)

    xs = [x0]
    saved = []
    xi = x0
    for i in range(DEPTH):
        kind, j = i % N_MIXERS, i // N_MIXERS
        h, h_t = prenorm(xi, scale[i], shift[i], name=f"prenorm{i}")
        lw = layer_weights(i, gathered)
        sv = dict(h=h, h_t=h_t, lw=lw)
        in_name = ("pool_in", "gla_in", "attn_in")[kind] + str(i)
        if i + 1 < DEPTH:
            proj, gathered = mm(h, lw['w_in'], b_split=True, name=in_name,
                                side=(GATHER_VIA_SIBLING, _pack(layer_shards(i + 1), BF16)))
        else:
            proj = mm(h, lw['w_in'], b_split=True, name=in_name)
        if kind == 0:
            ug = proj
            pooled = band(ug, BW, transpose=False, name=f"pool_band{i}")
            zz = gmm_nn(pooled, lw['w_grp'], name=f"pool_grp{i}")
            y2, y2_t = gated(zz, ug, BW, pool_scale_full[j:j + 1], name=f"pool_gate{i}")
            sv.update(ug=ug, pooled=pooled, z=zz)
        elif kind == 1:
            sv.update(proj=proj)
            for dname in ('f', 'b'):
                w1p = _pad_cols(w1_full[dname][j], LRP)
                w2p = _pad_rows(w2_full[dname][j], LRP)
                bias = (gla_fwd_b if dname == 'f' else gla_bwd_b)[j:j + 1]
                hw1 = mm(h, w1p, name=f"gla_w1{dname}{i}", tn=LRP)
                zg = mm(hw1, w2p, name=f"gla_w2{dname}{i}", out_dtype=F32, tk=LRP)
                o_d, st_d = gla_fwd(proj, zg, bias, reverse=(dname == 'b'), name=f"gla_scan_{dname}{i}")
                sv.update({f"hw1{dname}": hw1, f"z{dname}": zg, f"o{dname}": o_d, f"st{dname}": st_d,
                           f"w1p{dname}": w1p, f"w2p{dname}": w2p, f"bias{dname}": bias})
            y2, y2_t = gla_out(sv['of'], sv['ob'], proj, 2 * BW, gla_norm_g[j:j + 1], name=f"gla_out{i}")
        else:
            QW, KVW, _, _ = _attn_dims(proj)
            qr, kr, vext = attn_prep(proj, attn_q_norm_g[j:j + 1], attn_k_norm_g[j:j + 1], cos, sin,
                                     name=f"attn_prep{i}")
            o, lse = flash_fwd(qr, kr, vext, name=f"attn_flash{i}")
            y2, y2_t = attn_gate(o, proj, QW + 2 * KVW, name=f"attn_gate{i}")
            sv.update(proj=proj, qr=qr, kr=kr, o=o, lse=lse)
        sv.update(y2_t=y2_t)
        saved.append(sv)
        xi = mm(y2, lw['w_out'], name=f"out_proj{i}", col_scale=gate[i], add=xi, out_dtype=F32, tm=1024)
        xs.append(xi)

    loss_part, dx, d_final_g = loss_head(xi, final_norm_g.reshape(1, D), target, name="loss_head")

    d_mod = [None] * DEPTH
    layer_grads = [None] * DEPTH
    received = [None] * DEPTH
    small_grads = {}

    def in_dx(i, dproj, w_in, **kw):
        name = ("pool_in_dx", "gla_in_dx", "attn_in_dx")[i % N_MIXERS] + str(i)
        if i + 1 < DEPTH:
            dh, received[i + 1] = mm(dproj, w_in, name=name, **IN_DX_TILES, side=(CHIP_ROWS, chip_sums[i + 1]), **kw)
            return dh
        return mm(dproj, w_in, name=name, **IN_DX_TILES, **kw)

    def chip_sum(i, packed, from_sibling):
        core = lax.axis_index("c").astype(jnp.int32).reshape(1)
        return add_core_rows(core, packed, from_sibling, name=f"chip_sum_g{i}")

    chip_sums = [None] * DEPTH
    for i in reversed(range(DEPTH)):
        kind, j = i % N_MIXERS, i // N_MIXERS
        sv = saved[i]
        h_t, lw = sv['h_t'], sv['lw']
        if i + 1 < DEPTH:
            packed = _pack(layer_grads[i + 1], BF16, lead=NDEV)
            dy2, from_sibling = mm(dx, lw['w_out_t'], name=f"out_proj_dx{i}", a_scale=gate[i], tm=1024,
                                   side=(SIBLING_ROWS, packed))
            chip_sums[i + 1] = chip_sum(i + 1, packed, from_sibling)
        else:
            dy2 = mm(dx, lw['w_out_t'], name=f"out_proj_dx{i}", a_scale=gate[i], tm=1024)
        g_raw = mm(sv['y2_t'], dx, name=f"out_proj_dw{i}", out_dtype=F32, tk=1024)
        g_wout, d_gate = out_proj_grads(g_raw, lw['w_out'], gate[i], name=f"out_proj_dgate{i}")
        g_wout = g_wout.reshape(NDEV, BW // NDEV, D)
        if kind == 0:
            ug = sv['ug']
            dz, dproj, d_ps = gated_bwd(dy2, sv['z'], ug, BW, pool_scale_full[j:j + 1], name=f"pool_gate_bwd{i}")
            dpooled = gmm_nn(dz, lw['w_grp'], tb=True, name=f"pool_grp_dx{i}")
            g_grp = gmm_tn(sv['pooled'], dz, len(POOL_WINDOWS), name=f"pool_grp_dw{i}")
            dproj = band(dpooled, BW, transpose=True, name=f"pool_band_bwd{i}", into=dproj)
            dh = in_dx(i, dproj, lw['w_in_t'])
            g_win = mm(h_t, dproj, out_split=pool_w_in.shape[2], name=f"pool_in_dw{i}")
            Gp, Cg = g_grp.shape[0], g_grp.shape[1]
            g_grp = g_grp.astype(BF16).reshape(Gp, NDEV, Cg // NDEV, Cg).transpose(1, 0, 2, 3)
            layer_grads[i] = [g_win, g_grp, g_wout]
            small_grads.setdefault('pool_scale', [None] * n_pool)[j] = d_ps
        elif kind == 1:
            proj = sv['proj']
            do, dproj, d_ng = gla_out_bwd(dy2, sv['of'], sv['ob'], proj, 2 * BW, gla_norm_g[j:j + 1],
                                          name=f"gla_out_bwd{i}")
            prev = None
            dh_acc = None
            for dname in ('f', 'b'):
                scan_args = (proj, sv[f"z{dname}"], sv[f"bias{dname}"], sv[f"st{dname}"], do, prev)
                if dname == 'f':
                    *prev, dzg, dbias = gla_bwd(*scan_args, reverse=False, name=f"gla_scan_bwd_f{i}")
                else:
                    dproj, dzg, dbias = gla_bwd(*scan_args, reverse=True, name=f"gla_scan_bwd_b{i}", into=dproj)
                dhw1 = mm(dzg, sv[f"w2p{dname}"], tb=True, name=f"gla_w2{dname}_dx{i}", tn=LRP)
                g_w2 = mm(sv[f"hw1{dname}"], dzg, ta=True, out_dtype=F32, name=f"gla_w2{dname}_dw{i}", tm=LRP)
                g_w1 = mm(h_t, dhw1, out_dtype=F32, name=f"gla_w1{dname}_dw{i}", tn=LRP)
                dh_acc = mm(dhw1, sv[f"w1p{dname}"], tb=True, add=dh_acc, name=f"gla_w1{dname}_dx{i}", tk=LRP)
                key = 'gla_fwd' if dname == 'f' else 'gla_bwd'
                small_grads[key + '_w1'] = g_w1[:, :LR]
                small_grads[key + '_w2'] = g_w2[:LR]
                small_grads[key + '_b'] = dbias
            dh = in_dx(i, dproj, lw['w_in_t'], add=dh_acc)
            layer_grads[i] = [mm(h_t, dproj, out_split=gla_w_in.shape[2], name=f"gla_in_dw{i}"), g_wout]
            small_grads['gla_norm_g'] = d_ng
        else:
            proj = sv['proj']
            QW, KVW, _, _ = _attn_dims(proj)
            do, dproj, delta = attn_gate_bwd(dy2, sv['o'], proj, QW + 2 * KVW, name=f"attn_gate_bwd{i}")
            dqr, dkr, dproj = flash_bwd(sv['qr'], sv['kr'], proj, QW + KVW, do, sv['lse'], delta, dproj,
                                        name=f"attn_flash_bwd{i}")
            dproj, d_qg, d_kg = attn_prep_bwd(dqr, dkr, proj, attn_q_norm_g[j:j + 1], attn_k_norm_g[j:j + 1],
                                              cos, sin, dproj, name=f"attn_prep_bwd{i}")
            dh = in_dx(i, dproj, lw['w_in_t'])
            layer_grads[i] = [mm(h_t, dproj, out_split=attn_w_in.shape[2], name=f"attn_in_dw{i}"), g_wout]
            small_grads['attn_q_norm_g'] = d_qg
            small_grads['attn_k_norm_g'] = d_kg
        dx, d_scale, d_shift = prenorm_bwd(xs[i], dh, dx, scale[i], name=f"prenorm_bwd{i}")
        d_mod[i] = jnp.concatenate([d_shift, d_scale, d_gate], axis=1)
    grad_x = dx.reshape(1, T, D)

    packed = _pack(layer_grads[0], BF16, lead=NDEV)
    chip_sums[0] = chip_sum(0, packed, exchange(SIBLING_ROWS, packed, name="exchange_g0_sibling"))
    received[0] = exchange(CHIP_ROWS, chip_sums[0], name="exchange_g0_chips")
    per_name = {}
    for i in range(DEPTH):
        summed = sum_slots(received[i], name=f"sum_g{i}")
        parts = _unpack(summed, [a.shape for a in layer_shards(i)])
        for n, g in zip(layer_names[i % N_MIXERS], parts):
            per_name.setdefault(n, []).append(g)
    big_g = {n: jnp.stack(gs) for n, gs in per_name.items()}

    small_order = ['b_mod', 'gla_fwd_b', 'gla_bwd_b', 'gla_norm_g', 'attn_q_norm_g', 'attn_k_norm_g', 'final_norm_g',
                   'pool_scale', 'gla_fwd_w1', 'gla_fwd_w2', 'gla_bwd_w1', 'gla_bwd_w2']
    small_grads['b_mod'] = jnp.concatenate(d_mod, axis=0)
    small_grads['final_norm_g'] = d_final_g
    small_grads['pool_scale'] = jnp.concatenate(small_grads['pool_scale'], axis=0)
    part_items = [jnp.pad(loss_part.reshape(1), (0, LANES - 1))] + [small_grads[n] for n in small_order]
    part_shapes = [a.shape for a in part_items]
    g4 = exchange(GATHER, _pack(part_items, F32), name="gather_parts")
    tot = dict(zip(['loss'] + small_order, _unpack(sum_slots(g4, name="sum_parts"), part_shapes)))
    loss = tot['loss'][0]
    d_mod_all = _unpack(g4, part_shapes, lead=NDEV)[1]

    grads = {}
    grads.update(big_g)
    grads['b_mod'] = tot['b_mod']
    for n in ('gla_fwd_b', 'gla_bwd_b', 'gla_norm_g', 'attn_q_norm_g', 'attn_k_norm_g'):
        grads[n] = tot[n].reshape(W[n].shape)
    grads['final_norm_g'] = tot['final_norm_g'].reshape(D)
    ps_n = pool_scale.shape[1]
    grads['pool_scale'] = lax.dynamic_slice_in_dim(tot['pool_scale'], me * ps_n, ps_n, axis=1)
    rows = gla_fwd_w1.shape[1]
    cols = gla_fwd_w2.shape[2]
    for key in ('gla_fwd', 'gla_bwd'):
        grads[key + '_w1'] = lax.dynamic_slice_in_dim(tot[key + '_w1'], me * rows, rows, axis=0).reshape(1, rows, LR)
        grads[key + '_w2'] = lax.dynamic_slice_in_dim(tot[key + '_w2'], me * cols, cols, axis=1).reshape(1, LR, cols)

    c_t = c_all.T
    dm_slab = lax.dynamic_slice_in_dim(d_mod_all, me * Dm, Dm, axis=2)
    grads['w_mod'] = jnp.stack([outer_silu(c_t, dm_slab[:, i], name=f"mod_dw{i}") for i in range(DEPTH)])

    deltas, new_m, new_v = {}, {}, {}
    for n in WEIGHTS:
        deltas[n], new_m[n], new_v[n] = _adamw_nd(W[n], grads[n], M[n], V[n], name=f"adamw_{n}")

    return (loss, grad_x, *[grads[n] for n in WEIGHTS], *[deltas[n] for n in WEIGHTS],
            *[new_m[n] for n in WEIGHTS], *[new_v[n] for n in WEIGHTS])
```

```python
import functools
import math

import jax
import jax.numpy as jnp
from jax import lax
from jax.experimental import pallas as pl
from jax.experimental.pallas import tpu as pltpu

F32 = jnp.float32
BF16 = jnp.bfloat16
NDEV = 8
LANES = 128
VMEM_LIMIT = 56 * 1024 * 1024

D_MODEL = 2048
DEPTH = 4
N_MIXERS = 3
GRID_W = 64
NORM_EPS = 1e-6
POOL_WINDOWS = (2, 4, 8, 16)
GLA_HEADS = 4
GLA_LOWRANK = 16
GLA_TAU = 16.0
GLA_CHUNK = 128
ATTN_HEAD_DIM = 128
ATTN_KV_HEADS = 4
ROPE_THETA = 10000.0
ADAM_LR = 0.001
ADAM_B1 = 0.9
ADAM_B2 = 0.999
ADAM_EPS = 1e-08
ADAM_WD = 0.01
ADAM_STEP = 10

WEIGHTS = ['w_mod', 'b_mod', 'pool_w_in', 'pool_w_grp', 'pool_scale', 'pool_w_out', 'gla_w_in', 'gla_fwd_w1',
           'gla_fwd_w2', 'gla_fwd_b', 'gla_bwd_w1', 'gla_bwd_w2', 'gla_bwd_b', 'gla_norm_g', 'gla_w_out',
           'attn_w_in', 'attn_q_norm_g', 'attn_k_norm_g', 'attn_w_out', 'final_norm_g']


def _params(n_axes=0):
    sem = ("arbitrary",) * n_axes if n_axes else None
    return pltpu.CompilerParams(dimension_semantics=sem, vmem_limit_bytes=VMEM_LIMIT)


def _silu(g):
    return g * jax.nn.sigmoid(g)


def _dsilu(g):
    s = jax.nn.sigmoid(g)
    return s * (1.0 + g * (1.0 - s))


def _dot(a, b):
    return jnp.dot(a, b, preferred_element_type=F32)


def _dot_nt(a, b):
    return lax.dot_general(a, b, (((1,), (1,)), ((), ())), preferred_element_type=F32)


def _dot_tn(a, b):
    return lax.dot_general(a, b, (((0,), (0,)), ((), ())), preferred_element_type=F32)


def _tile(n, pref):
    if n <= pref:
        return n
    for step in (128, 16, 8):
        t = pref - pref % step
        while t >= step:
            if n % t == 0:
                return t
            t -= step
    raise ValueError((n, pref))


def _peer(k):
    x, y, c = lax.axis_index("x"), lax.axis_index("y"), lax.axis_index("c")
    px = 1 - x if k & 4 else x
    py = 1 - y if k & 2 else y
    pc = 1 - c if k & 1 else c
    return (px, py, pc), 4 * px + 2 * py + pc


def _my_id():
    return 4 * lax.axis_index("x") + 2 * lax.axis_index("y") + lax.axis_index("c")


NCHIP = NDEV // 2
GATHER = "gather"
GATHER_VIA_SIBLING = "g2"
SIBLING_ROWS = "d2d"
CHIP_ROWS = "ici"
ICI_FLIPS = (4, 2, 6)

EXCHANGE_SCRATCH = [pltpu.SemaphoreType.DMA((NDEV - 1,)), pltpu.SemaphoreType.DMA((NDEV - 1,)),
                    pltpu.SemaphoreType.DMA]


def _exchange_slots(kind):
    return NCHIP if kind in (SIBLING_ROWS, CHIP_ROWS) else NDEV


def _exchange_plan(kind, x_ref, o_ref):
    me = _my_id()
    c = lax.axis_index("c")
    chip = me // 2
    sib, sib_id = _peer(1)
    if kind == GATHER:
        sends = [(x_ref, o_ref.at[me], _peer(k)[0], o_ref.at[_peer(k)[1]]) for k in range(1, NDEV)]
        return (x_ref, o_ref.at[me]), sends
    if kind == GATHER_VIA_SIBLING:
        sends = [(x_ref, o_ref.at[me], sib, o_ref.at[sib_id])]
        sends += [(x_ref, o_ref.at[me], _peer(k)[0], o_ref.at[_peer(k)[1]]) for k in ICI_FLIPS]
        sends += [(o_ref.at[_peer(k)[1]], o_ref.at[_peer(k)[1]], sib, o_ref.at[_peer(k ^ 1)[1]]) for k in ICI_FLIPS]
        return (x_ref, o_ref.at[me]), sends
    if kind == SIBLING_ROWS:
        return None, [(x_ref.at[2 * q + (1 - c)], o_ref.at[q], sib, o_ref.at[q]) for q in range(NCHIP)]
    assert kind == CHIP_ROWS, kind
    sends = [(x_ref.at[chip ^ (k >> 1)], o_ref.at[chip], _peer(k)[0], o_ref.at[chip ^ (k >> 1)]) for k in ICI_FLIPS]
    return (x_ref.at[chip], o_ref.at[chip]), sends


def _exchange_copy(n, send, send_sems, recv_sems, incoming):
    src, dst, peer, lands = send
    return pltpu.make_async_remote_copy(
        src_ref=src, dst_ref=lands if incoming else dst, send_sem=send_sems.at[n], recv_sem=recv_sems.at[n],
        device_id=peer, device_id_type=pl.DeviceIdType.MESH)


def _exchange_start(kind, x_ref, o_ref, send_sems, recv_sems, local_sem):
    local, sends = _exchange_plan(kind, x_ref, o_ref)
    if local is not None:
        pltpu.make_async_copy(*local, local_sem).start()
    first = sends[:4] if kind == GATHER_VIA_SIBLING else sends
    for n, send in enumerate(first):
        _exchange_copy(n, send, send_sems, recv_sems, False).start()


def _exchange_wait(kind, x_ref, o_ref, send_sems, recv_sems, local_sem):
    local, sends = _exchange_plan(kind, x_ref, o_ref)
    order = list(range(len(sends)))
    if kind == GATHER_VIA_SIBLING:
        for j in range(3):
            _exchange_copy(1 + j, sends[1 + j], send_sems, recv_sems, True).wait_recv()
            _exchange_copy(4 + j, sends[4 + j], send_sems, recv_sems, False).start()
        order = [0, 4, 5, 6]
    for n in order:
        _exchange_copy(n, sends[n], send_sems, recv_sems, True).wait_recv()
    for n, send in enumerate(sends):
        _exchange_copy(n, send, send_sems, recv_sems, False).wait_send()
    if local is not None:
        pltpu.make_async_copy(*local, local_sem).wait()


def _exchange_out(kind, x):
    return jax.ShapeDtypeStruct((_exchange_slots(kind),) + tuple(x.shape[-2:]), x.dtype)


def exchange(kind, x, name):
    def body(x_ref, o_ref, send_sems, recv_sems, local_sem):
        _exchange_start(kind, x_ref, o_ref, send_sems, recv_sems, local_sem)
        _exchange_wait(kind, x_ref, o_ref, send_sems, recv_sems, local_sem)

    return pl.pallas_call(
        body, name=name, out_shape=_exchange_out(kind, x),
        in_specs=[pl.BlockSpec(memory_space=pl.ANY)],
        out_specs=pl.BlockSpec(memory_space=pl.ANY),
        scratch_shapes=EXCHANGE_SCRATCH,
    )(x)


def _pack(arrs, dtype, lead=None):
    unit = 16 * LANES
    if lead is None:
        flat = [a.astype(dtype).reshape(-1) for a in arrs]
        n = sum(f.shape[0] for f in flat)
        pad = (-n) % unit
        if pad:
            flat.append(jnp.zeros((pad,), dtype))
        return jnp.concatenate(flat).reshape(-1, LANES)
    flat = [a.astype(dtype).reshape(lead, -1) for a in arrs]
    n = sum(f.shape[1] for f in flat)
    pad = (-n) % unit
    if pad:
        flat.append(jnp.zeros((lead, pad), dtype))
    return jnp.concatenate(flat, axis=1).reshape(lead, -1, LANES)


def _unpack(buf, shapes, lead=None):
    out = []
    off = 0
    if lead is None:
        flat = buf.reshape(-1)
        for s in shapes:
            n = math.prod(s)
            out.append(flat[off:off + n].reshape(s))
            off += n
        return out
    flat = buf.reshape(lead, -1)
    for s in shapes:
        n = math.prod(s)
        out.append(flat[:, off:off + n].reshape((lead,) + tuple(s)))
        off += n
    return out


def mm(a, b, *, name, ta=False, tb=False, b_split=False, out_split=0, out_dtype=BF16, tm=2048, tn=1024, tk=None,
       bias=None, add=None, a_silu=False, a_scale=None, col_scale=None, side=None):
    K, M = a.shape if ta else a.shape[::-1]
    if b_split:
        S, d1, n = b.shape
        if tb:
            N, Kb = d1, S * n
        else:
            Kb, N = d1, S * n
    else:
        Kb, N = b.shape[::-1] if tb else b.shape
    assert Kb == K, (a.shape, b.shape, ta, tb, b_split)
    if tk is None:
        tk = 1024 if ta else 2048
    tm = _tile(M, tm)
    if b_split and not tb:
        tn = _tile(n, tn)
    elif out_split:
        tn = _tile(out_split, tn)
    else:
        tn = _tile(N, tn)
    if b_split and tb:
        tk = _tile(n, 1024)
    else:
        tk = _tile(K, tk)
    nk = K // tk

    a_spec = pl.BlockSpec((tk, tm), lambda i, j, k: (k, i)) if ta else pl.BlockSpec((tm, tk), lambda i, j, k: (i, k))
    if b_split and not tb:
        per = n // tn
        b_spec = pl.BlockSpec((None, tk, tn), lambda i, j, k: (j // per, k, j % per))
    elif b_split and tb:
        per = n // tk
        b_spec = pl.BlockSpec((None, tn, tk), lambda i, j, k: (k // per, j, k % per))
    elif tb:
        b_spec = pl.BlockSpec((tn, tk), lambda i, j, k: (j, k))
    else:
        b_spec = pl.BlockSpec((tk, tn), lambda i, j, k: (k, j))
    if out_split:
        per_o = out_split // tn
        o_spec = pl.BlockSpec((None, tm, tn), lambda i, j, k: (j // per_o, i, j % per_o))
        o_shape = jax.ShapeDtypeStruct((N // out_split, M, out_split), out_dtype)
    else:
        o_spec = pl.BlockSpec((tm, tn), lambda i, j, k: (i, j))
        o_shape = jax.ShapeDtypeStruct((M, N), out_dtype)
    ins = [a, b]
    in_specs = [a_spec, b_spec]
    if bias is not None:
        ins.append(bias)
        in_specs.append(pl.BlockSpec((1, tn), lambda i, j, k: (0, j)))
    if add is not None:
        ins.append(add)
        in_specs.append(pl.BlockSpec((tm, tn), lambda i, j, k: (i, j)))
    if a_scale is not None:
        assert not ta
        ins.append(a_scale)
        in_specs.append(pl.BlockSpec((1, tk), lambda i, j, k: (0, k)))
    if col_scale is not None:
        ins.append(col_scale)
        in_specs.append(pl.BlockSpec((1, tn), lambda i, j, k: (0, j)))
    has_bias, has_add, has_side = bias is not None, add is not None, side is not None
    has_ascale, has_cscale = a_scale is not None, col_scale is not None
    out_specs, out_shapes = [o_spec], [o_shape]
    scratch = [pltpu.VMEM((tm, tn) if nk > 1 else (8, LANES), F32)]
    if has_side:
        side_kind, side_x = side
        ins.append(side_x)
        in_specs.append(pl.BlockSpec(memory_space=pl.ANY))
        out_specs.append(pl.BlockSpec(memory_space=pl.ANY))
        out_shapes.append(_exchange_out(side_kind, side_x))
        scratch += EXCHANGE_SCRATCH
    gm, gn = M // tm, N // tn

    def body(*refs):
        a_ref, b_ref = refs[0], refs[1]
        pos = 2
        bias_ref = add_ref = None
        if has_bias:
            bias_ref = refs[pos]
            pos += 1
        if has_add:
            add_ref = refs[pos]
            pos += 1
        if has_ascale:
            ascale_ref = refs[pos]
            pos += 1
        if has_cscale:
            cscale_ref = refs[pos]
            pos += 1
        if has_side:
            side_refs = (side_kind, refs[pos], refs[pos + 2]) + tuple(refs[pos + 4:pos + 7])
            o_ref, acc_ref = refs[pos + 1], refs[pos + 3]
        else:
            o_ref, acc_ref = refs[pos], refs[pos + 1]
        k = pl.program_id(2)
        if has_side:
            i_, j_ = pl.program_id(0), pl.program_id(1)

            @pl.when((i_ == 0) & (j_ == 0) & (k == 0))
            def _():
                _exchange_start(*side_refs)

        def finish(r):
            if has_cscale:
                r = r * cscale_ref[...]
            if has_bias:
                r = r + bias_ref[...]
            if has_add:
                r = r + add_ref[...].astype(F32)
            o_ref[...] = r.astype(out_dtype)

        av = a_ref[...]
        if a_silu:
            av = _silu(av.astype(F32))
        if has_ascale:
            av = av.astype(F32) * ascale_ref[...]
        av = av.astype(BF16)
        bv = b_ref[...].astype(BF16)
        dn = (((0 if ta else 1,), (1 if tb else 0,)), ((), ()))
        part = lax.dot_general(av, bv, dn, preferred_element_type=F32)
        if nk == 1:
            finish(part)
        else:
            @pl.when(k == 0)
            def _():
                acc_ref[...] = part

            @pl.when(k > 0)
            def _():
                acc_ref[...] += part

            @pl.when(k == nk - 1)
            def _():
                finish(acc_ref[...])

        if has_side:
            @pl.when((i_ == gm - 1) & (j_ == gn - 1) & (k == nk - 1))
            def _():
                _exchange_wait(*side_refs)

    res = pl.pallas_call(
        body, name=name, grid=(gm, gn, nk), in_specs=in_specs, out_specs=out_specs, out_shape=out_shapes,
        scratch_shapes=scratch, compiler_params=_params(3))(*ins)
    return (res[0], res[1]) if has_side else res[0]


def gmm_nn(a, w, *, name, tb=False, out_dtype=BF16, tm=2048):
    T = a.shape[0]
    G = w.shape[0]
    Kg = a.shape[1] // G
    Ng = w.shape[1] if tb else w.shape[2]
    tm = _tile(T, tm)

    def body(a_ref, w_ref, o_ref):
        wv = w_ref[...].astype(BF16)
        av = a_ref[...].astype(BF16)
        r = _dot_nt(av, wv) if tb else _dot(av, wv)
        o_ref[...] = r.astype(out_dtype)

    return pl.pallas_call(
        body, name=name, grid=(G, T // tm),
        in_specs=[pl.BlockSpec((tm, Kg), lambda g, i: (i, g)),
                  pl.BlockSpec((None,) + tuple(w.shape[1:]), lambda g, i: (g, 0, 0))],
        out_specs=pl.BlockSpec((tm, Ng), lambda g, i: (i, g)),
        out_shape=jax.ShapeDtypeStruct((T, G * Ng), out_dtype), compiler_params=_params(2))(a, w)


def gmm_tn(a, b, G, *, name, tk=2048):
    T = a.shape[0]
    Kg = a.shape[1] // G
    Ng = b.shape[1] // G
    tk = _tile(T, tk)

    def body(a_ref, b_ref, o_ref):
        @pl.when(pl.program_id(1) == 0)
        def _():
            o_ref[...] = jnp.zeros_like(o_ref)

        o_ref[...] += _dot_tn(a_ref[...].astype(BF16), b_ref[...].astype(BF16))

    return pl.pallas_call(
        body, name=name, grid=(G, T // tk),
        in_specs=[pl.BlockSpec((tk, Kg), lambda g, i: (i, g)), pl.BlockSpec((tk, Ng), lambda g, i: (i, g))],
        out_specs=pl.BlockSpec((None, Kg, Ng), lambda g, i: (g, 0, 0)),
        out_shape=jax.ShapeDtypeStruct((G, Kg, Ng), F32), compiler_params=_params(2))(a, b)


def _row(tr, w, cb=0):
    return pl.BlockSpec((tr, w), lambda i: (i, cb))


def _bc(w):
    return pl.BlockSpec((1, w), lambda i: (0, 0))


def prenorm(x, scale, shift, *, name):
    T, D = x.shape
    tr = _tile(T, 512)

    def body(x_ref, sc_ref, sh_ref, h_ref, ht_ref):
        xv = x_ref[...]
        rstd = lax.rsqrt(jnp.mean(xv * xv, axis=-1, keepdims=True) + NORM_EPS)
        hv = ((xv * rstd) * (1.0 + sc_ref[...]) + sh_ref[...]).astype(BF16)
        h_ref[...] = hv
        ht_ref[...] = hv.T

    return pl.pallas_call(
        body, name=name, grid=(T // tr,), in_specs=[_row(tr, D), _bc(D), _bc(D)],
        out_specs=[_row(tr, D), pl.BlockSpec((D, tr), lambda i: (0, i))],
        out_shape=[jax.ShapeDtypeStruct((T, D), BF16), jax.ShapeDtypeStruct((D, T), BF16)],
        compiler_params=_params(1))(x, scale, shift)


def prenorm_bwd(x, dh, dxn, scale, *, name):
    T, D = x.shape
    tr = _tile(T, 512)

    def body(x_ref, dh_ref, dxn_ref, sc_ref, dx_ref, dsc_ref, dsh_ref):
        @pl.when(pl.program_id(0) == 0)
        def _():
            dsc_ref[...] = jnp.zeros_like(dsc_ref)
            dsh_ref[...] = jnp.zeros_like(dsh_ref)

        xv = x_ref[...]
        dhv = dh_ref[...].astype(F32)
        rstd = lax.rsqrt(jnp.mean(xv * xv, axis=-1, keepdims=True) + NORM_EPS)
        r = xv * rstd
        dsc_ref[...] += jnp.sum(dhv * r, axis=0, keepdims=True)
        dsh_ref[...] += jnp.sum(dhv, axis=0, keepdims=True)
        dr = dhv * (1.0 + sc_ref[...])
        dx_ref[...] = dxn_ref[...] + rstd * (dr - r * jnp.mean(dr * r, axis=-1, keepdims=True))

    return pl.pallas_call(
        body, name=name, grid=(T // tr,), in_specs=[_row(tr, D), _row(tr, D), _row(tr, D), _bc(D)],
        out_specs=[_row(tr, D), _bc(D), _bc(D)],
        out_shape=[jax.ShapeDtypeStruct((T, D), F32), jax.ShapeDtypeStruct((1, D), F32),
                   jax.ShapeDtypeStruct((1, D), F32)], compiler_params=_params(1))(x, dh, dxn, scale)


def out_proj_grads(g, w, gate, *, name):
    K, D = g.shape
    tr = _tile(K, 256)

    def body(g_ref, w_ref, gate_ref, dw_ref, dg_ref):
        @pl.when(pl.program_id(0) == 0)
        def _():
            dg_ref[...] = jnp.zeros_like(dg_ref)

        gv = g_ref[...]
        dw_ref[...] = (gv * gate_ref[...]).astype(BF16)
        dg_ref[...] += jnp.sum(gv * w_ref[...].astype(F32), axis=0, keepdims=True)

    return pl.pallas_call(
        body, name=name, grid=(K // tr,), in_specs=[_row(tr, D), _row(tr, D), _bc(D)],
        out_specs=[_row(tr, D), _bc(D)],
        out_shape=[jax.ShapeDtypeStruct((K, D), BF16), jax.ShapeDtypeStruct((1, D), F32)],
        compiler_params=_params(1))(g, w, gate)


def loss_head(x, g, target, *, name):
    T, D = x.shape
    tr = _tile(T, 512)

    def body(x_ref, g_ref, t_ref, loss_ref, dx_ref, dg_ref):
        @pl.when(pl.program_id(0) == 0)
        def _():
            loss_ref[...] = jnp.zeros_like(loss_ref)
            dg_ref[...] = jnp.zeros_like(dg_ref)

        xv = x_ref[...]
        gv = g_ref[...]
        rstd = lax.rsqrt(jnp.mean(xv * xv, axis=-1, keepdims=True) + NORM_EPS)
        r = xv * rstd
        e = r * gv - t_ref[...]
        loss_ref[...] += 0.5 * jnp.sum(jnp.mean(e * e, axis=-1, keepdims=True), axis=0, keepdims=True)
        dout = e * (1.0 / D)
        dg_ref[...] += jnp.sum(dout * r, axis=0, keepdims=True)
        dr = dout * gv
        dx_ref[...] = rstd * (dr - r * jnp.mean(dr * r, axis=-1, keepdims=True))

    return pl.pallas_call(
        body, name=name, grid=(T // tr,), in_specs=[_row(tr, D), _bc(D), _row(tr, D)],
        out_specs=[pl.BlockSpec((1, 1), lambda i: (0, 0)), _row(tr, D), _bc(D)],
        out_shape=[jax.ShapeDtypeStruct((1, 1), F32), jax.ShapeDtypeStruct((T, D), F32),
                   jax.ShapeDtypeStruct((1, D), F32)], compiler_params=_params(1))(x, g, target)


def _col_tile(W, off):
    cw = math.gcd(W, off) if off else W
    cw = math.gcd(cw, 1024) if cw > 1024 else cw
    return cw


def gated(a, proj, g_off, scale, *, name):
    T, W = a.shape
    cw = _col_tile(W, g_off)
    gb = g_off // cw
    tr = _tile(T, 512)

    def body(a_ref, g_ref, s_ref, o_ref, ot_ref):
        y = (a_ref[...].astype(F32) * s_ref[...] * _silu(g_ref[...].astype(F32))).astype(BF16)
        o_ref[...] = y
        ot_ref[...] = y.T

    return pl.pallas_call(
        body, name=name, grid=(W // cw, T // tr),
        in_specs=[pl.BlockSpec((tr, cw), lambda j, i: (i, j)), pl.BlockSpec((tr, cw), lambda j, i: (i, gb + j)),
                  pl.BlockSpec((1, cw), lambda j, i: (0, j))],
        out_specs=[pl.BlockSpec((tr, cw), lambda j, i: (i, j)), pl.BlockSpec((cw, tr), lambda j, i: (j, i))],
        out_shape=[jax.ShapeDtypeStruct((T, W), BF16), jax.ShapeDtypeStruct((W, T), BF16)],
        compiler_params=_params(2))(a, proj, scale)


def gated_bwd(dy2, a, proj, g_off, scale, *, name):
    T, W = a.shape
    cw = _col_tile(W, g_off)
    gb = g_off // cw
    tr = _tile(T, 512)

    def body(d_ref, a_ref, g_ref, s_ref, da_ref, dg_ref, ds_ref):
        @pl.when(pl.program_id(1) == 0)
        def _():
            ds_ref[...] = jnp.zeros_like(ds_ref)

        d = d_ref[...].astype(F32)
        av = a_ref[...].astype(F32)
        gv = g_ref[...].astype(F32)
        sv = s_ref[...]
        dsg = d * _silu(gv)
        da_ref[...] = (dsg * sv).astype(BF16)
        dg_ref[...] = (d * av * sv * _dsilu(gv)).astype(BF16)
        ds_ref[...] += jnp.sum(dsg * av, axis=0, keepdims=True)

    blk = pl.BlockSpec((tr, cw), lambda j, i: (i, j))
    return pl.pallas_call(
        body, name=name, grid=(W // cw, T // tr),
        in_specs=[blk, blk, pl.BlockSpec((tr, cw), lambda j, i: (i, gb + j)),
                  pl.BlockSpec((1, cw), lambda j, i: (0, j))],
        out_specs=[blk, pl.BlockSpec((tr, cw), lambda j, i: (i, gb + j)), pl.BlockSpec((1, cw), lambda j, i: (0, j))],
        out_shape=[jax.ShapeDtypeStruct((T, W), BF16), jax.ShapeDtypeStruct(proj.shape, BF16),
                   jax.ShapeDtypeStruct((1, W), F32)], compiler_params=_params(2))(dy2, a, proj, scale)


HALO = 16


def band(u, W, *, transpose, name, into=None):
    T = u.shape[0]
    R = _tile(T, 256)
    nb = T // R
    G = len(POOL_WINDOWS)
    Cg = W // G
    hal = min(HALO, R)

    def body(p_ref, c_ref, n_ref, *rest):
        o_ref = rest[-1]
        i = pl.program_id(0)
        out_pos = lax.broadcasted_iota(jnp.int32, (R, 1), 0) + i * R
        parts = ((p_ref, i * R - hal, hal, R - hal), (c_ref, i * R, R, 0), (n_ref, (i + 1) * R, hal, 0))
        for gi, w in enumerate(POOL_WINDOWS):
            half = w // 2
            cols = slice(gi * Cg, (gi + 1) * Cg)
            acc = jnp.zeros((R, Cg), F32)
            for ref, base, n, r0 in parts:
                src_pos = lax.broadcasted_iota(jnp.int32, (1, n), 1) + base
                valid = (src_pos >= 0) & (src_pos < T)
                src = ref[r0:r0 + n, cols]
                if not transpose:
                    m = (src_pos >= out_pos - half) & (src_pos < out_pos + half) & valid
                else:
                    m = (out_pos >= src_pos - half) & (out_pos < src_pos + half) & valid
                    sp = lax.broadcasted_iota(jnp.int32, (n, 1), 0) + base
                    cnt = jnp.minimum(sp + half, T) - jnp.maximum(sp - half, 0)
                    src = (src.astype(F32) / jnp.maximum(cnt, 1).astype(F32)).astype(BF16)
                acc = acc + _dot(m.astype(BF16), src.astype(BF16))
            if not transpose:
                cnt = jnp.minimum(out_pos + half, T) - jnp.maximum(out_pos - half, 0)
                acc = acc / cnt.astype(F32)
            o_ref[:, cols] = (acc - c_ref[:, cols].astype(F32)).astype(BF16)

    in_specs = [pl.BlockSpec((R, W), lambda i: (jnp.maximum(i - 1, 0), 0)), pl.BlockSpec((R, W), lambda i: (i, 0)),
                pl.BlockSpec((R, W), lambda i: (jnp.minimum(i + 1, nb - 1), 0))]
    ins, out_cols, aliases = [u, u, u], W, {}
    if into is not None:
        ins.append(into)
        in_specs.append(pl.BlockSpec(memory_space=pl.ANY))
        out_cols, aliases = into.shape[1], {3: 0}
    return pl.pallas_call(
        body, name=name, grid=(nb,), in_specs=in_specs, out_specs=pl.BlockSpec((R, W), lambda i: (i, 0)),
        out_shape=jax.ShapeDtypeStruct((T, out_cols), BF16), input_output_aliases=aliases,
        compiler_params=_params(1))(*ins)


def _log_sigmoid(x):
    return jnp.minimum(x, 0.0) - jnp.log1p(jnp.exp(-jnp.abs(x)))


def _split3(x):
    hi = x.astype(BF16)
    r1 = x - hi.astype(F32)
    md = r1.astype(BF16)
    lo = (r1 - md.astype(F32)).astype(BF16)
    return hi, md, lo


def _tri_sum(tri, x):
    hi, md, lo = _split3(x)
    return _dot(tri, hi) + _dot(tri, md) + _dot(tri, lo)


def _gla_masks(C, reverse):
    row = lax.broadcasted_iota(jnp.int32, (C, C), 0)
    col = lax.broadcasted_iota(jnp.int32, (C, C), 1)
    if not reverse:
        return (col <= row), (col >= row), (col <= row)
    return (col >= row), (col <= row), (col > row)


def _gla_dims(proj):
    VW = proj.shape[1] // 3
    KW = VW // 2
    return KW, VW, KW // GLA_HEADS, VW // GLA_HEADS


def gla_fwd(proj, z, bias, *, reverse, name):
    T = proj.shape[0]
    KW, VW, DK, DV = _gla_dims(proj)
    H = GLA_HEADS
    C = _tile(T, GLA_CHUNK)
    NC = T // C
    cidx = (lambda i: NC - 1 - i) if reverse else (lambda i: i)
    last = 0 if reverse else C - 1

    def body(q_ref, k_ref, v_ref, z_ref, b_ref, o_ref, s_ref, S_scr):
        @pl.when(pl.program_id(0) == 0)
        def _():
            S_scr[...] = jnp.zeros_like(S_scr)

        cum, _, amask = _gla_masks(C, reverse)
        la = _log_sigmoid(z_ref[...] + b_ref[...]) * (1.0 / GLA_TAU)
        b = _tri_sum(cum.astype(BF16), la)
        for h in range(H):
            ks = slice(h * DK, (h + 1) * DK)
            vs = slice(h * DV, (h + 1) * DV)
            bh = b[:, ks]
            mid = bh[C // 2:C // 2 + 1, :]
            bl = bh[last:last + 1, :]
            q = q_ref[:, ks].astype(F32) * (DK ** -0.5)
            k = k_ref[:, ks].astype(F32)
            v = v_ref[:, vs]
            qe = (q * jnp.exp(bh)).astype(BF16)
            qt = (q * jnp.exp(bh - mid)).astype(BF16)
            kt = (k * jnp.exp(mid - bh)).astype(BF16)
            kd = (k * jnp.exp(bl - bh)).astype(BF16)
            St = S_scr[h]
            Sb = St.astype(BF16)
            s_ref[0, h] = Sb
            A = jnp.where(amask, _dot_nt(qt, kt), 0.0).astype(BF16)
            o_ref[:, vs] = _dot_nt(qe, Sb) + _dot(A, v)
            S_scr[h] = St * jnp.exp(bl) + _dot_tn(v, kd)

    return pl.pallas_call(
        body, name=name, grid=(NC,),
        in_specs=[pl.BlockSpec((C, KW), lambda i: (cidx(i), 0)), pl.BlockSpec((C, KW), lambda i: (cidx(i), 1)),
                  pl.BlockSpec((C, VW), lambda i: (cidx(i), 1)), pl.BlockSpec((C, KW), lambda i: (cidx(i), 0)),
                  pl.BlockSpec((1, KW), lambda i: (0, 0))],
        out_specs=[pl.BlockSpec((C, VW), lambda i: (cidx(i), 0)),
                   pl.BlockSpec((1, H, DV, DK), lambda i: (cidx(i), 0, 0, 0))],
        out_shape=[jax.ShapeDtypeStruct((T, VW), F32), jax.ShapeDtypeStruct((NC, H, DV, DK), BF16)],
        scratch_shapes=[pltpu.VMEM((H, DV, DK), F32)], compiler_params=_params(1))(proj, proj, proj, z, bias)


def gla_bwd(proj, z, bias, states, do, prev, *, reverse, name, into=None):
    T = proj.shape[0]
    KW, VW, DK, DV = _gla_dims(proj)
    H = GLA_HEADS
    C = _tile(T, GLA_CHUNK)
    NC = T // C
    cidx = (lambda i: i) if reverse else (lambda i: NC - 1 - i)
    last = 0 if reverse else C - 1
    has_prev = prev is not None
    merged = into is not None
    assert has_prev or not merged
    odt = BF16 if has_prev else F32

    def body(*refs):
        q_ref, k_ref, v_ref, z_ref, b_ref, s_ref, do_ref = refs[:7]
        pos = 7
        if has_prev:
            pq_ref, pk_ref, pv_ref = refs[7:10]
            pos = 10
        if merged:
            dp_ref, dz_ref, db_ref, dS_scr = refs[pos + 1:pos + 5]
        else:
            dq_ref, dk_ref, dv_ref, dz_ref, db_ref, dS_scr = refs[pos:pos + 6]

        @pl.when(pl.program_id(0) == 0)
        def _():
            dS_scr[...] = jnp.zeros_like(dS_scr)
            db_ref[...] = jnp.zeros_like(db_ref)

        cum, cum_t, amask = _gla_masks(C, reverse)
        xg = z_ref[...] + b_ref[...]
        la = _log_sigmoid(xg) * (1.0 / GLA_TAU)
        b = _tri_sum(cum.astype(BF16), la)
        cum_t_bf = cum_t.astype(BF16)
        for h in range(H):
            ks = slice(h * DK, (h + 1) * DK)
            vs = slice(h * DV, (h + 1) * DV)
            bh = b[:, ks]
            mid = bh[C // 2:C // 2 + 1, :]
            bl = bh[last:last + 1, :]
            q = q_ref[:, ks].astype(F32) * (DK ** -0.5)
            k = k_ref[:, ks].astype(F32)
            v = v_ref[:, vs]
            dov = do_ref[:, vs]
            e_b = jnp.exp(bh)
            e_up = jnp.exp(bh - mid)
            e_dn = jnp.exp(mid - bh)
            e_l = jnp.exp(bl - bh)
            e_bl = jnp.exp(bl)
            qe = (q * e_b).astype(BF16)
            qt = (q * e_up).astype(BF16)
            kt = (k * e_dn).astype(BF16)
            kd = (k * e_l).astype(BF16)
            Sb = s_ref[0, h]
            dSt = dS_scr[h]
            dSb = dSt.astype(BF16)
            A = jnp.where(amask, _dot_nt(qt, kt), 0.0).astype(BF16)
            dA = jnp.where(amask, _dot_nt(dov, v), 0.0).astype(BF16)
            dv = _dot_tn(A, dov) + _dot_nt(kd, dSb)
            dq = _dot(dA, kt) * e_up + _dot(dov, Sb) * e_b
            dk_state = _dot(v, dSb) * e_l
            dk = _dot_tn(dA, qt) * e_dn + dk_state
            d_b = dq * q - dk * k
            d_bl = (jnp.sum(dk_state * k, axis=0, keepdims=True)
                    + e_bl * jnp.sum(dSt * Sb.astype(F32), axis=0, keepdims=True))
            d_la = _tri_sum(cum_t_bf, d_b) + d_bl
            dz = d_la * (1.0 / GLA_TAU) * jax.nn.sigmoid(-xg[:, ks])
            dq = dq * (DK ** -0.5)
            if has_prev:
                dq = dq + pq_ref[:, ks]
                dk = dk + pk_ref[:, ks]
                dv = dv + pv_ref[:, vs]
            if merged:
                dp_ref[:, ks] = dq.astype(BF16)
                dp_ref[:, slice(KW + h * DK, KW + (h + 1) * DK)] = dk.astype(BF16)
                dp_ref[:, slice(2 * KW + h * DV, 2 * KW + (h + 1) * DV)] = dv.astype(BF16)
            else:
                dq_ref[:, ks] = dq.astype(odt)
                dk_ref[:, ks] = dk.astype(odt)
                dv_ref[:, vs] = dv.astype(odt)
            dz_ref[:, ks] = dz.astype(BF16)
            db_ref[:, ks] += jnp.sum(dz, axis=0, keepdims=True)
            dS_scr[h] = dSt * e_bl + _dot_tn(dov, qe)

    kspec = lambda cb: pl.BlockSpec((C, KW), lambda i: (cidx(i), cb))
    vspec = lambda cb: pl.BlockSpec((C, VW), lambda i: (cidx(i), cb))
    ins = [proj, proj, proj, z, bias, states, do]
    in_specs = [kspec(0), kspec(1), vspec(1), kspec(0), pl.BlockSpec((1, KW), lambda i: (0, 0)),
                pl.BlockSpec((1, H, DV, DK), lambda i: (cidx(i), 0, 0, 0)), vspec(0)]
    if has_prev:
        ins += list(prev)
        in_specs += [kspec(0), kspec(0), vspec(0)]
    tail_specs = [kspec(0), pl.BlockSpec((1, KW), lambda i: (0, 0))]
    tail_shapes = [jax.ShapeDtypeStruct((T, KW), BF16), jax.ShapeDtypeStruct((1, KW), F32)]
    if merged:
        ins.append(into)
        in_specs.append(pl.BlockSpec(memory_space=pl.ANY))
        out_specs = [pl.BlockSpec((C, 2 * KW + VW), lambda i: (cidx(i), 0))] + tail_specs
        out_shapes = [jax.ShapeDtypeStruct(into.shape, BF16)] + tail_shapes
        aliases = {len(ins) - 1: 0}
    else:
        out_specs = [kspec(0), kspec(0), vspec(0)] + tail_specs
        out_shapes = [jax.ShapeDtypeStruct((T, KW), odt), jax.ShapeDtypeStruct((T, KW), odt),
                      jax.ShapeDtypeStruct((T, VW), odt)] + tail_shapes
        aliases = {}
    return pl.pallas_call(
        body, name=name, grid=(NC,), in_specs=in_specs, out_specs=out_specs, out_shape=out_shapes,
        input_output_aliases=aliases, scratch_shapes=[pltpu.VMEM((H, DV, DK), F32)],
        compiler_params=_params(1))(*ins)


def gla_out(o_f, o_b, proj, g_off, norm_g, *, name):
    T, VW = o_f.shape
    H = GLA_HEADS
    DV = VW // H
    gb = g_off // VW
    tr = _tile(T, 512)

    def body(f_ref, b_ref, g_ref, n_ref, y_ref, yt_ref):
        for h in range(H):
            vs = slice(h * DV, (h + 1) * DV)
            o = f_ref[:, vs] + b_ref[:, vs]
            rstd = lax.rsqrt(jnp.mean(o * o, axis=-1, keepdims=True) + NORM_EPS)
            y = (o * rstd * n_ref[...] * _silu(g_ref[:, vs].astype(F32))).astype(BF16)
            y_ref[:, vs] = y
            yt_ref[vs, :] = y.T

    return pl.pallas_call(
        body, name=name, grid=(T // tr,),
        in_specs=[_row(tr, VW), _row(tr, VW), _row(tr, VW, gb), _bc(DV)],
        out_specs=[_row(tr, VW), pl.BlockSpec((VW, tr), lambda i: (0, i))],
        out_shape=[jax.ShapeDtypeStruct((T, VW), BF16), jax.ShapeDtypeStruct((VW, T), BF16)],
        compiler_params=_params(1))(o_f, o_b, proj, norm_g)


def gla_out_bwd(dy2, o_f, o_b, proj, g_off, norm_g, *, name):
    T, VW = o_f.shape
    H = GLA_HEADS
    DV = VW // H
    gb = g_off // VW
    tr = _tile(T, 512)

    def body(d_ref, f_ref, b_ref, g_ref, n_ref, do_ref, dg_ref, dn_ref):
        @pl.when(pl.program_id(0) == 0)
        def _():
            dn_ref[...] = jnp.zeros_like(dn_ref)

        nv = n_ref[...]
        for h in range(H):
            vs = slice(h * DV, (h + 1) * DV)
            o = f_ref[:, vs] + b_ref[:, vs]
            rstd = lax.rsqrt(jnp.mean(o * o, axis=-1, keepdims=True) + NORM_EPS)
            r = o * rstd
            gv = g_ref[:, vs].astype(F32)
            d = d_ref[:, vs].astype(F32)
            dg_ref[:, vs] = (d * r * nv * _dsilu(gv)).astype(BF16)
            dn_o = d * _silu(gv)
            dn_ref[...] += jnp.sum(dn_o * r, axis=0, keepdims=True)
            dr = dn_o * nv
            do_ref[:, vs] = (rstd * (dr - r * jnp.mean(dr * r, axis=-1, keepdims=True))).astype(BF16)

    return pl.pallas_call(
        body, name=name, grid=(T // tr,),
        in_specs=[_row(tr, VW), _row(tr, VW), _row(tr, VW), _row(tr, VW, gb), _bc(DV)],
        out_specs=[_row(tr, VW), _row(tr, VW, gb), _bc(DV)],
        out_shape=[jax.ShapeDtypeStruct((T, VW), BF16), jax.ShapeDtypeStruct(proj.shape, BF16),
                   jax.ShapeDtypeStruct((1, DV), F32)], compiler_params=_params(1))(dy2, o_f, o_b, proj, norm_g)


def _rope_tables(T):
    hd = ATTN_HEAD_DIM
    axis_dim = hd // 2
    rows = T // GRID_W
    t = jnp.arange(T)
    row = (t // GRID_W - rows // 2).astype(F32)
    col = (t % GRID_W - GRID_W // 2).astype(F32)
    inv = ROPE_THETA ** (-jnp.arange(0, axis_dim, 2, dtype=F32) / axis_dim)
    ang = jnp.concatenate([row[:, None] * inv, col[:, None] * inv], axis=-1)
    cos = jnp.repeat(jnp.cos(ang), 2, axis=-1)
    sin = jnp.repeat(jnp.sin(ang), 2, axis=-1)
    sign = jnp.where(jnp.arange(hd) % 2 == 0, -1.0, 1.0).astype(F32)
    return cos, sin * sign


def _pair_swap(x):
    n = x.shape[-1]
    lane = lax.broadcasted_iota(jnp.int32, x.shape, x.ndim - 1)
    return jnp.where(lane % 2 == 0, pltpu.roll(x, n - 1, x.ndim - 1), pltpu.roll(x, 1, x.ndim - 1))


def _attn_dims(proj):
    hd = ATTN_HEAD_DIM
    kvw = ATTN_KV_HEADS * hd
    qw = (proj.shape[1] - 2 * kvw) // 2
    return qw, kvw, qw // hd, (qw // hd) // ATTN_KV_HEADS


LOG2E = 1.4426950408889634
LN2 = 0.6931471805599453


def _q_mult():
    return ATTN_HEAD_DIM ** -0.5 * LOG2E


def attn_prep(proj, qg, kg, cos, sin, *, name):
    T = proj.shape[0]
    QW, KVW, NH, G = _attn_dims(proj)
    hd = ATTN_HEAD_DIM
    KV = ATTN_KV_HEADS
    assert QW % KVW == 0
    tr = _tile(T, 512)

    def body(q_ref, k_ref, v_ref, qg_ref, kg_ref, c_ref, s_ref, qo_ref, ko_ref, vo_ref):
        cv, sv = c_ref[...], s_ref[...]

        def one(x, gain, mult):
            rstd = lax.rsqrt(jnp.mean(x * x, axis=-1, keepdims=True) + NORM_EPS)
            xs = x * rstd * gain
            return (xs * cv + _pair_swap(xs) * sv) * mult

        for h in range(NH):
            hs = slice(h * hd, (h + 1) * hd)
            qo_ref[h] = one(q_ref[:, hs].astype(F32), qg_ref[...], _q_mult()).astype(BF16)
        for h in range(KV):
            hs = slice(h * hd, (h + 1) * hd)
            ko_ref[:, hs] = one(k_ref[:, hs].astype(F32), kg_ref[...], 1.0).astype(BF16)
            vo_ref[:, 2 * h * hd:(2 * h + 1) * hd] = v_ref[:, hs]
            vo_ref[:, (2 * h + 1) * hd:(2 * h + 2) * hd] = jnp.ones((tr, hd), BF16)

    return pl.pallas_call(
        body, name=name, grid=(T // tr,),
        in_specs=[_row(tr, QW), _row(tr, KVW, QW // KVW), _row(tr, KVW, QW // KVW + 1), _bc(hd), _bc(hd),
                  _row(tr, hd), _row(tr, hd)],
        out_specs=[pl.BlockSpec((NH, tr, hd), lambda i: (0, i, 0)), _row(tr, KVW), _row(tr, 2 * KVW)],
        out_shape=[jax.ShapeDtypeStruct((NH, T, hd), BF16), jax.ShapeDtypeStruct((T, KVW), BF16),
                   jax.ShapeDtypeStruct((T, 2 * KVW), BF16)],
        compiler_params=_params(1))(proj, proj, proj, qg, kg, cos, sin)


def attn_prep_bwd(dqr, dkr, proj, qg, kg, cos, sin, into, *, name):
    T = proj.shape[0]
    QW, KVW, NH, G = _attn_dims(proj)
    hd = ATTN_HEAD_DIM
    tr = _tile(T, 512)

    def body(dq_ref, dk_ref, q_ref, k_ref, qg_ref, kg_ref, c_ref, s_ref, into_ref, o_ref, dqg_ref, dkg_ref):
        @pl.when(pl.program_id(0) == 0)
        def _():
            dqg_ref[...] = jnp.zeros_like(dqg_ref)
            dkg_ref[...] = jnp.zeros_like(dkg_ref)

        cv, sv = c_ref[...], s_ref[...]

        def one(d, x, gain, mult):
            d = d * mult
            dxs = d * cv - _pair_swap(d) * sv
            rstd = lax.rsqrt(jnp.mean(x * x, axis=-1, keepdims=True) + NORM_EPS)
            xn = x * rstd
            dgain = jnp.sum(dxs * xn, axis=0, keepdims=True)
            dxn = dxs * gain
            return rstd * (dxn - xn * jnp.mean(dxn * xn, axis=-1, keepdims=True)), dgain

        for h in range(NH):
            hs = slice(h * hd, (h + 1) * hd)
            dx, dgain = one(dq_ref[h].astype(F32), q_ref[:, hs].astype(F32), qg_ref[...], _q_mult())
            o_ref[:, hs] = dx.astype(BF16)
            dqg_ref[...] += dgain
        for h in range(ATTN_KV_HEADS):
            hs = slice(h * hd, (h + 1) * hd)
            dx, dgain = one(dk_ref[:, hs].astype(F32), k_ref[:, hs].astype(F32), kg_ref[...], 1.0)
            o_ref[:, slice(QW + h * hd, QW + (h + 1) * hd)] = dx.astype(BF16)
            dkg_ref[...] += dgain

    return pl.pallas_call(
        body, name=name, grid=(T // tr,),
        in_specs=[pl.BlockSpec((NH, tr, hd), lambda i: (0, i, 0)), _row(tr, KVW), _row(tr, QW),
                  _row(tr, KVW, QW // KVW), _bc(hd), _bc(hd), _row(tr, hd), _row(tr, hd),
                  pl.BlockSpec(memory_space=pl.ANY)],
        out_specs=[_row(tr, QW + KVW), _bc(hd), _bc(hd)],
        out_shape=[jax.ShapeDtypeStruct(into.shape, BF16), jax.ShapeDtypeStruct((1, hd), F32),
                   jax.ShapeDtypeStruct((1, hd), F32)],
        input_output_aliases={8: 0},
        compiler_params=_params(1))(dqr, dkr, proj, proj, qg, kg, cos, sin, into)


FLASH_BQ = 256
FLASH_BK = 512


def flash_fwd(q, kr, vext, *, name):
    NH, T, hd = q.shape
    KV = ATTN_KV_HEADS
    G = NH // KV
    bq = _tile(T, FLASH_BQ)
    bk = _tile(T, 2 * FLASH_BK)
    nk = T // bk

    def body(q_ref, k_ref, v_ref, o_ref, lse_ref, m_scr, acc_scr, sa_scr, sb_scr):
        m_scr[...] = jnp.full_like(m_scr, -jnp.inf)
        acc_scr[...] = jnp.zeros_like(acc_scr)

        def scores(c, s_scr):
            kc = k_ref[pl.ds(pl.multiple_of(c * bk, bk), bk), :]
            for g in range(G):
                s_scr[g] = _dot_nt(q_ref[g], kc)

        def consume(c, s_scr):
            vc = v_ref[pl.ds(pl.multiple_of(c * bk, bk), bk), :]
            for g in range(G):
                s = s_scr[g]
                m_old = m_scr[g]
                m_new = jnp.maximum(m_old, jnp.max(s, axis=-1, keepdims=True))
                p = jnp.exp2(s - m_new)
                acc_scr[g] = jnp.exp2(m_old - m_new) * acc_scr[g] + _dot(p.astype(BF16), vc)
                m_scr[g] = m_new

        scores(0, sa_scr)
        if nk % 2 == 0:
            def pair(t, carry):
                scores(2 * t + 1, sb_scr)
                consume(2 * t, sa_scr)
                scores(2 * t + 2, sa_scr)
                consume(2 * t + 1, sb_scr)
                return carry

            lax.fori_loop(0, nk // 2 - 1, pair, 0)
            scores(nk - 1, sb_scr)
            consume(nk - 2, sa_scr)
            consume(nk - 1, sb_scr)
        else:
            def single(c, carry):
                consume(c, sa_scr)
                scores(c + 1, sa_scr)
                return carry

            lax.fori_loop(0, nk - 1, single, 0)
            consume(nk - 1, sa_scr)
        for g in range(G):
            a = acc_scr[g]
            l = a[:, hd:]
            o_ref[g] = a[:, :hd] / l
            lse_ref[g] = m_scr[g] + jnp.log2(l[:, 0:1])

    return pl.pallas_call(
        body, name=name, grid=(KV, T // bq),
        in_specs=[pl.BlockSpec((G, bq, hd), lambda h, i: (h, i, 0)), pl.BlockSpec((T, hd), lambda h, i: (0, h)),
                  pl.BlockSpec((T, 2 * hd), lambda h, i: (0, h))],
        out_specs=[pl.BlockSpec((G, bq, hd), lambda h, i: (h, i, 0)),
                   pl.BlockSpec((G, bq, 1), lambda h, i: (h, i, 0))],
        out_shape=[jax.ShapeDtypeStruct((NH, T, hd), F32), jax.ShapeDtypeStruct((NH, T, 1), F32)],
        scratch_shapes=[pltpu.VMEM((G, bq, 1), F32), pltpu.VMEM((G, bq, 2 * hd), F32),
                        pltpu.VMEM((G, bq, bk), F32), pltpu.VMEM((G, bq, bk), F32)],
        compiler_params=_params(2))(q, kr, vext)


def flash_bwd(q, kr, proj, v_off, do, lse, delta, into, *, name):
    NH, T, hd = q.shape
    KV = ATTN_KV_HEADS
    G = NH // KV
    bq = _tile(T, FLASH_BQ)
    bk = _tile(T, FLASH_BK)
    nk = T // bk
    nq = T // bq
    vb = v_off // hd

    def body(q_ref, k_ref, v_ref, do_ref, lse_ref, dl_ref, into_ref, dq_ref, dk_ref, dv_ref, dq_scr, dk_scr, dv_scr,
             sa_scr, pa_scr, sb_scr, pb_scr):
        i = pl.program_id(1)

        @pl.when(i == 0)
        def _():
            dk_scr[...] = jnp.zeros_like(dk_scr)
            dv_scr[...] = jnp.zeros_like(dv_scr)

        dq_scr[...] = jnp.zeros_like(dq_scr)

        def scores(c, s_scr, dp_scr):
            off = pl.multiple_of(c * bk, bk)
            kc = k_ref[pl.ds(off, bk), :]
            vc = v_ref[pl.ds(off, bk), :]
            for g in range(G):
                s_scr[g] = _dot_nt(q_ref[g], kc)
                dp_scr[g] = _dot_nt(do_ref[g], vc)

        def consume(c, s_scr, dp_scr):
            off = pl.multiple_of(c * bk, bk)
            kc = k_ref[pl.ds(off, bk), :]
            dk_c = jnp.zeros((bk, hd), F32)
            dv_c = jnp.zeros((bk, hd), F32)
            for g in range(G):
                p = jnp.exp2(s_scr[g] - lse_ref[g])
                ds = (p * (dp_scr[g] - dl_ref[g])).astype(BF16)
                dq_scr[g] += _dot(ds, kc)
                dv_c = dv_c + _dot_tn(p.astype(BF16), do_ref[g])
                dk_c = dk_c + _dot_tn(ds, q_ref[g])
            dk_scr[pl.ds(off, bk), :] += dk_c
            dv_scr[pl.ds(off, bk), :] += dv_c

        scores(0, sa_scr, pa_scr)
        if nk % 2 == 0:
            def pair(t, carry):
                scores(2 * t + 1, sb_scr, pb_scr)
                consume(2 * t, sa_scr, pa_scr)
                scores(2 * t + 2, sa_scr, pa_scr)
                consume(2 * t + 1, sb_scr, pb_scr)
                return carry

            lax.fori_loop(0, nk // 2 - 1, pair, 0)
            scores(nk - 1, sb_scr, pb_scr)
            consume(nk - 2, sa_scr, pa_scr)
            consume(nk - 1, sb_scr, pb_scr)
        else:
            def single(c, carry):
                consume(c, sa_scr, pa_scr)
                scores(c + 1, sa_scr, pa_scr)
                return carry

            lax.fori_loop(0, nk - 1, single, 0)
            consume(nk - 1, sa_scr, pa_scr)
        dq_ref[...] = dq_scr[...].astype(BF16)

        @pl.when(i == nq - 1)
        def _():
            dk_ref[...] = dk_scr[...].astype(BF16)
            dv_ref[...] = (dv_scr[...] * LOG2E).astype(BF16)

    qspec = pl.BlockSpec((G, bq, hd), lambda h, i: (h, i, 0))
    cspec = pl.BlockSpec((G, bq, 1), lambda h, i: (h, i, 0))
    kspec = pl.BlockSpec((T, hd), lambda h, i: (0, h))
    vspec = pl.BlockSpec((T, hd), lambda h, i: (0, vb + h))
    return pl.pallas_call(
        body, name=name, grid=(KV, nq),
        in_specs=[qspec, kspec, vspec, qspec, cspec, cspec, pl.BlockSpec(memory_space=pl.ANY)],
        out_specs=[qspec, kspec, vspec],
        out_shape=[jax.ShapeDtypeStruct((NH, T, hd), BF16), jax.ShapeDtypeStruct((T, KV * hd), BF16),
                   jax.ShapeDtypeStruct(into.shape, BF16)],
        input_output_aliases={6: 2},
        scratch_shapes=[pltpu.VMEM((G, bq, hd), F32), pltpu.VMEM((T, hd), F32), pltpu.VMEM((T, hd), F32)]
        + [pltpu.VMEM((G, bq, bk), F32)] * 4,
        compiler_params=_params(2))(q, kr, proj, do, lse, delta, into)


def attn_gate(o, proj, g_off, *, name):
    NH, T, hd = o.shape
    W = NH * hd
    cw = _col_tile(W, g_off)
    hc = cw // hd
    gb = g_off // cw
    tr = _tile(T, 512)

    def body(o_ref, g_ref, y_ref, yt_ref):
        for h in range(hc):
            hs = slice(h * hd, (h + 1) * hd)
            y = (o_ref[h] * _silu(g_ref[:, hs].astype(F32))).astype(BF16)
            y_ref[:, hs] = y
            yt_ref[hs, :] = y.T

    return pl.pallas_call(
        body, name=name, grid=(W // cw, T // tr),
        in_specs=[pl.BlockSpec((hc, tr, hd), lambda j, i: (j, i, 0)), pl.BlockSpec((tr, cw), lambda j, i: (i, gb + j))],
        out_specs=[pl.BlockSpec((tr, cw), lambda j, i: (i, j)), pl.BlockSpec((cw, tr), lambda j, i: (j, i))],
        out_shape=[jax.ShapeDtypeStruct((T, W), BF16), jax.ShapeDtypeStruct((W, T), BF16)],
        compiler_params=_params(2))(o, proj)


def attn_gate_bwd(dy2, o, proj, g_off, *, name):
    NH, T, hd = o.shape
    W = NH * hd
    cw = _col_tile(W, g_off)
    hc = cw // hd
    gb = g_off // cw
    tr = _tile(T, 512)

    def body(d_ref, o_ref, g_ref, do_ref, dg_ref, dl_ref):
        for h in range(hc):
            hs = slice(h * hd, (h + 1) * hd)
            d = d_ref[:, hs].astype(F32)
            gv = g_ref[:, hs].astype(F32)
            ov = o_ref[h]
            dov = d * _silu(gv) * LN2
            do_ref[h] = dov.astype(BF16)
            dg_ref[:, hs] = (d * ov * _dsilu(gv)).astype(BF16)
            dl_ref[h] = jnp.sum(dov * ov, axis=-1, keepdims=True)

    return pl.pallas_call(
        body, name=name, grid=(W // cw, T // tr),
        in_specs=[pl.BlockSpec((tr, cw), lambda j, i: (i, j)), pl.BlockSpec((hc, tr, hd), lambda j, i: (j, i, 0)),
                  pl.BlockSpec((tr, cw), lambda j, i: (i, gb + j))],
        out_specs=[pl.BlockSpec((hc, tr, hd), lambda j, i: (j, i, 0)), pl.BlockSpec((tr, cw), lambda j, i: (i, gb + j)),
                   pl.BlockSpec((hc, tr, 1), lambda j, i: (j, i, 0))],
        out_shape=[jax.ShapeDtypeStruct((NH, T, hd), BF16), jax.ShapeDtypeStruct(proj.shape, BF16),
                   jax.ShapeDtypeStruct((NH, T, 1), F32)], compiler_params=_params(2))(dy2, o, proj)


def outer_silu(c_t, dm, *, name):
    K, B = c_t.shape
    N = dm.shape[1]
    tk = _tile(K, 256)

    def body(c_ref, d_ref, o_ref):
        s = _silu(c_ref[...])
        acc = jnp.zeros((tk, N), F32)
        for b in range(B):
            acc = acc + s[:, b:b + 1] * d_ref[b:b + 1, :]
        o_ref[...] = acc

    return pl.pallas_call(
        body, name=name, grid=(K // tk,),
        in_specs=[pl.BlockSpec((tk, B), lambda i: (i, 0)), pl.BlockSpec((B, N), lambda i: (0, 0))],
        out_specs=pl.BlockSpec((tk, N), lambda i: (i, 0)),
        out_shape=jax.ShapeDtypeStruct((K, N), F32), compiler_params=_params(1))(c_t, dm)


def add_core_rows(core, a, b, *, name):
    S, R, C = b.shape
    tr = _tile(R, 2048)

    def body(core_ref, a_ref, b_ref, o_ref):
        o_ref[...] = (a_ref[...].astype(F32) + b_ref[...].astype(F32)).astype(BF16)

    blk = pl.BlockSpec((None, tr, C), lambda q, i, core_ref: (q, i, 0))
    grid_spec = pltpu.PrefetchScalarGridSpec(
        num_scalar_prefetch=1, grid=(S, R // tr),
        in_specs=[pl.BlockSpec((None, tr, C), lambda q, i, core_ref: (2 * q + core_ref[0], i, 0)), blk],
        out_specs=blk)
    return pl.pallas_call(
        body, name=name, grid_spec=grid_spec, out_shape=jax.ShapeDtypeStruct((S, R, C), BF16),
        compiler_params=_params(2))(core, a, b)


def sum_slots(x, *, name):
    S, R, C = x.shape
    tr = _tile(R, 512)

    def body(x_ref, o_ref):
        acc = x_ref[0].astype(F32)
        for s in range(1, S):
            acc = acc + x_ref[s].astype(F32)
        o_ref[...] = acc

    return pl.pallas_call(
        body, name=name, grid=(R // tr,), in_specs=[pl.BlockSpec((S, tr, C), lambda i: (0, i, 0))],
        out_specs=pl.BlockSpec((tr, C), lambda i: (i, 0)),
        out_shape=jax.ShapeDtypeStruct((R, C), F32), compiler_params=_params(1))(x)


def adamw(w, g, m, v, *, name):
    R, C = w.shape
    tr = _tile(R, 512) if R % 8 == 0 else R

    def body(w_ref, g_ref, m_ref, v_ref, d_ref, nm_ref, nv_ref):
        gv = g_ref[...]
        mn = ADAM_B1 * m_ref[...] + (1.0 - ADAM_B1) * gv
        vn = ADAM_B2 * v_ref[...] + (1.0 - ADAM_B2) * jnp.square(gv)
        m_hat = mn / (1.0 - ADAM_B1 ** ADAM_STEP)
        v_hat = vn / (1.0 - ADAM_B2 ** ADAM_STEP)
        d_ref[...] = -ADAM_LR * (m_hat / (jnp.sqrt(v_hat) + ADAM_EPS) + ADAM_WD * w_ref[...])
        nm_ref[...] = mn
        nv_ref[...] = vn

    blk = pl.BlockSpec((tr, C), lambda i: (i, 0))
    sh = jax.ShapeDtypeStruct((R, C), F32)
    return pl.pallas_call(
        body, name=name, grid=(R // tr,), in_specs=[blk] * 4, out_specs=[blk] * 3, out_shape=[sh] * 3,
        compiler_params=_params(1))(w, g, m, v)


def _adamw_nd(w, g, m, v, name):
    shp = w.shape
    if w.ndim == 1:
        two = (1, shp[0])
    else:
        two = (math.prod(shp[:-1]), shp[-1])
    d, nm, nv = adamw(w.reshape(two), g.reshape(two), m.reshape(two), v.reshape(two), name=name)
    return d.reshape(shp), nm.reshape(shp), nv.reshape(shp)


IN_DX_TILES = dict(tm=1024, tn=512, tk=8192)


def _pad_cols(w, n):
    return jnp.pad(w, ((0, 0), (0, n - w.shape[1])))


def _pad_rows(w, n):
    return jnp.pad(w, ((0, n - w.shape[0]), (0, 0)))


def kernel(x, c, w_mod, b_mod, pool_w_in, pool_w_grp, pool_scale, pool_w_out, gla_w_in, gla_fwd_w1, gla_fwd_w2, gla_fwd_b, gla_bwd_w1, gla_bwd_w2, gla_bwd_b, gla_norm_g, gla_w_out, attn_w_in, attn_q_norm_g, attn_k_norm_g, attn_w_out, final_norm_g, loss_target, m_w_mod, m_b_mod, m_pool_w_in, m_pool_w_grp, m_pool_scale, m_pool_w_out, m_gla_w_in, m_gla_fwd_w1, m_gla_fwd_w2, m_gla_fwd_b, m_gla_bwd_w1, m_gla_bwd_w2, m_gla_bwd_b, m_gla_norm_g, m_gla_w_out, m_attn_w_in, m_attn_q_norm_g, m_attn_k_norm_g, m_attn_w_out, m_final_norm_g, v_w_mod, v_b_mod, v_pool_w_in, v_pool_w_grp, v_pool_scale, v_pool_w_out, v_gla_w_in, v_gla_fwd_w1, v_gla_fwd_w2, v_gla_fwd_b, v_gla_bwd_w1, v_gla_bwd_w2, v_gla_bwd_b, v_gla_norm_g, v_gla_w_out, v_attn_w_in, v_attn_q_norm_g, v_attn_k_norm_g, v_attn_w_out, v_final_norm_g):
    W = dict(w_mod=w_mod, b_mod=b_mod, pool_w_in=pool_w_in, pool_w_grp=pool_w_grp, pool_scale=pool_scale,
             pool_w_out=pool_w_out, gla_w_in=gla_w_in, gla_fwd_w1=gla_fwd_w1, gla_fwd_w2=gla_fwd_w2,
             gla_fwd_b=gla_fwd_b, gla_bwd_w1=gla_bwd_w1, gla_bwd_w2=gla_bwd_w2, gla_bwd_b=gla_bwd_b,
             gla_norm_g=gla_norm_g, gla_w_out=gla_w_out, attn_w_in=attn_w_in, attn_q_norm_g=attn_q_norm_g,
             attn_k_norm_g=attn_k_norm_g, attn_w_out=attn_w_out, final_norm_g=final_norm_g)
    M = dict(w_mod=m_w_mod, b_mod=m_b_mod, pool_w_in=m_pool_w_in, pool_w_grp=m_pool_w_grp, pool_scale=m_pool_scale,
             pool_w_out=m_pool_w_out, gla_w_in=m_gla_w_in, gla_fwd_w1=m_gla_fwd_w1, gla_fwd_w2=m_gla_fwd_w2,
             gla_fwd_b=m_gla_fwd_b, gla_bwd_w1=m_gla_bwd_w1, gla_bwd_w2=m_gla_bwd_w2, gla_bwd_b=m_gla_bwd_b,
             gla_norm_g=m_gla_norm_g, gla_w_out=m_gla_w_out, attn_w_in=m_attn_w_in, attn_q_norm_g=m_attn_q_norm_g,
             attn_k_norm_g=m_attn_k_norm_g, attn_w_out=m_attn_w_out, final_norm_g=m_final_norm_g)
    V = dict(w_mod=v_w_mod, b_mod=v_b_mod, pool_w_in=v_pool_w_in, pool_w_grp=v_pool_w_grp, pool_scale=v_pool_scale,
             pool_w_out=v_pool_w_out, gla_w_in=v_gla_w_in, gla_fwd_w1=v_gla_fwd_w1, gla_fwd_w2=v_gla_fwd_w2,
             gla_fwd_b=v_gla_fwd_b, gla_bwd_w1=v_gla_bwd_w1, gla_bwd_w2=v_gla_bwd_w2, gla_bwd_b=v_gla_bwd_b,
             gla_norm_g=v_gla_norm_g, gla_w_out=v_gla_w_out, attn_w_in=v_attn_w_in, attn_q_norm_g=v_attn_q_norm_g,
             attn_k_norm_g=v_attn_k_norm_g, attn_w_out=v_attn_w_out, final_norm_g=v_final_norm_g)

    me = _my_id()
    T, D = x.shape[1], x.shape[2]
    x0 = x.reshape(T, D)
    target = loss_target.reshape(T, D)
    BW = D
    n_pool, n_gla, n_attn = pool_w_in.shape[0], gla_w_in.shape[0], attn_w_in.shape[0]
    LR = GLA_LOWRANK
    LRP = LANES
    KWg = gla_fwd_w2.shape[2] * NDEV
    Dm = w_mod.shape[2]

    small_shard_names = ['pool_scale', 'gla_fwd_w1', 'gla_fwd_w2', 'gla_bwd_w1', 'gla_bwd_w2']
    small_items = [c] + [W[n] for n in small_shard_names]
    small_shapes = [a.shape for a in small_items]
    g1 = exchange(GATHER, _pack(small_items, F32), name="gather_small")
    c_all_, ps_all, fw1_all, fw2_all, bw1_all, bw2_all = _unpack(g1, small_shapes, lead=NDEV)
    c_all = c_all_.reshape(NDEV, D)
    pool_scale_full = jnp.transpose(ps_all, (1, 0, 2)).reshape(n_pool, BW)
    w1_full = {'f': fw1_all.transpose(1, 0, 2, 3).reshape(n_gla, D, LR),
               'b': bw1_all.transpose(1, 0, 2, 3).reshape(n_gla, D, LR)}
    w2_full = {'f': fw2_all.transpose(1, 2, 0, 3).reshape(n_gla, LR, KWg),
               'b': bw2_all.transpose(1, 2, 0, 3).reshape(n_gla, LR, KWg)}

    c16 = _pad_rows(c_all, 16)
    b_slab = lax.dynamic_slice_in_dim(b_mod, me * Dm, Dm, axis=1)
    mod_parts = [mm(c16, w_mod[i], name=f"mod_fwd{i}", out_dtype=F32, a_silu=True, tm=16, tn=Dm,
                    bias=b_slab[i:i + 1])[:NDEV] for i in range(DEPTH)]
    mod_slab = jnp.stack(mod_parts)
    g2 = exchange(GATHER, _pack([mod_slab], F32), name="gather_mod")
    (mod_all,) = _unpack(g2, [mod_slab.shape], lead=NDEV)
    mod_mine = lax.dynamic_index_in_dim(mod_all, me, axis=2, keepdims=False)
    mod_mine = mod_mine.transpose(1, 0, 2).reshape(DEPTH, NDEV * Dm)
    shift = [mod_mine[i:i + 1, 0:D] for i in range(DEPTH)]
    scale = [mod_mine[i:i + 1, D:2 * D] for i in range(DEPTH)]
    gate = [mod_mine[i:i + 1, 2 * D:3 * D] for i in range(DEPTH)]

    layer_names = {0: ['pool_w_in', 'pool_w_grp', 'pool_w_out'], 1: ['gla_w_in', 'gla_w_out'],
                   2: ['attn_w_in', 'attn_w_out']}

    def layer_shards(i):
        return [W[n][i // N_MIXERS] for n in layer_names[i % N_MIXERS]]

    def layer_weights(i, gathered):
        parts = _unpack(gathered, [a.shape for a in layer_shards(i)], lead=NDEV)
        w_out = parts[-1].reshape(BW, D)
        out = dict(w_in=parts[0], w_out=w_out, w_in_t=parts[0].transpose(0, 2, 1).reshape(-1, D), w_out_t=w_out.T)
        if i % N_MIXERS == 0:
            g = parts[1]
            out['w_grp'] = g.transpose(1, 0, 2, 3).reshape(g.shape[1], -1, g.shape[3])
        return out

    gathered = exchange(GATHER_VIA_SIBLING, _pack(layer_shards(0), BF16), name="gather_w0")

    cos, sin = _rope_tables(T)

    xs = [x0]
    saved = []
    xi = x0
    for i in range(DEPTH):
        kind, j = i % N_MIXERS, i // N_MIXERS
        h, h_t = prenorm(xi, scale[i], shift[i], name=f"prenorm{i}")
        lw = layer_weights(i, gathered)
        sv = dict(h=h, h_t=h_t, lw=lw)
        in_name = ("pool_in", "gla_in", "attn_in")[kind] + str(i)
        if i + 1 < DEPTH:
            proj, gathered = mm(h, lw['w_in'], b_split=True, name=in_name,
                                side=(GATHER_VIA_SIBLING, _pack(layer_shards(i + 1), BF16)))
        else:
            proj = mm(h, lw['w_in'], b_split=True, name=in_name)
        if kind == 0:
            ug = proj
            pooled = band(ug, BW, transpose=False, name=f"pool_band{i}")
            zz = gmm_nn(pooled, lw['w_grp'], name=f"pool_grp{i}")
            y2, y2_t = gated(zz, ug, BW, pool_scale_full[j:j + 1], name=f"pool_gate{i}")
            sv.update(ug=ug, pooled=pooled, z=zz)
        elif kind == 1:
            sv.update(proj=proj)
            for dname in ('f', 'b'):
                w1p = _pad_cols(w1_full[dname][j], LRP)
                w2p = _pad_rows(w2_full[dname][j], LRP)
                bias = (gla_fwd_b if dname == 'f' else gla_bwd_b)[j:j + 1]
                hw1 = mm(h, w1p, name=f"gla_w1{dname}{i}", tn=LRP)
                zg = mm(hw1, w2p, name=f"gla_w2{dname}{i}", out_dtype=F32, tk=LRP)
                o_d, st_d = gla_fwd(proj, zg, bias, reverse=(dname == 'b'), name=f"gla_scan_{dname}{i}")
                sv.update({f"hw1{dname}": hw1, f"z{dname}": zg, f"o{dname}": o_d, f"st{dname}": st_d,
                           f"w1p{dname}": w1p, f"w2p{dname}": w2p, f"bias{dname}": bias})
            y2, y2_t = gla_out(sv['of'], sv['ob'], proj, 2 * BW, gla_norm_g[j:j + 1], name=f"gla_out{i}")
        else:
            QW, KVW, _, _ = _attn_dims(proj)
            qr, kr, vext = attn_prep(proj, attn_q_norm_g[j:j + 1], attn_k_norm_g[j:j + 1], cos, sin,
                                     name=f"attn_prep{i}")
            o, lse = flash_fwd(qr, kr, vext, name=f"attn_flash{i}")
            y2, y2_t = attn_gate(o, proj, QW + 2 * KVW, name=f"attn_gate{i}")
            sv.update(proj=proj, qr=qr, kr=kr, o=o, lse=lse)
        sv.update(y2_t=y2_t)
        saved.append(sv)
        xi = mm(y2, lw['w_out'], name=f"out_proj{i}", col_scale=gate[i], add=xi, out_dtype=F32, tm=1024)
        xs.append(xi)

    loss_part, dx, d_final_g = loss_head(xi, final_norm_g.reshape(1, D), target, name="loss_head")

    d_mod = [None] * DEPTH
    layer_grads = [None] * DEPTH
    received = [None] * DEPTH
    small_grads = {}

    def in_dx(i, dproj, w_in, **kw):
        name = ("pool_in_dx", "gla_in_dx", "attn_in_dx")[i % N_MIXERS] + str(i)
        if i + 1 < DEPTH:
            dh, received[i + 1] = mm(dproj, w_in, name=name, **IN_DX_TILES, side=(CHIP_ROWS, chip_sums[i + 1]), **kw)
            return dh
        return mm(dproj, w_in, name=name, **IN_DX_TILES, **kw)

    def chip_sum(i, packed, from_sibling):
        core = lax.axis_index("c").astype(jnp.int32).reshape(1)
        return add_core_rows(core, packed, from_sibling, name=f"chip_sum_g{i}")

    chip_sums = [None] * DEPTH
    for i in reversed(range(DEPTH)):
        kind, j = i % N_MIXERS, i // N_MIXERS
        sv = saved[i]
        h_t, lw = sv['h_t'], sv['lw']
        if i + 1 < DEPTH:
            packed = _pack(layer_grads[i + 1], BF16, lead=NDEV)
            dy2, from_sibling = mm(dx, lw['w_out_t'], name=f"out_proj_dx{i}", a_scale=gate[i], tm=1024,
                                   side=(SIBLING_ROWS, packed))
            chip_sums[i + 1] = chip_sum(i + 1, packed, from_sibling)
        else:
            dy2 = mm(dx, lw['w_out_t'], name=f"out_proj_dx{i}", a_scale=gate[i], tm=1024)
        g_raw = mm(sv['y2_t'], dx, name=f"out_proj_dw{i}", out_dtype=F32, tk=1024)
        g_wout, d_gate = out_proj_grads(g_raw, lw['w_out'], gate[i], name=f"out_proj_dgate{i}")
        g_wout = g_wout.reshape(NDEV, BW // NDEV, D)
        if kind == 0:
            ug = sv['ug']
            dz, dproj, d_ps = gated_bwd(dy2, sv['z'], ug, BW, pool_scale_full[j:j + 1], name=f"pool_gate_bwd{i}")
            dpooled = gmm_nn(dz, lw['w_grp'], tb=True, name=f"pool_grp_dx{i}")
            g_grp = gmm_tn(sv['pooled'], dz, len(POOL_WINDOWS), name=f"pool_grp_dw{i}")
            dproj = band(dpooled, BW, transpose=True, name=f"pool_band_bwd{i}", into=dproj)
            dh = in_dx(i, dproj, lw['w_in_t'])
            g_win = mm(h_t, dproj, out_split=pool_w_in.shape[2], name=f"pool_in_dw{i}")
            Gp, Cg = g_grp.shape[0], g_grp.shape[1]
            g_grp = g_grp.astype(BF16).reshape(Gp, NDEV, Cg // NDEV, Cg).transpose(1, 0, 2, 3)
            layer_grads[i] = [g_win, g_grp, g_wout]
            small_grads.setdefault('pool_scale', [None] * n_pool)[j] = d_ps
        elif kind == 1:
            proj = sv['proj']
            do, dproj, d_ng = gla_out_bwd(dy2, sv['of'], sv['ob'], proj, 2 * BW, gla_norm_g[j:j + 1],
                                          name=f"gla_out_bwd{i}")
            prev = None
            dh_acc = None
            for dname in ('f', 'b'):
                scan_args = (proj, sv[f"z{dname}"], sv[f"bias{dname}"], sv[f"st{dname}"], do, prev)
                if dname == 'f':
                    *prev, dzg, dbias = gla_bwd(*scan_args, reverse=False, name=f"gla_scan_bwd_f{i}")
                else:
                    dproj, dzg, dbias = gla_bwd(*scan_args, reverse=True, name=f"gla_scan_bwd_b{i}", into=dproj)
                dhw1 = mm(dzg, sv[f"w2p{dname}"], tb=True, name=f"gla_w2{dname}_dx{i}", tn=LRP)
                g_w2 = mm(sv[f"hw1{dname}"], dzg, ta=True, out_dtype=F32, name=f"gla_w2{dname}_dw{i}", tm=LRP)
                g_w1 = mm(h_t, dhw1, out_dtype=F32, name=f"gla_w1{dname}_dw{i}", tn=LRP)
                dh_acc = mm(dhw1, sv[f"w1p{dname}"], tb=True, add=dh_acc, name=f"gla_w1{dname}_dx{i}", tk=LRP)
                key = 'gla_fwd' if dname == 'f' else 'gla_bwd'
                small_grads[key + '_w1'] = g_w1[:, :LR]
                small_grads[key + '_w2'] = g_w2[:LR]
                small_grads[key + '_b'] = dbias
            dh = in_dx(i, dproj, lw['w_in_t'], add=dh_acc)
            layer_grads[i] = [mm(h_t, dproj, out_split=gla_w_in.shape[2], name=f"gla_in_dw{i}"), g_wout]
            small_grads['gla_norm_g'] = d_ng
        else:
            proj = sv['proj']
            QW, KVW, _, _ = _attn_dims(proj)
            do, dproj, delta = attn_gate_bwd(dy2, sv['o'], proj, QW + 2 * KVW, name=f"attn_gate_bwd{i}")
            dqr, dkr, dproj = flash_bwd(sv['qr'], sv['kr'], proj, QW + KVW, do, sv['lse'], delta, dproj,
                                        name=f"attn_flash_bwd{i}")
            dproj, d_qg, d_kg = attn_prep_bwd(dqr, dkr, proj, attn_q_norm_g[j:j + 1], attn_k_norm_g[j:j + 1],
                                              cos, sin, dproj, name=f"attn_prep_bwd{i}")
            dh = in_dx(i, dproj, lw['w_in_t'])
            layer_grads[i] = [mm(h_t, dproj, out_split=attn_w_in.shape[2], name=f"attn_in_dw{i}"), g_wout]
            small_grads['attn_q_norm_g'] = d_qg
            small_grads['attn_k_norm_g'] = d_kg
        dx, d_scale, d_shift = prenorm_bwd(xs[i], dh, dx, scale[i], name=f"prenorm_bwd{i}")
        d_mod[i] = jnp.concatenate([d_shift, d_scale, d_gate], axis=1)
    grad_x = dx.reshape(1, T, D)

    packed = _pack(layer_grads[0], BF16, lead=NDEV)
    chip_sums[0] = chip_sum(0, packed, exchange(SIBLING_ROWS, packed, name="exchange_g0_sibling"))
    received[0] = exchange(CHIP_ROWS, chip_sums[0], name="exchange_g0_chips")
    per_name = {}
    for i in range(DEPTH):
        summed = sum_slots(received[i], name=f"sum_g{i}")
        parts = _unpack(summed, [a.shape for a in layer_shards(i)])
        for n, g in zip(layer_names[i % N_MIXERS], parts):
            per_name.setdefault(n, []).append(g)
    big_g = {n: jnp.stack(gs) for n, gs in per_name.items()}

    small_order = ['b_mod', 'gla_fwd_b', 'gla_bwd_b', 'gla_norm_g', 'attn_q_norm_g', 'attn_k_norm_g', 'final_norm_g',
                   'pool_scale', 'gla_fwd_w1', 'gla_fwd_w2', 'gla_bwd_w1', 'gla_bwd_w2']
    small_grads['b_mod'] = jnp.concatenate(d_mod, axis=0)
    small_grads['final_norm_g'] = d_final_g
    small_grads['pool_scale'] = jnp.concatenate(small_grads['pool_scale'], axis=0)
    part_items = [jnp.pad(loss_part.reshape(1), (0, LANES - 1))] + [small_grads[n] for n in small_order]
    part_shapes = [a.shape for a in part_items]
    g4 = exchange(GATHER, _pack(part_items, F32), name="gather_parts")
    tot = dict(zip(['loss'] + small_order, _unpack(sum_slots(g4, name="sum_parts"), part_shapes)))
    loss = tot['loss'][0]
    d_mod_all = _unpack(g4, part_shapes, lead=NDEV)[1]

    grads = {}
    grads.update(big_g)
    grads['b_mod'] = tot['b_mod']
    for n in ('gla_fwd_b', 'gla_bwd_b', 'gla_norm_g', 'attn_q_norm_g', 'attn_k_norm_g'):
        grads[n] = tot[n].reshape(W[n].shape)
    grads['final_norm_g'] = tot['final_norm_g'].reshape(D)
    ps_n = pool_scale.shape[1]
    grads['pool_scale'] = lax.dynamic_slice_in_dim(tot['pool_scale'], me * ps_n, ps_n, axis=1)
    rows = gla_fwd_w1.shape[1]
    cols = gla_fwd_w2.shape[2]
    for key in ('gla_fwd', 'gla_bwd'):
        grads[key + '_w1'] = lax.dynamic_slice_in_dim(tot[key + '_w1'], me * rows, rows, axis=0).reshape(1, rows, LR)
        grads[key + '_w2'] = lax.dynamic_slice_in_dim(tot[key + '_w2'], me * cols, cols, axis=1).reshape(1, LR, cols)

    c_t = c_all.T
    dm_slab = lax.dynamic_slice_in_dim(d_mod_all, me * Dm, Dm, axis=2)
    grads['w_mod'] = jnp.stack([outer_silu(c_t, dm_slab[:, i], name=f"mod_dw{i}") for i in range(DEPTH)])

    deltas, new_m, new_v = {}, {}, {}
    for n in WEIGHTS:
        deltas[n], new_m[n], new_v[n] = _adamw_nd(W[n], grads[n], M[n], V[n], name=f"adamw_{n}")

    return (loss, grad_x, *[grads[n] for n in WEIGHTS], *[deltas[n] for n in WEIGHTS],
            *[new_m[n] for n in WEIGHTS], *[new_v[n] for n in WEIGHTS])
```
